```python
import math
import jax
import jax.numpy as jnp
from jax import lax
import numpy as np

D_MODEL = 4096
BATCH = 8
SEQ = 4096
DEPTH = 2

N_BRANCH = 4
BRANCH_WIDTH = D_MODEL // 4
CONV_WIDTH = 4
NORM_EPS = 1e-6
DN_HEAD_DIM = 128
DN_HEADS = BRANCH_WIDTH // DN_HEAD_DIM
DN_CHUNK = 64
LRU_WIDTH = BRANCH_WIDTH
LRU_BLOCKS = 8
LRU_BLOCK = LRU_WIDTH // LRU_BLOCKS
LRU_C = 8.0
SSM_WIDTH = BRANCH_WIDTH
SSM_GROUP = 16
SSM_GROUPS = SSM_WIDTH // SSM_GROUP
SSM_STATE = 64
MEM_LEN = 256
MEM_HEADS = 4
MEM_HEAD_DIM = BRANCH_WIDTH // MEM_HEADS
GATE_RANK = 256
IN_SIZES = (BRANCH_WIDTH, BRANCH_WIDTH, BRANCH_WIDTH, BRANCH_WIDTH, DN_HEADS, DN_HEADS,
            LRU_WIDTH, LRU_WIDTH, SSM_WIDTH, SSM_WIDTH, BRANCH_WIDTH, BRANCH_WIDTH, GATE_RANK)
D_IN = sum(IN_SIZES)

kernel_name = "hybrid_gated_parallel_mixers"


def _rms_norm(x, w):
    xf = x.astype(jnp.float32)
    var = jnp.mean(xf * xf, axis=-1, keepdims=True)
    return (xf * lax.rsqrt(var + NORM_EPS) * w.astype(jnp.float32)).astype(x.dtype)


def _l2_norm(x):
    return x * lax.rsqrt(jnp.sum(x * x, axis=-1, keepdims=True) + NORM_EPS)


def _causal_conv(x, w):
    s = x.shape[1]
    xp = jnp.pad(x, ((0, 0), (CONV_WIDTH - 1, 0), (0, 0)))
    y = xp[:, 0:s] * w[0]
    for j in range(1, CONV_WIDTH):
        y = y + xp[:, j:j + s] * w[j]
    return y


def _linear_combine(e1, e2):
    a1, b1 = e1
    a2, b2 = e2
    return a1 * a2, a2 * b1 + b2


def _complex_combine(e1, e2):
    a1r, a1i, b1r, b1i = e1
    a2r, a2i, b2r, b2i = e2
    ar = a2r * a1r - a2i * a1i
    ai = a2r * a1i + a2i * a1r
    br = a2r * b1r - a2i * b1i + b2r
    bi = a2r * b1i + a2i * b1r + b2i
    return ar, ai, br, bi


def _chunk_gated_delta_rule(q, k, v, g, beta):
    f32 = jnp.float32
    b, s, h, dk = q.shape
    dv = v.shape[-1]
    n = s // DN_CHUNK

    def to_chunks(t):
        return t.astype(f32).reshape(b, n, DN_CHUNK, h, -1).transpose(0, 3, 1, 2, 4)

    q = to_chunks(q) * (dk ** -0.5)
    k = to_chunks(k)
    v = to_chunks(v)
    beta = to_chunks(beta[..., None])
    g = jnp.cumsum(to_chunks(g[..., None])[..., 0], axis=-1)
    idx = jnp.arange(DN_CHUNK)
    causal = idx[:, None] >= idx[None, :]
    strict = idx[:, None] > idx[None, :]
    decay = jnp.exp(jnp.where(causal, g[..., :, None] - g[..., None, :], -jnp.inf))
    k_beta = k * beta
    v_beta = v * beta
    kk = jnp.einsum("bhncd,bhnjd->bhncj", k_beta, k) * decay
    lower = jnp.where(strict, kk, 0.0) + jnp.eye(DN_CHUNK, dtype=f32)
    rhs = jnp.concatenate([v_beta, k_beta * jnp.exp(g)[..., None]], axis=-1)
    sol = lax.linalg.triangular_solve(lower, rhs, left_side=True, lower=True, unit_diagonal=True)
    u, w = sol[..., :dv], sol[..., dv:]
    qk = jnp.where(causal, jnp.einsum("bhncd,bhnjd->bhncj", q, k) * decay, 0.0)
    g_last = g[..., -1]
    k_dec = k * jnp.exp(g_last[..., None] - g)[..., None]
    q_dec = q * jnp.exp(g)[..., None]

    def step(state, xs):
        q_c, qk_c, u_c, w_c, k_c, gl_c = xs
        v_new = u_c - jnp.einsum("bhcd,bhde->bhce", w_c, state)
        out = jnp.einsum("bhcd,bhde->bhce", q_c, state) + jnp.einsum("bhcj,bhje->bhce", qk_c, v_new)
        state = state * jnp.exp(gl_c)[..., None, None] + jnp.einsum("bhcd,bhce->bhde", k_c, v_new)
        return state, out

    xs = tuple(jnp.moveaxis(t, 2, 0) for t in (q_dec, qk, u, w, k_dec, g_last))
    state0 = jnp.zeros((b, h, dk, dv), f32)
    _, out = lax.scan(step, state0, xs)
    return out.transpose(1, 0, 3, 2, 4).reshape(b, s, h, dv)


def _deltanet_branch(q, k, v, z, beta_logit, alpha_logit, conv_w, a_log, dt_bias, norm_w):
    f32 = jnp.float32
    b, s, _ = q.shape
    qkv = jax.nn.silu(_causal_conv(jnp.concatenate([q, k, v], axis=-1).astype(f32), conv_w.astype(f32)))
    q, k, v = jnp.split(qkv, 3, axis=-1)
    q = _l2_norm(q.reshape(b, s, DN_HEADS, DN_HEAD_DIM))
    k = _l2_norm(k.reshape(b, s, DN_HEADS, DN_HEAD_DIM))
    v = v.reshape(b, s, DN_HEADS, DN_HEAD_DIM)
    beta = jax.nn.sigmoid(beta_logit.astype(f32))
    g = -jnp.exp(a_log.astype(f32)) * jax.nn.softplus(alpha_logit.astype(f32) + dt_bias.astype(f32))
    o = _chunk_gated_delta_rule(q, k, v, g, beta)
    o = _rms_norm(o, norm_w) * jax.nn.silu(z.astype(f32)).reshape(b, s, DN_HEADS, DN_HEAD_DIM)
    return o.reshape(b, s, BRANCH_WIDTH)


def _rglru_branch(xb, z, conv_w, conv_b, w_r, b_r, w_i, b_i, lam):
    f32 = jnp.float32
    b, s, _ = xb.shape
    xc = _causal_conv(xb.astype(f32), conv_w.astype(f32)) + conv_b.astype(f32)
    blocks = xc.reshape(b, s, LRU_BLOCKS, LRU_BLOCK)
    r = jax.nn.sigmoid(jnp.einsum("bsni,nij->bsnj", blocks, w_r.astype(f32)).reshape(b, s, LRU_WIDTH) + b_r.astype(f32))
    i = jax.nn.sigmoid(jnp.einsum("bsni,nij->bsnj", blocks, w_i.astype(f32)).reshape(b, s, LRU_WIDTH) + b_i.astype(f32))
    log_a = -LRU_C * r * jax.nn.softplus(-lam.astype(f32))
    a = jnp.exp(log_a)
    inp = jnp.sqrt(-jnp.expm1(2.0 * log_a)) * (i * xc)
    _, hs = lax.associative_scan(_linear_combine, (a, inp), axis=1)
    return hs * jax.nn.silu(z.astype(f32))


def _s5_branch(u, z, log_dt, a_re, a_im, b_re, b_im, c_re, c_im, d_skip, w_glu, b_glu):
    f32 = jnp.float32
    b, s, _ = u.shape
    ug = u.astype(f32).reshape(b, s, SSM_GROUPS, SSM_GROUP)
    a_re = a_re.astype(f32)
    a_im = a_im.astype(f32)
    b_re = b_re.astype(f32)
    b_im = b_im.astype(f32)
    dt = jnp.exp(log_dt.astype(f32))[:, None]
    mag = jnp.exp(dt * a_re)
    ab_re = mag * jnp.cos(dt * a_im)
    ab_im = mag * jnp.sin(dt * a_im)
    den = a_re * a_re + a_im * a_im
    f_re = ((ab_re - 1.0) * a_re + ab_im * a_im) / den
    f_im = (ab_im * a_re - (ab_re - 1.0) * a_im) / den
    bb_re = f_re[..., None] * b_re - f_im[..., None] * b_im
    bb_im = f_re[..., None] * b_im + f_im[..., None] * b_re
    bu_re = jnp.einsum("bsgc,gnc->bsgn", ug, bb_re)
    bu_im = jnp.einsum("bsgc,gnc->bsgn", ug, bb_im)
    shape = bu_re.shape
    _, _, x_re, x_im = lax.associative_scan(
        _complex_combine,
        (jnp.broadcast_to(ab_re, shape), jnp.broadcast_to(ab_im, shape), bu_re, bu_im),
        axis=1)
    y = (jnp.einsum("bsgn,gcn->bsgc", x_re, c_re.astype(f32))
         - jnp.einsum("bsgn,gcn->bsgc", x_im, c_im.astype(f32))
         + d_skip.astype(f32) * ug)
    y = jax.nn.gelu(y.reshape(b, s, SSM_WIDTH))
    val, gate = jnp.split(y @ w_glu.astype(f32) + b_glu.astype(f32), 2, axis=-1)
    return val * jax.nn.sigmoid(gate) * jax.nn.silu(z.astype(f32))


def _memory_branch(q, z, mem, mem_norm_w, w_kv):
    f32 = jnp.float32
    b, s, _ = q.shape
    m_len = mem.shape[1]
    m = _rms_norm(mem, mem_norm_w)
    k, v = jnp.split(m @ w_kv, 2, axis=-1)
    k = k.astype(f32).reshape(b, m_len, MEM_HEADS, MEM_HEAD_DIM)
    v = v.astype(f32).reshape(b, m_len, MEM_HEADS, MEM_HEAD_DIM)
    qh = q.astype(f32).reshape(b, s, MEM_HEADS, MEM_HEAD_DIM)
    scores = jnp.einsum("bshd,bmhd->bhsm", qh, k) * (MEM_HEAD_DIM ** -0.5)
    p = jax.nn.softmax(scores, axis=-1)
    o = jnp.einsum("bhsm,bmhd->bshd", p, v).reshape(b, s, BRANCH_WIDTH)
    return o * jax.nn.silu(z.astype(f32))


def _fwd_setup_inputs(seed: int = 0) -> dict:
    key = jax.random.key(seed)
    ks = jax.random.split(key, 32)
    f32 = jnp.float32

    def nrm(k, shape, scale):
        return jax.random.normal(k, shape, f32) * scale

    def unif(k, shape, lo, hi):
        return jax.random.uniform(k, shape, f32, lo, hi)

    x = nrm(ks[0], (BATCH, SEQ, D_MODEL), 1.0)
    mem = nrm(ks[1], (BATCH, MEM_LEN, D_MODEL), 1.0)
    norm_w = 1.0 + nrm(ks[2], (DEPTH, D_MODEL), 0.02)
    w_in = nrm(ks[3], (DEPTH, D_MODEL, D_IN), D_MODEL ** -0.5)
    dn_conv_w = nrm(ks[4], (DEPTH, CONV_WIDTH, 3 * BRANCH_WIDTH), CONV_WIDTH ** -0.5)
    dn_a_log = jnp.log(unif(ks[5], (DEPTH, DN_HEADS), 1.0, 16.0))
    dn_dt = jnp.exp(unif(ks[6], (DEPTH, DN_HEADS), math.log(1e-3), math.log(1e-1)))
    dn_dt_bias = dn_dt + jnp.log(-jnp.expm1(-dn_dt))
    dn_norm_w = 1.0 + nrm(ks[7], (DEPTH, DN_HEAD_DIM), 0.02)
    lru_conv_w = nrm(ks[8], (DEPTH, CONV_WIDTH, LRU_WIDTH), CONV_WIDTH ** -0.5)
    lru_conv_b = nrm(ks[9], (DEPTH, LRU_WIDTH), 0.01)
    lru_w_r = nrm(ks[10], (DEPTH, LRU_BLOCKS, LRU_BLOCK, LRU_BLOCK), LRU_BLOCK ** -0.5)
    lru_b_r = nrm(ks[11], (DEPTH, LRU_WIDTH), 0.01)
    lru_w_i = nrm(ks[12], (DEPTH, LRU_BLOCKS, LRU_BLOCK, LRU_BLOCK), LRU_BLOCK ** -0.5)
    lru_b_i = nrm(ks[13], (DEPTH, LRU_WIDTH), 0.01)
    a_pow = unif(ks[14], (DEPTH, LRU_WIDTH), 0.9, 0.999)
    a0 = a_pow ** (1.0 / LRU_C)
    lru_lambda = jnp.log(a0) - jnp.log1p(-a0)
    ssm_log_dt = unif(ks[15], (DEPTH, SSM_GROUPS), math.log(1e-3), math.log(1e-1))
    ssm_a_re = -0.5 + nrm(ks[16], (DEPTH, SSM_GROUPS, SSM_STATE), 0.01)
    ssm_a_im = math.pi * jnp.arange(SSM_STATE, dtype=f32) + nrm(ks[17], (DEPTH, SSM_GROUPS, SSM_STATE), 0.01)
    ssm_b_re = nrm(ks[18], (DEPTH, SSM_GROUPS, SSM_STATE, SSM_GROUP), (2 * SSM_GROUP) ** -0.5)
    ssm_b_im = nrm(ks[19], (DEPTH, SSM_GROUPS, SSM_STATE, SSM_GROUP), (2 * SSM_GROUP) ** -0.5)
    ssm_c_re = nrm(ks[20], (DEPTH, SSM_GROUPS, SSM_GROUP, SSM_STATE), (2 * SSM_STATE) ** -0.5)
    ssm_c_im = nrm(ks[21], (DEPTH, SSM_GROUPS, SSM_GROUP, SSM_STATE), (2 * SSM_STATE) ** -0.5)
    ssm_d = nrm(ks[22], (DEPTH, SSM_GROUPS, SSM_GROUP), 1.0)
    ssm_w_glu = nrm(ks[23], (DEPTH, SSM_WIDTH, 2 * SSM_WIDTH), SSM_WIDTH ** -0.5)
    ssm_b_glu = nrm(ks[24], (DEPTH, 2 * SSM_WIDTH), 0.01)
    mem_norm_w = 1.0 + nrm(ks[25], (DEPTH, D_MODEL), 0.02)
    w_kv = nrm(ks[26], (DEPTH, D_MODEL, 2 * BRANCH_WIDTH), D_MODEL ** -0.5)
    w_gate = nrm(ks[27], (DEPTH, N_BRANCH, GATE_RANK, D_MODEL), GATE_RANK ** -0.5)
    b_gate = nrm(ks[28], (DEPTH, N_BRANCH, D_MODEL), 0.01)
    w_branch = nrm(ks[29], (DEPTH, N_BRANCH, BRANCH_WIDTH, D_MODEL), BRANCH_WIDTH ** -0.5)
    w_out = nrm(ks[30], (DEPTH, D_MODEL, D_MODEL), D_MODEL ** -0.5)
    final_norm_w = 1.0 + nrm(ks[31], (D_MODEL,), 0.02)
    return {
        "x": x, "mem": mem, "norm_w": norm_w, "w_in": w_in,
        "dn_conv_w": dn_conv_w, "dn_a_log": dn_a_log, "dn_dt_bias": dn_dt_bias, "dn_norm_w": dn_norm_w,
        "lru_conv_w": lru_conv_w, "lru_conv_b": lru_conv_b, "lru_w_r": lru_w_r, "lru_b_r": lru_b_r,
        "lru_w_i": lru_w_i, "lru_b_i": lru_b_i, "lru_lambda": lru_lambda,
        "ssm_log_dt": ssm_log_dt, "ssm_a_re": ssm_a_re, "ssm_a_im": ssm_a_im,
        "ssm_b_re": ssm_b_re, "ssm_b_im": ssm_b_im, "ssm_c_re": ssm_c_re, "ssm_c_im": ssm_c_im,
        "ssm_d": ssm_d, "ssm_w_glu": ssm_w_glu, "ssm_b_glu": ssm_b_glu,
        "mem_norm_w": mem_norm_w, "w_kv": w_kv, "w_gate": w_gate, "b_gate": b_gate,
        "w_branch": w_branch, "w_out": w_out, "final_norm_w": final_norm_w,
    }


def _fwd_reference(x, mem, norm_w, w_in, dn_conv_w, dn_a_log, dn_dt_bias, dn_norm_w,
              lru_conv_w, lru_conv_b, lru_w_r, lru_b_r, lru_w_i, lru_b_i, lru_lambda,
              ssm_log_dt, ssm_a_re, ssm_a_im, ssm_b_re, ssm_b_im, ssm_c_re, ssm_c_im,
              ssm_d, ssm_w_glu, ssm_b_glu, mem_norm_w, w_kv, w_gate, b_gate,
              w_branch, w_out, final_norm_w):
    offsets = np.cumsum(IN_SIZES)[:-1].tolist()
    for l in range(DEPTH):
        h = _rms_norm(x, norm_w[l])
        (dq, dk, dv, dz, d_beta, d_alpha, lx, lz, su, sz, mq, mz, g_low) = jnp.split(
            h @ w_in[l], offsets, axis=-1)
        o_a = _deltanet_branch(dq, dk, dv, dz, d_beta, d_alpha, dn_conv_w[l], dn_a_log[l],
                               dn_dt_bias[l], dn_norm_w[l])
        o_b = _rglru_branch(lx, lz, lru_conv_w[l], lru_conv_b[l], lru_w_r[l], lru_b_r[l],
                            lru_w_i[l], lru_b_i[l], lru_lambda[l])
        o_c = _s5_branch(su, sz, ssm_log_dt[l], ssm_a_re[l], ssm_a_im[l], ssm_b_re[l], ssm_b_im[l],
                         ssm_c_re[l], ssm_c_im[l], ssm_d[l], ssm_w_glu[l], ssm_b_glu[l])
        o_d = _memory_branch(mq, mz, mem, mem_norm_w[l], w_kv[l])
        branches = (o_a, o_b, o_c, o_d)
        merged = jnp.zeros(h.shape, jnp.float32)
        for n in range(N_BRANCH):
            gate = jax.nn.sigmoid((g_low @ w_gate[l, n] + b_gate[l, n]).astype(jnp.float32))
            merged = merged + gate * (branches[n] @ w_branch[l, n].astype(jnp.float32))
        x = x + (merged.astype(x.dtype) @ w_out[l]).astype(x.dtype)
    return _rms_norm(x, final_norm_w)


import jax as _jax
import jax.numpy as _jnp

TWIN_FORMAT = 'train_step'
FWD_PARAMS = ['x', 'mem', 'norm_w', 'w_in', 'dn_conv_w', 'dn_a_log', 'dn_dt_bias', 'dn_norm_w', 'lru_conv_w', 'lru_conv_b', 'lru_w_r', 'lru_b_r', 'lru_w_i', 'lru_b_i', 'lru_lambda', 'ssm_log_dt', 'ssm_a_re', 'ssm_a_im', 'ssm_b_re', 'ssm_b_im', 'ssm_c_re', 'ssm_c_im', 'ssm_d', 'ssm_w_glu', 'ssm_b_glu', 'mem_norm_w', 'w_kv', 'w_gate', 'b_gate', 'w_branch', 'w_out', 'final_norm_w']
TWIN_WEIGHTS = ['norm_w', 'w_in', 'dn_conv_w', 'dn_a_log', 'dn_dt_bias', 'dn_norm_w', 'lru_conv_w', 'lru_conv_b', 'lru_w_r', 'lru_b_r', 'lru_w_i', 'lru_b_i', 'lru_lambda', 'ssm_log_dt', 'ssm_a_re', 'ssm_a_im', 'ssm_b_re', 'ssm_b_im', 'ssm_c_re', 'ssm_c_im', 'ssm_d', 'ssm_w_glu', 'ssm_b_glu', 'mem_norm_w', 'w_kv', 'w_gate', 'b_gate', 'w_branch', 'w_out', 'final_norm_w']
TWIN_DIFF_INPUT = 'x'
TWIN_INPUTS = ['x', 'mem', 'norm_w', 'w_in', 'dn_conv_w', 'dn_a_log', 'dn_dt_bias', 'dn_norm_w', 'lru_conv_w', 'lru_conv_b', 'lru_w_r', 'lru_b_r', 'lru_w_i', 'lru_b_i', 'lru_lambda', 'ssm_log_dt', 'ssm_a_re', 'ssm_a_im', 'ssm_b_re', 'ssm_b_im', 'ssm_c_re', 'ssm_c_im', 'ssm_d', 'ssm_w_glu', 'ssm_b_glu', 'mem_norm_w', 'w_kv', 'w_gate', 'b_gate', 'w_branch', 'w_out', 'final_norm_w', 'loss_target', 'm_norm_w', 'm_w_in', 'm_dn_conv_w', 'm_dn_a_log', 'm_dn_dt_bias', 'm_dn_norm_w', 'm_lru_conv_w', 'm_lru_conv_b', 'm_lru_w_r', 'm_lru_b_r', 'm_lru_w_i', 'm_lru_b_i', 'm_lru_lambda', 'm_ssm_log_dt', 'm_ssm_a_re', 'm_ssm_a_im', 'm_ssm_b_re', 'm_ssm_b_im', 'm_ssm_c_re', 'm_ssm_c_im', 'm_ssm_d', 'm_ssm_w_glu', 'm_ssm_b_glu', 'm_mem_norm_w', 'm_w_kv', 'm_w_gate', 'm_b_gate', 'm_w_branch', 'm_w_out', 'm_final_norm_w', 'v_norm_w', 'v_w_in', 'v_dn_conv_w', 'v_dn_a_log', 'v_dn_dt_bias', 'v_dn_norm_w', 'v_lru_conv_w', 'v_lru_conv_b', 'v_lru_w_r', 'v_lru_b_r', 'v_lru_w_i', 'v_lru_b_i', 'v_lru_lambda', 'v_ssm_log_dt', 'v_ssm_a_re', 'v_ssm_a_im', 'v_ssm_b_re', 'v_ssm_b_im', 'v_ssm_c_re', 'v_ssm_c_im', 'v_ssm_d', 'v_ssm_w_glu', 'v_ssm_b_glu', 'v_mem_norm_w', 'v_w_kv', 'v_w_gate', 'v_b_gate', 'v_w_branch', 'v_w_out', 'v_final_norm_w']
TWIN_OUTPUTS = ['loss', 'grad_x', 'grad_norm_w', 'grad_w_in', 'grad_dn_conv_w', 'grad_dn_a_log', 'grad_dn_dt_bias', 'grad_dn_norm_w', 'grad_lru_conv_w', 'grad_lru_conv_b', 'grad_lru_w_r', 'grad_lru_b_r', 'grad_lru_w_i', 'grad_lru_b_i', 'grad_lru_lambda', 'grad_ssm_log_dt', 'grad_ssm_a_re', 'grad_ssm_a_im', 'grad_ssm_b_re', 'grad_ssm_b_im', 'grad_ssm_c_re', 'grad_ssm_c_im', 'grad_ssm_d', 'grad_ssm_w_glu', 'grad_ssm_b_glu', 'grad_mem_norm_w', 'grad_w_kv', 'grad_w_gate', 'grad_b_gate', 'grad_w_branch', 'grad_w_out', 'grad_final_norm_w', 'delta_norm_w', 'delta_w_in', 'delta_dn_conv_w', 'delta_dn_a_log', 'delta_dn_dt_bias', 'delta_dn_norm_w', 'delta_lru_conv_w', 'delta_lru_conv_b', 'delta_lru_w_r', 'delta_lru_b_r', 'delta_lru_w_i', 'delta_lru_b_i', 'delta_lru_lambda', 'delta_ssm_log_dt', 'delta_ssm_a_re', 'delta_ssm_a_im', 'delta_ssm_b_re', 'delta_ssm_b_im', 'delta_ssm_c_re', 'delta_ssm_c_im', 'delta_ssm_d', 'delta_ssm_w_glu', 'delta_ssm_b_glu', 'delta_mem_norm_w', 'delta_w_kv', 'delta_w_gate', 'delta_b_gate', 'delta_w_branch', 'delta_w_out', 'delta_final_norm_w', 'new_m_norm_w', 'new_m_w_in', 'new_m_dn_conv_w', 'new_m_dn_a_log', 'new_m_dn_dt_bias', 'new_m_dn_norm_w', 'new_m_lru_conv_w', 'new_m_lru_conv_b', 'new_m_lru_w_r', 'new_m_lru_b_r', 'new_m_lru_w_i', 'new_m_lru_b_i', 'new_m_lru_lambda', 'new_m_ssm_log_dt', 'new_m_ssm_a_re', 'new_m_ssm_a_im', 'new_m_ssm_b_re', 'new_m_ssm_b_im', 'new_m_ssm_c_re', 'new_m_ssm_c_im', 'new_m_ssm_d', 'new_m_ssm_w_glu', 'new_m_ssm_b_glu', 'new_m_mem_norm_w', 'new_m_w_kv', 'new_m_w_gate', 'new_m_b_gate', 'new_m_w_branch', 'new_m_w_out', 'new_m_final_norm_w', 'new_v_norm_w', 'new_v_w_in', 'new_v_dn_conv_w', 'new_v_dn_a_log', 'new_v_dn_dt_bias', 'new_v_dn_norm_w', 'new_v_lru_conv_w', 'new_v_lru_conv_b', 'new_v_lru_w_r', 'new_v_lru_b_r', 'new_v_lru_w_i', 'new_v_lru_b_i', 'new_v_lru_lambda', 'new_v_ssm_log_dt', 'new_v_ssm_a_re', 'new_v_ssm_a_im', 'new_v_ssm_b_re', 'new_v_ssm_b_im', 'new_v_ssm_c_re', 'new_v_ssm_c_im', 'new_v_ssm_d', 'new_v_ssm_w_glu', 'new_v_ssm_b_glu', 'new_v_mem_norm_w', 'new_v_w_kv', 'new_v_w_gate', 'new_v_b_gate', 'new_v_w_branch', 'new_v_w_out', 'new_v_final_norm_w']
TWIN_LEAF_KINDS = {'loss': 'loss', 'grad_x': 'grad_x', 'grad_norm_w': 'grad_w', 'grad_w_in': 'grad_w', 'grad_dn_conv_w': 'grad_w', 'grad_dn_a_log': 'grad_w', 'grad_dn_dt_bias': 'grad_w', 'grad_dn_norm_w': 'grad_w', 'grad_lru_conv_w': 'grad_w', 'grad_lru_conv_b': 'grad_w', 'grad_lru_w_r': 'grad_w', 'grad_lru_b_r': 'grad_w', 'grad_lru_w_i': 'grad_w', 'grad_lru_b_i': 'grad_w', 'grad_lru_lambda': 'grad_w', 'grad_ssm_log_dt': 'grad_w', 'grad_ssm_a_re': 'grad_w', 'grad_ssm_a_im': 'grad_w', 'grad_ssm_b_re': 'grad_w', 'grad_ssm_b_im': 'grad_w', 'grad_ssm_c_re': 'grad_w', 'grad_ssm_c_im': 'grad_w', 'grad_ssm_d': 'grad_w', 'grad_ssm_w_glu': 'grad_w', 'grad_ssm_b_glu': 'grad_w', 'grad_mem_norm_w': 'grad_w', 'grad_w_kv': 'grad_w', 'grad_w_gate': 'grad_w', 'grad_b_gate': 'grad_w', 'grad_w_branch': 'grad_w', 'grad_w_out': 'grad_w', 'grad_final_norm_w': 'grad_w', 'delta_norm_w': 'delta_w', 'delta_w_in': 'delta_w', 'delta_dn_conv_w': 'delta_w', 'delta_dn_a_log': 'delta_w', 'delta_dn_dt_bias': 'delta_w', 'delta_dn_norm_w': 'delta_w', 'delta_lru_conv_w': 'delta_w', 'delta_lru_conv_b': 'delta_w', 'delta_lru_w_r': 'delta_w', 'delta_lru_b_r': 'delta_w', 'delta_lru_w_i': 'delta_w', 'delta_lru_b_i': 'delta_w', 'delta_lru_lambda': 'delta_w', 'delta_ssm_log_dt': 'delta_w', 'delta_ssm_a_re': 'delta_w', 'delta_ssm_a_im': 'delta_w', 'delta_ssm_b_re': 'delta_w', 'delta_ssm_b_im': 'delta_w', 'delta_ssm_c_re': 'delta_w', 'delta_ssm_c_im': 'delta_w', 'delta_ssm_d': 'delta_w', 'delta_ssm_w_glu': 'delta_w', 'delta_ssm_b_glu': 'delta_w', 'delta_mem_norm_w': 'delta_w', 'delta_w_kv': 'delta_w', 'delta_w_gate': 'delta_w', 'delta_b_gate': 'delta_w', 'delta_w_branch': 'delta_w', 'delta_w_out': 'delta_w', 'delta_final_norm_w': 'delta_w', 'new_m_norm_w': 'new_m', 'new_m_w_in': 'new_m', 'new_m_dn_conv_w': 'new_m', 'new_m_dn_a_log': 'new_m', 'new_m_dn_dt_bias': 'new_m', 'new_m_dn_norm_w': 'new_m', 'new_m_lru_conv_w': 'new_m', 'new_m_lru_conv_b': 'new_m', 'new_m_lru_w_r': 'new_m', 'new_m_lru_b_r': 'new_m', 'new_m_lru_w_i': 'new_m', 'new_m_lru_b_i': 'new_m', 'new_m_lru_lambda': 'new_m', 'new_m_ssm_log_dt': 'new_m', 'new_m_ssm_a_re': 'new_m', 'new_m_ssm_a_im': 'new_m', 'new_m_ssm_b_re': 'new_m', 'new_m_ssm_b_im': 'new_m', 'new_m_ssm_c_re': 'new_m', 'new_m_ssm_c_im': 'new_m', 'new_m_ssm_d': 'new_m', 'new_m_ssm_w_glu': 'new_m', 'new_m_ssm_b_glu': 'new_m', 'new_m_mem_norm_w': 'new_m', 'new_m_w_kv': 'new_m', 'new_m_w_gate': 'new_m', 'new_m_b_gate': 'new_m', 'new_m_w_branch': 'new_m', 'new_m_w_out': 'new_m', 'new_m_final_norm_w': 'new_m', 'new_v_norm_w': 'new_v', 'new_v_w_in': 'new_v', 'new_v_dn_conv_w': 'new_v', 'new_v_dn_a_log': 'new_v', 'new_v_dn_dt_bias': 'new_v', 'new_v_dn_norm_w': 'new_v', 'new_v_lru_conv_w': 'new_v', 'new_v_lru_conv_b': 'new_v', 'new_v_lru_w_r': 'new_v', 'new_v_lru_b_r': 'new_v', 'new_v_lru_w_i': 'new_v', 'new_v_lru_b_i': 'new_v', 'new_v_lru_lambda': 'new_v', 'new_v_ssm_log_dt': 'new_v', 'new_v_ssm_a_re': 'new_v', 'new_v_ssm_a_im': 'new_v', 'new_v_ssm_b_re': 'new_v', 'new_v_ssm_b_im': 'new_v', 'new_v_ssm_c_re': 'new_v', 'new_v_ssm_c_im': 'new_v', 'new_v_ssm_d': 'new_v', 'new_v_ssm_w_glu': 'new_v', 'new_v_ssm_b_glu': 'new_v', 'new_v_mem_norm_w': 'new_v', 'new_v_w_kv': 'new_v', 'new_v_w_gate': 'new_v', 'new_v_b_gate': 'new_v', 'new_v_w_branch': 'new_v', 'new_v_w_out': 'new_v', 'new_v_final_norm_w': 'new_v'}


def _forward(args):
    return _fwd_reference(*[args[k] for k in FWD_PARAMS])


def _output_shape():
    out = _jax.eval_shape(lambda: _forward(_fwd_setup_inputs(0)))
    return out.shape, out.dtype

N_MICROBATCH = 1
ADAM_LR = 0.001
ADAM_B1 = 0.9
ADAM_B2 = 0.999
ADAM_EPS = 1e-08
ADAM_WD = 0.01
ADAM_STEP = 10
PER_EXAMPLE_BATCH_AXIS = {'x': 0, 'mem': 0, 'loss_target': 0}
SHARED_INPUTS = []
_WEIGHT_DTYPES = {'norm_w': _jnp.float32, 'w_in': _jnp.float32, 'dn_conv_w': _jnp.float32, 'dn_a_log': _jnp.float32, 'dn_dt_bias': _jnp.float32, 'dn_norm_w': _jnp.float32, 'lru_conv_w': _jnp.float32, 'lru_conv_b': _jnp.float32, 'lru_w_r': _jnp.float32, 'lru_b_r': _jnp.float32, 'lru_w_i': _jnp.float32, 'lru_b_i': _jnp.float32, 'lru_lambda': _jnp.float32, 'ssm_log_dt': _jnp.float32, 'ssm_a_re': _jnp.float32, 'ssm_a_im': _jnp.float32, 'ssm_b_re': _jnp.float32, 'ssm_b_im': _jnp.float32, 'ssm_c_re': _jnp.float32, 'ssm_c_im': _jnp.float32, 'ssm_d': _jnp.float32, 'ssm_w_glu': _jnp.float32, 'ssm_b_glu': _jnp.float32, 'mem_norm_w': _jnp.float32, 'w_kv': _jnp.float32, 'w_gate': _jnp.float32, 'b_gate': _jnp.float32, 'w_branch': _jnp.float32, 'w_out': _jnp.float32, 'final_norm_w': _jnp.float32}
MOMENT_SCALE = {'norm_w': 2.706825e-02, 'w_in': 1.739576e-02, 'dn_conv_w': 2.067055e-02, 'dn_a_log': 1.753966e-01, 'dn_dt_bias': 1.634344e-01, 'dn_norm_w': 7.432823e-02, 'lru_conv_w': 2.125299e-02, 'lru_conv_b': 2.620482e-01, 'lru_w_r': 6.732436e-03, 'lru_b_r': 5.115192e-03, 'lru_w_i': 1.231979e-02, 'lru_b_i': 7.246381e-03, 'lru_lambda': 9.293022e-03, 'ssm_log_dt': 3.661316e-01, 'ssm_a_re': 5.699563e-04, 'ssm_a_im': 5.395841e-04, 'ssm_b_re': 3.447897e-04, 'ssm_b_im': 3.368276e-04, 'ssm_c_re': 6.798653e-04, 'ssm_c_im': 6.650051e-04, 'ssm_d': 1.081876e-02, 'ssm_w_glu': 7.059937e-03, 'ssm_b_glu': 1.067846e-02, 'mem_norm_w': 2.107769e-03, 'w_kv': 2.807140e-03, 'w_gate': 3.313316e-03, 'b_gate': 3.448785e-03, 'w_branch': 8.733338e-03, 'w_out': 1.746630e-02, 'final_norm_w': 7.976934e+00}


def _to_microbatches(a, axis):
    t = _jnp.moveaxis(a, axis, 0)
    t = t.reshape((N_MICROBATCH, t.shape[0] // N_MICROBATCH) + t.shape[1:])
    return _jnp.moveaxis(t, 1, axis + 1)


def setup_inputs(seed: int = 0) -> dict:
    inp = _fwd_setup_inputs(seed)
    key = _jax.random.fold_in(_jax.random.key(seed), 7919)
    shape, _ = _output_shape()
    out = dict(inp)
    out["loss_target"] = _jax.random.normal(_jax.random.fold_in(key, 0), shape, _jnp.float32)
    for i, name in enumerate(TWIN_WEIGHTS):
        w = inp[name].astype(_jnp.float32)
        if MOMENT_SCALE is None:
            s = _jnp.sqrt(_jnp.mean(_jnp.square(w)) + 1e-30)
        else:
            s = MOMENT_SCALE[name]
        km, kv = _jax.random.split(_jax.random.fold_in(key, i + 1))
        out[name] = w
        out["m_" + name] = s * _jax.random.normal(km, w.shape, _jnp.float32)
        out["v_" + name] = (s * s) * _jax.random.uniform(kv, w.shape, _jnp.float32, 0.5, 1.5)
    if N_MICROBATCH > 1:
        for name, axis in PER_EXAMPLE_BATCH_AXIS.items():
            out[name] = _to_microbatches(out[name], axis)
    return {'x': out['x'], 'mem': out['mem'], 'norm_w': out['norm_w'], 'w_in': out['w_in'], 'dn_conv_w': out['dn_conv_w'], 'dn_a_log': out['dn_a_log'], 'dn_dt_bias': out['dn_dt_bias'], 'dn_norm_w': out['dn_norm_w'], 'lru_conv_w': out['lru_conv_w'], 'lru_conv_b': out['lru_conv_b'], 'lru_w_r': out['lru_w_r'], 'lru_b_r': out['lru_b_r'], 'lru_w_i': out['lru_w_i'], 'lru_b_i': out['lru_b_i'], 'lru_lambda': out['lru_lambda'], 'ssm_log_dt': out['ssm_log_dt'], 'ssm_a_re': out['ssm_a_re'], 'ssm_a_im': out['ssm_a_im'], 'ssm_b_re': out['ssm_b_re'], 'ssm_b_im': out['ssm_b_im'], 'ssm_c_re': out['ssm_c_re'], 'ssm_c_im': out['ssm_c_im'], 'ssm_d': out['ssm_d'], 'ssm_w_glu': out['ssm_w_glu'], 'ssm_b_glu': out['ssm_b_glu'], 'mem_norm_w': out['mem_norm_w'], 'w_kv': out['w_kv'], 'w_gate': out['w_gate'], 'b_gate': out['b_gate'], 'w_branch': out['w_branch'], 'w_out': out['w_out'], 'final_norm_w': out['final_norm_w'], 'loss_target': out['loss_target'], 'm_norm_w': out['m_norm_w'], 'm_w_in': out['m_w_in'], 'm_dn_conv_w': out['m_dn_conv_w'], 'm_dn_a_log': out['m_dn_a_log'], 'm_dn_dt_bias': out['m_dn_dt_bias'], 'm_dn_norm_w': out['m_dn_norm_w'], 'm_lru_conv_w': out['m_lru_conv_w'], 'm_lru_conv_b': out['m_lru_conv_b'], 'm_lru_w_r': out['m_lru_w_r'], 'm_lru_b_r': out['m_lru_b_r'], 'm_lru_w_i': out['m_lru_w_i'], 'm_lru_b_i': out['m_lru_b_i'], 'm_lru_lambda': out['m_lru_lambda'], 'm_ssm_log_dt': out['m_ssm_log_dt'], 'm_ssm_a_re': out['m_ssm_a_re'], 'm_ssm_a_im': out['m_ssm_a_im'], 'm_ssm_b_re': out['m_ssm_b_re'], 'm_ssm_b_im': out['m_ssm_b_im'], 'm_ssm_c_re': out['m_ssm_c_re'], 'm_ssm_c_im': out['m_ssm_c_im'], 'm_ssm_d': out['m_ssm_d'], 'm_ssm_w_glu': out['m_ssm_w_glu'], 'm_ssm_b_glu': out['m_ssm_b_glu'], 'm_mem_norm_w': out['m_mem_norm_w'], 'm_w_kv': out['m_w_kv'], 'm_w_gate': out['m_w_gate'], 'm_b_gate': out['m_b_gate'], 'm_w_branch': out['m_w_branch'], 'm_w_out': out['m_w_out'], 'm_final_norm_w': out['m_final_norm_w'], 'v_norm_w': out['v_norm_w'], 'v_w_in': out['v_w_in'], 'v_dn_conv_w': out['v_dn_conv_w'], 'v_dn_a_log': out['v_dn_a_log'], 'v_dn_dt_bias': out['v_dn_dt_bias'], 'v_dn_norm_w': out['v_dn_norm_w'], 'v_lru_conv_w': out['v_lru_conv_w'], 'v_lru_conv_b': out['v_lru_conv_b'], 'v_lru_w_r': out['v_lru_w_r'], 'v_lru_b_r': out['v_lru_b_r'], 'v_lru_w_i': out['v_lru_w_i'], 'v_lru_b_i': out['v_lru_b_i'], 'v_lru_lambda': out['v_lru_lambda'], 'v_ssm_log_dt': out['v_ssm_log_dt'], 'v_ssm_a_re': out['v_ssm_a_re'], 'v_ssm_a_im': out['v_ssm_a_im'], 'v_ssm_b_re': out['v_ssm_b_re'], 'v_ssm_b_im': out['v_ssm_b_im'], 'v_ssm_c_re': out['v_ssm_c_re'], 'v_ssm_c_im': out['v_ssm_c_im'], 'v_ssm_d': out['v_ssm_d'], 'v_ssm_w_glu': out['v_ssm_w_glu'], 'v_ssm_b_glu': out['v_ssm_b_glu'], 'v_mem_norm_w': out['v_mem_norm_w'], 'v_w_kv': out['v_w_kv'], 'v_w_gate': out['v_w_gate'], 'v_b_gate': out['v_b_gate'], 'v_w_branch': out['v_w_branch'], 'v_w_out': out['v_w_out'], 'v_final_norm_w': out['v_final_norm_w']}


def _loss(weights, diff, rest, loss_target):
    with _jax.named_scope("forward"):
        args = {**rest, TWIN_DIFF_INPUT: diff, **{k: w.astype(_WEIGHT_DTYPES[k]) for k, w in weights.items()}}
        y = _forward(args)
    with _jax.named_scope("loss_head"):
        err = _jnp.square(y.astype(_jnp.float32) - loss_target)
        return 0.5 * _jnp.sum(_jnp.mean(err, axis=-1)) if err.ndim else 0.5 * err


def _adamw(w, g, m, v):
    m = ADAM_B1 * m + (1.0 - ADAM_B1) * g
    v = ADAM_B2 * v + (1.0 - ADAM_B2) * _jnp.square(g)
    m_hat = m / (1.0 - ADAM_B1 ** ADAM_STEP)
    v_hat = v / (1.0 - ADAM_B2 ** ADAM_STEP)
    delta = -ADAM_LR * (m_hat / (_jnp.sqrt(v_hat) + ADAM_EPS) + ADAM_WD * w)
    return delta, m, v


def reference(x, mem, norm_w, w_in, dn_conv_w, dn_a_log, dn_dt_bias, dn_norm_w, lru_conv_w, lru_conv_b, lru_w_r, lru_b_r, lru_w_i, lru_b_i, lru_lambda, ssm_log_dt, ssm_a_re, ssm_a_im, ssm_b_re, ssm_b_im, ssm_c_re, ssm_c_im, ssm_d, ssm_w_glu, ssm_b_glu, mem_norm_w, w_kv, w_gate, b_gate, w_branch, w_out, final_norm_w, loss_target, m_norm_w, m_w_in, m_dn_conv_w, m_dn_a_log, m_dn_dt_bias, m_dn_norm_w, m_lru_conv_w, m_lru_conv_b, m_lru_w_r, m_lru_b_r, m_lru_w_i, m_lru_b_i, m_lru_lambda, m_ssm_log_dt, m_ssm_a_re, m_ssm_a_im, m_ssm_b_re, m_ssm_b_im, m_ssm_c_re, m_ssm_c_im, m_ssm_d, m_ssm_w_glu, m_ssm_b_glu, m_mem_norm_w, m_w_kv, m_w_gate, m_b_gate, m_w_branch, m_w_out, m_final_norm_w, v_norm_w, v_w_in, v_dn_conv_w, v_dn_a_log, v_dn_dt_bias, v_dn_norm_w, v_lru_conv_w, v_lru_conv_b, v_lru_w_r, v_lru_b_r, v_lru_w_i, v_lru_b_i, v_lru_lambda, v_ssm_log_dt, v_ssm_a_re, v_ssm_a_im, v_ssm_b_re, v_ssm_b_im, v_ssm_c_re, v_ssm_c_im, v_ssm_d, v_ssm_w_glu, v_ssm_b_glu, v_mem_norm_w, v_w_kv, v_w_gate, v_b_gate, v_w_branch, v_w_out, v_final_norm_w):
    given = dict(x=x, mem=mem, norm_w=norm_w, w_in=w_in, dn_conv_w=dn_conv_w, dn_a_log=dn_a_log, dn_dt_bias=dn_dt_bias, dn_norm_w=dn_norm_w, lru_conv_w=lru_conv_w, lru_conv_b=lru_conv_b, lru_w_r=lru_w_r, lru_b_r=lru_b_r, lru_w_i=lru_w_i, lru_b_i=lru_b_i, lru_lambda=lru_lambda, ssm_log_dt=ssm_log_dt, ssm_a_re=ssm_a_re, ssm_a_im=ssm_a_im, ssm_b_re=ssm_b_re, ssm_b_im=ssm_b_im, ssm_c_re=ssm_c_re, ssm_c_im=ssm_c_im, ssm_d=ssm_d, ssm_w_glu=ssm_w_glu, ssm_b_glu=ssm_b_glu, mem_norm_w=mem_norm_w, w_kv=w_kv, w_gate=w_gate, b_gate=b_gate, w_branch=w_branch, w_out=w_out, final_norm_w=final_norm_w, loss_target=loss_target, m_norm_w=m_norm_w, m_w_in=m_w_in, m_dn_conv_w=m_dn_conv_w, m_dn_a_log=m_dn_a_log, m_dn_dt_bias=m_dn_dt_bias, m_dn_norm_w=m_dn_norm_w, m_lru_conv_w=m_lru_conv_w, m_lru_conv_b=m_lru_conv_b, m_lru_w_r=m_lru_w_r, m_lru_b_r=m_lru_b_r, m_lru_w_i=m_lru_w_i, m_lru_b_i=m_lru_b_i, m_lru_lambda=m_lru_lambda, m_ssm_log_dt=m_ssm_log_dt, m_ssm_a_re=m_ssm_a_re, m_ssm_a_im=m_ssm_a_im, m_ssm_b_re=m_ssm_b_re, m_ssm_b_im=m_ssm_b_im, m_ssm_c_re=m_ssm_c_re, m_ssm_c_im=m_ssm_c_im, m_ssm_d=m_ssm_d, m_ssm_w_glu=m_ssm_w_glu, m_ssm_b_glu=m_ssm_b_glu, m_mem_norm_w=m_mem_norm_w, m_w_kv=m_w_kv, m_w_gate=m_w_gate, m_b_gate=m_b_gate, m_w_branch=m_w_branch, m_w_out=m_w_out, m_final_norm_w=m_final_norm_w, v_norm_w=v_norm_w, v_w_in=v_w_in, v_dn_conv_w=v_dn_conv_w, v_dn_a_log=v_dn_a_log, v_dn_dt_bias=v_dn_dt_bias, v_dn_norm_w=v_dn_norm_w, v_lru_conv_w=v_lru_conv_w, v_lru_conv_b=v_lru_conv_b, v_lru_w_r=v_lru_w_r, v_lru_b_r=v_lru_b_r, v_lru_w_i=v_lru_w_i, v_lru_b_i=v_lru_b_i, v_lru_lambda=v_lru_lambda, v_ssm_log_dt=v_ssm_log_dt, v_ssm_a_re=v_ssm_a_re, v_ssm_a_im=v_ssm_a_im, v_ssm_b_re=v_ssm_b_re, v_ssm_b_im=v_ssm_b_im, v_ssm_c_re=v_ssm_c_re, v_ssm_c_im=v_ssm_c_im, v_ssm_d=v_ssm_d, v_ssm_w_glu=v_ssm_w_glu, v_ssm_b_glu=v_ssm_b_glu, v_mem_norm_w=v_mem_norm_w, v_w_kv=v_w_kv, v_w_gate=v_w_gate, v_b_gate=v_b_gate, v_w_branch=v_w_branch, v_w_out=v_w_out, v_final_norm_w=v_final_norm_w)
    weights = {n: given[n] for n in TWIN_WEIGHTS}
    shared = {n: given[n] for n in SHARED_INPUTS}
    per_example = {n: given[n] for n in ['x', 'mem']}
    grad_fn = _jax.value_and_grad(_loss, argnums=(0, 1))

    def one_microbatch(ex, loss_target):
        ex = dict(ex)
        diff = ex.pop(TWIN_DIFF_INPUT)
        return grad_fn(weights, diff, {**shared, **ex}, loss_target)

    if N_MICROBATCH == 1:
        loss, (grad_w, grad_x) = one_microbatch(per_example, given["loss_target"])
    else:
        def body(carry, xs):
            loss_sum, grad_sum = carry
            l_k, (gw_k, gx_k) = one_microbatch(xs[0], xs[1])
            with _jax.named_scope("update"):
                return (loss_sum + l_k, _jax.tree.map(_jnp.add, grad_sum, gw_k)), gx_k

        init = (_jnp.zeros((), _jnp.float32), _jax.tree.map(_jnp.zeros_like, weights))
        (loss, grad_w), grad_x = _jax.lax.scan(body, init, (per_example, given["loss_target"]))
    with _jax.named_scope("update"):
        delta_w, new_m, new_v = {}, {}, {}
        for n in TWIN_WEIGHTS:
            delta_w[n], new_m[n], new_v[n] = _adamw(weights[n], grad_w[n], given["m_" + n], given["v_" + n])
    return (loss, grad_x, *[grad_w[n] for n in TWIN_WEIGHTS], *[delta_w[n] for n in TWIN_WEIGHTS],
            *[new_m[n] for n in TWIN_WEIGHTS], *[new_v[n] for n in TWIN_WEIGHTS])
```

```python
import functools
import math

import jax
import jax.numpy as jnp
import numpy as np
from jax import lax
from jax.experimental import pallas as pl
from jax.experimental.pallas import tpu as pltpu

F32 = jnp.float32
BF16 = jnp.bfloat16
HIGHEST = lax.Precision.HIGHEST
MESH_ID = pl.DeviceIdType.MESH

NORM_EPS = 1e-6
CONV_WIDTH = 4
DN_HEAD_DIM = 128
DN_CHUNK = 64
LRU_C = 8.0
MEM_HEADS = 4
N_BRANCH = 4
ADAM_LR, ADAM_B1, ADAM_B2, ADAM_EPS, ADAM_WD, ADAM_STEP = 0.001, 0.9, 0.999, 1e-08, 0.01, 10

LANES = 128
SUBLANES = 8
VMEM_LIMIT = 56 * 2 ** 20
PACK_COLS = 1024
PACK_ROW_ALIGN = 64
BA_PAD = 256


def _tile(n, pref, align=LANES):
    if n <= pref:
        return n
    t = pref - pref % align
    while t > 0:
        if n % t == 0:
            return t
        t -= align
    return n


def _pcall(body, *, name, out_shape, grid=(), in_specs=None, out_specs=None, scratch=(), sem=None,
           num_prefetch=0, **kw):
    params = dict(vmem_limit_bytes=VMEM_LIMIT)
    if sem is not None:
        params["dimension_semantics"] = sem
    params.update(kw.pop("params", {}))
    if num_prefetch:
        return pl.pallas_call(
            body, name=name, out_shape=out_shape,
            grid_spec=pltpu.PrefetchScalarGridSpec(num_scalar_prefetch=num_prefetch, grid=grid, in_specs=in_specs,
                                                   out_specs=out_specs, scratch_shapes=list(scratch)),
            compiler_params=pltpu.CompilerParams(**params), interpret=False)
    return pl.pallas_call(body, name=name, out_shape=out_shape, grid=grid, in_specs=in_specs, out_specs=out_specs,
                          scratch_shapes=list(scratch), compiler_params=pltpu.CompilerParams(**params),
                          interpret=False)


@functools.partial(jax.custom_vjp, nondiff_argnums=(2, 3))
def _bdot(a, b, ca, cb):
    return lax.dot_general(a.astype(BF16), b.astype(BF16), (((ca,), (cb,)), ((), ())), preferred_element_type=F32)


def _bdot_fwd(a, b, ca, cb):
    return _bdot(a, b, ca, cb), (a, b)


def _bdot_bwd(ca, cb, res, ct):
    a, b = res
    da = _bdot(ct, b, 1, 1 - cb) if ca == 1 else _bdot(b, ct, 1 - cb, 1)
    db = _bdot(a, ct, 1 - ca, 0) if cb == 0 else _bdot(ct, a, 0, 1 - ca)
    return da.astype(a.dtype), db.astype(b.dtype)


_bdot.defvjp(_bdot_fwd, _bdot_bwd)


def _hdot(a, b):
    return jnp.dot(a, b, precision=HIGHEST, preferred_element_type=F32)


def _sigmoid(x):
    return 1.0 / (1.0 + jnp.exp(-x))


def _silu(x):
    return x * _sigmoid(x)


def _softplus(x):
    return jnp.maximum(x, 0.0) + jnp.log(1.0 + jnp.exp(-jnp.abs(x)))


def _expm1(x):
    small = x * (1.0 + x * (0.5 + x * (1.0 / 6.0 + x * (1.0 / 24.0 + x * (1.0 / 120.0 + x * (1.0 / 720.0))))))
    return jnp.where(jnp.abs(x) < 0.1, small, jnp.exp(x) - 1.0)


def _gelu(x):
    return 0.5 * x * (1.0 + jnp.tanh(math.sqrt(2.0 / math.pi) * (x + 0.044715 * x * x * x)))


def _rms(x, w):
    var = jnp.mean(x * x, axis=-1, keepdims=True)
    return x * lax.rsqrt(var + NORM_EPS) * w


def _pick_lane(v, idx):
    lane = lax.broadcasted_iota(jnp.int32, v.shape, 1)
    return jnp.sum(jnp.where(lane == idx, v, 0.0), axis=1, keepdims=True)


def _pick_row(v, idx):
    row = lax.broadcasted_iota(jnp.int32, v.shape, 0)
    return jnp.sum(jnp.where(row == idx, v, 0.0), axis=0, keepdims=True)


def _mm(a, b, *, name, ta=False, tb=False, out_dtype=F32, add=None, bias=None, la=None, lb=None,
        tm=1024, tn=1024, tk=512):
    a2 = a.shape[-2:]
    b2 = b.shape[-2:]
    m, k = (a2[1], a2[0]) if ta else a2
    n = b2[0] if tb else b2[1]
    assert (b2[1] if tb else b2[0]) == k
    tm, tn, tk = _tile(m, tm), _tile(n, tn), _tile(k, tk)
    nk = k // tk

    def a_map(i, j, kk):
        idx = (kk, i) if ta else (i, kk)
        return idx if la is None else (la,) + idx

    def b_map(i, j, kk):
        idx = (j, kk) if tb else (kk, j)
        return idx if lb is None else (lb,) + idx

    a_blk = (tk, tm) if ta else (tm, tk)
    b_blk = (tn, tk) if tb else (tk, tn)
    in_specs = [pl.BlockSpec(a_blk if la is None else (None,) + a_blk, a_map),
                pl.BlockSpec(b_blk if lb is None else (None,) + b_blk, b_map)]
    operands = [a, b]
    if add is not None:
        in_specs.append(pl.BlockSpec((tm, tn), lambda i, j, kk: (i, j)))
        operands.append(add)
    if bias is not None:
        in_specs.append(pl.BlockSpec((1, tn), lambda i, j, kk: (0, j)))
        operands.append(bias)
    dims = (((0 if ta else 1,), (1 if tb else 0,)), ((), ()))

    def body(*refs):
        a_ref, b_ref = refs[0], refs[1]
        rest = list(refs[2:])
        add_ref = rest.pop(0) if add is not None else None
        bias_ref = rest.pop(0) if bias is not None else None
        o_ref, acc_ref = rest
        kk = pl.program_id(2)

        @pl.when(kk == 0)
        def _():
            acc_ref[...] = jnp.zeros_like(acc_ref)

        acc_ref[...] += lax.dot_general(a_ref[...].astype(BF16), b_ref[...].astype(BF16), dims,
                                        preferred_element_type=F32)

        @pl.when(kk == nk - 1)
        def _():
            r = acc_ref[...]
            if add_ref is not None:
                r = r + add_ref[...].astype(F32)
            if bias_ref is not None:
                r = r + bias_ref[...]
            o_ref[...] = r.astype(out_dtype)

    return _pcall(body, name=name, out_shape=jax.ShapeDtypeStruct((m, n), out_dtype), grid=(m // tm, n // tn, nk),
                  in_specs=in_specs, out_specs=pl.BlockSpec((tm, tn), lambda i, j, kk: (i, j)),
                  scratch=[pltpu.VMEM((tm, tn), F32)], sem=("parallel", "parallel", "arbitrary"))(*operands)


def _rms_fwd(x, w, *, name):
    s, d = x.shape
    t = _tile(s, 256, SUBLANES)

    def body(x_ref, w_ref, o_ref):
        o_ref[...] = _rms(x_ref[...], w_ref[...]).astype(BF16)

    return _pcall(body, name=name, out_shape=jax.ShapeDtypeStruct((s, d), BF16), grid=(s // t,),
                  in_specs=[pl.BlockSpec((t, d), lambda i: (i, 0)), pl.BlockSpec((1, d), lambda i: (0, 0))],
                  out_specs=pl.BlockSpec((t, d), lambda i: (i, 0)), sem=("parallel",))(x, w.reshape(1, d))


def _rms_bwd(x, w, dh, res, *, name):
    s, d = x.shape
    t = _tile(s, 256, SUBLANES)

    def body(*refs):
        if res is None:
            x_ref, w_ref, dh_ref, dx_ref, dw_ref = refs
            res_ref = None
        else:
            x_ref, w_ref, dh_ref, res_ref, dx_ref, dw_ref = refs
        _, vjp = jax.vjp(_rms, x_ref[...], w_ref[...])
        dx, dw = vjp(dh_ref[...].astype(F32))
        if res_ref is not None:
            dx = dx + res_ref[...]
        dx_ref[...] = dx

        @pl.when(pl.program_id(0) == 0)
        def _():
            dw_ref[...] = jnp.zeros_like(dw_ref)

        dw_ref[...] += dw

    tok = pl.BlockSpec((t, d), lambda i: (i, 0))
    row = pl.BlockSpec((1, d), lambda i: (0, 0))
    operands = [x, w.reshape(1, d), dh] + ([] if res is None else [res])
    return _pcall(body, name=name,
                  out_shape=(jax.ShapeDtypeStruct((s, d), F32), jax.ShapeDtypeStruct((1, d), F32)), grid=(s // t,),
                  in_specs=[tok, row, tok] + ([] if res is None else [tok]), out_specs=(tok, row),
                  sem=("arbitrary",))(*operands)


def _loss_head(x, w, target, *, name):
    s, d = x.shape
    t = _tile(s, 256, SUBLANES)

    def body(x_ref, w_ref, t_ref, loss_ref, dx_ref, dw_ref):
        def f(xv, wv):
            err = _rms(xv, wv) - t_ref[...]
            return 0.5 * jnp.sum(jnp.mean(err * err, axis=-1))

        val, vjp = jax.vjp(f, x_ref[...], w_ref[...])
        dx, dw = vjp(jnp.ones((), F32))
        dx_ref[...] = dx

        @pl.when(pl.program_id(0) == 0)
        def _():
            dw_ref[...] = jnp.zeros_like(dw_ref)
            loss_ref[...] = jnp.zeros_like(loss_ref)

        dw_ref[...] += dw
        loss_ref[...] += jnp.full(loss_ref.shape, val, F32)

    tok = pl.BlockSpec((t, d), lambda i: (i, 0))
    row = pl.BlockSpec((1, d), lambda i: (0, 0))
    return _pcall(body, name=name,
                  out_shape=(jax.ShapeDtypeStruct((1, LANES), F32), jax.ShapeDtypeStruct((s, d), F32),
                             jax.ShapeDtypeStruct((1, d), F32)),
                  grid=(s // t,), in_specs=[tok, row, tok],
                  out_specs=(pl.BlockSpec((1, LANES), lambda i: (0, 0)), tok, row), sem=("arbitrary",))(
                      x, w.reshape(1, d), target)


def _conv_shifts(prev8, cur, t):
    xp = jnp.concatenate([prev8, cur], axis=0)
    out = []
    for j in range(CONV_WIDTH):
        k = CONV_WIDTH - 1 - j
        out.append(cur if k == 0 else pltpu.roll(xp, k, 0)[SUBLANES:SUBLANES + t])
    return out


def _conv_fwd(p, col_blk, width, w8, b, *, name):
    s = p.shape[0]
    t = _tile(s, 256, SUBLANES)
    r8 = t // SUBLANES

    def body(cur_ref, prev_ref, w_ref, b_ref, y_ref):
        i = pl.program_id(0)
        prev8 = jnp.where(i == 0, 0.0, prev_ref[...])
        sh = _conv_shifts(prev8, cur_ref[...], t)
        w = w_ref[...]
        y = b_ref[...] + sh[0] * w[0:1]
        for j in range(1, CONV_WIDTH):
            y = y + sh[j] * w[j:j + 1]
        y_ref[...] = y

    return _pcall(body, name=name, out_shape=jax.ShapeDtypeStruct((s, width), F32), grid=(s // t,),
                  in_specs=[pl.BlockSpec((t, width), lambda i: (i, col_blk)),
                            pl.BlockSpec((SUBLANES, width), lambda i: (jnp.maximum(i * r8 - 1, 0), col_blk)),
                            pl.BlockSpec((SUBLANES, width), lambda i: (0, 0)),
                            pl.BlockSpec((1, width), lambda i: (0, 0))],
                  out_specs=pl.BlockSpec((t, width), lambda i: (i, 0)), sem=("parallel",))(p, p, w8, b)


def _conv_bwd(p, col_blk, width, w8, dy, *, name):
    s = p.shape[0]
    t = _tile(s, 256, SUBLANES)
    r8 = t // SUBLANES
    nt = s // t

    def body(cur_ref, prev_ref, w_ref, dy_ref, dyn_ref, dx_ref, dw_ref, db_ref):
        i = pl.program_id(0)
        prev8 = jnp.where(i == 0, 0.0, prev_ref[...])
        sh = _conv_shifts(prev8, cur_ref[...], t)
        dy = dy_ref[...]
        next8 = jnp.where(i == nt - 1, 0.0, dyn_ref[...])
        dyp = jnp.concatenate([dy, next8], axis=0)
        w = w_ref[...]
        rows = lax.broadcasted_iota(jnp.int32, (SUBLANES, width), 0)
        dx = dy * w[CONV_WIDTH - 1:CONV_WIDTH]
        dw = jnp.zeros((SUBLANES, width), F32)
        for j in range(CONV_WIDTH):
            k = CONV_WIDTH - 1 - j
            if k:
                dx = dx + pltpu.roll(dyp, t + SUBLANES - k, 0)[0:t] * w[j:j + 1]
            dw = dw + jnp.where(rows == j, jnp.sum(dy * sh[j], axis=0, keepdims=True), 0.0)
        dx_ref[...] = dx

        @pl.when(i == 0)
        def _():
            dw_ref[...] = jnp.zeros_like(dw_ref)
            db_ref[...] = jnp.zeros_like(db_ref)

        dw_ref[...] += dw
        db_ref[...] += jnp.sum(dy, axis=0, keepdims=True)

    return _pcall(body, name=name,
                  out_shape=(jax.ShapeDtypeStruct((s, width), F32), jax.ShapeDtypeStruct((SUBLANES, width), F32),
                             jax.ShapeDtypeStruct((1, width), F32)),
                  grid=(nt,),
                  in_specs=[pl.BlockSpec((t, width), lambda i: (i, col_blk)),
                            pl.BlockSpec((SUBLANES, width), lambda i: (jnp.maximum(i * r8 - 1, 0), col_blk)),
                            pl.BlockSpec((SUBLANES, width), lambda i: (0, 0)),
                            pl.BlockSpec((t, width), lambda i: (i, 0)),
                            pl.BlockSpec((SUBLANES, width), lambda i: (jnp.minimum((i + 1) * r8, s // SUBLANES - 1), 0))],
                  out_specs=(pl.BlockSpec((t, width), lambda i: (i, 0)),
                             pl.BlockSpec((SUBLANES, width), lambda i: (0, 0)),
                             pl.BlockSpec((1, width), lambda i: (0, 0))),
                  sem=("arbitrary",))(p, p, w8, dy, dy)


def _dn_chunk(state, c, z, ba, alog_row, dt_row, nw_row, *, heads, bw):
    cs = c.shape[0]
    hd = DN_HEAD_DIM
    qkv = _silu(c)
    gfull = -jnp.exp(alog_row) * _softplus(ba + dt_row)
    beta_full = _sigmoid(ba)
    ri = lax.broadcasted_iota(jnp.int32, (cs, cs), 0)
    ci = lax.broadcasted_iota(jnp.int32, (cs, cs), 1)
    causal = ri >= ci
    strict = ri > ci
    tril = causal.astype(F32)
    eye = (ri == ci).astype(F32)
    gc = _hdot(tril, gfull)
    gct = lax.dot_general(gfull, tril, (((0,), (1,)), ((), ())), precision=HIGHEST,
                          preferred_element_type=F32)
    outs, states = [], []
    for h in range(heads):
        q = qkv[:, h * hd:(h + 1) * hd]
        k = qkv[:, bw + h * hd:bw + (h + 1) * hd]
        v = qkv[:, 2 * bw + h * hd:2 * bw + (h + 1) * hd]
        q = q * lax.rsqrt(jnp.sum(q * q, axis=-1, keepdims=True) + NORM_EPS) * (hd ** -0.5)
        k = k * lax.rsqrt(jnp.sum(k * k, axis=-1, keepdims=True) + NORM_EPS)
        beta = _pick_lane(beta_full, h)
        g_col = _pick_lane(gc, heads + h)
        g_row = _pick_row(gct, heads + h)
        decay = jnp.exp(jnp.where(causal, g_col - g_row, -1e30))
        k_beta = k * beta
        v_beta = v * beta
        kk = _bdot(k_beta, k, 1, 1) * decay
        m = -jnp.where(strict, kk, 0.0)
        tinv = eye + m
        pw = m
        for _ in range(int(math.log2(cs)) - 1):
            pw = _hdot(pw, pw)
            tinv = tinv + _hdot(tinv, pw)
        rhs = jnp.concatenate([v_beta, k_beta * jnp.exp(g_col)], axis=-1)
        sol = _hdot(tinv, rhs)
        u, w = sol[:, :hd], sol[:, hd:]
        qk = jnp.where(causal, _bdot(q, k, 1, 1) * decay, 0.0)
        g_last = _pick_row(g_col, cs - 1)
        k_dec = k * jnp.exp(g_last - g_col)
        q_dec = q * jnp.exp(g_col)
        s_h = state[h]
        v_new = u - _bdot(w, s_h, 1, 0)
        o = _bdot(q_dec, s_h, 1, 0) + _bdot(qk, v_new, 1, 0)
        states.append(s_h * jnp.exp(g_last) + _bdot(k_dec, v_new, 0, 0))
        outs.append(_rms(o, nw_row) * _silu(z[:, h * hd:(h + 1) * hd]))
    return jnp.concatenate(outs, axis=-1), jnp.stack(states, axis=0)


def _dn_rows(a_log, dt_bias, heads):
    z = jnp.zeros((heads,), F32)
    pad = jnp.zeros((BA_PAD - 2 * heads,), F32)
    return (jnp.concatenate([z, a_log, pad]).reshape(1, BA_PAD), jnp.concatenate([z, dt_bias, pad]).reshape(1, BA_PAD))


def _dn_fwd(p, conv, a_log, dt_bias, norm_w, *, bw, ba_blk, name):
    s = p.shape[0]
    heads = bw // DN_HEAD_DIM
    cs = min(DN_CHUNK, s)
    n = s // cs
    hd = DN_HEAD_DIM
    alog_row, dt_row = _dn_rows(a_log, dt_bias, heads)
    fn = functools.partial(_dn_chunk, heads=heads, bw=bw)

    def body(c_ref, z_ref, ba_ref, al_ref, dt_ref, nw_ref, o_ref, save_ref, st_ref):
        @pl.when(pl.program_id(0) == 0)
        def _():
            st_ref[...] = jnp.zeros_like(st_ref)

        st = st_ref[...]
        save_ref[...] = st
        o, new = fn(st, c_ref[...], z_ref[...], ba_ref[...], al_ref[...], dt_ref[...], nw_ref[...])
        o_ref[...] = o.astype(BF16)
        st_ref[...] = new

    row = lambda wd: pl.BlockSpec((1, wd), lambda i: (0, 0))
    return _pcall(body, name=name,
                  out_shape=(jax.ShapeDtypeStruct((s, bw), BF16), jax.ShapeDtypeStruct((n, heads, hd, hd), F32)),
                  grid=(n,),
                  in_specs=[pl.BlockSpec((cs, 3 * bw), lambda i: (i, 0)), pl.BlockSpec((cs, bw), lambda i: (i, 3)),
                            pl.BlockSpec((cs, BA_PAD), lambda i: (i, ba_blk)), row(BA_PAD), row(BA_PAD), row(hd)],
                  out_specs=(pl.BlockSpec((cs, bw), lambda i: (i, 0)),
                             pl.BlockSpec((None, heads, hd, hd), lambda i: (i, 0, 0, 0))),
                  scratch=[pltpu.VMEM((heads, hd, hd), F32)], sem=("arbitrary",))(
                      conv, p, p, alog_row, dt_row, norm_w.reshape(1, hd))


def _dn_bwd(p, conv, states, d_o, a_log, dt_bias, norm_w, *, bw, ba_blk, name):
    s = p.shape[0]
    heads = bw // DN_HEAD_DIM
    cs = min(DN_CHUNK, s)
    n = s // cs
    hd = DN_HEAD_DIM
    alog_row, dt_row = _dn_rows(a_log, dt_bias, heads)
    fn = functools.partial(_dn_chunk, heads=heads, bw=bw)

    def body(c_ref, z_ref, ba_ref, st_ref, do_ref, al_ref, dt_ref, nw_ref,
             dc_ref, dz_ref, dba_ref, dal_ref, ddt_ref, dnw_ref, dst_ref):
        @pl.when(pl.program_id(0) == 0)
        def _():
            dst_ref[...] = jnp.zeros_like(dst_ref)
            dal_ref[...] = jnp.zeros_like(dal_ref)
            ddt_ref[...] = jnp.zeros_like(ddt_ref)
            dnw_ref[...] = jnp.zeros_like(dnw_ref)

        _, vjp = jax.vjp(fn, st_ref[...], c_ref[...], z_ref[...], ba_ref[...], al_ref[...], dt_ref[...], nw_ref[...])
        dst, dc, dz, dba, dal, ddt, dnw = vjp((do_ref[...].astype(F32), dst_ref[...]))
        dst_ref[...] = dst
        dc_ref[...] = dc
        dz_ref[...] = dz
        dba_ref[...] = dba
        dal_ref[...] += dal
        ddt_ref[...] += ddt
        dnw_ref[...] += dnw

    rev = lambda i: n - 1 - i
    row = lambda wd: pl.BlockSpec((1, wd), lambda i: (0, 0))
    return _pcall(body, name=name,
                  out_shape=(jax.ShapeDtypeStruct((s, 3 * bw), F32), jax.ShapeDtypeStruct((s, bw), F32),
                             jax.ShapeDtypeStruct((s, BA_PAD), F32), jax.ShapeDtypeStruct((1, BA_PAD), F32),
                             jax.ShapeDtypeStruct((1, BA_PAD), F32), jax.ShapeDtypeStruct((1, hd), F32)),
                  grid=(n,),
                  in_specs=[pl.BlockSpec((cs, 3 * bw), lambda i: (rev(i), 0)),
                            pl.BlockSpec((cs, bw), lambda i: (rev(i), 3)),
                            pl.BlockSpec((cs, BA_PAD), lambda i: (rev(i), ba_blk)),
                            pl.BlockSpec((None, heads, hd, hd), lambda i: (rev(i), 0, 0, 0)),
                            pl.BlockSpec((cs, bw), lambda i: (rev(i), 0)), row(BA_PAD), row(BA_PAD), row(hd)],
                  out_specs=(pl.BlockSpec((cs, 3 * bw), lambda i: (rev(i), 0)),
                             pl.BlockSpec((cs, bw), lambda i: (rev(i), 0)),
                             pl.BlockSpec((cs, BA_PAD), lambda i: (rev(i), 0)), row(BA_PAD), row(BA_PAD), row(hd)),
                  scratch=[pltpu.VMEM((heads, hd, hd), F32)], sem=("arbitrary",))(
                      conv, p, p, states, d_o, alog_row, dt_row, norm_w.reshape(1, hd))


def _lru_gates(xc, wr, br, wi, bi, lam):
    r = _sigmoid(_bdot(xc, wr, 1, 0) + br)
    i = _sigmoid(_bdot(xc, wi, 1, 0) + bi)
    log_a = -LRU_C * r * _softplus(-lam)
    return jnp.exp(log_a), jnp.sqrt(-_expm1(2.0 * log_a)) * (i * xc)


def _scan_rows(t, step, carry):
    def trip(g, cr):
        base = pl.multiple_of(g * SUBLANES, SUBLANES)
        for r in range(SUBLANES):
            cr = step(base + r, cr)
        return cr
    return lax.fori_loop(0, t // SUBLANES, trip, carry)


def _scan_rows_rev(t, step, carry):
    def trip(g, cr):
        base = pl.multiple_of((t // SUBLANES - 1 - g) * SUBLANES, SUBLANES)
        for r in range(SUBLANES - 1, -1, -1):
            cr = step(base + r, cr)
        return cr
    return lax.fori_loop(0, t // SUBLANES, trip, carry)


def _lru_fwd(p, xc, wr, br, wi, bi, lam, *, bw, z_blk, name):
    s = p.shape[0]
    t = _tile(s, 256, SUBLANES)
    nt = s // t

    def body(xc_ref, z_ref, wr_ref, br_ref, wi_ref, bi_ref, lam_ref, o_ref, save_ref, a_s, b_s, h_s, carry_s):
        @pl.when(pl.program_id(0) == 0)
        def _():
            carry_s[...] = jnp.zeros_like(carry_s)

        a, inp = _lru_gates(xc_ref[...], wr_ref[...], br_ref[...], wi_ref[...], bi_ref[...], lam_ref[...])
        a_s[...] = a
        b_s[...] = inp
        h0 = carry_s[...]
        save_ref[...] = h0

        def step(r, h):
            h = a_s[pl.ds(r, 1), :] * h + b_s[pl.ds(r, 1), :]
            h_s[pl.ds(r, 1), :] = h
            return h

        carry_s[...] = _scan_rows(t, step, h0)
        o_ref[...] = (h_s[...] * _silu(z_ref[...])).astype(BF16)

    tok = pl.BlockSpec((t, bw), lambda i: (i, 0))
    row = pl.BlockSpec((1, bw), lambda i: (0, 0))
    mat = pl.BlockSpec((bw, bw), lambda i: (0, 0))
    return _pcall(body, name=name,
                  out_shape=(jax.ShapeDtypeStruct((s, bw), BF16), jax.ShapeDtypeStruct((nt, 1, bw), F32)), grid=(nt,),
                  in_specs=[tok, pl.BlockSpec((t, bw), lambda i: (i, z_blk)), mat, row, mat, row, row],
                  out_specs=(tok, pl.BlockSpec((None, 1, bw), lambda i: (i, 0, 0))),
                  scratch=[pltpu.VMEM((t, bw), F32)] * 3 + [pltpu.VMEM((1, bw), F32)], sem=("arbitrary",))(
                      xc, p, wr, br, wi, bi, lam)


def _lru_bwd(p, xc, saves, d_o, wr, br, wi, bi, lam, *, bw, z_blk, name):
    s = p.shape[0]
    t = _tile(s, 256, SUBLANES)
    nt = s // t

    def body(xc_ref, z_ref, sv_ref, do_ref, wr_ref, br_ref, wi_ref, bi_ref, lam_ref,
             dxc_ref, dz_ref, dwr_ref, dwi_ref, dbr_ref, dbi_ref, dlam_ref, a_s, b_s, h_s, g_s, carry_s):
        @pl.when(pl.program_id(0) == 0)
        def _():
            carry_s[...] = jnp.zeros_like(carry_s)
            for r in (dwr_ref, dwi_ref, dbr_ref, dbi_ref, dlam_ref):
                r[...] = jnp.zeros_like(r)

        (a, inp), vjp_g = jax.vjp(_lru_gates, xc_ref[...], wr_ref[...], br_ref[...], wi_ref[...], bi_ref[...],
                                  lam_ref[...])
        a_s[...] = a
        b_s[...] = inp
        h0 = sv_ref[...]

        def fstep(r, h):
            h_s[pl.ds(r, 1), :] = h
            return a_s[pl.ds(r, 1), :] * h + b_s[pl.ds(r, 1), :]

        _scan_rows(t, fstep, h0)
        a = a_s[...]
        hs = a * h_s[...] + b_s[...]
        z = z_ref[...]
        d_o = do_ref[...].astype(F32)
        _, vjp_o = jax.vjp(lambda hv, zv: hv * _silu(zv), hs, z)
        dhs, dz = vjp_o(d_o)
        dz_ref[...] = dz
        g_s[...] = dhs

        def bstep(r, cr):
            g = g_s[pl.ds(r, 1), :] + cr
            g_s[pl.ds(r, 1), :] = g
            return a_s[pl.ds(r, 1), :] * g

        carry_s[...] = _scan_rows_rev(t, bstep, carry_s[...])
        g = g_s[...]
        dxc, dwr, dbr, dwi, dbi, dlam = vjp_g((g * h_s[...], g))
        dxc_ref[...] = dxc
        dwr_ref[...] += dwr
        dwi_ref[...] += dwi
        dbr_ref[...] += dbr
        dbi_ref[...] += dbi
        dlam_ref[...] += dlam

    rev = lambda i: nt - 1 - i
    tok = pl.BlockSpec((t, bw), lambda i: (rev(i), 0))
    row = pl.BlockSpec((1, bw), lambda i: (0, 0))
    mat = pl.BlockSpec((bw, bw), lambda i: (0, 0))
    sd = jax.ShapeDtypeStruct
    return _pcall(body, name=name,
                  out_shape=(sd((s, bw), F32), sd((s, bw), F32), sd((bw, bw), F32), sd((bw, bw), F32),
                             sd((1, bw), F32), sd((1, bw), F32), sd((1, bw), F32)),
                  grid=(nt,),
                  in_specs=[tok, pl.BlockSpec((t, bw), lambda i: (rev(i), z_blk)),
                            pl.BlockSpec((None, 1, bw), lambda i: (rev(i), 0, 0)), tok, mat, row, mat, row, row],
                  out_specs=(tok, tok, mat, mat, row, row, row),
                  scratch=[pltpu.VMEM((t, bw), F32)] * 4 + [pltpu.VMEM((1, bw), F32)], sem=("arbitrary",))(
                      xc, p, saves, d_o, wr, br, wi, bi, lam)


def _s5_prep(log_dt, a_re, a_im, b_re, b_im, c_re, c_im, d_skip):
    g, n = a_re.shape
    gs = d_skip.shape[1]
    gpb = LANES // gs
    nb = g // gpb
    dt = jnp.exp(log_dt)[:, None]
    mag = jnp.exp(dt * a_re)
    ab_re = mag * jnp.cos(dt * a_im)
    ab_im = mag * jnp.sin(dt * a_im)
    den = a_re * a_re + a_im * a_im
    f_re = ((ab_re - 1.0) * a_re + ab_im * a_im) / den
    f_im = (ab_im * a_re - (ab_re - 1.0) * a_im) / den
    bb_re = f_re[..., None] * b_re - f_im[..., None] * b_im
    bb_im = f_re[..., None] * b_im + f_im[..., None] * b_re
    eye = jnp.eye(gpb, dtype=F32)

    def b_dense(bb):
        t = bb.reshape(nb, gpb, n, gs)
        return jnp.einsum("bgnc,gh->bgchn", t, eye).reshape(nb, gpb * gs, gpb * n)

    def c_dense(cc):
        t = cc.reshape(nb, gpb, gs, n)
        return jnp.einsum("bgcn,gh->bgnhc", t, eye).reshape(nb, gpb * n, gpb * gs)

    lanes = gpb * n
    sub = lanes // LANES
    return (ab_re.reshape(nb, sub, LANES), ab_im.reshape(nb, sub, LANES), b_dense(bb_re), b_dense(bb_im),
            c_dense(c_re), c_dense(c_im), d_skip.reshape(1, g * gs))


def _s5_out(xre, xim, cre, cim, d, u):
    return _gelu(_bdot(xre, cre, 1, 0) - _bdot(xim, cim, 1, 0) + d * u)


def _s5_fwd(p, prep, *, bw, u_blk0, name):
    s = p.shape[0]
    are, aim, bre, bim, cre, cim, d = prep
    nb, sub, _ = are.shape
    lanes = sub * LANES
    t = _tile(s, 256, SUBLANES)
    nt = s // t

    def body(u_ref, are_ref, aim_ref, bre_ref, bim_ref, cre_ref, cim_ref, d_ref, y_ref, save_ref,
             bre_s, bim_s, xre_s, xim_s, carry_s):
        @pl.when(pl.program_id(1) == 0)
        def _():
            carry_s[...] = jnp.zeros_like(carry_s)

        u = u_ref[...]
        bre_s[...] = _bdot(u, bre_ref[...], 1, 0).reshape(t, sub, LANES)
        bim_s[...] = _bdot(u, bim_ref[...], 1, 0).reshape(t, sub, LANES)
        ar, ai = are_ref[...], aim_ref[...]
        save_ref[...] = carry_s[...]

        def step(r, cr):
            xr, xi = cr
            nr = ar * xr - ai * xi + bre_s[r]
            ni = ar * xi + ai * xr + bim_s[r]
            xre_s[r] = nr
            xim_s[r] = ni
            return nr, ni

        xr, xi = lax.fori_loop(0, t, step, (carry_s[0], carry_s[1]))
        carry_s[0] = xr
        carry_s[1] = xi
        y_ref[...] = _s5_out(xre_s[...].reshape(t, lanes), xim_s[...].reshape(t, lanes), cre_ref[...], cim_ref[...],
                             d_ref[...], u).astype(BF16)

    vec = pl.BlockSpec((None, sub, LANES), lambda j, i: (j, 0, 0))
    bmat = pl.BlockSpec((None, LANES, lanes), lambda j, i: (j, 0, 0))
    cmat = pl.BlockSpec((None, lanes, LANES), lambda j, i: (j, 0, 0))
    return _pcall(body, name=name,
                  out_shape=(jax.ShapeDtypeStruct((s, bw), BF16), jax.ShapeDtypeStruct((nb, nt, 2, sub, LANES), F32)),
                  grid=(nb, nt),
                  in_specs=[pl.BlockSpec((t, LANES), lambda j, i: (i, u_blk0 + j)), vec, vec, bmat, bmat, cmat, cmat,
                            pl.BlockSpec((1, LANES), lambda j, i: (0, j))],
                  out_specs=(pl.BlockSpec((t, LANES), lambda j, i: (i, j)),
                             pl.BlockSpec((None, None, 2, sub, LANES), lambda j, i: (j, i, 0, 0, 0))),
                  scratch=[pltpu.VMEM((t, sub, LANES), F32)] * 4 + [pltpu.VMEM((2, sub, LANES), F32)],
                  sem=("parallel", "arbitrary"))(p, are, aim, bre, bim, cre, cim, d)


def _s5_bwd(p, prep, saves, dyg, *, bw, u_blk0, name):
    s = p.shape[0]
    are, aim, bre, bim, cre, cim, d = prep
    nb, sub, _ = are.shape
    lanes = sub * LANES
    t = _tile(s, 256, SUBLANES)
    nt = s // t

    def body(u_ref, dy_ref, sv_ref, are_ref, aim_ref, bre_ref, bim_ref, cre_ref, cim_ref, d_ref,
             du_ref, dar_ref, dai_ref, dbre_ref, dbim_ref, dcre_ref, dcim_ref, dd_ref,
             bre_s, bim_s, xre_s, xim_s, carry_s):
        @pl.when(pl.program_id(1) == 0)
        def _():
            carry_s[...] = jnp.zeros_like(carry_s)
            for r in (dar_ref, dai_ref, dbre_ref, dbim_ref, dcre_ref, dcim_ref, dd_ref):
                r[...] = jnp.zeros_like(r)

        u = u_ref[...]
        bre_s[...] = _bdot(u, bre_ref[...], 1, 0).reshape(t, sub, LANES)
        bim_s[...] = _bdot(u, bim_ref[...], 1, 0).reshape(t, sub, LANES)
        ar, ai = are_ref[...], aim_ref[...]

        def fstep(r, cr):
            xr, xi = cr
            nr = ar * xr - ai * xi + bre_s[r]
            ni = ar * xi + ai * xr + bim_s[r]
            xre_s[r] = nr
            xim_s[r] = ni
            return nr, ni

        lax.fori_loop(0, t, fstep, (sv_ref[0], sv_ref[1]))
        _, vjp_o = jax.vjp(_s5_out, xre_s[...].reshape(t, lanes), xim_s[...].reshape(t, lanes), cre_ref[...],
                           cim_ref[...], d_ref[...], u)
        dxre, dxim, dcre, dcim, dd, du = vjp_o(dy_ref[...].astype(F32))
        dcre_ref[...] += dcre.astype(F32)
        dcim_ref[...] += dcim.astype(F32)
        dd_ref[...] += dd
        bre_s[...] = dxre.reshape(t, sub, LANES)
        bim_s[...] = dxim.reshape(t, sub, LANES)

        def bstep(k, cr):
            r = t - 1 - k
            gr, gi, dar, dai = cr
            gr = bre_s[r] + gr
            gi = bim_s[r] + gi
            bre_s[r] = gr
            bim_s[r] = gi
            pr = jnp.where(r == 0, sv_ref[0], xre_s[jnp.maximum(r - 1, 0)])
            pi = jnp.where(r == 0, sv_ref[1], xim_s[jnp.maximum(r - 1, 0)])
            dar = dar + gr * pr + gi * pi
            dai = dai + gi * pr - gr * pi
            return ar * gr + ai * gi, ar * gi - ai * gr, dar, dai

        zero = jnp.zeros((sub, LANES), F32)
        gr, gi, dar, dai = lax.fori_loop(0, t, bstep, (carry_s[0], carry_s[1], zero, zero))
        carry_s[0] = gr
        carry_s[1] = gi
        dar_ref[...] += dar
        dai_ref[...] += dai
        dbu_re = bre_s[...].reshape(t, lanes)
        dbu_im = bim_s[...].reshape(t, lanes)
        du_ref[...] = du + _bdot(dbu_re, bre_ref[...], 1, 1) + _bdot(dbu_im, bim_ref[...], 1, 1)
        dbre_ref[...] += _bdot(u, dbu_re, 0, 0)
        dbim_ref[...] += _bdot(u, dbu_im, 0, 0)

    rev = lambda i: nt - 1 - i
    vec = pl.BlockSpec((None, sub, LANES), lambda j, i: (j, 0, 0))
    bmat = pl.BlockSpec((None, LANES, lanes), lambda j, i: (j, 0, 0))
    cmat = pl.BlockSpec((None, lanes, LANES), lambda j, i: (j, 0, 0))
    drow = pl.BlockSpec((1, LANES), lambda j, i: (0, j))
    sd = jax.ShapeDtypeStruct
    return _pcall(body, name=name,
                  out_shape=(sd((s, bw), F32), sd(are.shape, F32), sd(aim.shape, F32), sd(bre.shape, F32),
                             sd(bim.shape, F32), sd(cre.shape, F32), sd(cim.shape, F32), sd((1, bw), F32)),
                  grid=(nb, nt),
                  in_specs=[pl.BlockSpec((t, LANES), lambda j, i: (rev(i), u_blk0 + j)),
                            pl.BlockSpec((t, LANES), lambda j, i: (rev(i), j)),
                            pl.BlockSpec((None, None, 2, sub, LANES), lambda j, i: (j, rev(i), 0, 0, 0)),
                            vec, vec, bmat, bmat, cmat, cmat, drow],
                  out_specs=(pl.BlockSpec((t, LANES), lambda j, i: (rev(i), j)), vec, vec, bmat, bmat, cmat, cmat, drow),
                  scratch=[pltpu.VMEM((t, sub, LANES), F32)] * 4 + [pltpu.VMEM((2, sub, LANES), F32)],
                  sem=("parallel", "arbitrary"))(p, dyg, saves, are, aim, bre, bim, cre, cim, d)


def _glu_gate(pre, z, bw):
    return pre[:, :bw] * _sigmoid(pre[:, bw:]) * _silu(z)


def _glu_fwd(pre, p, *, bw, z_blk, name):
    s = pre.shape[0]
    t = _tile(s, 256, SUBLANES)

    def body(pre_ref, z_ref, o_ref):
        o_ref[...] = _glu_gate(pre_ref[...], z_ref[...], bw).astype(BF16)

    return _pcall(body, name=name, out_shape=jax.ShapeDtypeStruct((s, bw), BF16), grid=(s // t,),
                  in_specs=[pl.BlockSpec((t, 2 * bw), lambda i: (i, 0)), pl.BlockSpec((t, bw), lambda i: (i, z_blk))],
                  out_specs=pl.BlockSpec((t, bw), lambda i: (i, 0)), sem=("parallel",))(pre, p)


def _glu_bwd(pre, p, d_o, *, bw, z_blk, name):
    s = pre.shape[0]
    t = _tile(s, 256, SUBLANES)

    def body(pre_ref, z_ref, do_ref, dpre_ref, dz_ref, db_ref):
        _, vjp = jax.vjp(functools.partial(_glu_gate, bw=bw), pre_ref[...], z_ref[...])
        dpre, dz = vjp(do_ref[...].astype(F32))
        dpre_ref[...] = dpre.astype(BF16)
        dz_ref[...] = dz

        @pl.when(pl.program_id(0) == 0)
        def _():
            db_ref[...] = jnp.zeros_like(db_ref)

        db_ref[...] += jnp.sum(dpre, axis=0, keepdims=True)

    sd = jax.ShapeDtypeStruct
    return _pcall(body, name=name, out_shape=(sd((s, 2 * bw), BF16), sd((s, bw), F32), sd((1, 2 * bw), F32)),
                  grid=(s // t,),
                  in_specs=[pl.BlockSpec((t, 2 * bw), lambda i: (i, 0)), pl.BlockSpec((t, bw), lambda i: (i, z_blk)),
                            pl.BlockSpec((t, bw), lambda i: (i, 0))],
                  out_specs=(pl.BlockSpec((t, 2 * bw), lambda i: (i, 0)), pl.BlockSpec((t, bw), lambda i: (i, 0)),
                             pl.BlockSpec((1, 2 * bw), lambda i: (0, 0))), sem=("arbitrary",))(pre, p, d_o)


def _attn_tile(q, z, kv, *, bw):
    hd = bw // MEM_HEADS
    outs = []
    for h in range(MEM_HEADS):
        k = kv[:, h * hd:(h + 1) * hd]
        v = kv[:, bw + h * hd:bw + (h + 1) * hd]
        sc = _bdot(q[:, h * hd:(h + 1) * hd], k, 1, 1) * (hd ** -0.5)
        e = jnp.exp(sc - lax.stop_gradient(jnp.max(sc, axis=-1, keepdims=True)))
        prob = e / jnp.sum(e, axis=-1, keepdims=True)
        outs.append(_bdot(prob, v, 1, 0))
    return jnp.concatenate(outs, axis=-1) * _silu(z)


def _attn_fwd(p, kv, *, bw, q_blk, z_blk, name):
    s = p.shape[0]
    m = kv.shape[0]
    t = _tile(s, 256, SUBLANES)

    def body(q_ref, z_ref, kv_ref, o_ref):
        o_ref[...] = _attn_tile(q_ref[...], z_ref[...], kv_ref[...], bw=bw).astype(BF16)

    return _pcall(body, name=name, out_shape=jax.ShapeDtypeStruct((s, bw), BF16), grid=(s // t,),
                  in_specs=[pl.BlockSpec((t, bw), lambda i: (i, q_blk)), pl.BlockSpec((t, bw), lambda i: (i, z_blk)),
                            pl.BlockSpec((m, 2 * bw), lambda i: (0, 0))],
                  out_specs=pl.BlockSpec((t, bw), lambda i: (i, 0)), sem=("parallel",))(p, p, kv)


def _attn_bwd(p, kv, d_o, *, bw, q_blk, z_blk, name):
    s = p.shape[0]
    m = kv.shape[0]
    t = _tile(s, 256, SUBLANES)

    def body(q_ref, z_ref, kv_ref, do_ref, dq_ref, dz_ref, dkv_ref):
        _, vjp = jax.vjp(functools.partial(_attn_tile, bw=bw), q_ref[...], z_ref[...], kv_ref[...])
        dq, dz, dkv = vjp(do_ref[...].astype(F32))
        dq_ref[...] = dq
        dz_ref[...] = dz

        @pl.when(pl.program_id(0) == 0)
        def _():
            dkv_ref[...] = jnp.zeros_like(dkv_ref)

        dkv_ref[...] += dkv

    sd = jax.ShapeDtypeStruct
    tok = pl.BlockSpec((t, bw), lambda i: (i, 0))
    return _pcall(body, name=name, out_shape=(sd((s, bw), F32), sd((s, bw), F32), sd((m, 2 * bw), F32)),
                  grid=(s // t,),
                  in_specs=[pl.BlockSpec((t, bw), lambda i: (i, q_blk)), pl.BlockSpec((t, bw), lambda i: (i, z_blk)),
                            pl.BlockSpec((m, 2 * bw), lambda i: (0, 0)), tok],
                  out_specs=(tok, tok, pl.BlockSpec((m, 2 * bw), lambda i: (0, 0))), sem=("arbitrary",))(p, p, kv, d_o)


def _merge_fwd(p, o4, wg, bg, wb, *, rank, g_blk, name):
    s = p.shape[0]
    _, bw, d = wb.shape
    tm, tn = _tile(s, 512), _tile(d, 512)

    def body(g_ref, o_ref, wg_ref, bg_ref, wb_ref, out_ref):
        g = g_ref[...]
        acc = jnp.zeros((tm, tn), F32)
        for n in range(N_BRANCH):
            gate = _sigmoid(_bdot(g, wg_ref[n], 1, 0) + bg_ref[n])
            acc = acc + gate * _bdot(o_ref[n], wb_ref[n], 1, 0)
        out_ref[...] = acc.astype(BF16)

    return _pcall(body, name=name, out_shape=jax.ShapeDtypeStruct((s, d), BF16), grid=(s // tm, d // tn),
                  in_specs=[pl.BlockSpec((tm, rank), lambda i, j: (i, g_blk)),
                            pl.BlockSpec((N_BRANCH, tm, bw), lambda i, j: (0, i, 0)),
                            pl.BlockSpec((N_BRANCH, rank, tn), lambda i, j: (0, 0, j)),
                            pl.BlockSpec((N_BRANCH, 1, tn), lambda i, j: (0, 0, j)),
                            pl.BlockSpec((N_BRANCH, bw, tn), lambda i, j: (0, 0, j))],
                  out_specs=pl.BlockSpec((tm, tn), lambda i, j: (i, j)), sem=("parallel", "parallel"))(
                      p, o4, wg, bg, wb)


def _merge_bwd(p, o4, wg, bg, wb, dmerged, *, rank, g_blk, name):
    s = p.shape[0]
    _, bw, d = wb.shape
    tm, tn = _tile(s, 512), _tile(d, 512)

    def body(g_ref, o_ref, wg_ref, bg_ref, wb_ref, dm_ref, dpre_ref, dbr_ref, dbg_ref):
        g = g_ref[...]
        dm = dm_ref[...].astype(F32)

        @pl.when(pl.program_id(1) == 0)
        def _():
            dbg_ref[...] = jnp.zeros_like(dbg_ref)

        for n in range(N_BRANCH):
            gate = _sigmoid(_bdot(g, wg_ref[n], 1, 0) + bg_ref[n])
            br = _bdot(o_ref[n], wb_ref[n], 1, 0)
            dpre = dm * br * gate * (1.0 - gate)
            dpre_ref[n] = dpre.astype(BF16)
            dbr_ref[n] = (dm * gate).astype(BF16)
            dbg_ref[n] += jnp.sum(dpre, axis=0, keepdims=True)

    sd = jax.ShapeDtypeStruct
    big = pl.BlockSpec((N_BRANCH, tm, tn), lambda j, i: (0, i, j))
    return _pcall(body, name=name,
                  out_shape=(sd((N_BRANCH, s, d), BF16), sd((N_BRANCH, s, d), BF16), sd((N_BRANCH, 1, d), F32)),
                  grid=(d // tn, s // tm),
                  in_specs=[pl.BlockSpec((tm, rank), lambda j, i: (i, g_blk)),
                            pl.BlockSpec((N_BRANCH, tm, bw), lambda j, i: (0, i, 0)),
                            pl.BlockSpec((N_BRANCH, rank, tn), lambda j, i: (0, 0, j)),
                            pl.BlockSpec((N_BRANCH, 1, tn), lambda j, i: (0, 0, j)),
                            pl.BlockSpec((N_BRANCH, bw, tn), lambda j, i: (0, 0, j)),
                            pl.BlockSpec((tm, tn), lambda j, i: (i, j))],
                  out_specs=(big, big, pl.BlockSpec((N_BRANCH, 1, tn), lambda j, i: (0, 0, j))),
                  sem=("parallel", "arbitrary"))(p, o4, wg, bg, wb, dmerged)


def _adamw(w, g, m, v, *, name):
    r, cdim = w.shape
    tr = _tile(r, 128, SUBLANES)
    bc1 = 1.0 - ADAM_B1 ** ADAM_STEP
    bc2 = 1.0 - ADAM_B2 ** ADAM_STEP

    def body(w_ref, g_ref, m_ref, v_ref, d_ref, nm_ref, nv_ref):
        gv = g_ref[...]
        nm = ADAM_B1 * m_ref[...] + (1.0 - ADAM_B1) * gv
        nv = ADAM_B2 * v_ref[...] + (1.0 - ADAM_B2) * (gv * gv)
        d_ref[...] = -ADAM_LR * ((nm / bc1) / (jnp.sqrt(nv / bc2) + ADAM_EPS) + ADAM_WD * w_ref[...])
        nm_ref[...] = nm
        nv_ref[...] = nv

    blk = pl.BlockSpec((tr, cdim), lambda i: (i, 0))
    sd = jax.ShapeDtypeStruct((r, cdim), F32)
    return _pcall(body, name=name, out_shape=(sd, sd, sd), grid=(r // tr,), in_specs=[blk] * 4, out_specs=(blk,) * 3,
                  sem=("parallel",))(w, g, m, v)


HBM_SPEC = pl.BlockSpec(memory_space=pl.ANY)


def _place():
    x, y, c = lax.axis_index("x"), lax.axis_index("y"), lax.axis_index("c")
    return x, y, c, [(1 - x, y), (x, 1 - y), (1 - x, 1 - y)]


def _rcopy(src, dst, send_sem, recv_sem, device):
    return pltpu.make_async_remote_copy(src_ref=src, dst_ref=dst, send_sem=send_sem, recv_sem=recv_sem,
                                        device_id=device, device_id_type=MESH_ID)


def _comm_call(body, *, name, out_shape, n_in, n_sems, local_sem=False):
    scratch = [pltpu.SemaphoreType.DMA((n_sems,)), pltpu.SemaphoreType.DMA((n_sems,))]
    if local_sem:
        scratch.append(pltpu.SemaphoreType.DMA)
    return pl.pallas_call(body, name=name, out_shape=out_shape, in_specs=[HBM_SPEC] * n_in, out_specs=HBM_SPEC,
                          scratch_shapes=scratch, interpret=False)


def _ag4(buf, *, name):
    rows, cdim = buf.shape
    half = rows // 2

    def body(in_ref, out_ref, send_sems, recv_sems, local_sem):
        x, y, c, chips = _place()
        sibling = (x, y, 1 - c)

        def part(px, py, h):
            return out_ref.at[2 * px + py, pl.ds(h * half, half), :]

        mine = pltpu.make_async_copy(in_ref, out_ref.at[2 * x + y], local_sem)
        mine.start()
        first = [_rcopy(in_ref.at[pl.ds(c * half, half), :], part(x, y, c), send_sems.at[j], recv_sems.at[j],
                        (*chip, c)) for j, chip in enumerate(chips)]
        for cp in first:
            cp.start()
        passed = [_rcopy(part(*chip, c), part(*chip, c), send_sems.at[3 + j], recv_sems.at[3 + j], sibling)
                  for j, chip in enumerate(chips)]
        for j, chip in enumerate(chips):
            _rcopy(part(*chip, c), part(*chip, c), send_sems.at[j], recv_sems.at[j], (*chip, c)).wait_recv()
            passed[j].start()
        for j, chip in enumerate(chips):
            _rcopy(part(*chip, 1 - c), part(*chip, 1 - c), send_sems.at[3 + j], recv_sems.at[3 + j],
                   sibling).wait_recv()
        for cp in first + passed:
            cp.wait_send()
        mine.wait()

    return _comm_call(body, name=name, out_shape=jax.ShapeDtypeStruct((4, rows, cdim), buf.dtype), n_in=1, n_sems=6,
                      local_sem=True)(buf)


def _sib_halves(g4, *, name):
    _, rows, cdim = g4.shape
    half = rows // 2

    def body(g_ref, recv_ref, send_sems, recv_sems):
        x, y, c, _ = _place()
        cp = _rcopy(g_ref.at[:, pl.ds((1 - c) * half, half), :], recv_ref, send_sems.at[0], recv_sems.at[0],
                    (x, y, 1 - c))
        cp.start()
        cp.wait()

    return _comm_call(body, name=name, out_shape=jax.ShapeDtypeStruct((4, half, cdim), g4.dtype), n_in=1, n_sems=1)(g4)


def _to_chips(h4, *, name):
    _, n, cdim = h4.shape

    def body(h_ref, recv_ref, send_sems, recv_sems):
        x, y, c, chips = _place()
        cps = [_rcopy(h_ref.at[2 * chip[0] + chip[1]], recv_ref.at[j], send_sems.at[j], recv_sems.at[j], (*chip, c))
               for j, chip in enumerate(chips)]
        for cp in cps:
            cp.start()
        for cp in cps:
            cp.wait()

    return _comm_call(body, name=name, out_shape=jax.ShapeDtypeStruct((3, n, cdim), h4.dtype), n_in=1, n_sems=3)(h4)


def _join_halves(q, *, name):
    n, cdim = q.shape

    def body(q_ref, out_ref, send_sems, recv_sems, local_sem):
        x, y, c, _ = _place()
        mine = pltpu.make_async_copy(q_ref, out_ref.at[pl.ds(c * n, n), :], local_sem)
        mine.start()
        cp = _rcopy(q_ref, out_ref.at[pl.ds(c * n, n), :], send_sems.at[0], recv_sems.at[0], (x, y, 1 - c))
        cp.start()
        cp.wait_send()
        _rcopy(q_ref, out_ref.at[pl.ds((1 - c) * n, n), :], send_sems.at[0], recv_sems.at[0], (x, y, 1 - c)).wait_recv()
        mine.wait()

    return _comm_call(body, name=name, out_shape=jax.ShapeDtypeStruct((2 * n, cdim), q.dtype), n_in=1, n_sems=1,
                      local_sem=True)(q)


def _swap_sibling(buf, *, name):
    def body(b_ref, recv_ref, send_sems, recv_sems):
        x, y, c, _ = _place()
        cp = _rcopy(b_ref, recv_ref, send_sems.at[0], recv_sems.at[0], (x, y, 1 - c))
        cp.start()
        cp.wait()

    return _comm_call(body, name=name, out_shape=jax.ShapeDtypeStruct(buf.shape, buf.dtype), n_in=1, n_sems=1)(buf)


def _gather_chips(buf, *, name):
    n, cdim = buf.shape

    def body(b_ref, out_ref, send_sems, recv_sems, local_sem):
        x, y, c, chips = _place()
        mine = pltpu.make_async_copy(b_ref, out_ref.at[2 * x + y], local_sem)
        mine.start()
        cps = [_rcopy(b_ref, out_ref.at[2 * x + y], send_sems.at[j], recv_sems.at[j], (*chip, c))
               for j, chip in enumerate(chips)]
        for cp in cps:
            cp.start()
        for j, chip in enumerate(chips):
            slot = out_ref.at[2 * chip[0] + chip[1]]
            _rcopy(slot, slot, send_sems.at[j], recv_sems.at[j], (*chip, c)).wait_recv()
        for cp in cps:
            cp.wait_send()
        mine.wait()

    return _comm_call(body, name=name, out_shape=jax.ShapeDtypeStruct((4, n, cdim), buf.dtype), n_in=1, n_sems=3,
                      local_sem=True)(buf)


def _sum_sib(g4, recv, c_idx, *, name):
    _, rows, cdim = g4.shape
    half = rows // 2
    tr = _tile(half, 256, 16)
    nh = half // tr

    def body(c_ref, g_ref, r_ref, o_ref):
        o_ref[...] = (g_ref[...].astype(F32) + r_ref[...].astype(F32)).astype(o_ref.dtype)

    return _pcall(body, name=name, out_shape=jax.ShapeDtypeStruct((4, half, cdim), g4.dtype), grid=(4, nh),
                  num_prefetch=1,
                  in_specs=[pl.BlockSpec((None, tr, cdim), lambda k, i, c_ref: (k, c_ref[0] * nh + i, 0)),
                            pl.BlockSpec((None, tr, cdim), lambda k, i, c_ref: (k, i, 0))],
                  out_specs=pl.BlockSpec((None, tr, cdim), lambda k, i, c_ref: (k, i, 0)),
                  sem=("parallel", "parallel"))(c_idx, g4, recv)


def _sum_chips(h4, recv3, chip_idx, *, name):
    _, n, cdim = h4.shape
    tr = _tile(n, 256, 16)

    def body(k_ref, h_ref, r_ref, o_ref):
        acc = h_ref[...].astype(F32)
        for j in range(3):
            acc = acc + r_ref[j].astype(F32)
        o_ref[...] = acc

    return _pcall(body, name=name, out_shape=jax.ShapeDtypeStruct((n, cdim), F32), grid=(n // tr,), num_prefetch=1,
                  in_specs=[pl.BlockSpec((None, tr, cdim), lambda i, k_ref: (k_ref[0], i, 0)),
                            pl.BlockSpec((3, tr, cdim), lambda i, k_ref: (0, i, 0))],
                  out_specs=pl.BlockSpec((tr, cdim), lambda i, k_ref: (i, 0)), sem=("parallel",))(chip_idx, h4, recv3)


def _add2(a, b, *, name):
    n, cdim = a.shape
    tr = _tile(n, 256, SUBLANES)

    def body(a_ref, b_ref, o_ref):
        o_ref[...] = a_ref[...] + b_ref[...]

    blk = pl.BlockSpec((tr, cdim), lambda i: (i, 0))
    return _pcall(body, name=name, out_shape=jax.ShapeDtypeStruct((n, cdim), F32), grid=(n // tr,), in_specs=[blk, blk],
                  out_specs=blk, sem=("parallel",))(a, b)


def _sum4(x4, *, name):
    _, n, cdim = x4.shape
    tr = _tile(n, 256, SUBLANES)

    def body(x_ref, o_ref):
        o_ref[...] = ((x_ref[0] + x_ref[1]) + x_ref[2]) + x_ref[3]

    return _pcall(body, name=name, out_shape=jax.ShapeDtypeStruct((n, cdim), F32), grid=(n // tr,),
                  in_specs=[pl.BlockSpec((4, tr, cdim), lambda i: (0, i, 0))],
                  out_specs=pl.BlockSpec((tr, cdim), lambda i: (i, 0)), sem=("parallel",))(x4)


def _reduce_scatter(g4, c_idx, chip_idx, *, name):
    recv = _sib_halves(g4, name=name + "_sib")
    h4 = _sum_sib(g4, recv, c_idx, name=name + "_sum1")
    recv3 = _to_chips(h4, name=name + "_ici")
    q = _sum_chips(h4, recv3, chip_idx, name=name + "_sum2")
    return _join_halves(q, name=name + "_join")


def _all_reduce(buf, *, name):
    pair = _add2(buf, _swap_sibling(buf, name=name + "_sib"), name=name + "_add")
    return _sum4(_gather_chips(pair, name=name + "_ici"), name=name + "_sum")


def _pack_rows(flat, lead, align):
    n = flat.shape[-1]
    unit = PACK_COLS * align
    total = -(-n // unit) * unit
    flat = jnp.pad(flat, [(0, 0)] * len(lead) + [(0, total - n)])
    return flat.reshape(*lead, total // PACK_COLS, PACK_COLS)


def _split_axis(t, axis):
    return jnp.stack(jnp.split(t, 4, axis=axis), axis=0).reshape(4, -1)


def _join_axis(g, shard_shape, axis):
    return jnp.concatenate(list(g.reshape(4, *shard_shape)), axis=axis)


BIG = (("w_in", 1), ("ssm_w_glu", 1), ("w_kv", 0), ("w_gate", 2), ("w_branch", 2), ("w_out", 0))
SMALL_SHARDED = (("dn_conv_w", 1), ("lru_conv_w", 1), ("b_gate", 1))
SMALL = ("norm_w", "dn_a_log", "dn_dt_bias", "dn_norm_w", "lru_conv_b", "lru_w_r", "lru_b_r", "lru_w_i", "lru_b_i",
         "lru_lambda", "ssm_log_dt", "ssm_a_re", "ssm_a_im", "ssm_b_re", "ssm_b_im", "ssm_c_re", "ssm_c_im", "ssm_d",
         "ssm_b_glu", "mem_norm_w")
WEIGHTS = ("norm_w", "w_in", "dn_conv_w", "dn_a_log", "dn_dt_bias", "dn_norm_w", "lru_conv_w", "lru_conv_b",
           "lru_w_r", "lru_b_r", "lru_w_i", "lru_b_i", "lru_lambda", "ssm_log_dt", "ssm_a_re", "ssm_a_im", "ssm_b_re",
           "ssm_b_im", "ssm_c_re", "ssm_c_im", "ssm_d", "ssm_w_glu", "ssm_b_glu", "mem_norm_w", "w_kv", "w_gate",
           "b_gate", "w_branch", "w_out", "final_norm_w")


def _gather_layer(wts, l):
    segs = [wts[n][l].astype(BF16).reshape(-1) for n, _ in BIG]
    segs += [lax.bitcast_convert_type(wts[n][l], BF16).reshape(-1) for n, _ in SMALL_SHARDED]
    g = _ag4(_pack_rows(jnp.concatenate(segs), (), PACK_ROW_ALIGN), name=f"gather_w{l}")
    g = g.reshape(4, -1)
    out, off = {}, 0
    for n, ax in BIG:
        shp = wts[n].shape[1:]
        sz = math.prod(shp)
        out[n] = _join_axis(g[:, off:off + sz], shp, ax)
        off += sz
    for n, ax in SMALL_SHARDED:
        shp = wts[n].shape[1:]
        sz = 2 * math.prod(shp)
        full = lax.bitcast_convert_type(g[:, off:off + sz].reshape(4, *shp, 2), F32)
        out[n] = jnp.concatenate(list(full), axis=ax)
        off += sz
    return out


def _scatter_layer(grads, wts, c_idx, chip_idx, l):
    g4 = _pack_rows(jnp.concatenate([_split_axis(grads[n].astype(BF16), ax) for n, ax in BIG], axis=1), (4,),
                    PACK_ROW_ALIGN)
    r = _reduce_scatter(g4, c_idx, chip_idx, name=f"scatter_g{l}").reshape(-1)
    out, off = {}, 0
    for n, _ in BIG:
        shp = wts[n].shape[1:]
        sz = math.prod(shp)
        out[n] = r[off:off + sz].reshape(shp)
        off += sz
    return out


def _w_in_layout(w, bw, heads, rank):
    d = w.shape[0]
    ba = 4 * bw
    rest = ba + 2 * heads
    return jnp.concatenate([w[:, :ba], w[:, rest:], w[:, ba:rest], jnp.zeros((d, BA_PAD - 2 * heads), w.dtype)], axis=1)


def _w_in_unlayout(dw, bw, heads, rank):
    ba = 4 * bw
    tail = 10 * bw + rank
    return jnp.concatenate([dw[:, :ba], dw[:, tail:tail + 2 * heads], dw[:, ba:tail]], axis=1)


def _lru_dense(w):
    nb, blk, _ = w.shape
    return jnp.einsum("nij,nm->nimj", w, jnp.eye(nb, dtype=w.dtype)).reshape(nb * blk, nb * blk)


def _w8(w):
    return jnp.concatenate([w, jnp.zeros((SUBLANES - CONV_WIDTH, w.shape[1]), w.dtype)], axis=0)


def _layer_fwd(x, mem, full, prm, l):
    s, d = x.shape
    bw = d // N_BRANCH
    heads = bw // DN_HEAD_DIM
    rank = full["w_gate"].shape[1]
    tag = f"l{l}_"
    sv = {"x": x}
    w_in = _w_in_layout(full["w_in"], bw, heads, rank)
    sv["w_in"] = w_in
    h = _rms_fwd(x, prm["norm_w"], name=tag + "norm")
    p = _mm(h, w_in, name=tag + "in_proj")
    sv["h"], sv["p"] = h, p
    conv_a = _conv_fwd(p, 0, 3 * bw, _w8(full["dn_conv_w"]), jnp.zeros((1, 3 * bw), F32), name=tag + "dn_conv")
    ba_blk = (10 * bw + rank) // BA_PAD
    o_a, dn_states = _dn_fwd(p, conv_a, prm["dn_a_log"], prm["dn_dt_bias"], prm["dn_norm_w"], bw=bw, ba_blk=ba_blk,
                             name=tag + "dn")
    sv["conv_a"], sv["dn_states"] = conv_a, dn_states
    xc = _conv_fwd(p, 4, bw, _w8(full["lru_conv_w"]), prm["lru_conv_b"].reshape(1, bw), name=tag + "lru_conv")
    (wr, wi), lru_vjp = jax.vjp(lambda a, b: (_lru_dense(a), _lru_dense(b)), prm["lru_w_r"], prm["lru_w_i"])
    row = lambda v: v.reshape(1, bw)
    lru_args = (wr, row(prm["lru_b_r"]), wi, row(prm["lru_b_i"]), row(prm["lru_lambda"]))
    o_b, lru_saves = _lru_fwd(p, xc, *lru_args, bw=bw, z_blk=5, name=tag + "lru")
    sv["xc"], sv["lru_saves"], sv["lru_args"], sv["lru_vjp"] = xc, lru_saves, lru_args, lru_vjp
    prep, s5_vjp = jax.vjp(_s5_prep, prm["ssm_log_dt"], prm["ssm_a_re"], prm["ssm_a_im"], prm["ssm_b_re"],
                           prm["ssm_b_im"], prm["ssm_c_re"], prm["ssm_c_im"], prm["ssm_d"])
    u_blk0 = 6 * bw // LANES
    yg, s5_saves = _s5_fwd(p, prep, bw=bw, u_blk0=u_blk0, name=tag + "s5")
    pre = _mm(yg, full["ssm_w_glu"], bias=prm["ssm_b_glu"].reshape(1, 2 * bw), name=tag + "glu_proj")
    o_c = _glu_fwd(pre, p, bw=bw, z_blk=7, name=tag + "glu")
    sv["prep"], sv["s5_vjp"], sv["s5_saves"], sv["yg"], sv["pre"] = prep, s5_vjp, s5_saves, yg, pre
    mem_n = _rms_fwd(mem, prm["mem_norm_w"], name=tag + "mem_norm")
    kv = _mm(mem_n, full["w_kv"], name=tag + "kv_proj")
    o_d = _attn_fwd(p, kv, bw=bw, q_blk=8, z_blk=9, name=tag + "attn")
    sv["mem_n"], sv["kv"] = mem_n, kv
    o4 = jnp.stack([o_a, o_b, o_c, o_d], axis=0)
    bg = full["b_gate"].reshape(N_BRANCH, 1, d)
    g_blk = 10 * bw // rank
    merged = _merge_fwd(p, o4, full["w_gate"], bg, full["w_branch"], rank=rank, g_blk=g_blk, name=tag + "merge")
    sv["o4"], sv["bg"], sv["merged"] = o4, bg, merged
    return _mm(merged, full["w_out"], add=x, name=tag + "out_proj"), sv


def _layer_bwd(dx_out, mem, sv, full, prm, l):
    x, p, h = sv["x"], sv["p"], sv["h"]
    s, d = x.shape
    bw = d // N_BRANCH
    heads = bw // DN_HEAD_DIM
    rank = full["w_gate"].shape[1]
    tag = f"l{l}b_"
    big, small = {}, {}
    dmerged = _mm(dx_out, full["w_out"], tb=True, out_dtype=BF16, name=tag + "out_dx")
    big["w_out"] = _mm(sv["merged"], dx_out, ta=True, out_dtype=BF16, name=tag + "out_dw")
    g_blk = 10 * bw // rank
    dpre, dbr, dbg = _merge_bwd(p, sv["o4"], full["w_gate"], sv["bg"], full["w_branch"], dmerged, rank=rank,
                                g_blk=g_blk, name=tag + "merge")
    small["b_gate"] = dbg.reshape(N_BRANCH, d)
    glow = p[:, 10 * bw:10 * bw + rank].astype(BF16)
    dglow = None
    dwg, dwb, d_o = [], [], []
    for n in range(N_BRANCH):
        dglow = _mm(dpre, full["w_gate"], la=n, lb=n, tb=True, add=dglow, name=tag + f"gate_dx{n}")
        dwg.append(_mm(glow, dpre, ta=True, lb=n, out_dtype=BF16, name=tag + f"gate_dw{n}"))
        d_o.append(_mm(dbr, full["w_branch"], la=n, lb=n, tb=True, name=tag + f"branch_dx{n}"))
        dwb.append(_mm(sv["o4"], dbr, ta=True, la=n, lb=n, out_dtype=BF16, name=tag + f"branch_dw{n}"))
    big["w_gate"] = jnp.stack(dwg, axis=0)
    big["w_branch"] = jnp.stack(dwb, axis=0)
    ba_blk = (10 * bw + rank) // BA_PAD
    dconv, dz_a, dba, dal, ddt, dnw = _dn_bwd(p, sv["conv_a"], sv["dn_states"], d_o[0], prm["dn_a_log"],
                                              prm["dn_dt_bias"], prm["dn_norm_w"], bw=bw, ba_blk=ba_blk,
                                              name=tag + "dn")
    small["dn_a_log"] = dal[0, heads:2 * heads]
    small["dn_dt_bias"] = ddt[0, heads:2 * heads]
    small["dn_norm_w"] = dnw[0]
    dqkv, dw8_a, _ = _conv_bwd(p, 0, 3 * bw, _w8(full["dn_conv_w"]), dconv, name=tag + "dn_conv")
    small["dn_conv_w"] = dw8_a[:CONV_WIDTH]
    dxc, dz_b, dwr, dwi, dbr_, dbi_, dlam = _lru_bwd(p, sv["xc"], sv["lru_saves"], d_o[1], *sv["lru_args"], bw=bw,
                                                     z_blk=5, name=tag + "lru")
    small["lru_w_r"], small["lru_w_i"] = sv["lru_vjp"]((dwr, dwi))
    small["lru_b_r"], small["lru_b_i"], small["lru_lambda"] = dbr_[0], dbi_[0], dlam[0]
    dlx, dw8_b, dcb = _conv_bwd(p, 4, bw, _w8(full["lru_conv_w"]), dxc, name=tag + "lru_conv")
    small["lru_conv_w"] = dw8_b[:CONV_WIDTH]
    small["lru_conv_b"] = dcb[0]
    dpre_glu, dz_c, dbglu = _glu_bwd(sv["pre"], p, d_o[2], bw=bw, z_blk=7, name=tag + "glu")
    small["ssm_b_glu"] = dbglu[0]
    dyg = _mm(dpre_glu, full["ssm_w_glu"], tb=True, name=tag + "glu_dx")
    big["ssm_w_glu"] = _mm(sv["yg"], dpre_glu, ta=True, out_dtype=BF16, name=tag + "glu_dw")
    s5 = _s5_bwd(p, sv["prep"], sv["s5_saves"], dyg, bw=bw, u_blk0=6 * bw // LANES, name=tag + "s5")
    du = s5[0]
    (small["ssm_log_dt"], small["ssm_a_re"], small["ssm_a_im"], small["ssm_b_re"], small["ssm_b_im"],
     small["ssm_c_re"], small["ssm_c_im"], small["ssm_d"]) = sv["s5_vjp"](tuple(s5[1:]))
    dq, dz_d, dkv = _attn_bwd(p, sv["kv"], d_o[3], bw=bw, q_blk=8, z_blk=9, name=tag + "attn")
    big["w_kv"] = _mm(sv["mem_n"], dkv, ta=True, out_dtype=BF16, name=tag + "kv_dw")
    dmem_n = _mm(dkv, full["w_kv"], tb=True, name=tag + "kv_dx")
    _, dmnw = _rms_bwd(mem, prm["mem_norm_w"], dmem_n, None, name=tag + "mem_norm")
    small["mem_norm_w"] = dmnw[0]
    dp = jnp.concatenate([dqkv, dz_a, dlx, dz_b, du, dz_c, dq, dz_d, dglow, dba], axis=1).astype(BF16)
    dh = _mm(dp, sv["w_in"], tb=True, name=tag + "in_dx")
    big["w_in"] = _w_in_unlayout(_mm(h, dp, ta=True, out_dtype=BF16, name=tag + "in_dw"), bw, heads, rank)
    dx, dnw_in = _rms_bwd(x, prm["norm_w"], dh, dx_out, name=tag + "norm")
    small["norm_w"] = dnw_in[0]
    return dx, big, small


def _step(wts, mom, vel, x, mem, target):
    depth = wts["norm_w"].shape[0]
    xi, yi, ci = lax.axis_index("x"), lax.axis_index("y"), lax.axis_index("c")
    c_idx = ci.astype(jnp.int32).reshape(1)
    chip = (2 * xi + yi).astype(jnp.int32)
    chip_idx = chip.reshape(1)
    x, mem, target = x[0], mem[0], target[0]

    fulls = [_gather_layer(wts, l) for l in range(depth)]
    prms = [{n: wts[n][l] for n in SMALL} for l in range(depth)]
    saves = []
    act = x
    for l in range(depth):
        act, sv = _layer_fwd(act, mem, fulls[l], prms[l], l)
        saves.append(sv)
    loss_part, dx, dfw = _loss_head(act, wts["final_norm_w"], target, name="loss_head")
    loss = lax.psum(loss_part[0, 0], ("x", "y", "c"))

    big_g = [None] * depth
    small_g = [None] * depth
    for l in reversed(range(depth)):
        dx, big, small_g[l] = _layer_bwd(dx, mem, saves[l], fulls[l], prms[l], l)
        big_g[l] = _scatter_layer(big, wts, c_idx, chip_idx, l)

    names = SMALL + tuple(n for n, _ in SMALL_SHARDED)
    flat = jnp.concatenate([small_g[l][n].reshape(-1) for l in range(depth) for n in names] + [dfw.reshape(-1)])
    red = _all_reduce(_pack_rows(flat, (), 256), name="reduce_small").reshape(-1)
    grads, off = {n: [] for n in names}, 0
    for l in range(depth):
        for n in names:
            shp = small_g[l][n].shape
            sz = math.prod(shp)
            grads[n].append(red[off:off + sz].reshape(shp))
            off += sz
    grads = {n: jnp.stack(v, axis=0) for n, v in grads.items()}
    grads["final_norm_w"] = red[off:off + dfw.size].reshape(wts["final_norm_w"].shape)
    for n, ax in SMALL_SHARDED:
        width = wts[n].shape[-1]
        grads[n] = lax.dynamic_slice_in_dim(grads[n], chip * width, width, axis=ax + 1)
    for n, _ in BIG:
        grads[n] = jnp.stack([big_g[l][n] for l in range(depth)], axis=0)

    delta, new_m, new_v = {}, {}, {}
    for n, _ in BIG:
        shp = wts[n].shape
        two = lambda t: t.reshape(-1, shp[-1])
        dlt, nm, nv = _adamw(two(wts[n]), two(grads[n]), two(mom[n]), two(vel[n]), name="adamw_" + n)
        delta[n], new_m[n], new_v[n] = dlt.reshape(shp), nm.reshape(shp), nv.reshape(shp)
    rest = [n for n in WEIGHTS if n not in dict(BIG)]
    cat = lambda src: _pack_rows(jnp.concatenate([src[n].reshape(-1) for n in rest]), (), SUBLANES)
    dlt, nm, nv = _adamw(cat(wts), cat(grads), cat(mom), cat(vel), name="adamw_small")
    off = 0
    for n in rest:
        shp = wts[n].shape
        sz = math.prod(shp)
        for dst, src in ((delta, dlt), (new_m, nm), (new_v, nv)):
            dst[n] = src.reshape(-1)[off:off + sz].reshape(shp)
        off += sz
    return (loss, dx[None], *[grads[n] for n in WEIGHTS], *[delta[n] for n in WEIGHTS], *[new_m[n] for n in WEIGHTS],
            *[new_v[n] for n in WEIGHTS])


def kernel(x, mem, norm_w, w_in, dn_conv_w, dn_a_log, dn_dt_bias, dn_norm_w, lru_conv_w, lru_conv_b, lru_w_r, lru_b_r, lru_w_i, lru_b_i, lru_lambda, ssm_log_dt, ssm_a_re, ssm_a_im, ssm_b_re, ssm_b_im, ssm_c_re, ssm_c_im, ssm_d, ssm_w_glu, ssm_b_glu, mem_norm_w, w_kv, w_gate, b_gate, w_branch, w_out, final_norm_w, loss_target, m_norm_w, m_w_in, m_dn_conv_w, m_dn_a_log, m_dn_dt_bias, m_dn_norm_w, m_lru_conv_w, m_lru_conv_b, m_lru_w_r, m_lru_b_r, m_lru_w_i, m_lru_b_i, m_lru_lambda, m_ssm_log_dt, m_ssm_a_re, m_ssm_a_im, m_ssm_b_re, m_ssm_b_im, m_ssm_c_re, m_ssm_c_im, m_ssm_d, m_ssm_w_glu, m_ssm_b_glu, m_mem_norm_w, m_w_kv, m_w_gate, m_b_gate, m_w_branch, m_w_out, m_final_norm_w, v_norm_w, v_w_in, v_dn_conv_w, v_dn_a_log, v_dn_dt_bias, v_dn_norm_w, v_lru_conv_w, v_lru_conv_b, v_lru_w_r, v_lru_b_r, v_lru_w_i, v_lru_b_i, v_lru_lambda, v_ssm_log_dt, v_ssm_a_re, v_ssm_a_im, v_ssm_b_re, v_ssm_b_im, v_ssm_c_re, v_ssm_c_im, v_ssm_d, v_ssm_w_glu, v_ssm_b_glu, v_mem_norm_w, v_w_kv, v_w_gate, v_b_gate, v_w_branch, v_w_out, v_final_norm_w):
    given = dict(locals())
    wts = {n: given[n] for n in WEIGHTS}
    mom = {n: given["m_" + n] for n in WEIGHTS}
    vel = {n: given["v_" + n] for n in WEIGHTS}
    return _step(wts, mom, vel, x, mem, loss_target)
```

```python
import functools
import math

import jax
import jax.numpy as jnp
import numpy as np
from jax import lax
from jax.experimental import pallas as pl
from jax.experimental.pallas import tpu as pltpu

F32 = jnp.float32
BF16 = jnp.bfloat16
HIGHEST = lax.Precision.HIGHEST
MESH_ID = pl.DeviceIdType.MESH

NORM_EPS = 1e-6
CONV_WIDTH = 4
DN_HEAD_DIM = 128
DN_CHUNK = 64
LRU_C = 8.0
MEM_HEADS = 4
N_BRANCH = 4
ADAM_LR, ADAM_B1, ADAM_B2, ADAM_EPS, ADAM_WD, ADAM_STEP = 0.001, 0.9, 0.999, 1e-08, 0.01, 10

LANES = 128
SUBLANES = 8
VMEM_LIMIT = 56 * 2 ** 20
PACK_COLS = 1024
ELEMENTWISE_BLOCK_BYTES = 2 * 2 ** 20
BA_PAD = 256


def _tile(n, pref, align=LANES):
    if n <= pref:
        return n
    t = pref - pref % align
    while t > 0:
        if n % t == 0:
            return t
        t -= align
    return n


def _pcall(body, *, name, out_shape, grid=(), in_specs=None, out_specs=None, scratch=(), sem=None,
           num_prefetch=0, **kw):
    params = dict(vmem_limit_bytes=VMEM_LIMIT)
    if sem is not None:
        params["dimension_semantics"] = sem
    params.update(kw.pop("params", {}))
    if num_prefetch:
        return pl.pallas_call(
            body, name=name, out_shape=out_shape,
            grid_spec=pltpu.PrefetchScalarGridSpec(num_scalar_prefetch=num_prefetch, grid=grid, in_specs=in_specs,
                                                   out_specs=out_specs, scratch_shapes=list(scratch)),
            compiler_params=pltpu.CompilerParams(**params), interpret=False)
    return pl.pallas_call(body, name=name, out_shape=out_shape, grid=grid, in_specs=in_specs, out_specs=out_specs,
                          scratch_shapes=list(scratch), compiler_params=pltpu.CompilerParams(**params),
                          interpret=False)


@functools.partial(jax.custom_vjp, nondiff_argnums=(2, 3))
def _bdot(a, b, ca, cb):
    return lax.dot_general(a.astype(BF16), b.astype(BF16), (((ca,), (cb,)), ((), ())), preferred_element_type=F32)


def _bdot_fwd(a, b, ca, cb):
    return _bdot(a, b, ca, cb), (a, b)


def _bdot_bwd(ca, cb, res, ct):
    a, b = res
    da = _bdot(ct, b, 1, 1 - cb) if ca == 1 else _bdot(b, ct, 1 - cb, 1)
    db = _bdot(a, ct, 1 - ca, 0) if cb == 0 else _bdot(ct, a, 0, 1 - ca)
    return da.astype(a.dtype), db.astype(b.dtype)


_bdot.defvjp(_bdot_fwd, _bdot_bwd)


def _hdot(a, b):
    return jnp.dot(a, b, precision=HIGHEST, preferred_element_type=F32)


def _sigmoid(x):
    return 1.0 / (1.0 + jnp.exp(-x))


def _silu(x):
    return x * _sigmoid(x)


def _softplus(x):
    return jnp.maximum(x, 0.0) + jnp.log(1.0 + jnp.exp(-jnp.abs(x)))


def _expm1(x):
    small = x * (1.0 + x * (0.5 + x * (1.0 / 6.0 + x * (1.0 / 24.0 + x * (1.0 / 120.0 + x * (1.0 / 720.0))))))
    return jnp.where(jnp.abs(x) < 0.1, small, jnp.exp(x) - 1.0)


def _gelu(x):
    return 0.5 * x * (1.0 + jnp.tanh(math.sqrt(2.0 / math.pi) * (x + 0.044715 * x * x * x)))


def _rms(x, w):
    var = jnp.mean(x * x, axis=-1, keepdims=True)
    return x * lax.rsqrt(var + NORM_EPS) * w


def _pick_lane(v, idx):
    lane = lax.broadcasted_iota(jnp.int32, v.shape, 1)
    return jnp.sum(jnp.where(lane == idx, v, 0.0), axis=1, keepdims=True)


def _pick_row(v, idx):
    row = lax.broadcasted_iota(jnp.int32, v.shape, 0)
    return jnp.sum(jnp.where(row == idx, v, 0.0), axis=0, keepdims=True)


def _mm(a, b, *, name, ta=False, tb=False, out_dtype=F32, add=None, bias=None, la=None, lb=None, nsplit=None,
        tm=1024, tn=1024, tk=512):
    a2 = a.shape[-2:]
    b2 = b.shape[-2:]
    m, k = (a2[1], a2[0]) if ta else a2
    n = b2[0] if tb else b2[1]
    assert (b2[1] if tb else b2[0]) == k
    tm, tn, tk = _tile(m, tm), _tile(n // (nsplit or 1), tn), _tile(k, tk)
    nk = k // tk

    def a_map(i, j, kk):
        idx = (kk, i) if ta else (i, kk)
        return idx if la is None else (la,) + idx

    def b_map(i, j, kk):
        idx = (j, kk) if tb else (kk, j)
        return idx if lb is None else (lb,) + idx

    a_blk = (tk, tm) if ta else (tm, tk)
    b_blk = (tn, tk) if tb else (tk, tn)
    in_specs = [pl.BlockSpec(a_blk if la is None else (None,) + a_blk, a_map),
                pl.BlockSpec(b_blk if lb is None else (None,) + b_blk, b_map)]
    operands = [a, b]
    if add is not None:
        in_specs.append(pl.BlockSpec((tm, tn), lambda i, j, kk: (i, j)))
        operands.append(add)
    if bias is not None:
        in_specs.append(pl.BlockSpec((1, tn), lambda i, j, kk: (0, j)))
        operands.append(bias)
    dims = (((0 if ta else 1,), (1 if tb else 0,)), ((), ()))

    def body(*refs):
        a_ref, b_ref = refs[0], refs[1]
        rest = list(refs[2:])
        add_ref = rest.pop(0) if add is not None else None
        bias_ref = rest.pop(0) if bias is not None else None
        o_ref, acc_ref = rest
        kk = pl.program_id(2)

        @pl.when(kk == 0)
        def _():
            acc_ref[...] = jnp.zeros_like(acc_ref)

        acc_ref[...] += lax.dot_general(a_ref[...].astype(BF16), b_ref[...].astype(BF16), dims,
                                        preferred_element_type=F32)

        @pl.when(kk == nk - 1)
        def _():
            r = acc_ref[...]
            if add_ref is not None:
                r = r + add_ref[...].astype(F32)
            if bias_ref is not None:
                r = r + bias_ref[...]
            o_ref[...] = r.astype(out_dtype)

    if nsplit is None:
        out_shape = jax.ShapeDtypeStruct((m, n), out_dtype)
        out_spec = pl.BlockSpec((tm, tn), lambda i, j, kk: (i, j))
    else:
        per = n // nsplit // tn
        out_shape = jax.ShapeDtypeStruct((nsplit, m, n // nsplit), out_dtype)
        out_spec = pl.BlockSpec((None, tm, tn), lambda i, j, kk: (j // per, i, j % per))
    return _pcall(body, name=name, out_shape=out_shape, grid=(m // tm, n // tn, nk), in_specs=in_specs,
                  out_specs=out_spec, scratch=[pltpu.VMEM((tm, tn), F32)],
                  sem=("parallel", "parallel", "arbitrary"))(*operands)


def _rms_fwd(x, w, *, name):
    s, d = x.shape
    t = _tile(s, 256, SUBLANES)

    def body(x_ref, w_ref, o_ref):
        o_ref[...] = _rms(x_ref[...], w_ref[...]).astype(BF16)

    return _pcall(body, name=name, out_shape=jax.ShapeDtypeStruct((s, d), BF16), grid=(s // t,),
                  in_specs=[pl.BlockSpec((t, d), lambda i: (i, 0)), pl.BlockSpec((1, d), lambda i: (0, 0))],
                  out_specs=pl.BlockSpec((t, d), lambda i: (i, 0)), sem=("parallel",))(x, w.reshape(1, d))


def _rms_bwd(x, w, dh, res, *, name):
    s, d = x.shape
    t = _tile(s, 256, SUBLANES)

    def body(*refs):
        if res is None:
            x_ref, w_ref, dh_ref, dx_ref, dw_ref = refs
            res_ref = None
        else:
            x_ref, w_ref, dh_ref, res_ref, dx_ref, dw_ref = refs
        _, vjp = jax.vjp(_rms, x_ref[...], w_ref[...])
        dx, dw = vjp(dh_ref[...].astype(F32))
        if res_ref is not None:
            dx = dx + res_ref[...]
        dx_ref[...] = dx

        @pl.when(pl.program_id(0) == 0)
        def _():
            dw_ref[...] = jnp.zeros_like(dw_ref)

        dw_ref[...] += dw

    tok = pl.BlockSpec((t, d), lambda i: (i, 0))
    row = pl.BlockSpec((1, d), lambda i: (0, 0))
    operands = [x, w.reshape(1, d), dh] + ([] if res is None else [res])
    return _pcall(body, name=name,
                  out_shape=(jax.ShapeDtypeStruct((s, d), F32), jax.ShapeDtypeStruct((1, d), F32)), grid=(s // t,),
                  in_specs=[tok, row, tok] + ([] if res is None else [tok]), out_specs=(tok, row),
                  sem=("arbitrary",))(*operands)


def _loss_head(x, w, target, *, name):
    s, d = x.shape
    t = _tile(s, 256, SUBLANES)

    def body(x_ref, w_ref, t_ref, loss_ref, dx_ref, dw_ref):
        def f(xv, wv):
            err = _rms(xv, wv) - t_ref[...]
            return 0.5 * jnp.sum(jnp.mean(err * err, axis=-1))

        val, vjp = jax.vjp(f, x_ref[...], w_ref[...])
        dx, dw = vjp(jnp.ones((), F32))
        dx_ref[...] = dx

        @pl.when(pl.program_id(0) == 0)
        def _():
            dw_ref[...] = jnp.zeros_like(dw_ref)
            loss_ref[...] = jnp.zeros_like(loss_ref)

        dw_ref[...] += dw
        loss_ref[...] += jnp.full(loss_ref.shape, val, F32)

    tok = pl.BlockSpec((t, d), lambda i: (i, 0))
    row = pl.BlockSpec((1, d), lambda i: (0, 0))
    return _pcall(body, name=name,
                  out_shape=(jax.ShapeDtypeStruct((1, LANES), F32), jax.ShapeDtypeStruct((s, d), F32),
                             jax.ShapeDtypeStruct((1, d), F32)),
                  grid=(s // t,), in_specs=[tok, row, tok],
                  out_specs=(pl.BlockSpec((1, LANES), lambda i: (0, 0)), tok, row), sem=("arbitrary",))(
                      x, w.reshape(1, d), target)


def _conv_shifts(prev8, cur, t):
    xp = jnp.concatenate([prev8, cur], axis=0)
    out = []
    for j in range(CONV_WIDTH):
        k = CONV_WIDTH - 1 - j
        out.append(cur if k == 0 else pltpu.roll(xp, k, 0)[SUBLANES:SUBLANES + t])
    return out


def _conv_fwd(p, col_blk, width, w8, b, *, name):
    s = p.shape[0]
    t = _tile(s, 256, SUBLANES)
    r8 = t // SUBLANES

    def body(cur_ref, prev_ref, w_ref, b_ref, y_ref):
        i = pl.program_id(0)
        prev8 = jnp.where(i == 0, 0.0, prev_ref[...])
        sh = _conv_shifts(prev8, cur_ref[...], t)
        w = w_ref[...]
        y = b_ref[...] + sh[0] * w[0:1]
        for j in range(1, CONV_WIDTH):
            y = y + sh[j] * w[j:j + 1]
        y_ref[...] = y

    return _pcall(body, name=name, out_shape=jax.ShapeDtypeStruct((s, width), F32), grid=(s // t,),
                  in_specs=[pl.BlockSpec((t, width), lambda i: (i, col_blk)),
                            pl.BlockSpec((SUBLANES, width), lambda i: (jnp.maximum(i * r8 - 1, 0), col_blk)),
                            pl.BlockSpec((SUBLANES, width), lambda i: (0, 0)),
                            pl.BlockSpec((1, width), lambda i: (0, 0))],
                  out_specs=pl.BlockSpec((t, width), lambda i: (i, 0)), sem=("parallel",))(p, p, w8, b)


def _conv_bwd(p, col_blk, width, w8, dy, *, name):
    s = p.shape[0]
    t = _tile(s, 256, SUBLANES)
    r8 = t // SUBLANES
    nt = s // t

    def body(cur_ref, prev_ref, w_ref, dy_ref, dyn_ref, dx_ref, dw_ref, db_ref):
        i = pl.program_id(0)
        prev8 = jnp.where(i == 0, 0.0, prev_ref[...])
        sh = _conv_shifts(prev8, cur_ref[...], t)
        dy = dy_ref[...]
        next8 = jnp.where(i == nt - 1, 0.0, dyn_ref[...])
        dyp = jnp.concatenate([dy, next8], axis=0)
        w = w_ref[...]
        rows = lax.broadcasted_iota(jnp.int32, (SUBLANES, width), 0)
        dx = dy * w[CONV_WIDTH - 1:CONV_WIDTH]
        dw = jnp.zeros((SUBLANES, width), F32)
        for j in range(CONV_WIDTH):
            k = CONV_WIDTH - 1 - j
            if k:
                dx = dx + pltpu.roll(dyp, t + SUBLANES - k, 0)[0:t] * w[j:j + 1]
            dw = dw + jnp.where(rows == j, jnp.sum(dy * sh[j], axis=0, keepdims=True), 0.0)
        dx_ref[...] = dx

        @pl.when(i == 0)
        def _():
            dw_ref[...] = jnp.zeros_like(dw_ref)
            db_ref[...] = jnp.zeros_like(db_ref)

        dw_ref[...] += dw
        db_ref[...] += jnp.sum(dy, axis=0, keepdims=True)

    return _pcall(body, name=name,
                  out_shape=(jax.ShapeDtypeStruct((s, width), F32), jax.ShapeDtypeStruct((SUBLANES, width), F32),
                             jax.ShapeDtypeStruct((1, width), F32)),
                  grid=(nt,),
                  in_specs=[pl.BlockSpec((t, width), lambda i: (i, col_blk)),
                            pl.BlockSpec((SUBLANES, width), lambda i: (jnp.maximum(i * r8 - 1, 0), col_blk)),
                            pl.BlockSpec((SUBLANES, width), lambda i: (0, 0)),
                            pl.BlockSpec((t, width), lambda i: (i, 0)),
                            pl.BlockSpec((SUBLANES, width), lambda i: (jnp.minimum((i + 1) * r8, s // SUBLANES - 1), 0))],
                  out_specs=(pl.BlockSpec((t, width), lambda i: (i, 0)),
                             pl.BlockSpec((SUBLANES, width), lambda i: (0, 0)),
                             pl.BlockSpec((1, width), lambda i: (0, 0))),
                  sem=("arbitrary",))(p, p, w8, dy, dy)


def _dn_chunk(state, c, z, ba, alog_row, dt_row, nw_row, *, heads, bw):
    cs = c.shape[0]
    hd = DN_HEAD_DIM
    qkv = _silu(c)
    gfull = -jnp.exp(alog_row) * _softplus(ba + dt_row)
    beta_full = _sigmoid(ba)
    ri = lax.broadcasted_iota(jnp.int32, (cs, cs), 0)
    ci = lax.broadcasted_iota(jnp.int32, (cs, cs), 1)
    causal = ri >= ci
    strict = ri > ci
    tril = causal.astype(F32)
    eye = (ri == ci).astype(F32)
    gc = _hdot(tril, gfull)
    gct = lax.dot_general(gfull, tril, (((0,), (1,)), ((), ())), precision=HIGHEST,
                          preferred_element_type=F32)
    outs, states = [], []
    for h in range(heads):
        q = qkv[:, h * hd:(h + 1) * hd]
        k = qkv[:, bw + h * hd:bw + (h + 1) * hd]
        v = qkv[:, 2 * bw + h * hd:2 * bw + (h + 1) * hd]
        q = q * lax.rsqrt(jnp.sum(q * q, axis=-1, keepdims=True) + NORM_EPS) * (hd ** -0.5)
        k = k * lax.rsqrt(jnp.sum(k * k, axis=-1, keepdims=True) + NORM_EPS)
        beta = _pick_lane(beta_full, h)
        g_col = _pick_lane(gc, heads + h)
        g_row = _pick_row(gct, heads + h)
        decay = jnp.exp(jnp.where(causal, g_col - g_row, -1e30))
        k_beta = k * beta
        v_beta = v * beta
        kk = _bdot(k_beta, k, 1, 1) * decay
        m = -jnp.where(strict, kk, 0.0)
        tinv = eye + m
        pw = m
        for _ in range(int(math.log2(cs)) - 1):
            pw = _hdot(pw, pw)
            tinv = tinv + _hdot(tinv, pw)
        rhs = jnp.concatenate([v_beta, k_beta * jnp.exp(g_col)], axis=-1)
        sol = _hdot(tinv, rhs)
        u, w = sol[:, :hd], sol[:, hd:]
        qk = jnp.where(causal, _bdot(q, k, 1, 1) * decay, 0.0)
        g_last = _pick_row(g_col, cs - 1)
        k_dec = k * jnp.exp(g_last - g_col)
        q_dec = q * jnp.exp(g_col)
        s_h = state[h]
        v_new = u - _bdot(w, s_h, 1, 0)
        o = _bdot(q_dec, s_h, 1, 0) + _bdot(qk, v_new, 1, 0)
        states.append(s_h * jnp.exp(g_last) + _bdot(k_dec, v_new, 0, 0))
        outs.append(_rms(o, nw_row) * _silu(z[:, h * hd:(h + 1) * hd]))
    return jnp.concatenate(outs, axis=-1), jnp.stack(states, axis=0)


def _dn_rows(a_log, dt_bias, heads):
    z = jnp.zeros((heads,), F32)
    pad = jnp.zeros((BA_PAD - 2 * heads,), F32)
    return (jnp.concatenate([z, a_log, pad]).reshape(1, BA_PAD), jnp.concatenate([z, dt_bias, pad]).reshape(1, BA_PAD))


def _dn_fwd(p, conv, a_log, dt_bias, norm_w, *, bw, ba_blk, name):
    s = p.shape[0]
    heads = bw // DN_HEAD_DIM
    cs = min(DN_CHUNK, s)
    n = s // cs
    hd = DN_HEAD_DIM
    alog_row, dt_row = _dn_rows(a_log, dt_bias, heads)
    fn = functools.partial(_dn_chunk, heads=heads, bw=bw)

    def body(c_ref, z_ref, ba_ref, al_ref, dt_ref, nw_ref, o_ref, save_ref, st_ref):
        @pl.when(pl.program_id(0) == 0)
        def _():
            st_ref[...] = jnp.zeros_like(st_ref)

        st = st_ref[...]
        save_ref[...] = st
        o, new = fn(st, c_ref[...], z_ref[...], ba_ref[...], al_ref[...], dt_ref[...], nw_ref[...])
        o_ref[...] = o.astype(BF16)
        st_ref[...] = new

    row = lambda wd: pl.BlockSpec((1, wd), lambda i: (0, 0))
    return _pcall(body, name=name,
                  out_shape=(jax.ShapeDtypeStruct((s, bw), BF16), jax.ShapeDtypeStruct((n, heads, hd, hd), F32)),
                  grid=(n,),
                  in_specs=[pl.BlockSpec((cs, 3 * bw), lambda i: (i, 0)), pl.BlockSpec((cs, bw), lambda i: (i, 3)),
                            pl.BlockSpec((cs, BA_PAD), lambda i: (i, ba_blk)), row(BA_PAD), row(BA_PAD), row(hd)],
                  out_specs=(pl.BlockSpec((cs, bw), lambda i: (i, 0)),
                             pl.BlockSpec((None, heads, hd, hd), lambda i: (i, 0, 0, 0))),
                  scratch=[pltpu.VMEM((heads, hd, hd), F32)], sem=("arbitrary",))(
                      conv, p, p, alog_row, dt_row, norm_w.reshape(1, hd))


def _dn_bwd(p, conv, states, d_o, a_log, dt_bias, norm_w, *, bw, ba_blk, name):
    s = p.shape[0]
    heads = bw // DN_HEAD_DIM
    cs = min(DN_CHUNK, s)
    n = s // cs
    hd = DN_HEAD_DIM
    alog_row, dt_row = _dn_rows(a_log, dt_bias, heads)
    fn = functools.partial(_dn_chunk, heads=heads, bw=bw)

    def body(c_ref, z_ref, ba_ref, st_ref, do_ref, al_ref, dt_ref, nw_ref,
             dc_ref, dz_ref, dba_ref, dal_ref, ddt_ref, dnw_ref, dst_ref):
        @pl.when(pl.program_id(0) == 0)
        def _():
            dst_ref[...] = jnp.zeros_like(dst_ref)
            dal_ref[...] = jnp.zeros_like(dal_ref)
            ddt_ref[...] = jnp.zeros_like(ddt_ref)
            dnw_ref[...] = jnp.zeros_like(dnw_ref)

        _, vjp = jax.vjp(fn, st_ref[...], c_ref[...], z_ref[...], ba_ref[...], al_ref[...], dt_ref[...], nw_ref[...])
        dst, dc, dz, dba, dal, ddt, dnw = vjp((do_ref[...].astype(F32), dst_ref[...]))
        dst_ref[...] = dst
        dc_ref[...] = dc
        dz_ref[...] = dz
        dba_ref[...] = dba
        dal_ref[...] += dal
        ddt_ref[...] += ddt
        dnw_ref[...] += dnw

    rev = lambda i: n - 1 - i
    row = lambda wd: pl.BlockSpec((1, wd), lambda i: (0, 0))
    return _pcall(body, name=name,
                  out_shape=(jax.ShapeDtypeStruct((s, 3 * bw), F32), jax.ShapeDtypeStruct((s, bw), F32),
                             jax.ShapeDtypeStruct((s, BA_PAD), F32), jax.ShapeDtypeStruct((1, BA_PAD), F32),
                             jax.ShapeDtypeStruct((1, BA_PAD), F32), jax.ShapeDtypeStruct((1, hd), F32)),
                  grid=(n,),
                  in_specs=[pl.BlockSpec((cs, 3 * bw), lambda i: (rev(i), 0)),
                            pl.BlockSpec((cs, bw), lambda i: (rev(i), 3)),
                            pl.BlockSpec((cs, BA_PAD), lambda i: (rev(i), ba_blk)),
                            pl.BlockSpec((None, heads, hd, hd), lambda i: (rev(i), 0, 0, 0)),
                            pl.BlockSpec((cs, bw), lambda i: (rev(i), 0)), row(BA_PAD), row(BA_PAD), row(hd)],
                  out_specs=(pl.BlockSpec((cs, 3 * bw), lambda i: (rev(i), 0)),
                             pl.BlockSpec((cs, bw), lambda i: (rev(i), 0)),
                             pl.BlockSpec((cs, BA_PAD), lambda i: (rev(i), 0)), row(BA_PAD), row(BA_PAD), row(hd)),
                  scratch=[pltpu.VMEM((heads, hd, hd), F32)], sem=("arbitrary",))(
                      conv, p, p, states, d_o, alog_row, dt_row, norm_w.reshape(1, hd))


def _lru_gates(xc, wr, br, wi, bi, lam):
    r = _sigmoid(_bdot(xc, wr, 1, 0) + br)
    i = _sigmoid(_bdot(xc, wi, 1, 0) + bi)
    log_a = -LRU_C * r * _softplus(-lam)
    return jnp.exp(log_a), jnp.sqrt(-_expm1(2.0 * log_a)) * (i * xc)


def _scan_rows(t, step, carry):
    def trip(g, cr):
        base = pl.multiple_of(g * SUBLANES, SUBLANES)
        for r in range(SUBLANES):
            cr = step(base + r, cr)
        return cr
    return lax.fori_loop(0, t // SUBLANES, trip, carry)


def _scan_rows_rev(t, step, carry):
    def trip(g, cr):
        base = pl.multiple_of((t // SUBLANES - 1 - g) * SUBLANES, SUBLANES)
        for r in range(SUBLANES - 1, -1, -1):
            cr = step(base + r, cr)
        return cr
    return lax.fori_loop(0, t // SUBLANES, trip, carry)


def _lru_fwd(p, xc, wr, br, wi, bi, lam, *, bw, z_blk, name):
    s = p.shape[0]
    t = _tile(s, 256, SUBLANES)
    nt = s // t

    def body(xc_ref, z_ref, wr_ref, br_ref, wi_ref, bi_ref, lam_ref, o_ref, save_ref, a_s, b_s, h_s, carry_s):
        @pl.when(pl.program_id(0) == 0)
        def _():
            carry_s[...] = jnp.zeros_like(carry_s)

        a, inp = _lru_gates(xc_ref[...], wr_ref[...], br_ref[...], wi_ref[...], bi_ref[...], lam_ref[...])
        a_s[...] = a
        b_s[...] = inp
        h0 = carry_s[...]
        save_ref[...] = h0

        def step(r, h):
            h = a_s[pl.ds(r, 1), :] * h + b_s[pl.ds(r, 1), :]
            h_s[pl.ds(r, 1), :] = h
            return h

        carry_s[...] = _scan_rows(t, step, h0)
        o_ref[...] = (h_s[...] * _silu(z_ref[...])).astype(BF16)

    tok = pl.BlockSpec((t, bw), lambda i: (i, 0))
    row = pl.BlockSpec((1, bw), lambda i: (0, 0))
    mat = pl.BlockSpec((bw, bw), lambda i: (0, 0))
    return _pcall(body, name=name,
                  out_shape=(jax.ShapeDtypeStruct((s, bw), BF16), jax.ShapeDtypeStruct((nt, 1, bw), F32)), grid=(nt,),
                  in_specs=[tok, pl.BlockSpec((t, bw), lambda i: (i, z_blk)), mat, row, mat, row, row],
                  out_specs=(tok, pl.BlockSpec((None, 1, bw), lambda i: (i, 0, 0))),
                  scratch=[pltpu.VMEM((t, bw), F32)] * 3 + [pltpu.VMEM((1, bw), F32)], sem=("arbitrary",))(
                      xc, p, wr, br, wi, bi, lam)


def _lru_bwd(p, xc, saves, d_o, wr, br, wi, bi, lam, *, bw, z_blk, name):
    s = p.shape[0]
    t = _tile(s, 256, SUBLANES)
    nt = s // t

    def body(xc_ref, z_ref, sv_ref, do_ref, wr_ref, br_ref, wi_ref, bi_ref, lam_ref,
             dxc_ref, dz_ref, dwr_ref, dwi_ref, dbr_ref, dbi_ref, dlam_ref, a_s, b_s, h_s, g_s, carry_s):
        @pl.when(pl.program_id(0) == 0)
        def _():
            carry_s[...] = jnp.zeros_like(carry_s)
            for r in (dwr_ref, dwi_ref, dbr_ref, dbi_ref, dlam_ref):
                r[...] = jnp.zeros_like(r)

        (a, inp), vjp_g = jax.vjp(_lru_gates, xc_ref[...], wr_ref[...], br_ref[...], wi_ref[...], bi_ref[...],
                                  lam_ref[...])
        a_s[...] = a
        b_s[...] = inp
        h0 = sv_ref[...]

        def fstep(r, h):
            h_s[pl.ds(r, 1), :] = h
            return a_s[pl.ds(r, 1), :] * h + b_s[pl.ds(r, 1), :]

        _scan_rows(t, fstep, h0)
        a = a_s[...]
        hs = a * h_s[...] + b_s[...]
        z = z_ref[...]
        d_o = do_ref[...].astype(F32)
        _, vjp_o = jax.vjp(lambda hv, zv: hv * _silu(zv), hs, z)
        dhs, dz = vjp_o(d_o)
        dz_ref[...] = dz
        g_s[...] = dhs

        def bstep(r, cr):
            g = g_s[pl.ds(r, 1), :] + cr
            g_s[pl.ds(r, 1), :] = g
            return a_s[pl.ds(r, 1), :] * g

        carry_s[...] = _scan_rows_rev(t, bstep, carry_s[...])
        g = g_s[...]
        dxc, dwr, dbr, dwi, dbi, dlam = vjp_g((g * h_s[...], g))
        dxc_ref[...] = dxc
        dwr_ref[...] += dwr
        dwi_ref[...] += dwi
        dbr_ref[...] += dbr
        dbi_ref[...] += dbi
        dlam_ref[...] += dlam

    rev = lambda i: nt - 1 - i
    tok = pl.BlockSpec((t, bw), lambda i: (rev(i), 0))
    row = pl.BlockSpec((1, bw), lambda i: (0, 0))
    mat = pl.BlockSpec((bw, bw), lambda i: (0, 0))
    sd = jax.ShapeDtypeStruct
    return _pcall(body, name=name,
                  out_shape=(sd((s, bw), F32), sd((s, bw), F32), sd((bw, bw), F32), sd((bw, bw), F32),
                             sd((1, bw), F32), sd((1, bw), F32), sd((1, bw), F32)),
                  grid=(nt,),
                  in_specs=[tok, pl.BlockSpec((t, bw), lambda i: (rev(i), z_blk)),
                            pl.BlockSpec((None, 1, bw), lambda i: (rev(i), 0, 0)), tok, mat, row, mat, row, row],
                  out_specs=(tok, tok, mat, mat, row, row, row),
                  scratch=[pltpu.VMEM((t, bw), F32)] * 4 + [pltpu.VMEM((1, bw), F32)], sem=("arbitrary",))(
                      xc, p, saves, d_o, wr, br, wi, bi, lam)


def _s5_prep(log_dt, a_re, a_im, b_re, b_im, c_re, c_im, d_skip):
    g, n = a_re.shape
    gs = d_skip.shape[1]
    gpb = LANES // gs
    nb = g // gpb
    dt = jnp.exp(log_dt)[:, None]
    mag = jnp.exp(dt * a_re)
    ab_re = mag * jnp.cos(dt * a_im)
    ab_im = mag * jnp.sin(dt * a_im)
    den = a_re * a_re + a_im * a_im
    f_re = ((ab_re - 1.0) * a_re + ab_im * a_im) / den
    f_im = (ab_im * a_re - (ab_re - 1.0) * a_im) / den
    bb_re = f_re[..., None] * b_re - f_im[..., None] * b_im
    bb_im = f_re[..., None] * b_im + f_im[..., None] * b_re
    eye = jnp.eye(gpb, dtype=F32)

    def b_dense(bb):
        t = bb.reshape(nb, gpb, n, gs)
        return jnp.einsum("bgnc,gh->bgchn", t, eye).reshape(nb, gpb * gs, gpb * n)

    def c_dense(cc):
        t = cc.reshape(nb, gpb, gs, n)
        return jnp.einsum("bgcn,gh->bgnhc", t, eye).reshape(nb, gpb * n, gpb * gs)

    lanes = gpb * n
    sub = lanes // LANES
    return (ab_re.reshape(nb, sub, LANES), ab_im.reshape(nb, sub, LANES), b_dense(bb_re), b_dense(bb_im),
            c_dense(c_re), c_dense(c_im), d_skip.reshape(1, g * gs))


def _s5_out(xre, xim, cre, cim, d, u):
    return _gelu(_bdot(xre, cre, 1, 0) - _bdot(xim, cim, 1, 0) + d * u)


def _s5_fwd(p, prep, *, bw, u_blk0, name):
    s = p.shape[0]
    are, aim, bre, bim, cre, cim, d = prep
    nb, sub, _ = are.shape
    lanes = sub * LANES
    t = _tile(s, 256, SUBLANES)
    nt = s // t

    def body(u_ref, are_ref, aim_ref, bre_ref, bim_ref, cre_ref, cim_ref, d_ref, y_ref, save_ref,
             bre_s, bim_s, xre_s, xim_s, carry_s):
        @pl.when(pl.program_id(1) == 0)
        def _():
            carry_s[...] = jnp.zeros_like(carry_s)

        u = u_ref[...]
        bre_s[...] = _bdot(u, bre_ref[...], 1, 0).reshape(t, sub, LANES)
        bim_s[...] = _bdot(u, bim_ref[...], 1, 0).reshape(t, sub, LANES)
        ar, ai = are_ref[...], aim_ref[...]
        save_ref[...] = carry_s[...]

        def step(r, cr):
            xr, xi = cr
            nr = ar * xr - ai * xi + bre_s[r]
            ni = ar * xi + ai * xr + bim_s[r]
            xre_s[r] = nr
            xim_s[r] = ni
            return nr, ni

        xr, xi = lax.fori_loop(0, t, step, (carry_s[0], carry_s[1]))
        carry_s[0] = xr
        carry_s[1] = xi
        y_ref[...] = _s5_out(xre_s[...].reshape(t, lanes), xim_s[...].reshape(t, lanes), cre_ref[...], cim_ref[...],
                             d_ref[...], u).astype(BF16)

    vec = pl.BlockSpec((None, sub, LANES), lambda j, i: (j, 0, 0))
    bmat = pl.BlockSpec((None, LANES, lanes), lambda j, i: (j, 0, 0))
    cmat = pl.BlockSpec((None, lanes, LANES), lambda j, i: (j, 0, 0))
    return _pcall(body, name=name,
                  out_shape=(jax.ShapeDtypeStruct((s, bw), BF16), jax.ShapeDtypeStruct((nb, nt, 2, sub, LANES), F32)),
                  grid=(nb, nt),
                  in_specs=[pl.BlockSpec((t, LANES), lambda j, i: (i, u_blk0 + j)), vec, vec, bmat, bmat, cmat, cmat,
                            pl.BlockSpec((1, LANES), lambda j, i: (0, j))],
                  out_specs=(pl.BlockSpec((t, LANES), lambda j, i: (i, j)),
                             pl.BlockSpec((None, None, 2, sub, LANES), lambda j, i: (j, i, 0, 0, 0))),
                  scratch=[pltpu.VMEM((t, sub, LANES), F32)] * 4 + [pltpu.VMEM((2, sub, LANES), F32)],
                  sem=("parallel", "arbitrary"))(p, are, aim, bre, bim, cre, cim, d)


def _s5_bwd(p, prep, saves, dyg, *, bw, u_blk0, name):
    s = p.shape[0]
    are, aim, bre, bim, cre, cim, d = prep
    nb, sub, _ = are.shape
    lanes = sub * LANES
    t = _tile(s, 256, SUBLANES)
    nt = s // t

    def body(u_ref, dy_ref, sv_ref, are_ref, aim_ref, bre_ref, bim_ref, cre_ref, cim_ref, d_ref,
             du_ref, dar_ref, dai_ref, dbre_ref, dbim_ref, dcre_ref, dcim_ref, dd_ref,
             bre_s, bim_s, xre_s, xim_s, carry_s):
        @pl.when(pl.program_id(1) == 0)
        def _():
            carry_s[...] = jnp.zeros_like(carry_s)
            for r in (dar_ref, dai_ref, dbre_ref, dbim_ref, dcre_ref, dcim_ref, dd_ref):
                r[...] = jnp.zeros_like(r)

        u = u_ref[...]
        bre_s[...] = _bdot(u, bre_ref[...], 1, 0).reshape(t, sub, LANES)
        bim_s[...] = _bdot(u, bim_ref[...], 1, 0).reshape(t, sub, LANES)
        ar, ai = are_ref[...], aim_ref[...]

        def fstep(r, cr):
            xr, xi = cr
            nr = ar * xr - ai * xi + bre_s[r]
            ni = ar * xi + ai * xr + bim_s[r]
            xre_s[r] = nr
            xim_s[r] = ni
            return nr, ni

        lax.fori_loop(0, t, fstep, (sv_ref[0], sv_ref[1]))
        _, vjp_o = jax.vjp(_s5_out, xre_s[...].reshape(t, lanes), xim_s[...].reshape(t, lanes), cre_ref[...],
                           cim_ref[...], d_ref[...], u)
        dxre, dxim, dcre, dcim, dd, du = vjp_o(dy_ref[...].astype(F32))
        dcre_ref[...] += dcre.astype(F32)
        dcim_ref[...] += dcim.astype(F32)
        dd_ref[...] += dd
        bre_s[...] = dxre.reshape(t, sub, LANES)
        bim_s[...] = dxim.reshape(t, sub, LANES)

        def bstep(k, cr):
            r = t - 1 - k
            gr, gi, dar, dai = cr
            gr = bre_s[r] + gr
            gi = bim_s[r] + gi
            bre_s[r] = gr
            bim_s[r] = gi
            pr = jnp.where(r == 0, sv_ref[0], xre_s[jnp.maximum(r - 1, 0)])
            pi = jnp.where(r == 0, sv_ref[1], xim_s[jnp.maximum(r - 1, 0)])
            dar = dar + gr * pr + gi * pi
            dai = dai + gi * pr - gr * pi
            return ar * gr + ai * gi, ar * gi - ai * gr, dar, dai

        zero = jnp.zeros((sub, LANES), F32)
        gr, gi, dar, dai = lax.fori_loop(0, t, bstep, (carry_s[0], carry_s[1], zero, zero))
        carry_s[0] = gr
        carry_s[1] = gi
        dar_ref[...] += dar
        dai_ref[...] += dai
        dbu_re = bre_s[...].reshape(t, lanes)
        dbu_im = bim_s[...].reshape(t, lanes)
        du_ref[...] = du + _bdot(dbu_re, bre_ref[...], 1, 1) + _bdot(dbu_im, bim_ref[...], 1, 1)
        dbre_ref[...] += _bdot(u, dbu_re, 0, 0)
        dbim_ref[...] += _bdot(u, dbu_im, 0, 0)

    rev = lambda i: nt - 1 - i
    vec = pl.BlockSpec((None, sub, LANES), lambda j, i: (j, 0, 0))
    bmat = pl.BlockSpec((None, LANES, lanes), lambda j, i: (j, 0, 0))
    cmat = pl.BlockSpec((None, lanes, LANES), lambda j, i: (j, 0, 0))
    drow = pl.BlockSpec((1, LANES), lambda j, i: (0, j))
    sd = jax.ShapeDtypeStruct
    return _pcall(body, name=name,
                  out_shape=(sd((s, bw), F32), sd(are.shape, F32), sd(aim.shape, F32), sd(bre.shape, F32),
                             sd(bim.shape, F32), sd(cre.shape, F32), sd(cim.shape, F32), sd((1, bw), F32)),
                  grid=(nb, nt),
                  in_specs=[pl.BlockSpec((t, LANES), lambda j, i: (rev(i), u_blk0 + j)),
                            pl.BlockSpec((t, LANES), lambda j, i: (rev(i), j)),
                            pl.BlockSpec((None, None, 2, sub, LANES), lambda j, i: (j, rev(i), 0, 0, 0)),
                            vec, vec, bmat, bmat, cmat, cmat, drow],
                  out_specs=(pl.BlockSpec((t, LANES), lambda j, i: (rev(i), j)), vec, vec, bmat, bmat, cmat, cmat, drow),
                  scratch=[pltpu.VMEM((t, sub, LANES), F32)] * 4 + [pltpu.VMEM((2, sub, LANES), F32)],
                  sem=("parallel", "arbitrary"))(p, dyg, saves, are, aim, bre, bim, cre, cim, d)


def _glu_gate(pre, z, bw):
    return pre[:, :bw] * _sigmoid(pre[:, bw:]) * _silu(z)


def _glu_fwd(pre, p, *, bw, z_blk, name):
    s = pre.shape[0]
    t = _tile(s, 256, SUBLANES)

    def body(pre_ref, z_ref, o_ref):
        o_ref[...] = _glu_gate(pre_ref[...], z_ref[...], bw).astype(BF16)

    return _pcall(body, name=name, out_shape=jax.ShapeDtypeStruct((s, bw), BF16), grid=(s // t,),
                  in_specs=[pl.BlockSpec((t, 2 * bw), lambda i: (i, 0)), pl.BlockSpec((t, bw), lambda i: (i, z_blk))],
                  out_specs=pl.BlockSpec((t, bw), lambda i: (i, 0)), sem=("parallel",))(pre, p)


def _glu_bwd(pre, p, d_o, *, bw, z_blk, name):
    s = pre.shape[0]
    t = _tile(s, 256, SUBLANES)

    def body(pre_ref, z_ref, do_ref, dpre_ref, dz_ref, db_ref):
        _, vjp = jax.vjp(functools.partial(_glu_gate, bw=bw), pre_ref[...], z_ref[...])
        dpre, dz = vjp(do_ref[...].astype(F32))
        dpre_ref[...] = dpre.astype(BF16)
        dz_ref[...] = dz

        @pl.when(pl.program_id(0) == 0)
        def _():
            db_ref[...] = jnp.zeros_like(db_ref)

        db_ref[...] += jnp.sum(dpre, axis=0, keepdims=True)

    sd = jax.ShapeDtypeStruct
    return _pcall(body, name=name, out_shape=(sd((s, 2 * bw), BF16), sd((s, bw), F32), sd((1, 2 * bw), F32)),
                  grid=(s // t,),
                  in_specs=[pl.BlockSpec((t, 2 * bw), lambda i: (i, 0)), pl.BlockSpec((t, bw), lambda i: (i, z_blk)),
                            pl.BlockSpec((t, bw), lambda i: (i, 0))],
                  out_specs=(pl.BlockSpec((t, 2 * bw), lambda i: (i, 0)), pl.BlockSpec((t, bw), lambda i: (i, 0)),
                             pl.BlockSpec((1, 2 * bw), lambda i: (0, 0))), sem=("arbitrary",))(pre, p, d_o)


def _attn_tile(q, z, kv, *, bw):
    hd = bw // MEM_HEADS
    outs = []
    for h in range(MEM_HEADS):
        k = kv[:, h * hd:(h + 1) * hd]
        v = kv[:, bw + h * hd:bw + (h + 1) * hd]
        sc = _bdot(q[:, h * hd:(h + 1) * hd], k, 1, 1) * (hd ** -0.5)
        e = jnp.exp(sc - lax.stop_gradient(jnp.max(sc, axis=-1, keepdims=True)))
        prob = e / jnp.sum(e, axis=-1, keepdims=True)
        outs.append(_bdot(prob, v, 1, 0))
    return jnp.concatenate(outs, axis=-1) * _silu(z)


def _attn_fwd(p, kv, *, bw, q_blk, z_blk, name):
    s = p.shape[0]
    m = kv.shape[0]
    t = _tile(s, 256, SUBLANES)

    def body(q_ref, z_ref, kv_ref, o_ref):
        o_ref[...] = _attn_tile(q_ref[...], z_ref[...], kv_ref[...], bw=bw).astype(BF16)

    return _pcall(body, name=name, out_shape=jax.ShapeDtypeStruct((s, bw), BF16), grid=(s // t,),
                  in_specs=[pl.BlockSpec((t, bw), lambda i: (i, q_blk)), pl.BlockSpec((t, bw), lambda i: (i, z_blk)),
                            pl.BlockSpec((m, 2 * bw), lambda i: (0, 0))],
                  out_specs=pl.BlockSpec((t, bw), lambda i: (i, 0)), sem=("parallel",))(p, p, kv)


def _attn_bwd(p, kv, d_o, *, bw, q_blk, z_blk, name):
    s = p.shape[0]
    m = kv.shape[0]
    t = _tile(s, 256, SUBLANES)

    def body(q_ref, z_ref, kv_ref, do_ref, dq_ref, dz_ref, dkv_ref):
        _, vjp = jax.vjp(functools.partial(_attn_tile, bw=bw), q_ref[...], z_ref[...], kv_ref[...])
        dq, dz, dkv = vjp(do_ref[...].astype(F32))
        dq_ref[...] = dq
        dz_ref[...] = dz

        @pl.when(pl.program_id(0) == 0)
        def _():
            dkv_ref[...] = jnp.zeros_like(dkv_ref)

        dkv_ref[...] += dkv

    sd = jax.ShapeDtypeStruct
    tok = pl.BlockSpec((t, bw), lambda i: (i, 0))
    return _pcall(body, name=name, out_shape=(sd((s, bw), F32), sd((s, bw), F32), sd((m, 2 * bw), F32)),
                  grid=(s // t,),
                  in_specs=[pl.BlockSpec((t, bw), lambda i: (i, q_blk)), pl.BlockSpec((t, bw), lambda i: (i, z_blk)),
                            pl.BlockSpec((m, 2 * bw), lambda i: (0, 0)), tok],
                  out_specs=(tok, tok, pl.BlockSpec((m, 2 * bw), lambda i: (0, 0))), sem=("arbitrary",))(p, p, kv, d_o)


def _merge_fwd(p, o4, wg, bg, wb, *, rank, g_blk, name):
    s = p.shape[0]
    _, bw, d = wb.shape
    tm, tn = _tile(s, 512), _tile(d, 512)

    def body(g_ref, o_ref, wg_ref, bg_ref, wb_ref, out_ref):
        g = g_ref[...]
        acc = jnp.zeros((tm, tn), F32)
        for n in range(N_BRANCH):
            gate = _sigmoid(_bdot(g, wg_ref[n], 1, 0) + bg_ref[n])
            acc = acc + gate * _bdot(o_ref[n], wb_ref[n], 1, 0)
        out_ref[...] = acc.astype(BF16)

    return _pcall(body, name=name, out_shape=jax.ShapeDtypeStruct((s, d), BF16), grid=(s // tm, d // tn),
                  in_specs=[pl.BlockSpec((tm, rank), lambda i, j: (i, g_blk)),
                            pl.BlockSpec((N_BRANCH, tm, bw), lambda i, j: (0, i, 0)),
                            pl.BlockSpec((N_BRANCH, rank, tn), lambda i, j: (0, 0, j)),
                            pl.BlockSpec((N_BRANCH, 1, tn), lambda i, j: (0, 0, j)),
                            pl.BlockSpec((N_BRANCH, bw, tn), lambda i, j: (0, 0, j))],
                  out_specs=pl.BlockSpec((tm, tn), lambda i, j: (i, j)), sem=("parallel", "parallel"))(
                      p, o4, wg, bg, wb)


def _merge_bwd(p, o4, wg, bg, wb, dmerged, *, rank, g_blk, name):
    s = p.shape[0]
    _, bw, d = wb.shape
    tm, tn = _tile(s, 512), _tile(d, 512)

    def body(g_ref, o_ref, wg_ref, bg_ref, wb_ref, dm_ref, dpre_ref, dbr_ref, dbg_ref):
        g = g_ref[...]
        dm = dm_ref[...].astype(F32)

        @pl.when(pl.program_id(1) == 0)
        def _():
            dbg_ref[...] = jnp.zeros_like(dbg_ref)

        for n in range(N_BRANCH):
            gate = _sigmoid(_bdot(g, wg_ref[n], 1, 0) + bg_ref[n])
            br = _bdot(o_ref[n], wb_ref[n], 1, 0)
            dpre = dm * br * gate * (1.0 - gate)
            dpre_ref[n] = dpre.astype(BF16)
            dbr_ref[n] = (dm * gate).astype(BF16)
            dbg_ref[n] += jnp.sum(dpre, axis=0, keepdims=True)

    sd = jax.ShapeDtypeStruct
    big = pl.BlockSpec((N_BRANCH, tm, tn), lambda j, i: (0, i, j))
    return _pcall(body, name=name,
                  out_shape=(sd((N_BRANCH, s, d), BF16), sd((N_BRANCH, s, d), BF16), sd((N_BRANCH, 1, d), F32)),
                  grid=(d // tn, s // tm),
                  in_specs=[pl.BlockSpec((tm, rank), lambda j, i: (i, g_blk)),
                            pl.BlockSpec((N_BRANCH, tm, bw), lambda j, i: (0, i, 0)),
                            pl.BlockSpec((N_BRANCH, rank, tn), lambda j, i: (0, 0, j)),
                            pl.BlockSpec((N_BRANCH, 1, tn), lambda j, i: (0, 0, j)),
                            pl.BlockSpec((N_BRANCH, bw, tn), lambda j, i: (0, 0, j)),
                            pl.BlockSpec((tm, tn), lambda j, i: (i, j))],
                  out_specs=(big, big, pl.BlockSpec((N_BRANCH, 1, tn), lambda j, i: (0, 0, j))),
                  sem=("parallel", "arbitrary"))(p, o4, wg, bg, wb, dmerged)


def _adamw(w, g, m, v, *, name):
    r, cdim = w.shape
    tr = _tile(r, 128, SUBLANES)
    bc1 = 1.0 - ADAM_B1 ** ADAM_STEP
    bc2 = 1.0 - ADAM_B2 ** ADAM_STEP

    def body(w_ref, g_ref, m_ref, v_ref, d_ref, nm_ref, nv_ref):
        gv = g_ref[...]
        nm = ADAM_B1 * m_ref[...] + (1.0 - ADAM_B1) * gv
        nv = ADAM_B2 * v_ref[...] + (1.0 - ADAM_B2) * (gv * gv)
        d_ref[...] = -ADAM_LR * ((nm / bc1) / (jnp.sqrt(nv / bc2) + ADAM_EPS) + ADAM_WD * w_ref[...])
        nm_ref[...] = nm
        nv_ref[...] = nv

    blk = pl.BlockSpec((tr, cdim), lambda i: (i, 0))
    sd = jax.ShapeDtypeStruct((r, cdim), F32)
    return _pcall(body, name=name, out_shape=(sd, sd, sd), grid=(r // tr,), in_specs=[blk] * 4, out_specs=(blk,) * 3,
                  sem=("parallel",))(w, g, m, v)


HBM_SPEC = pl.BlockSpec(memory_space=pl.ANY)


def _place():
    x, y, c = lax.axis_index("x"), lax.axis_index("y"), lax.axis_index("c")
    return x, y, c, [(1 - x, y), (x, 1 - y), (1 - x, 1 - y)]


def _rcopy(src, dst, send_sem, recv_sem, device):
    return pltpu.make_async_remote_copy(src_ref=src, dst_ref=dst, send_sem=send_sem, recv_sem=recv_sem,
                                        device_id=device, device_id_type=MESH_ID)


def _comm_call(body, *, name, out_shape, n_in, n_sems, n_local=0):
    scratch = [pltpu.SemaphoreType.DMA((n_sems,)), pltpu.SemaphoreType.DMA((n_sems,))]
    if n_local:
        scratch.append(pltpu.SemaphoreType.DMA((n_local,)))
    multi = isinstance(out_shape, (tuple, list))
    return pl.pallas_call(body, name=name, out_shape=out_shape, in_specs=[HBM_SPEC] * n_in,
                          out_specs=tuple(HBM_SPEC for _ in out_shape) if multi else HBM_SPEC,
                          scratch_shapes=scratch, interpret=False)


def _ag4(items, *, name):
    n = len(items)

    def body(*refs):
        ins, outs = refs[:n], refs[n:2 * n]
        send_sems, recv_sems, local_sems = refs[2 * n:]
        x, y, c, chips = _place()
        sibling = (x, y, 1 - c)

        def part(i, px, py, h):
            half = ins[i].shape[0] // 2
            return outs[i].at[2 * px + py, pl.ds(h * half, half)]

        local = [pltpu.make_async_copy(ins[i], outs[i].at[2 * x + y], local_sems.at[i]) for i in range(n)]
        for cp in local:
            cp.start()
        first, passed = [], []
        for i in range(n):
            half = ins[i].shape[0] // 2
            for j, chip in enumerate(chips):
                first.append(_rcopy(ins[i].at[pl.ds(c * half, half)], part(i, x, y, c), send_sems.at[6 * i + j],
                                    recv_sems.at[6 * i + j], (*chip, c)))
                passed.append(_rcopy(part(i, *chip, c), part(i, *chip, c), send_sems.at[6 * i + 3 + j],
                                     recv_sems.at[6 * i + 3 + j], sibling))
        for cp in first:
            cp.start()
        for i in range(n):
            for j, chip in enumerate(chips):
                _rcopy(part(i, *chip, c), part(i, *chip, c), send_sems.at[6 * i + j], recv_sems.at[6 * i + j],
                       (*chip, c)).wait_recv()
                passed[3 * i + j].start()
        for i in range(n):
            for j, chip in enumerate(chips):
                _rcopy(part(i, *chip, 1 - c), part(i, *chip, 1 - c), send_sems.at[6 * i + 3 + j],
                       recv_sems.at[6 * i + 3 + j], sibling).wait_recv()
        for cp in first + passed:
            cp.wait_send()
        for cp in local:
            cp.wait()

    return _comm_call(body, name=name, out_shape=tuple(jax.ShapeDtypeStruct((4,) + t.shape, t.dtype) for t in items),
                      n_in=n, n_sems=6 * n, n_local=n)(*items)


def _sib_halves(items, *, name):
    n = len(items)

    def body(*refs):
        ins, outs = refs[:n], refs[n:2 * n]
        send_sems, recv_sems = refs[2 * n:]
        x, y, c, _ = _place()
        cps = []
        for i in range(n):
            half = ins[i].shape[1] // 2
            cps.append(_rcopy(ins[i].at[:, pl.ds((1 - c) * half, half)], outs[i], send_sems.at[i], recv_sems.at[i],
                              (x, y, 1 - c)))
        for cp in cps:
            cp.start()
        for cp in cps:
            cp.wait()

    shapes = tuple(jax.ShapeDtypeStruct((4, t.shape[1] // 2) + t.shape[2:], t.dtype) for t in items)
    return _comm_call(body, name=name, out_shape=shapes, n_in=n, n_sems=n)(*items)


def _to_chips(items, *, name):
    n = len(items)

    def body(*refs):
        ins, outs = refs[:n], refs[n:2 * n]
        send_sems, recv_sems = refs[2 * n:]
        x, y, c, chips = _place()
        cps = [_rcopy(ins[i].at[2 * chip[0] + chip[1]], outs[i].at[j], send_sems.at[3 * i + j],
                      recv_sems.at[3 * i + j], (*chip, c)) for i in range(n) for j, chip in enumerate(chips)]
        for cp in cps:
            cp.start()
        for cp in cps:
            cp.wait()

    shapes = tuple(jax.ShapeDtypeStruct((3,) + t.shape[1:], t.dtype) for t in items)
    return _comm_call(body, name=name, out_shape=shapes, n_in=n, n_sems=3 * n)(*items)


def _join_halves(items, *, name):
    n = len(items)

    def body(*refs):
        ins, outs = refs[:n], refs[n:2 * n]
        send_sems, recv_sems, local_sems = refs[2 * n:]
        x, y, c, _ = _place()
        sibling = (x, y, 1 - c)
        local, cps = [], []
        for i in range(n):
            a = ins[i].shape[0]
            local.append(pltpu.make_async_copy(ins[i], outs[i].at[pl.ds(c * a, a)], local_sems.at[i]))
            cps.append(_rcopy(ins[i], outs[i].at[pl.ds(c * a, a)], send_sems.at[i], recv_sems.at[i], sibling))
        for cp in local + cps:
            cp.start()
        for i in range(n):
            a = ins[i].shape[0]
            cps[i].wait_send()
            _rcopy(ins[i], outs[i].at[pl.ds((1 - c) * a, a)], send_sems.at[i], recv_sems.at[i], sibling).wait_recv()
        for cp in local:
            cp.wait()

    shapes = tuple(jax.ShapeDtypeStruct((2 * t.shape[0],) + t.shape[1:], t.dtype) for t in items)
    return _comm_call(body, name=name, out_shape=shapes, n_in=n, n_sems=n, n_local=n)(*items)


def _swap_sibling(buf, *, name):
    def body(b_ref, recv_ref, send_sems, recv_sems):
        x, y, c, _ = _place()
        cp = _rcopy(b_ref, recv_ref, send_sems.at[0], recv_sems.at[0], (x, y, 1 - c))
        cp.start()
        cp.wait()

    return _comm_call(body, name=name, out_shape=jax.ShapeDtypeStruct(buf.shape, buf.dtype), n_in=1, n_sems=1)(buf)


def _gather_chips(buf, *, name):
    n, cdim = buf.shape

    def body(b_ref, out_ref, send_sems, recv_sems, local_sems):
        x, y, c, chips = _place()
        mine = pltpu.make_async_copy(b_ref, out_ref.at[2 * x + y], local_sems.at[0])
        mine.start()
        cps = [_rcopy(b_ref, out_ref.at[2 * x + y], send_sems.at[j], recv_sems.at[j], (*chip, c))
               for j, chip in enumerate(chips)]
        for cp in cps:
            cp.start()
        for j, chip in enumerate(chips):
            slot = out_ref.at[2 * chip[0] + chip[1]]
            _rcopy(slot, slot, send_sems.at[j], recv_sems.at[j], (*chip, c)).wait_recv()
        for cp in cps:
            cp.wait_send()
        mine.wait()

    return _comm_call(body, name=name, out_shape=jax.ShapeDtypeStruct((4, n, cdim), buf.dtype), n_in=1, n_sems=3,
                      n_local=1)(buf)


def _rows_per_block(rows, cdim, itemsize, align):
    return _tile(rows, max(align, ELEMENTWISE_BLOCK_BYTES // (cdim * itemsize) // align * align), align)


def _sum_sib(g4, recv, c_idx, *, name):
    _, rows, cdim = g4.shape
    half = rows // 2
    tr = _rows_per_block(half, cdim, 2, 16)
    nh = half // tr

    def body(c_ref, g_ref, r_ref, o_ref):
        o_ref[...] = (g_ref[...].astype(F32) + r_ref[...].astype(F32)).astype(o_ref.dtype)

    return _pcall(body, name=name, out_shape=jax.ShapeDtypeStruct((4, half, cdim), g4.dtype), grid=(4, nh),
                  num_prefetch=1,
                  in_specs=[pl.BlockSpec((None, tr, cdim), lambda k, i, c_ref: (k, c_ref[0] * nh + i, 0)),
                            pl.BlockSpec((None, tr, cdim), lambda k, i, c_ref: (k, i, 0))],
                  out_specs=pl.BlockSpec((None, tr, cdim), lambda k, i, c_ref: (k, i, 0)),
                  sem=("parallel", "parallel"))(c_idx, g4, recv)


def _sum_chips(h4, recv3, chip_idx, *, name):
    _, n, cdim = h4.shape
    tr = _rows_per_block(n, cdim, 4, 16)

    def body(k_ref, h_ref, r_ref, o_ref):
        acc = h_ref[...].astype(F32)
        for j in range(3):
            acc = acc + r_ref[j].astype(F32)
        o_ref[...] = acc

    return _pcall(body, name=name, out_shape=jax.ShapeDtypeStruct((n, cdim), F32), grid=(n // tr,), num_prefetch=1,
                  in_specs=[pl.BlockSpec((None, tr, cdim), lambda i, k_ref: (k_ref[0], i, 0)),
                            pl.BlockSpec((3, tr, cdim), lambda i, k_ref: (0, i, 0))],
                  out_specs=pl.BlockSpec((tr, cdim), lambda i, k_ref: (i, 0)), sem=("parallel",))(chip_idx, h4, recv3)


def _add2(a, b, *, name):
    n, cdim = a.shape
    tr = _tile(n, 256, SUBLANES)

    def body(a_ref, b_ref, o_ref):
        o_ref[...] = a_ref[...] + b_ref[...]

    blk = pl.BlockSpec((tr, cdim), lambda i: (i, 0))
    return _pcall(body, name=name, out_shape=jax.ShapeDtypeStruct((n, cdim), F32), grid=(n // tr,), in_specs=[blk, blk],
                  out_specs=blk, sem=("parallel",))(a, b)


def _sum4(x4, *, name):
    _, n, cdim = x4.shape
    tr = _tile(n, 256, SUBLANES)

    def body(x_ref, o_ref):
        o_ref[...] = ((x_ref[0] + x_ref[1]) + x_ref[2]) + x_ref[3]

    return _pcall(body, name=name, out_shape=jax.ShapeDtypeStruct((n, cdim), F32), grid=(n // tr,),
                  in_specs=[pl.BlockSpec((4, tr, cdim), lambda i: (0, i, 0))],
                  out_specs=pl.BlockSpec((tr, cdim), lambda i: (i, 0)), sem=("parallel",))(x4)


def _reduce_scatter(items, c_idx, chip_idx, *, name):
    n = len(items)
    recv = _sib_halves(items, name=name + "_sib")
    h4 = [_sum_sib(items[i], recv[i], c_idx, name=f"{name}_sum1_{i}") for i in range(n)]
    recv3 = _to_chips(h4, name=name + "_ici")
    q = [_sum_chips(h4[i], recv3[i], chip_idx, name=f"{name}_sum2_{i}") for i in range(n)]
    return _join_halves(q, name=name + "_join")


def _all_reduce(buf, *, name):
    pair = _add2(buf, _swap_sibling(buf, name=name + "_sib"), name=name + "_add")
    return _sum4(_gather_chips(pair, name=name + "_ici"), name=name + "_sum")


def _pack_rows(flat, lead, align):
    n = flat.shape[-1]
    unit = PACK_COLS * align
    total = -(-n // unit) * unit
    flat = jnp.pad(flat, [(0, 0)] * len(lead) + [(0, total - n)])
    return flat.reshape(*lead, total // PACK_COLS, PACK_COLS)


BIG = (("w_in", 1), ("ssm_w_glu", 1), ("w_kv", 0), ("w_gate", 2), ("w_branch", 2), ("w_out", 0))
SMALL_SHARDED = (("dn_conv_w", 1), ("lru_conv_w", 1), ("b_gate", 1))
SMALL = ("norm_w", "dn_a_log", "dn_dt_bias", "dn_norm_w", "lru_conv_b", "lru_w_r", "lru_b_r", "lru_w_i", "lru_b_i",
         "lru_lambda", "ssm_log_dt", "ssm_a_re", "ssm_a_im", "ssm_b_re", "ssm_b_im", "ssm_c_re", "ssm_c_im", "ssm_d",
         "ssm_b_glu", "mem_norm_w")
WEIGHTS = ("norm_w", "w_in", "dn_conv_w", "dn_a_log", "dn_dt_bias", "dn_norm_w", "lru_conv_w", "lru_conv_b",
           "lru_w_r", "lru_b_r", "lru_w_i", "lru_b_i", "lru_lambda", "ssm_log_dt", "ssm_a_re", "ssm_a_im", "ssm_b_re",
           "ssm_b_im", "ssm_c_re", "ssm_c_im", "ssm_d", "ssm_w_glu", "ssm_b_glu", "mem_norm_w", "w_kv", "w_gate",
           "b_gate", "w_branch", "w_out", "final_norm_w")


def _gather_layer(wts, l):
    small = _pack_rows(jnp.concatenate([wts[n][l].reshape(-1) for n, _ in SMALL_SHARDED]), (), 2 * SUBLANES)
    g = _ag4([wts[n][l].astype(BF16) for n, _ in BIG] + [small], name=f"gather_w{l}")
    out = {n: jnp.concatenate(list(g[i]), axis=ax) for i, (n, ax) in enumerate(BIG)}
    flat, off = g[-1].reshape(4, -1), 0
    for n, ax in SMALL_SHARDED:
        shp = wts[n].shape[1:]
        sz = math.prod(shp)
        out[n] = jnp.concatenate(list(flat[:, off:off + sz].reshape(4, *shp)), axis=ax)
        off += sz
    return out


def _scatter_layer(big, c_idx, chip_idx, l):
    d, d_in = big["w_in"].shape
    chip_rows = lambda t: t.reshape(4, t.shape[0] // 4, t.shape[1])
    items = [big["w_in"].reshape(d, 4, d_in // 4).transpose(1, 0, 2), big["ssm_w_glu"], chip_rows(big["w_kv"]),
             *big["w_gate"], *big["w_branch"], chip_rows(big["w_out"])]
    r = _reduce_scatter(items, c_idx, chip_idx, name=f"scatter_g{l}")
    ng, nb = len(big["w_gate"]), len(big["w_branch"])
    return {"w_in": r[0], "ssm_w_glu": r[1], "w_kv": r[2], "w_gate": jnp.stack(r[3:3 + ng], axis=0),
            "w_branch": jnp.stack(r[3 + ng:3 + ng + nb], axis=0), "w_out": r[3 + ng + nb]}


def _w_in_layout(w, bw, heads, rank):
    d = w.shape[0]
    ba = 4 * bw
    rest = ba + 2 * heads
    return jnp.concatenate([w[:, :ba], w[:, rest:], w[:, ba:rest], jnp.zeros((d, BA_PAD - 2 * heads), w.dtype)], axis=1)


def _w_in_unlayout(dw, bw, heads, rank):
    ba = 4 * bw
    tail = 10 * bw + rank
    return jnp.concatenate([dw[:, :ba], dw[:, tail:tail + 2 * heads], dw[:, ba:tail]], axis=1)


def _lru_dense(w):
    nb, blk, _ = w.shape
    return jnp.einsum("nij,nm->nimj", w, jnp.eye(nb, dtype=w.dtype)).reshape(nb * blk, nb * blk)


def _w8(w):
    return jnp.concatenate([w, jnp.zeros((SUBLANES - CONV_WIDTH, w.shape[1]), w.dtype)], axis=0)


def _layer_fwd(x, mem, full, prm, l):
    s, d = x.shape
    bw = d // N_BRANCH
    heads = bw // DN_HEAD_DIM
    rank = full["w_gate"].shape[1]
    tag = f"l{l}_"
    sv = {"x": x}
    w_in = _w_in_layout(full["w_in"], bw, heads, rank)
    sv["w_in"] = w_in
    h = _rms_fwd(x, prm["norm_w"], name=tag + "norm")
    p = _mm(h, w_in, name=tag + "in_proj")
    sv["h"], sv["p"] = h, p
    conv_a = _conv_fwd(p, 0, 3 * bw, _w8(full["dn_conv_w"]), jnp.zeros((1, 3 * bw), F32), name=tag + "dn_conv")
    ba_blk = (10 * bw + rank) // BA_PAD
    o_a, dn_states = _dn_fwd(p, conv_a, prm["dn_a_log"], prm["dn_dt_bias"], prm["dn_norm_w"], bw=bw, ba_blk=ba_blk,
                             name=tag + "dn")
    sv["conv_a"], sv["dn_states"] = conv_a, dn_states
    xc = _conv_fwd(p, 4, bw, _w8(full["lru_conv_w"]), prm["lru_conv_b"].reshape(1, bw), name=tag + "lru_conv")
    (wr, wi), lru_vjp = jax.vjp(lambda a, b: (_lru_dense(a), _lru_dense(b)), prm["lru_w_r"], prm["lru_w_i"])
    row = lambda v: v.reshape(1, bw)
    lru_args = (wr, row(prm["lru_b_r"]), wi, row(prm["lru_b_i"]), row(prm["lru_lambda"]))
    o_b, lru_saves = _lru_fwd(p, xc, *lru_args, bw=bw, z_blk=5, name=tag + "lru")
    sv["xc"], sv["lru_saves"], sv["lru_args"], sv["lru_vjp"] = xc, lru_saves, lru_args, lru_vjp
    prep, s5_vjp = jax.vjp(_s5_prep, prm["ssm_log_dt"], prm["ssm_a_re"], prm["ssm_a_im"], prm["ssm_b_re"],
                           prm["ssm_b_im"], prm["ssm_c_re"], prm["ssm_c_im"], prm["ssm_d"])
    u_blk0 = 6 * bw // LANES
    yg, s5_saves = _s5_fwd(p, prep, bw=bw, u_blk0=u_blk0, name=tag + "s5")
    pre = _mm(yg, full["ssm_w_glu"], bias=prm["ssm_b_glu"].reshape(1, 2 * bw), name=tag + "glu_proj")
    o_c = _glu_fwd(pre, p, bw=bw, z_blk=7, name=tag + "glu")
    sv["prep"], sv["s5_vjp"], sv["s5_saves"], sv["yg"], sv["pre"] = prep, s5_vjp, s5_saves, yg, pre
    mem_n = _rms_fwd(mem, prm["mem_norm_w"], name=tag + "mem_norm")
    kv = _mm(mem_n, full["w_kv"], name=tag + "kv_proj")
    o_d = _attn_fwd(p, kv, bw=bw, q_blk=8, z_blk=9, name=tag + "attn")
    sv["mem_n"], sv["kv"] = mem_n, kv
    o4 = jnp.stack([o_a, o_b, o_c, o_d], axis=0)
    bg = full["b_gate"].reshape(N_BRANCH, 1, d)
    g_blk = 10 * bw // rank
    merged = _merge_fwd(p, o4, full["w_gate"], bg, full["w_branch"], rank=rank, g_blk=g_blk, name=tag + "merge")
    sv["o4"], sv["bg"], sv["merged"] = o4, bg, merged
    return _mm(merged, full["w_out"], add=x, name=tag + "out_proj"), sv


def _layer_bwd(dx_out, mem, sv, full, prm, l):
    x, p, h = sv["x"], sv["p"], sv["h"]
    s, d = x.shape
    bw = d // N_BRANCH
    heads = bw // DN_HEAD_DIM
    rank = full["w_gate"].shape[1]
    tag = f"l{l}b_"
    big, small = {}, {}
    dmerged = _mm(dx_out, full["w_out"], tb=True, out_dtype=BF16, name=tag + "out_dx")
    big["w_out"] = _mm(sv["merged"], dx_out, ta=True, out_dtype=BF16, name=tag + "out_dw")
    g_blk = 10 * bw // rank
    dpre, dbr, dbg = _merge_bwd(p, sv["o4"], full["w_gate"], sv["bg"], full["w_branch"], dmerged, rank=rank,
                                g_blk=g_blk, name=tag + "merge")
    small["b_gate"] = dbg.reshape(N_BRANCH, d)
    glow = p[:, 10 * bw:10 * bw + rank].astype(BF16)
    dglow = None
    dwg, dwb, d_o = [], [], []
    for n in range(N_BRANCH):
        dglow = _mm(dpre, full["w_gate"], la=n, lb=n, tb=True, add=dglow, name=tag + f"gate_dx{n}")
        dwg.append(_mm(glow, dpre, ta=True, lb=n, out_dtype=BF16, nsplit=4, name=tag + f"gate_dw{n}"))
        d_o.append(_mm(dbr, full["w_branch"], la=n, lb=n, tb=True, name=tag + f"branch_dx{n}"))
        dwb.append(_mm(sv["o4"], dbr, ta=True, la=n, lb=n, out_dtype=BF16, nsplit=4, name=tag + f"branch_dw{n}"))
    big["w_gate"] = dwg
    big["w_branch"] = dwb
    ba_blk = (10 * bw + rank) // BA_PAD
    dconv, dz_a, dba, dal, ddt, dnw = _dn_bwd(p, sv["conv_a"], sv["dn_states"], d_o[0], prm["dn_a_log"],
                                              prm["dn_dt_bias"], prm["dn_norm_w"], bw=bw, ba_blk=ba_blk,
                                              name=tag + "dn")
    small["dn_a_log"] = dal[0, heads:2 * heads]
    small["dn_dt_bias"] = ddt[0, heads:2 * heads]
    small["dn_norm_w"] = dnw[0]
    dqkv, dw8_a, _ = _conv_bwd(p, 0, 3 * bw, _w8(full["dn_conv_w"]), dconv, name=tag + "dn_conv")
    small["dn_conv_w"] = dw8_a[:CONV_WIDTH]
    dxc, dz_b, dwr, dwi, dbr_, dbi_, dlam = _lru_bwd(p, sv["xc"], sv["lru_saves"], d_o[1], *sv["lru_args"], bw=bw,
                                                     z_blk=5, name=tag + "lru")
    small["lru_w_r"], small["lru_w_i"] = sv["lru_vjp"]((dwr, dwi))
    small["lru_b_r"], small["lru_b_i"], small["lru_lambda"] = dbr_[0], dbi_[0], dlam[0]
    dlx, dw8_b, dcb = _conv_bwd(p, 4, bw, _w8(full["lru_conv_w"]), dxc, name=tag + "lru_conv")
    small["lru_conv_w"] = dw8_b[:CONV_WIDTH]
    small["lru_conv_b"] = dcb[0]
    dpre_glu, dz_c, dbglu = _glu_bwd(sv["pre"], p, d_o[2], bw=bw, z_blk=7, name=tag + "glu")
    small["ssm_b_glu"] = dbglu[0]
    dyg = _mm(dpre_glu, full["ssm_w_glu"], tb=True, name=tag + "glu_dx")
    big["ssm_w_glu"] = _mm(sv["yg"], dpre_glu, ta=True, out_dtype=BF16, nsplit=4, name=tag + "glu_dw")
    s5 = _s5_bwd(p, sv["prep"], sv["s5_saves"], dyg, bw=bw, u_blk0=6 * bw // LANES, name=tag + "s5")
    du = s5[0]
    (small["ssm_log_dt"], small["ssm_a_re"], small["ssm_a_im"], small["ssm_b_re"], small["ssm_b_im"],
     small["ssm_c_re"], small["ssm_c_im"], small["ssm_d"]) = sv["s5_vjp"](tuple(s5[1:]))
    dq, dz_d, dkv = _attn_bwd(p, sv["kv"], d_o[3], bw=bw, q_blk=8, z_blk=9, name=tag + "attn")
    big["w_kv"] = _mm(sv["mem_n"], dkv, ta=True, out_dtype=BF16, name=tag + "kv_dw")
    dmem_n = _mm(dkv, full["w_kv"], tb=True, name=tag + "kv_dx")
    _, dmnw = _rms_bwd(mem, prm["mem_norm_w"], dmem_n, None, name=tag + "mem_norm")
    small["mem_norm_w"] = dmnw[0]
    dp = jnp.concatenate([dqkv, dz_a, dlx, dz_b, du, dz_c, dq, dz_d, dglow, dba], axis=1).astype(BF16)
    dh = _mm(dp, sv["w_in"], tb=True, name=tag + "in_dx")
    big["w_in"] = _w_in_unlayout(_mm(h, dp, ta=True, out_dtype=BF16, name=tag + "in_dw"), bw, heads, rank)
    dx, dnw_in = _rms_bwd(x, prm["norm_w"], dh, dx_out, name=tag + "norm")
    small["norm_w"] = dnw_in[0]
    return dx, big, small


def _step(wts, mom, vel, x, mem, target):
    depth = wts["norm_w"].shape[0]
    xi, yi, ci = lax.axis_index("x"), lax.axis_index("y"), lax.axis_index("c")
    c_idx = ci.astype(jnp.int32).reshape(1)
    chip = (2 * xi + yi).astype(jnp.int32)
    chip_idx = chip.reshape(1)
    x, mem, target = x[0], mem[0], target[0]

    fulls = [_gather_layer(wts, l) for l in range(depth)]
    prms = [{n: wts[n][l] for n in SMALL} for l in range(depth)]
    saves = []
    act = x
    for l in range(depth):
        act, sv = _layer_fwd(act, mem, fulls[l], prms[l], l)
        saves.append(sv)
    loss_part, dx, dfw = _loss_head(act, wts["final_norm_w"], target, name="loss_head")
    loss = lax.psum(loss_part[0, 0], ("x", "y", "c"))

    big_g = [None] * depth
    small_g = [None] * depth
    for l in reversed(range(depth)):
        dx, big, small_g[l] = _layer_bwd(dx, mem, saves[l], fulls[l], prms[l], l)
        big_g[l] = _scatter_layer(big, c_idx, chip_idx, l)

    names = SMALL + tuple(n for n, _ in SMALL_SHARDED)
    flat = jnp.concatenate([small_g[l][n].reshape(-1) for l in range(depth) for n in names] + [dfw.reshape(-1)])
    red = _all_reduce(_pack_rows(flat, (), 256), name="reduce_small").reshape(-1)
    grads, off = {n: [] for n in names}, 0
    for l in range(depth):
        for n in names:
            shp = small_g[l][n].shape
            sz = math.prod(shp)
            grads[n].append(red[off:off + sz].reshape(shp))
            off += sz
    grads = {n: jnp.stack(v, axis=0) for n, v in grads.items()}
    grads["final_norm_w"] = red[off:off + dfw.size].reshape(wts["final_norm_w"].shape)
    for n, ax in SMALL_SHARDED:
        width = wts[n].shape[-1]
        grads[n] = lax.dynamic_slice_in_dim(grads[n], chip * width, width, axis=ax + 1)
    for n, _ in BIG:
        grads[n] = jnp.stack([big_g[l][n] for l in range(depth)], axis=0)

    delta, new_m, new_v = {}, {}, {}
    for n, _ in BIG:
        shp = wts[n].shape
        two = lambda t: t.reshape(-1, shp[-1])
        dlt, nm, nv = _adamw(two(wts[n]), two(grads[n]), two(mom[n]), two(vel[n]), name="adamw_" + n)
        delta[n], new_m[n], new_v[n] = dlt.reshape(shp), nm.reshape(shp), nv.reshape(shp)
    rest = [n for n in WEIGHTS if n not in dict(BIG)]
    cat = lambda src: _pack_rows(jnp.concatenate([src[n].reshape(-1) for n in rest]), (), SUBLANES)
    dlt, nm, nv = _adamw(cat(wts), cat(grads), cat(mom), cat(vel), name="adamw_small")
    off = 0
    for n in rest:
        shp = wts[n].shape
        sz = math.prod(shp)
        for dst, src in ((delta, dlt), (new_m, nm), (new_v, nv)):
            dst[n] = src.reshape(-1)[off:off + sz].reshape(shp)
        off += sz
    return (loss, dx[None], *[grads[n] for n in WEIGHTS], *[delta[n] for n in WEIGHTS], *[new_m[n] for n in WEIGHTS],
            *[new_v[n] for n in WEIGHTS])


def kernel(x, mem, norm_w, w_in, dn_conv_w, dn_a_log, dn_dt_bias, dn_norm_w, lru_conv_w, lru_conv_b, lru_w_r, lru_b_r, lru_w_i, lru_b_i, lru_lambda, ssm_log_dt, ssm_a_re, ssm_a_im, ssm_b_re, ssm_b_im, ssm_c_re, ssm_c_im, ssm_d, ssm_w_glu, ssm_b_glu, mem_norm_w, w_kv, w_gate, b_gate, w_branch, w_out, final_norm_w, loss_target, m_norm_w, m_w_in, m_dn_conv_w, m_dn_a_log, m_dn_dt_bias, m_dn_norm_w, m_lru_conv_w, m_lru_conv_b, m_lru_w_r, m_lru_b_r, m_lru_w_i, m_lru_b_i, m_lru_lambda, m_ssm_log_dt, m_ssm_a_re, m_ssm_a_im, m_ssm_b_re, m_ssm_b_im, m_ssm_c_re, m_ssm_c_im, m_ssm_d, m_ssm_w_glu, m_ssm_b_glu, m_mem_norm_w, m_w_kv, m_w_gate, m_b_gate, m_w_branch, m_w_out, m_final_norm_w, v_norm_w, v_w_in, v_dn_conv_w, v_dn_a_log, v_dn_dt_bias, v_dn_norm_w, v_lru_conv_w, v_lru_conv_b, v_lru_w_r, v_lru_b_r, v_lru_w_i, v_lru_b_i, v_lru_lambda, v_ssm_log_dt, v_ssm_a_re, v_ssm_a_im, v_ssm_b_re, v_ssm_b_im, v_ssm_c_re, v_ssm_c_im, v_ssm_d, v_ssm_w_glu, v_ssm_b_glu, v_mem_norm_w, v_w_kv, v_w_gate, v_b_gate, v_w_branch, v_w_out, v_final_norm_w):
    given = dict(locals())
    wts = {n: given[n] for n in WEIGHTS}
    mom = {n: given["m_" + n] for n in WEIGHTS}
    vel = {n: given["v_" + n] for n in WEIGHTS}
    return _step(wts, mom, vel, x, mem, loss_target)
```

```python
import functools
import math

import jax
import jax.numpy as jnp
import numpy as np
from jax import lax
from jax.experimental import pallas as pl
from jax.experimental.pallas import tpu as pltpu

F32 = jnp.float32
BF16 = jnp.bfloat16
HIGHEST = lax.Precision.HIGHEST
MESH_ID = pl.DeviceIdType.MESH

NORM_EPS = 1e-6
CONV_WIDTH = 4
DN_HEAD_DIM = 128
DN_CHUNK = 64
LRU_C = 8.0
MEM_HEADS = 4
N_BRANCH = 4
ADAM_LR, ADAM_B1, ADAM_B2, ADAM_EPS, ADAM_WD, ADAM_STEP = 0.001, 0.9, 0.999, 1e-08, 0.01, 10

LANES = 128
SUBLANES = 8
VMEM_LIMIT = 56 * 2 ** 20
PACK_COLS = 1024
ELEMENTWISE_BLOCK_BYTES = 2 * 2 ** 20
BA_PAD = 256


def _tile(n, pref, align=LANES):
    if n <= pref:
        return n
    t = pref - pref % align
    while t > 0:
        if n % t == 0:
            return t
        t -= align
    return n


class _Comm:
    def __init__(self, inputs, out_shapes, n_sems, n_local, start, finish):
        self.inputs, self.out_shapes, self.n_sems, self.n_local = list(inputs), tuple(out_shapes), n_sems, n_local
        self.start, self.finish = start, finish

    def scratch(self):
        s = [pltpu.SemaphoreType.DMA((self.n_sems,)), pltpu.SemaphoreType.DMA((self.n_sems,))]
        return s + ([pltpu.SemaphoreType.DMA((self.n_local,))] if self.n_local else [])

    def split(self, refs):
        ni, no = len(self.inputs), len(self.out_shapes)
        sems = list(refs[ni + no:]) + ([] if self.n_local else [None])
        return (refs[:ni], refs[ni:ni + no], *sems)


def _pcall(body, *, name, out_shape, grid=(), in_specs=None, out_specs=None, scratch=(), sem=None,
           num_prefetch=0, comm=None):
    params = dict(vmem_limit_bytes=VMEM_LIMIT)
    if sem is not None:
        params["dimension_semantics"] = sem if comm is None else ("arbitrary",) * len(grid)
    scratch = list(scratch)
    if comm is None:
        run_body = body
    else:
        assert not num_prefetch
        single = not isinstance(out_shape, (tuple, list))
        outs = (out_shape,) if single else tuple(out_shape)
        ospecs = (out_specs,) if single else tuple(out_specs)
        n_in, n_out, n_scr = len(in_specs), len(outs), len(scratch)
        n_ci, n_co = len(comm.inputs), len(comm.out_shapes)
        in_specs = list(in_specs) + [HBM_SPEC] * n_ci
        out_shape = outs + comm.out_shapes
        out_specs = ospecs + (HBM_SPEC,) * n_co
        scratch = scratch + comm.scratch()

        def run_body(*refs):
            ins, rest = refs[:n_in], refs[n_in:]
            cins, rest = rest[:n_ci], rest[n_ci:]
            o, rest = rest[:n_out], rest[n_out:]
            couts, rest = rest[:n_co], rest[n_co:]
            cargs = comm.split((*cins, *couts, *rest[n_scr:]))
            first = functools.reduce(jnp.logical_and, [pl.program_id(a) == 0 for a in range(len(grid))])
            last = functools.reduce(jnp.logical_and, [pl.program_id(a) == grid[a] - 1 for a in range(len(grid))])

            @pl.when(first)
            def _():
                comm.start(*cargs)

            body(*ins, *o, *rest[:n_scr])

            @pl.when(last)
            def _():
                comm.finish(*cargs)

    if num_prefetch:
        call = pl.pallas_call(
            run_body, name=name, out_shape=out_shape,
            grid_spec=pltpu.PrefetchScalarGridSpec(num_scalar_prefetch=num_prefetch, grid=grid, in_specs=in_specs,
                                                   out_specs=out_specs, scratch_shapes=scratch),
            compiler_params=pltpu.CompilerParams(**params), interpret=False)
    else:
        call = pl.pallas_call(run_body, name=name, out_shape=out_shape, grid=grid, in_specs=in_specs,
                              out_specs=out_specs, scratch_shapes=scratch,
                              compiler_params=pltpu.CompilerParams(**params), interpret=False)
    if comm is None:
        return call

    def run(*operands):
        res = call(*operands, *comm.inputs)
        return (res[0] if single else tuple(res[:n_out])), tuple(res[n_out:])

    return run


def _run_comm(comm, *, name):
    def body(*refs):
        args = comm.split(refs)
        comm.start(*args)
        comm.finish(*args)

    return pl.pallas_call(body, name=name, out_shape=comm.out_shapes, in_specs=[HBM_SPEC] * len(comm.inputs),
                          out_specs=tuple(HBM_SPEC for _ in comm.out_shapes), scratch_shapes=comm.scratch(),
                          interpret=False)(*comm.inputs)


HBM_SPEC = pl.BlockSpec(memory_space=pl.ANY)


@functools.partial(jax.custom_vjp, nondiff_argnums=(2, 3))
def _bdot(a, b, ca, cb):
    return lax.dot_general(a.astype(BF16), b.astype(BF16), (((ca,), (cb,)), ((), ())), preferred_element_type=F32)


def _bdot_fwd(a, b, ca, cb):
    return _bdot(a, b, ca, cb), (a, b)


def _bdot_bwd(ca, cb, res, ct):
    a, b = res
    da = _bdot(ct, b, 1, 1 - cb) if ca == 1 else _bdot(b, ct, 1 - cb, 1)
    db = _bdot(a, ct, 1 - ca, 0) if cb == 0 else _bdot(ct, a, 0, 1 - ca)
    return da.astype(a.dtype), db.astype(b.dtype)


_bdot.defvjp(_bdot_fwd, _bdot_bwd)


def _split_bf16(a):
    hi = a.astype(BF16)
    return hi, (a - hi.astype(F32)).astype(BF16)


@functools.partial(jax.custom_vjp, nondiff_argnums=(2, 3))
def _xdot(a, b, ca, cb):
    dims = (((ca,), (cb,)), ((), ()))
    ah, al = _split_bf16(a)
    bh, bl = _split_bf16(b)
    dot = lambda p, q: lax.dot_general(p, q, dims, preferred_element_type=F32)
    return dot(ah, bh) + (dot(ah, bl) + dot(al, bh))


def _xdot_fwd(a, b, ca, cb):
    return _xdot(a, b, ca, cb), (a, b)


def _xdot_bwd(ca, cb, res, ct):
    a, b = res
    da = _xdot(ct, b, 1, 1 - cb) if ca == 1 else _xdot(b, ct, 1 - cb, 1)
    db = _xdot(a, ct, 1 - ca, 0) if cb == 0 else _xdot(ct, a, 0, 1 - ca)
    return da, db


_xdot.defvjp(_xdot_fwd, _xdot_bwd)


def _sigmoid(x):
    return 1.0 / (1.0 + jnp.exp(-x))


def _silu(x):
    return x * _sigmoid(x)


def _softplus(x):
    return jnp.maximum(x, 0.0) + jnp.log(1.0 + jnp.exp(-jnp.abs(x)))


def _expm1(x):
    small = x * (1.0 + x * (0.5 + x * (1.0 / 6.0 + x * (1.0 / 24.0 + x * (1.0 / 120.0 + x * (1.0 / 720.0))))))
    return jnp.where(jnp.abs(x) < 0.1, small, jnp.exp(x) - 1.0)


def _gelu(x):
    return 0.5 * x * (1.0 + jnp.tanh(math.sqrt(2.0 / math.pi) * (x + 0.044715 * x * x * x)))


def _rms(x, w):
    var = jnp.mean(x * x, axis=-1, keepdims=True)
    return x * lax.rsqrt(var + NORM_EPS) * w


def _pick_lane(v, idx):
    lane = lax.broadcasted_iota(jnp.int32, v.shape, 1)
    return jnp.sum(jnp.where(lane == idx, v, 0.0), axis=1, keepdims=True)


def _pick_row(v, idx):
    row = lax.broadcasted_iota(jnp.int32, v.shape, 0)
    return jnp.sum(jnp.where(row == idx, v, 0.0), axis=0, keepdims=True)


def _mm(a, b, *, name, ta=False, tb=False, out_dtype=F32, add=None, bias=None, la=None, lb=None, nsplit=None,
        comm=None, tm=1024, tn=1024, tk=512):
    a2 = a.shape[-2:]
    b2 = b.shape[-2:]
    m, k = (a2[1], a2[0]) if ta else a2
    n = b2[0] if tb else b2[1]
    assert (b2[1] if tb else b2[0]) == k
    tm, tn, tk = _tile(m, tm), _tile(n // (nsplit or 1), tn), _tile(k, tk)
    nk = k // tk

    def a_map(i, j, kk):
        idx = (kk, i) if ta else (i, kk)
        return idx if la is None else (la,) + idx

    def b_map(i, j, kk):
        idx = (j, kk) if tb else (kk, j)
        return idx if lb is None else (lb,) + idx

    a_blk = (tk, tm) if ta else (tm, tk)
    b_blk = (tn, tk) if tb else (tk, tn)
    in_specs = [pl.BlockSpec(a_blk if la is None else (None,) + a_blk, a_map),
                pl.BlockSpec(b_blk if lb is None else (None,) + b_blk, b_map)]
    operands = [a, b]
    if add is not None:
        in_specs.append(pl.BlockSpec((tm, tn), lambda i, j, kk: (i, j)))
        operands.append(add)
    if bias is not None:
        in_specs.append(pl.BlockSpec((1, tn), lambda i, j, kk: (0, j)))
        operands.append(bias)
    dims = (((0 if ta else 1,), (1 if tb else 0,)), ((), ()))

    def body(*refs):
        a_ref, b_ref = refs[0], refs[1]
        rest = list(refs[2:])
        add_ref = rest.pop(0) if add is not None else None
        bias_ref = rest.pop(0) if bias is not None else None
        o_ref, acc_ref = rest
        kk = pl.program_id(2)

        @pl.when(kk == 0)
        def _():
            acc_ref[...] = jnp.zeros_like(acc_ref)

        acc_ref[...] += lax.dot_general(a_ref[...].astype(BF16), b_ref[...].astype(BF16), dims,
                                        preferred_element_type=F32)

        @pl.when(kk == nk - 1)
        def _():
            r = acc_ref[...]
            if add_ref is not None:
                r = r + add_ref[...].astype(F32)
            if bias_ref is not None:
                r = r + bias_ref[...]
            o_ref[...] = r.astype(out_dtype)

    if nsplit is None:
        out_shape = jax.ShapeDtypeStruct((m, n), out_dtype)
        out_spec = pl.BlockSpec((tm, tn), lambda i, j, kk: (i, j))
    else:
        per = n // nsplit // tn
        out_shape = jax.ShapeDtypeStruct((nsplit, m, n // nsplit), out_dtype)
        out_spec = pl.BlockSpec((None, tm, tn), lambda i, j, kk: (j // per, i, j % per))
    return _pcall(body, name=name, comm=comm, out_shape=out_shape, grid=(m // tm, n // tn, nk), in_specs=in_specs,
                  out_specs=out_spec, scratch=[pltpu.VMEM((tm, tn), F32)],
                  sem=("parallel", "parallel", "arbitrary"))(*operands)


def _rms_fwd(x, w, *, name):
    s, d = x.shape
    t = _tile(s, 256, SUBLANES)

    def body(x_ref, w_ref, o_ref):
        o_ref[...] = _rms(x_ref[...], w_ref[...]).astype(BF16)

    return _pcall(body, name=name, out_shape=jax.ShapeDtypeStruct((s, d), BF16), grid=(s // t,),
                  in_specs=[pl.BlockSpec((t, d), lambda i: (i, 0)), pl.BlockSpec((1, d), lambda i: (0, 0))],
                  out_specs=pl.BlockSpec((t, d), lambda i: (i, 0)), sem=("parallel",))(x, w.reshape(1, d))


def _rms_bwd(x, w, dh, res, *, name):
    s, d = x.shape
    t = _tile(s, 256, SUBLANES)

    def body(*refs):
        if res is None:
            x_ref, w_ref, dh_ref, dx_ref, dw_ref = refs
            res_ref = None
        else:
            x_ref, w_ref, dh_ref, res_ref, dx_ref, dw_ref = refs
        _, vjp = jax.vjp(_rms, x_ref[...], w_ref[...])
        dx, dw = vjp(dh_ref[...].astype(F32))
        if res_ref is not None:
            dx = dx + res_ref[...]
        dx_ref[...] = dx

        @pl.when(pl.program_id(0) == 0)
        def _():
            dw_ref[...] = jnp.zeros_like(dw_ref)

        dw_ref[...] += dw

    tok = pl.BlockSpec((t, d), lambda i: (i, 0))
    row = pl.BlockSpec((1, d), lambda i: (0, 0))
    operands = [x, w.reshape(1, d), dh] + ([] if res is None else [res])
    return _pcall(body, name=name,
                  out_shape=(jax.ShapeDtypeStruct((s, d), F32), jax.ShapeDtypeStruct((1, d), F32)), grid=(s // t,),
                  in_specs=[tok, row, tok] + ([] if res is None else [tok]), out_specs=(tok, row),
                  sem=("arbitrary",))(*operands)


def _loss_head(x, w, target, *, name):
    s, d = x.shape
    t = _tile(s, 256, SUBLANES)

    def body(x_ref, w_ref, t_ref, loss_ref, dx_ref, dw_ref):
        def f(xv, wv):
            err = _rms(xv, wv) - t_ref[...]
            return 0.5 * jnp.sum(jnp.mean(err * err, axis=-1))

        val, vjp = jax.vjp(f, x_ref[...], w_ref[...])
        dx, dw = vjp(jnp.ones((), F32))
        dx_ref[...] = dx

        @pl.when(pl.program_id(0) == 0)
        def _():
            dw_ref[...] = jnp.zeros_like(dw_ref)
            loss_ref[...] = jnp.zeros_like(loss_ref)

        dw_ref[...] += dw
        loss_ref[...] += jnp.full(loss_ref.shape, val, F32)

    tok = pl.BlockSpec((t, d), lambda i: (i, 0))
    row = pl.BlockSpec((1, d), lambda i: (0, 0))
    return _pcall(body, name=name,
                  out_shape=(jax.ShapeDtypeStruct((1, LANES), F32), jax.ShapeDtypeStruct((s, d), F32),
                             jax.ShapeDtypeStruct((1, d), F32)),
                  grid=(s // t,), in_specs=[tok, row, tok],
                  out_specs=(pl.BlockSpec((1, LANES), lambda i: (0, 0)), tok, row), sem=("arbitrary",))(
                      x, w.reshape(1, d), target)


def _conv_shifts(prev8, cur, t):
    xp = jnp.concatenate([prev8, cur], axis=0)
    out = []
    for j in range(CONV_WIDTH):
        k = CONV_WIDTH - 1 - j
        out.append(cur if k == 0 else pltpu.roll(xp, k, 0)[SUBLANES:SUBLANES + t])
    return out


def _conv_fwd(p, col_blk, width, w8, b, *, name):
    s = p.shape[0]
    t = _tile(s, 256, SUBLANES)
    r8 = t // SUBLANES

    def body(cur_ref, prev_ref, w_ref, b_ref, y_ref):
        i = pl.program_id(0)
        prev8 = jnp.where(i == 0, 0.0, prev_ref[...])
        sh = _conv_shifts(prev8, cur_ref[...], t)
        w = w_ref[...]
        y = b_ref[...] + sh[0] * w[0:1]
        for j in range(1, CONV_WIDTH):
            y = y + sh[j] * w[j:j + 1]
        y_ref[...] = y

    return _pcall(body, name=name, out_shape=jax.ShapeDtypeStruct((s, width), F32), grid=(s // t,),
                  in_specs=[pl.BlockSpec((t, width), lambda i: (i, col_blk)),
                            pl.BlockSpec((SUBLANES, width), lambda i: (jnp.maximum(i * r8 - 1, 0), col_blk)),
                            pl.BlockSpec((SUBLANES, width), lambda i: (0, 0)),
                            pl.BlockSpec((1, width), lambda i: (0, 0))],
                  out_specs=pl.BlockSpec((t, width), lambda i: (i, 0)), sem=("parallel",))(p, p, w8, b)


def _conv_bwd(p, col_blk, width, w8, dy, *, name):
    s = p.shape[0]
    t = _tile(s, 256, SUBLANES)
    r8 = t // SUBLANES
    nt = s // t

    def body(cur_ref, prev_ref, w_ref, dy_ref, dyn_ref, dx_ref, dw_ref, db_ref):
        i = pl.program_id(0)
        prev8 = jnp.where(i == 0, 0.0, prev_ref[...])
        sh = _conv_shifts(prev8, cur_ref[...], t)
        dy = dy_ref[...]
        next8 = jnp.where(i == nt - 1, 0.0, dyn_ref[...])
        dyp = jnp.concatenate([dy, next8], axis=0)
        w = w_ref[...]
        rows = lax.broadcasted_iota(jnp.int32, (SUBLANES, width), 0)
        dx = dy * w[CONV_WIDTH - 1:CONV_WIDTH]
        dw = jnp.zeros((SUBLANES, width), F32)
        for j in range(CONV_WIDTH):
            k = CONV_WIDTH - 1 - j
            if k:
                dx = dx + pltpu.roll(dyp, t + SUBLANES - k, 0)[0:t] * w[j:j + 1]
            dw = dw + jnp.where(rows == j, jnp.sum(dy * sh[j], axis=0, keepdims=True), 0.0)
        dx_ref[...] = dx

        @pl.when(i == 0)
        def _():
            dw_ref[...] = jnp.zeros_like(dw_ref)
            db_ref[...] = jnp.zeros_like(db_ref)

        dw_ref[...] += dw
        db_ref[...] += jnp.sum(dy, axis=0, keepdims=True)

    return _pcall(body, name=name,
                  out_shape=(jax.ShapeDtypeStruct((s, width), F32), jax.ShapeDtypeStruct((SUBLANES, width), F32),
                             jax.ShapeDtypeStruct((1, width), F32)),
                  grid=(nt,),
                  in_specs=[pl.BlockSpec((t, width), lambda i: (i, col_blk)),
                            pl.BlockSpec((SUBLANES, width), lambda i: (jnp.maximum(i * r8 - 1, 0), col_blk)),
                            pl.BlockSpec((SUBLANES, width), lambda i: (0, 0)),
                            pl.BlockSpec((t, width), lambda i: (i, 0)),
                            pl.BlockSpec((SUBLANES, width), lambda i: (jnp.minimum((i + 1) * r8, s // SUBLANES - 1), 0))],
                  out_specs=(pl.BlockSpec((t, width), lambda i: (i, 0)),
                             pl.BlockSpec((SUBLANES, width), lambda i: (0, 0)),
                             pl.BlockSpec((1, width), lambda i: (0, 0))),
                  sem=("arbitrary",))(p, p, w8, dy, dy)


def _dn_chunk(state, c, z, ba, alog_row, dt_row, nw_row, *, heads, bw):
    cs = c.shape[0]
    hd = DN_HEAD_DIM
    qkv = _silu(c)
    gfull = -jnp.exp(alog_row) * _softplus(ba + dt_row)
    beta_full = _sigmoid(ba)
    ri = lax.broadcasted_iota(jnp.int32, (cs, cs), 0)
    ci = lax.broadcasted_iota(jnp.int32, (cs, cs), 1)
    causal = ri >= ci
    strict = ri > ci
    tril = causal.astype(F32)
    eye = (ri == ci).astype(F32)
    gc = _xdot(tril, gfull, 1, 0)
    gct = _xdot(gfull, tril, 0, 1)
    outs, states = [], []
    for h in range(heads):
        q = qkv[:, h * hd:(h + 1) * hd]
        k = qkv[:, bw + h * hd:bw + (h + 1) * hd]
        v = qkv[:, 2 * bw + h * hd:2 * bw + (h + 1) * hd]
        q = q * lax.rsqrt(jnp.sum(q * q, axis=-1, keepdims=True) + NORM_EPS) * (hd ** -0.5)
        k = k * lax.rsqrt(jnp.sum(k * k, axis=-1, keepdims=True) + NORM_EPS)
        beta = _pick_lane(beta_full, h)
        g_col = _pick_lane(gc, heads + h)
        g_row = _pick_row(gct, heads + h)
        decay = jnp.exp(jnp.where(causal, g_col - g_row, -1e30))
        k_beta = k * beta
        v_beta = v * beta
        kk = _bdot(k_beta, k, 1, 1) * decay
        m = -jnp.where(strict, kk, 0.0)
        tinv = eye + m
        pw = m
        for _ in range(int(math.log2(cs)) - 1):
            pw = _xdot(pw, pw, 1, 0)
            tinv = tinv + _xdot(tinv, pw, 1, 0)
        rhs = jnp.concatenate([v_beta, k_beta * jnp.exp(g_col)], axis=-1)
        sol = _xdot(tinv, rhs, 1, 0)
        u, w = sol[:, :hd], sol[:, hd:]
        qk = jnp.where(causal, _bdot(q, k, 1, 1) * decay, 0.0)
        g_last = _pick_row(g_col, cs - 1)
        k_dec = k * jnp.exp(g_last - g_col)
        q_dec = q * jnp.exp(g_col)
        s_h = state[h]
        v_new = u - _bdot(w, s_h, 1, 0)
        o = _bdot(q_dec, s_h, 1, 0) + _bdot(qk, v_new, 1, 0)
        states.append(s_h * jnp.exp(g_last) + _bdot(k_dec, v_new, 0, 0))
        outs.append(_rms(o, nw_row) * _silu(z[:, h * hd:(h + 1) * hd]))
    return jnp.concatenate(outs, axis=-1), tuple(states)


def _dn_rows(a_log, dt_bias, heads):
    z = jnp.zeros((heads,), F32)
    pad = jnp.zeros((BA_PAD - 2 * heads,), F32)
    return (jnp.concatenate([z, a_log, pad]).reshape(1, BA_PAD), jnp.concatenate([z, dt_bias, pad]).reshape(1, BA_PAD))


def _dn_fwd(p, conv, a_log, dt_bias, norm_w, *, bw, ba_blk, name, comm=None):
    s = p.shape[0]
    heads = bw // DN_HEAD_DIM
    cs = min(DN_CHUNK, s)
    n = s // cs
    hd = DN_HEAD_DIM
    alog_row, dt_row = _dn_rows(a_log, dt_bias, heads)
    fn = functools.partial(_dn_chunk, heads=heads, bw=bw)

    def body(c_ref, z_ref, ba_ref, al_ref, dt_ref, nw_ref, o_ref, save_ref, st_ref):
        @pl.when(pl.program_id(0) == 0)
        def _():
            st_ref[...] = jnp.zeros_like(st_ref)

        save_ref[...] = st_ref[...]
        o, new = fn(tuple(st_ref[h] for h in range(heads)), c_ref[...], z_ref[...], ba_ref[...], al_ref[...],
                    dt_ref[...], nw_ref[...])
        o_ref[...] = o.astype(BF16)
        for h in range(heads):
            st_ref[h] = new[h]

    row = lambda wd: pl.BlockSpec((1, wd), lambda i: (0, 0))
    return _pcall(body, name=name, comm=comm,
                  out_shape=(jax.ShapeDtypeStruct((s, bw), BF16), jax.ShapeDtypeStruct((n, heads, hd, hd), F32)),
                  grid=(n,),
                  in_specs=[pl.BlockSpec((cs, 3 * bw), lambda i: (i, 0)), pl.BlockSpec((cs, bw), lambda i: (i, 3)),
                            pl.BlockSpec((cs, BA_PAD), lambda i: (i, ba_blk)), row(BA_PAD), row(BA_PAD), row(hd)],
                  out_specs=(pl.BlockSpec((cs, bw), lambda i: (i, 0)),
                             pl.BlockSpec((None, heads, hd, hd), lambda i: (i, 0, 0, 0))),
                  scratch=[pltpu.VMEM((heads, hd, hd), F32)], sem=("arbitrary",))(
                      conv, p, p, alog_row, dt_row, norm_w.reshape(1, hd))


def _dn_bwd(p, conv, states, d_o, a_log, dt_bias, norm_w, *, bw, ba_blk, name, comm=None):
    s = p.shape[0]
    heads = bw // DN_HEAD_DIM
    cs = min(DN_CHUNK, s)
    n = s // cs
    hd = DN_HEAD_DIM
    alog_row, dt_row = _dn_rows(a_log, dt_bias, heads)
    fn = functools.partial(_dn_chunk, heads=heads, bw=bw)

    def body(c_ref, z_ref, ba_ref, st_ref, do_ref, al_ref, dt_ref, nw_ref,
             dc_ref, dz_ref, dba_ref, dal_ref, ddt_ref, dnw_ref, dst_ref):
        @pl.when(pl.program_id(0) == 0)
        def _():
            dst_ref[...] = jnp.zeros_like(dst_ref)
            dal_ref[...] = jnp.zeros_like(dal_ref)
            ddt_ref[...] = jnp.zeros_like(ddt_ref)
            dnw_ref[...] = jnp.zeros_like(dnw_ref)

        _, vjp = jax.vjp(fn, tuple(st_ref[h] for h in range(heads)), c_ref[...], z_ref[...], ba_ref[...],
                         al_ref[...], dt_ref[...], nw_ref[...])
        dst, dc, dz, dba, dal, ddt, dnw = vjp((do_ref[...].astype(F32), tuple(dst_ref[h] for h in range(heads))))
        for h in range(heads):
            dst_ref[h] = dst[h]
        dc_ref[...] = dc
        dz_ref[...] = dz
        dba_ref[...] = dba
        dal_ref[...] += dal
        ddt_ref[...] += ddt
        dnw_ref[...] += dnw

    rev = lambda i: n - 1 - i
    row = lambda wd: pl.BlockSpec((1, wd), lambda i: (0, 0))
    return _pcall(body, name=name, comm=comm,
                  out_shape=(jax.ShapeDtypeStruct((s, 3 * bw), F32), jax.ShapeDtypeStruct((s, bw), F32),
                             jax.ShapeDtypeStruct((s, BA_PAD), F32), jax.ShapeDtypeStruct((1, BA_PAD), F32),
                             jax.ShapeDtypeStruct((1, BA_PAD), F32), jax.ShapeDtypeStruct((1, hd), F32)),
                  grid=(n,),
                  in_specs=[pl.BlockSpec((cs, 3 * bw), lambda i: (rev(i), 0)),
                            pl.BlockSpec((cs, bw), lambda i: (rev(i), 3)),
                            pl.BlockSpec((cs, BA_PAD), lambda i: (rev(i), ba_blk)),
                            pl.BlockSpec((None, heads, hd, hd), lambda i: (rev(i), 0, 0, 0)),
                            pl.BlockSpec((cs, bw), lambda i: (rev(i), 0)), row(BA_PAD), row(BA_PAD), row(hd)],
                  out_specs=(pl.BlockSpec((cs, 3 * bw), lambda i: (rev(i), 0)),
                             pl.BlockSpec((cs, bw), lambda i: (rev(i), 0)),
                             pl.BlockSpec((cs, BA_PAD), lambda i: (rev(i), 0)), row(BA_PAD), row(BA_PAD), row(hd)),
                  scratch=[pltpu.VMEM((heads, hd, hd), F32)], sem=("arbitrary",))(
                      conv, p, p, states, d_o, alog_row, dt_row, norm_w.reshape(1, hd))


def _lru_gates(xc, wr, br, wi, bi, lam):
    r = _sigmoid(_bdot(xc, wr, 1, 0) + br)
    i = _sigmoid(_bdot(xc, wi, 1, 0) + bi)
    log_a = -LRU_C * r * _softplus(-lam)
    return jnp.exp(log_a), jnp.sqrt(-_expm1(2.0 * log_a)) * (i * xc)


def _scan_rows(t, step, carry):
    def trip(g, cr):
        base = pl.multiple_of(g * SUBLANES, SUBLANES)
        for r in range(SUBLANES):
            cr = step(base + r, cr)
        return cr
    return lax.fori_loop(0, t // SUBLANES, trip, carry)


def _scan_rows_rev(t, step, carry):
    def trip(g, cr):
        base = pl.multiple_of((t // SUBLANES - 1 - g) * SUBLANES, SUBLANES)
        for r in range(SUBLANES - 1, -1, -1):
            cr = step(base + r, cr)
        return cr
    return lax.fori_loop(0, t // SUBLANES, trip, carry)


def _lru_fwd(p, xc, wr, br, wi, bi, lam, *, bw, z_blk, name):
    s = p.shape[0]
    t = _tile(s, 256, SUBLANES)
    nt = s // t

    def body(xc_ref, z_ref, wr_ref, br_ref, wi_ref, bi_ref, lam_ref, o_ref, save_ref, a_s, b_s, h_s, carry_s):
        @pl.when(pl.program_id(0) == 0)
        def _():
            carry_s[...] = jnp.zeros_like(carry_s)

        a, inp = _lru_gates(xc_ref[...], wr_ref[...], br_ref[...], wi_ref[...], bi_ref[...], lam_ref[...])
        a_s[...] = a
        b_s[...] = inp
        h0 = carry_s[...]
        save_ref[...] = h0

        def step(r, h):
            h = a_s[pl.ds(r, 1), :] * h + b_s[pl.ds(r, 1), :]
            h_s[pl.ds(r, 1), :] = h
            return h

        carry_s[...] = _scan_rows(t, step, h0)
        o_ref[...] = (h_s[...] * _silu(z_ref[...])).astype(BF16)

    tok = pl.BlockSpec((t, bw), lambda i: (i, 0))
    row = pl.BlockSpec((1, bw), lambda i: (0, 0))
    mat = pl.BlockSpec((bw, bw), lambda i: (0, 0))
    return _pcall(body, name=name,
                  out_shape=(jax.ShapeDtypeStruct((s, bw), BF16), jax.ShapeDtypeStruct((nt, 1, bw), F32)), grid=(nt,),
                  in_specs=[tok, pl.BlockSpec((t, bw), lambda i: (i, z_blk)), mat, row, mat, row, row],
                  out_specs=(tok, pl.BlockSpec((None, 1, bw), lambda i: (i, 0, 0))),
                  scratch=[pltpu.VMEM((t, bw), F32)] * 3 + [pltpu.VMEM((1, bw), F32)], sem=("arbitrary",))(
                      xc, p, wr, br, wi, bi, lam)


def _lru_bwd(p, xc, saves, d_o, wr, br, wi, bi, lam, *, bw, z_blk, name):
    s = p.shape[0]
    t = _tile(s, 256, SUBLANES)
    nt = s // t

    def body(xc_ref, z_ref, sv_ref, do_ref, wr_ref, br_ref, wi_ref, bi_ref, lam_ref,
             dxc_ref, dz_ref, dwr_ref, dwi_ref, dbr_ref, dbi_ref, dlam_ref, a_s, b_s, h_s, g_s, carry_s):
        @pl.when(pl.program_id(0) == 0)
        def _():
            carry_s[...] = jnp.zeros_like(carry_s)
            for r in (dwr_ref, dwi_ref, dbr_ref, dbi_ref, dlam_ref):
                r[...] = jnp.zeros_like(r)

        (a, inp), vjp_g = jax.vjp(_lru_gates, xc_ref[...], wr_ref[...], br_ref[...], wi_ref[...], bi_ref[...],
                                  lam_ref[...])
        a_s[...] = a
        b_s[...] = inp
        h0 = sv_ref[...]

        def fstep(r, h):
            h_s[pl.ds(r, 1), :] = h
            return a_s[pl.ds(r, 1), :] * h + b_s[pl.ds(r, 1), :]

        _scan_rows(t, fstep, h0)
        a = a_s[...]
        hs = a * h_s[...] + b_s[...]
        z = z_ref[...]
        d_o = do_ref[...].astype(F32)
        _, vjp_o = jax.vjp(lambda hv, zv: hv * _silu(zv), hs, z)
        dhs, dz = vjp_o(d_o)
        dz_ref[...] = dz
        g_s[...] = dhs

        def bstep(r, cr):
            g = g_s[pl.ds(r, 1), :] + cr
            g_s[pl.ds(r, 1), :] = g
            return a_s[pl.ds(r, 1), :] * g

        carry_s[...] = _scan_rows_rev(t, bstep, carry_s[...])
        g = g_s[...]
        dxc, dwr, dbr, dwi, dbi, dlam = vjp_g((g * h_s[...], g))
        dxc_ref[...] = dxc
        dwr_ref[...] += dwr
        dwi_ref[...] += dwi
        dbr_ref[...] += dbr
        dbi_ref[...] += dbi
        dlam_ref[...] += dlam

    rev = lambda i: nt - 1 - i
    tok = pl.BlockSpec((t, bw), lambda i: (rev(i), 0))
    row = pl.BlockSpec((1, bw), lambda i: (0, 0))
    mat = pl.BlockSpec((bw, bw), lambda i: (0, 0))
    sd = jax.ShapeDtypeStruct
    return _pcall(body, name=name,
                  out_shape=(sd((s, bw), F32), sd((s, bw), F32), sd((bw, bw), F32), sd((bw, bw), F32),
                             sd((1, bw), F32), sd((1, bw), F32), sd((1, bw), F32)),
                  grid=(nt,),
                  in_specs=[tok, pl.BlockSpec((t, bw), lambda i: (rev(i), z_blk)),
                            pl.BlockSpec((None, 1, bw), lambda i: (rev(i), 0, 0)), tok, mat, row, mat, row, row],
                  out_specs=(tok, tok, mat, mat, row, row, row),
                  scratch=[pltpu.VMEM((t, bw), F32)] * 4 + [pltpu.VMEM((1, bw), F32)], sem=("arbitrary",))(
                      xc, p, saves, d_o, wr, br, wi, bi, lam)


def _s5_prep(log_dt, a_re, a_im, b_re, b_im, c_re, c_im, d_skip):
    g, n = a_re.shape
    gs = d_skip.shape[1]
    gpb = LANES // gs
    nb = g // gpb
    dt = jnp.exp(log_dt)[:, None]
    mag = jnp.exp(dt * a_re)
    ab_re = mag * jnp.cos(dt * a_im)
    ab_im = mag * jnp.sin(dt * a_im)
    den = a_re * a_re + a_im * a_im
    f_re = ((ab_re - 1.0) * a_re + ab_im * a_im) / den
    f_im = (ab_im * a_re - (ab_re - 1.0) * a_im) / den
    bb_re = f_re[..., None] * b_re - f_im[..., None] * b_im
    bb_im = f_re[..., None] * b_im + f_im[..., None] * b_re
    eye = jnp.eye(gpb, dtype=F32)

    def b_dense(bb):
        t = bb.reshape(nb, gpb, n, gs)
        return jnp.einsum("bgnc,gh->bgchn", t, eye).reshape(nb, gpb * gs, gpb * n)

    def c_dense(cc):
        t = cc.reshape(nb, gpb, gs, n)
        return jnp.einsum("bgcn,gh->bgnhc", t, eye).reshape(nb, gpb * n, gpb * gs)

    lanes = gpb * n
    sub = lanes // LANES
    return (ab_re.reshape(nb, sub, LANES), ab_im.reshape(nb, sub, LANES), b_dense(bb_re), b_dense(bb_im),
            c_dense(c_re), c_dense(c_im), d_skip.reshape(1, g * gs))


def _s5_out(xre, xim, cre, cim, d, u):
    return _gelu(_bdot(xre, cre, 1, 0) - _bdot(xim, cim, 1, 0) + d * u)


def _s5_fwd(p, prep, *, bw, u_blk0, name):
    s = p.shape[0]
    are, aim, bre, bim, cre, cim, d = prep
    nb, sub, _ = are.shape
    lanes = sub * LANES
    t = _tile(s, 256, SUBLANES)
    nt = s // t

    def body(u_ref, are_ref, aim_ref, bre_ref, bim_ref, cre_ref, cim_ref, d_ref, y_ref, save_ref,
             bre_s, bim_s, xre_s, xim_s, carry_s):
        @pl.when(pl.program_id(1) == 0)
        def _():
            carry_s[...] = jnp.zeros_like(carry_s)

        u = u_ref[...]
        bre_s[...] = _bdot(u, bre_ref[...], 1, 0).reshape(t, sub, LANES)
        bim_s[...] = _bdot(u, bim_ref[...], 1, 0).reshape(t, sub, LANES)
        ar, ai = are_ref[...], aim_ref[...]
        save_ref[...] = carry_s[...]

        def step(r, cr):
            xr, xi = cr
            nr = ar * xr - ai * xi + bre_s[r]
            ni = ar * xi + ai * xr + bim_s[r]
            xre_s[r] = nr
            xim_s[r] = ni
            return nr, ni

        xr, xi = lax.fori_loop(0, t, step, (carry_s[0], carry_s[1]))
        carry_s[0] = xr
        carry_s[1] = xi
        y_ref[...] = _s5_out(xre_s[...].reshape(t, lanes), xim_s[...].reshape(t, lanes), cre_ref[...], cim_ref[...],
                             d_ref[...], u).astype(BF16)

    vec = pl.BlockSpec((None, sub, LANES), lambda j, i: (j, 0, 0))
    bmat = pl.BlockSpec((None, LANES, lanes), lambda j, i: (j, 0, 0))
    cmat = pl.BlockSpec((None, lanes, LANES), lambda j, i: (j, 0, 0))
    return _pcall(body, name=name,
                  out_shape=(jax.ShapeDtypeStruct((s, bw), BF16), jax.ShapeDtypeStruct((nb, nt, 2, sub, LANES), F32)),
                  grid=(nb, nt),
                  in_specs=[pl.BlockSpec((t, LANES), lambda j, i: (i, u_blk0 + j)), vec, vec, bmat, bmat, cmat, cmat,
                            pl.BlockSpec((1, LANES), lambda j, i: (0, j))],
                  out_specs=(pl.BlockSpec((t, LANES), lambda j, i: (i, j)),
                             pl.BlockSpec((None, None, 2, sub, LANES), lambda j, i: (j, i, 0, 0, 0))),
                  scratch=[pltpu.VMEM((t, sub, LANES), F32)] * 4 + [pltpu.VMEM((2, sub, LANES), F32)],
                  sem=("parallel", "arbitrary"))(p, are, aim, bre, bim, cre, cim, d)


def _s5_bwd(p, prep, saves, dyg, *, bw, u_blk0, name):
    s = p.shape[0]
    are, aim, bre, bim, cre, cim, d = prep
    nb, sub, _ = are.shape
    lanes = sub * LANES
    t = _tile(s, 256, SUBLANES)
    nt = s // t

    def body(u_ref, dy_ref, sv_ref, are_ref, aim_ref, bre_ref, bim_ref, cre_ref, cim_ref, d_ref,
             du_ref, dar_ref, dai_ref, dbre_ref, dbim_ref, dcre_ref, dcim_ref, dd_ref,
             bre_s, bim_s, xre_s, xim_s, carry_s):
        @pl.when(pl.program_id(1) == 0)
        def _():
            carry_s[...] = jnp.zeros_like(carry_s)
            for r in (dar_ref, dai_ref, dbre_ref, dbim_ref, dcre_ref, dcim_ref, dd_ref):
                r[...] = jnp.zeros_like(r)

        u = u_ref[...]
        bre_s[...] = _bdot(u, bre_ref[...], 1, 0).reshape(t, sub, LANES)
        bim_s[...] = _bdot(u, bim_ref[...], 1, 0).reshape(t, sub, LANES)
        ar, ai = are_ref[...], aim_ref[...]

        def fstep(r, cr):
            xr, xi = cr
            nr = ar * xr - ai * xi + bre_s[r]
            ni = ar * xi + ai * xr + bim_s[r]
            xre_s[r] = nr
            xim_s[r] = ni
            return nr, ni

        lax.fori_loop(0, t, fstep, (sv_ref[0], sv_ref[1]))
        _, vjp_o = jax.vjp(_s5_out, xre_s[...].reshape(t, lanes), xim_s[...].reshape(t, lanes), cre_ref[...],
                           cim_ref[...], d_ref[...], u)
        dxre, dxim, dcre, dcim, dd, du = vjp_o(dy_ref[...].astype(F32))
        dcre_ref[...] += dcre.astype(F32)
        dcim_ref[...] += dcim.astype(F32)
        dd_ref[...] += dd
        bre_s[...] = dxre.reshape(t, sub, LANES)
        bim_s[...] = dxim.reshape(t, sub, LANES)

        def bstep(k, cr):
            r = t - 1 - k
            gr, gi, dar, dai = cr
            gr = bre_s[r] + gr
            gi = bim_s[r] + gi
            bre_s[r] = gr
            bim_s[r] = gi
            pr = jnp.where(r == 0, sv_ref[0], xre_s[jnp.maximum(r - 1, 0)])
            pi = jnp.where(r == 0, sv_ref[1], xim_s[jnp.maximum(r - 1, 0)])
            dar = dar + gr * pr + gi * pi
            dai = dai + gi * pr - gr * pi
            return ar * gr + ai * gi, ar * gi - ai * gr, dar, dai

        zero = jnp.zeros((sub, LANES), F32)
        gr, gi, dar, dai = lax.fori_loop(0, t, bstep, (carry_s[0], carry_s[1], zero, zero))
        carry_s[0] = gr
        carry_s[1] = gi
        dar_ref[...] += dar
        dai_ref[...] += dai
        dbu_re = bre_s[...].reshape(t, lanes)
        dbu_im = bim_s[...].reshape(t, lanes)
        du_ref[...] = du + _bdot(dbu_re, bre_ref[...], 1, 1) + _bdot(dbu_im, bim_ref[...], 1, 1)
        dbre_ref[...] += _bdot(u, dbu_re, 0, 0)
        dbim_ref[...] += _bdot(u, dbu_im, 0, 0)

    rev = lambda i: nt - 1 - i
    vec = pl.BlockSpec((None, sub, LANES), lambda j, i: (j, 0, 0))
    bmat = pl.BlockSpec((None, LANES, lanes), lambda j, i: (j, 0, 0))
    cmat = pl.BlockSpec((None, lanes, LANES), lambda j, i: (j, 0, 0))
    drow = pl.BlockSpec((1, LANES), lambda j, i: (0, j))
    sd = jax.ShapeDtypeStruct
    return _pcall(body, name=name,
                  out_shape=(sd((s, bw), F32), sd(are.shape, F32), sd(aim.shape, F32), sd(bre.shape, F32),
                             sd(bim.shape, F32), sd(cre.shape, F32), sd(cim.shape, F32), sd((1, bw), F32)),
                  grid=(nb, nt),
                  in_specs=[pl.BlockSpec((t, LANES), lambda j, i: (rev(i), u_blk0 + j)),
                            pl.BlockSpec((t, LANES), lambda j, i: (rev(i), j)),
                            pl.BlockSpec((None, None, 2, sub, LANES), lambda j, i: (j, rev(i), 0, 0, 0)),
                            vec, vec, bmat, bmat, cmat, cmat, drow],
                  out_specs=(pl.BlockSpec((t, LANES), lambda j, i: (rev(i), j)), vec, vec, bmat, bmat, cmat, cmat, drow),
                  scratch=[pltpu.VMEM((t, sub, LANES), F32)] * 4 + [pltpu.VMEM((2, sub, LANES), F32)],
                  sem=("parallel", "arbitrary"))(p, dyg, saves, are, aim, bre, bim, cre, cim, d)


def _glu_gate(pre, z, bw):
    return pre[:, :bw] * _sigmoid(pre[:, bw:]) * _silu(z)


def _glu_fwd(pre, p, *, bw, z_blk, name):
    s = pre.shape[0]
    t = _tile(s, 256, SUBLANES)

    def body(pre_ref, z_ref, o_ref):
        o_ref[...] = _glu_gate(pre_ref[...], z_ref[...], bw).astype(BF16)

    return _pcall(body, name=name, out_shape=jax.ShapeDtypeStruct((s, bw), BF16), grid=(s // t,),
                  in_specs=[pl.BlockSpec((t, 2 * bw), lambda i: (i, 0)), pl.BlockSpec((t, bw), lambda i: (i, z_blk))],
                  out_specs=pl.BlockSpec((t, bw), lambda i: (i, 0)), sem=("parallel",))(pre, p)


def _glu_bwd(pre, p, d_o, *, bw, z_blk, name):
    s = pre.shape[0]
    t = _tile(s, 256, SUBLANES)

    def body(pre_ref, z_ref, do_ref, dpre_ref, dz_ref, db_ref):
        _, vjp = jax.vjp(functools.partial(_glu_gate, bw=bw), pre_ref[...], z_ref[...])
        dpre, dz = vjp(do_ref[...].astype(F32))
        dpre_ref[...] = dpre.astype(BF16)
        dz_ref[...] = dz

        @pl.when(pl.program_id(0) == 0)
        def _():
            db_ref[...] = jnp.zeros_like(db_ref)

        db_ref[...] += jnp.sum(dpre, axis=0, keepdims=True)

    sd = jax.ShapeDtypeStruct
    return _pcall(body, name=name, out_shape=(sd((s, 2 * bw), BF16), sd((s, bw), F32), sd((1, 2 * bw), F32)),
                  grid=(s // t,),
                  in_specs=[pl.BlockSpec((t, 2 * bw), lambda i: (i, 0)), pl.BlockSpec((t, bw), lambda i: (i, z_blk)),
                            pl.BlockSpec((t, bw), lambda i: (i, 0))],
                  out_specs=(pl.BlockSpec((t, 2 * bw), lambda i: (i, 0)), pl.BlockSpec((t, bw), lambda i: (i, 0)),
                             pl.BlockSpec((1, 2 * bw), lambda i: (0, 0))), sem=("arbitrary",))(pre, p, d_o)


def _attn_tile(q, z, kv, *, bw):
    hd = bw // MEM_HEADS
    outs = []
    for h in range(MEM_HEADS):
        k = kv[:, h * hd:(h + 1) * hd]
        v = kv[:, bw + h * hd:bw + (h + 1) * hd]
        sc = _bdot(q[:, h * hd:(h + 1) * hd], k, 1, 1) * (hd ** -0.5)
        e = jnp.exp(sc - lax.stop_gradient(jnp.max(sc, axis=-1, keepdims=True)))
        prob = e / jnp.sum(e, axis=-1, keepdims=True)
        outs.append(_bdot(prob, v, 1, 0))
    return jnp.concatenate(outs, axis=-1) * _silu(z)


def _attn_fwd(p, kv, *, bw, q_blk, z_blk, name):
    s = p.shape[0]
    m = kv.shape[0]
    t = _tile(s, 256, SUBLANES)

    def body(q_ref, z_ref, kv_ref, o_ref):
        o_ref[...] = _attn_tile(q_ref[...], z_ref[...], kv_ref[...], bw=bw).astype(BF16)

    return _pcall(body, name=name, out_shape=jax.ShapeDtypeStruct((s, bw), BF16), grid=(s // t,),
                  in_specs=[pl.BlockSpec((t, bw), lambda i: (i, q_blk)), pl.BlockSpec((t, bw), lambda i: (i, z_blk)),
                            pl.BlockSpec((m, 2 * bw), lambda i: (0, 0))],
                  out_specs=pl.BlockSpec((t, bw), lambda i: (i, 0)), sem=("parallel",))(p, p, kv)


def _attn_bwd(p, kv, d_o, *, bw, q_blk, z_blk, name):
    s = p.shape[0]
    m = kv.shape[0]
    t = _tile(s, 256, SUBLANES)

    def body(q_ref, z_ref, kv_ref, do_ref, dq_ref, dz_ref, dkv_ref):
        _, vjp = jax.vjp(functools.partial(_attn_tile, bw=bw), q_ref[...], z_ref[...], kv_ref[...])
        dq, dz, dkv = vjp(do_ref[...].astype(F32))
        dq_ref[...] = dq
        dz_ref[...] = dz

        @pl.when(pl.program_id(0) == 0)
        def _():
            dkv_ref[...] = jnp.zeros_like(dkv_ref)

        dkv_ref[...] += dkv

    sd = jax.ShapeDtypeStruct
    tok = pl.BlockSpec((t, bw), lambda i: (i, 0))
    return _pcall(body, name=name, out_shape=(sd((s, bw), F32), sd((s, bw), F32), sd((m, 2 * bw), F32)),
                  grid=(s // t,),
                  in_specs=[pl.BlockSpec((t, bw), lambda i: (i, q_blk)), pl.BlockSpec((t, bw), lambda i: (i, z_blk)),
                            pl.BlockSpec((m, 2 * bw), lambda i: (0, 0)), tok],
                  out_specs=(tok, tok, pl.BlockSpec((m, 2 * bw), lambda i: (0, 0))), sem=("arbitrary",))(p, p, kv, d_o)


def _merge_fwd(p, o4, wg, bg, wb, *, rank, g_blk, name):
    s = p.shape[0]
    _, bw, d = wb.shape
    tm, tn = _tile(s, 512), _tile(d, 512)

    def body(g_ref, o_ref, wg_ref, bg_ref, wb_ref, out_ref):
        g = g_ref[...]
        acc = jnp.zeros((tm, tn), F32)
        for n in range(N_BRANCH):
            gate = _sigmoid(_bdot(g, wg_ref[n], 1, 0) + bg_ref[n])
            acc = acc + gate * _bdot(o_ref[n], wb_ref[n], 1, 0)
        out_ref[...] = acc.astype(BF16)

    return _pcall(body, name=name, out_shape=jax.ShapeDtypeStruct((s, d), BF16), grid=(s // tm, d // tn),
                  in_specs=[pl.BlockSpec((tm, rank), lambda i, j: (i, g_blk)),
                            pl.BlockSpec((N_BRANCH, tm, bw), lambda i, j: (0, i, 0)),
                            pl.BlockSpec((N_BRANCH, rank, tn), lambda i, j: (0, 0, j)),
                            pl.BlockSpec((N_BRANCH, 1, tn), lambda i, j: (0, 0, j)),
                            pl.BlockSpec((N_BRANCH, bw, tn), lambda i, j: (0, 0, j))],
                  out_specs=pl.BlockSpec((tm, tn), lambda i, j: (i, j)), sem=("parallel", "parallel"))(
                      p, o4, wg, bg, wb)


def _merge_bwd(p, o4, wg, bg, wb, dmerged, *, rank, g_blk, name):
    s = p.shape[0]
    _, bw, d = wb.shape
    tm, tn = _tile(s, 512), _tile(d, 512)

    def body(g_ref, o_ref, wg_ref, bg_ref, wb_ref, dm_ref, dpre_ref, dbr_ref, dbg_ref):
        g = g_ref[...]
        dm = dm_ref[...].astype(F32)

        @pl.when(pl.program_id(1) == 0)
        def _():
            dbg_ref[...] = jnp.zeros_like(dbg_ref)

        for n in range(N_BRANCH):
            gate = _sigmoid(_bdot(g, wg_ref[n], 1, 0) + bg_ref[n])
            br = _bdot(o_ref[n], wb_ref[n], 1, 0)
            dpre = dm * br * gate * (1.0 - gate)
            dpre_ref[n] = dpre.astype(BF16)
            dbr_ref[n] = (dm * gate).astype(BF16)
            dbg_ref[n] += jnp.sum(dpre, axis=0, keepdims=True)

    sd = jax.ShapeDtypeStruct
    big = pl.BlockSpec((N_BRANCH, tm, tn), lambda j, i: (0, i, j))
    return _pcall(body, name=name,
                  out_shape=(sd((N_BRANCH, s, d), BF16), sd((N_BRANCH, s, d), BF16), sd((N_BRANCH, 1, d), F32)),
                  grid=(d // tn, s // tm),
                  in_specs=[pl.BlockSpec((tm, rank), lambda j, i: (i, g_blk)),
                            pl.BlockSpec((N_BRANCH, tm, bw), lambda j, i: (0, i, 0)),
                            pl.BlockSpec((N_BRANCH, rank, tn), lambda j, i: (0, 0, j)),
                            pl.BlockSpec((N_BRANCH, 1, tn), lambda j, i: (0, 0, j)),
                            pl.BlockSpec((N_BRANCH, bw, tn), lambda j, i: (0, 0, j)),
                            pl.BlockSpec((tm, tn), lambda j, i: (i, j))],
                  out_specs=(big, big, pl.BlockSpec((N_BRANCH, 1, tn), lambda j, i: (0, 0, j))),
                  sem=("parallel", "arbitrary"))(p, o4, wg, bg, wb, dmerged)


def _adamw(w, g, m, v, *, name):
    r, cdim = w.shape
    tr = _tile(r, 128, SUBLANES)
    bc1 = 1.0 - ADAM_B1 ** ADAM_STEP
    bc2 = 1.0 - ADAM_B2 ** ADAM_STEP

    def body(w_ref, g_ref, m_ref, v_ref, d_ref, nm_ref, nv_ref):
        gv = g_ref[...]
        nm = ADAM_B1 * m_ref[...] + (1.0 - ADAM_B1) * gv
        nv = ADAM_B2 * v_ref[...] + (1.0 - ADAM_B2) * (gv * gv)
        d_ref[...] = -ADAM_LR * ((nm / bc1) / (jnp.sqrt(nv / bc2) + ADAM_EPS) + ADAM_WD * w_ref[...])
        nm_ref[...] = nm
        nv_ref[...] = nv

    blk = pl.BlockSpec((tr, cdim), lambda i: (i, 0))
    sd = jax.ShapeDtypeStruct((r, cdim), F32)
    return _pcall(body, name=name, out_shape=(sd, sd, sd), grid=(r // tr,), in_specs=[blk] * 4, out_specs=(blk,) * 3,
                  sem=("parallel",))(w, g, m, v)


HBM_SPEC = pl.BlockSpec(memory_space=pl.ANY)


def _place():
    x, y, c = lax.axis_index("x"), lax.axis_index("y"), lax.axis_index("c")
    return x, y, c, [(1 - x, y), (x, 1 - y), (1 - x, 1 - y)]


def _rcopy(src, dst, send_sem, recv_sem, device):
    return pltpu.make_async_remote_copy(src_ref=src, dst_ref=dst, send_sem=send_sem, recv_sem=recv_sem,
                                        device_id=device, device_id_type=MESH_ID)


def _comm_call(body, *, name, out_shape, n_in, n_sems, n_local=0):
    scratch = [pltpu.SemaphoreType.DMA((n_sems,)), pltpu.SemaphoreType.DMA((n_sems,))]
    if n_local:
        scratch.append(pltpu.SemaphoreType.DMA((n_local,)))
    multi = isinstance(out_shape, (tuple, list))
    return pl.pallas_call(body, name=name, out_shape=out_shape, in_specs=[HBM_SPEC] * n_in,
                          out_specs=tuple(HBM_SPEC for _ in out_shape) if multi else HBM_SPEC,
                          scratch_shapes=scratch, interpret=False)


def _ag4(items):
    n = len(items)

    def copies(ins, outs, send_sems, recv_sems, local_sems, second_stage):
        x, y, c, chips = _place()
        sibling = (x, y, 1 - c)

        def part(i, px, py, h):
            half = ins[i].shape[0] // 2
            return outs[i].at[2 * px + py, pl.ds(h * half, half)]

        local = [pltpu.make_async_copy(ins[i], outs[i].at[2 * x + y], local_sems.at[i]) for i in range(n)]
        first, landed, passed, arrived = [], [], [], []
        for i in range(n):
            half = ins[i].shape[0] // 2
            for j, chip in enumerate(chips):
                sems = (send_sems.at[6 * i + j], recv_sems.at[6 * i + j])
                sems2 = (send_sems.at[6 * i + 3 + j], recv_sems.at[6 * i + 3 + j])
                first.append(_rcopy(ins[i].at[pl.ds(c * half, half)], part(i, x, y, c), *sems, (*chip, c)))
                if second_stage:
                    landed.append(_rcopy(part(i, *chip, c), part(i, *chip, c), *sems, (*chip, c)))
                    passed.append(_rcopy(part(i, *chip, c), part(i, *chip, c), *sems2, sibling))
                    arrived.append(_rcopy(part(i, *chip, 1 - c), part(i, *chip, 1 - c), *sems2, sibling))
        return local, first, landed, passed, arrived

    def start(*refs):
        local, first, _, _, _ = copies(*refs, False)
        for cp in local + first:
            cp.start()

    def finish(*refs):
        local, first, landed, passed, arrived = copies(*refs, True)
        for k in range(3 * n):
            landed[k].wait_recv()
            passed[k].start()
        for cp in arrived:
            cp.wait_recv()
        for cp in first + passed:
            cp.wait_send()
        for cp in local:
            cp.wait()

    return _Comm(items, [jax.ShapeDtypeStruct((4,) + t.shape, t.dtype) for t in items], 6 * n, n, start, finish)


def _sib_halves(items, *, name):
    n = len(items)

    def body(*refs):
        ins, outs = refs[:n], refs[n:2 * n]
        send_sems, recv_sems = refs[2 * n:]
        x, y, c, _ = _place()
        cps = []
        for i in range(n):
            half = ins[i].shape[1] // 2
            cps.append(_rcopy(ins[i].at[:, pl.ds((1 - c) * half, half)], outs[i], send_sems.at[i], recv_sems.at[i],
                              (x, y, 1 - c)))
        for cp in cps:
            cp.start()
        for cp in cps:
            cp.wait()

    shapes = tuple(jax.ShapeDtypeStruct((4, t.shape[1] // 2) + t.shape[2:], t.dtype) for t in items)
    return _comm_call(body, name=name, out_shape=shapes, n_in=n, n_sems=n)(*items)


def _to_chips(items):
    n = len(items)

    def copies(ins, outs, send_sems, recv_sems, _):
        x, y, c, chips = _place()
        return [_rcopy(ins[i].at[2 * chip[0] + chip[1]], outs[i].at[j], send_sems.at[3 * i + j],
                       recv_sems.at[3 * i + j], (*chip, c)) for i in range(n) for j, chip in enumerate(chips)]

    def start(*refs):
        for cp in copies(*refs):
            cp.start()

    def finish(*refs):
        for cp in copies(*refs):
            cp.wait()

    return _Comm(items, [jax.ShapeDtypeStruct((3,) + t.shape[1:], t.dtype) for t in items], 3 * n, 0, start, finish)


def _join_halves(items, *, name):
    n = len(items)

    def body(*refs):
        ins, outs = refs[:n], refs[n:2 * n]
        send_sems, recv_sems, local_sems = refs[2 * n:]
        x, y, c, _ = _place()
        sibling = (x, y, 1 - c)
        local, cps = [], []
        for i in range(n):
            a = ins[i].shape[0]
            local.append(pltpu.make_async_copy(ins[i], outs[i].at[pl.ds(c * a, a)], local_sems.at[i]))
            cps.append(_rcopy(ins[i], outs[i].at[pl.ds(c * a, a)], send_sems.at[i], recv_sems.at[i], sibling))
        for cp in local + cps:
            cp.start()
        for i in range(n):
            a = ins[i].shape[0]
            cps[i].wait_send()
            _rcopy(ins[i], outs[i].at[pl.ds((1 - c) * a, a)], send_sems.at[i], recv_sems.at[i], sibling).wait_recv()
        for cp in local:
            cp.wait()

    shapes = tuple(jax.ShapeDtypeStruct((2 * t.shape[0],) + t.shape[1:], t.dtype) for t in items)
    return _comm_call(body, name=name, out_shape=shapes, n_in=n, n_sems=n, n_local=n)(*items)


def _swap_sibling(buf, *, name):
    def body(b_ref, recv_ref, send_sems, recv_sems):
        x, y, c, _ = _place()
        cp = _rcopy(b_ref, recv_ref, send_sems.at[0], recv_sems.at[0], (x, y, 1 - c))
        cp.start()
        cp.wait()

    return _comm_call(body, name=name, out_shape=jax.ShapeDtypeStruct(buf.shape, buf.dtype), n_in=1, n_sems=1)(buf)


def _gather_chips(buf, *, name):
    n, cdim = buf.shape

    def body(b_ref, out_ref, send_sems, recv_sems, local_sems):
        x, y, c, chips = _place()
        mine = pltpu.make_async_copy(b_ref, out_ref.at[2 * x + y], local_sems.at[0])
        mine.start()
        cps = [_rcopy(b_ref, out_ref.at[2 * x + y], send_sems.at[j], recv_sems.at[j], (*chip, c))
               for j, chip in enumerate(chips)]
        for cp in cps:
            cp.start()
        for j, chip in enumerate(chips):
            slot = out_ref.at[2 * chip[0] + chip[1]]
            _rcopy(slot, slot, send_sems.at[j], recv_sems.at[j], (*chip, c)).wait_recv()
        for cp in cps:
            cp.wait_send()
        mine.wait()

    return _comm_call(body, name=name, out_shape=jax.ShapeDtypeStruct((4, n, cdim), buf.dtype), n_in=1, n_sems=3,
                      n_local=1)(buf)


def _rows_per_block(rows, cdim, itemsize, align):
    return _tile(rows, max(align, ELEMENTWISE_BLOCK_BYTES // (cdim * itemsize) // align * align), align)


def _sum_sib(g4, recv, c_idx, *, name):
    _, rows, cdim = g4.shape
    half = rows // 2
    tr = _rows_per_block(half, cdim, 2, 16)
    nh = half // tr

    def body(c_ref, g_ref, r_ref, o_ref):
        o_ref[...] = (g_ref[...].astype(F32) + r_ref[...].astype(F32)).astype(o_ref.dtype)

    return _pcall(body, name=name, out_shape=jax.ShapeDtypeStruct((4, half, cdim), g4.dtype), grid=(4, nh),
                  num_prefetch=1,
                  in_specs=[pl.BlockSpec((None, tr, cdim), lambda k, i, c_ref: (k, c_ref[0] * nh + i, 0)),
                            pl.BlockSpec((None, tr, cdim), lambda k, i, c_ref: (k, i, 0))],
                  out_specs=pl.BlockSpec((None, tr, cdim), lambda k, i, c_ref: (k, i, 0)),
                  sem=("parallel", "parallel"))(c_idx, g4, recv)


def _sum_chips(h4, recv3, chip_idx, *, name):
    _, n, cdim = h4.shape
    tr = _rows_per_block(n, cdim, 4, 16)

    def body(k_ref, h_ref, r_ref, o_ref):
        acc = h_ref[...].astype(F32)
        for j in range(3):
            acc = acc + r_ref[j].astype(F32)
        o_ref[...] = acc

    return _pcall(body, name=name, out_shape=jax.ShapeDtypeStruct((n, cdim), F32), grid=(n // tr,), num_prefetch=1,
                  in_specs=[pl.BlockSpec((None, tr, cdim), lambda i, k_ref: (k_ref[0], i, 0)),
                            pl.BlockSpec((3, tr, cdim), lambda i, k_ref: (0, i, 0))],
                  out_specs=pl.BlockSpec((tr, cdim), lambda i, k_ref: (i, 0)), sem=("parallel",))(chip_idx, h4, recv3)


def _add2(a, b, *, name):
    n, cdim = a.shape
    tr = _tile(n, 256, SUBLANES)

    def body(a_ref, b_ref, o_ref):
        o_ref[...] = a_ref[...] + b_ref[...]

    blk = pl.BlockSpec((tr, cdim), lambda i: (i, 0))
    return _pcall(body, name=name, out_shape=jax.ShapeDtypeStruct((n, cdim), F32), grid=(n // tr,), in_specs=[blk, blk],
                  out_specs=blk, sem=("parallel",))(a, b)


def _sum4(x4, *, name):
    _, n, cdim = x4.shape
    tr = _tile(n, 256, SUBLANES)

    def body(x_ref, o_ref):
        o_ref[...] = ((x_ref[0] + x_ref[1]) + x_ref[2]) + x_ref[3]

    return _pcall(body, name=name, out_shape=jax.ShapeDtypeStruct((n, cdim), F32), grid=(n // tr,),
                  in_specs=[pl.BlockSpec((4, tr, cdim), lambda i: (0, i, 0))],
                  out_specs=pl.BlockSpec((tr, cdim), lambda i: (i, 0)), sem=("parallel",))(x4)


def _rs_pair(items, c_idx, *, name):
    recv = _sib_halves(items, name=name + "_sib")
    return [_sum_sib(items[i], recv[i], c_idx, name=f"{name}_sum1_{i}") for i in range(len(items))]


def _rs_close(h4, recv3, chip_idx, *, name):
    q = [_sum_chips(h4[i], recv3[i], chip_idx, name=f"{name}_sum2_{i}") for i in range(len(h4))]
    return _join_halves(q, name=name + "_join")


def _all_reduce(buf, *, name):
    pair = _add2(buf, _swap_sibling(buf, name=name + "_sib"), name=name + "_add")
    return _sum4(_gather_chips(pair, name=name + "_ici"), name=name + "_sum")


def _pack_rows(flat, lead, align):
    n = flat.shape[-1]
    unit = PACK_COLS * align
    total = -(-n // unit) * unit
    flat = jnp.pad(flat, [(0, 0)] * len(lead) + [(0, total - n)])
    return flat.reshape(*lead, total // PACK_COLS, PACK_COLS)


BIG = (("w_in", 1), ("ssm_w_glu", 1), ("w_kv", 0), ("w_gate", 2), ("w_branch", 2), ("w_out", 0))
SMALL_SHARDED = (("dn_conv_w", 1), ("lru_conv_w", 1), ("b_gate", 1))
SMALL = ("norm_w", "dn_a_log", "dn_dt_bias", "dn_norm_w", "lru_conv_b", "lru_w_r", "lru_b_r", "lru_w_i", "lru_b_i",
         "lru_lambda", "ssm_log_dt", "ssm_a_re", "ssm_a_im", "ssm_b_re", "ssm_b_im", "ssm_c_re", "ssm_c_im", "ssm_d",
         "ssm_b_glu", "mem_norm_w")
WEIGHTS = ("norm_w", "w_in", "dn_conv_w", "dn_a_log", "dn_dt_bias", "dn_norm_w", "lru_conv_w", "lru_conv_b",
           "lru_w_r", "lru_b_r", "lru_w_i", "lru_b_i", "lru_lambda", "ssm_log_dt", "ssm_a_re", "ssm_a_im", "ssm_b_re",
           "ssm_b_im", "ssm_c_re", "ssm_c_im", "ssm_d", "ssm_w_glu", "ssm_b_glu", "mem_norm_w", "w_kv", "w_gate",
           "b_gate", "w_branch", "w_out", "final_norm_w")


REST = BIG[1:]


def _gather_rest(wts, l):
    small = _pack_rows(jnp.concatenate([wts[n][l].reshape(-1) for n, _ in SMALL_SHARDED]), (), 2 * SUBLANES)
    return _ag4([wts[n][l].astype(BF16) for n, _ in REST] + [small])


def _full_rest(g, wts):
    out = {n: jnp.concatenate(list(g[i]), axis=ax) for i, (n, ax) in enumerate(REST)}
    flat, off = g[-1].reshape(4, -1), 0
    for n, ax in SMALL_SHARDED:
        shp = wts[n].shape[1:]
        sz = math.prod(shp)
        out[n] = jnp.concatenate(list(flat[:, off:off + sz].reshape(4, *shp)), axis=ax)
        off += sz
    return out


def _chip_rows(t):
    return t.reshape(4, t.shape[0] // 4, t.shape[1])


def _w_in_layout(w, bw, heads, rank):
    d = w.shape[0]
    ba = 4 * bw
    rest = ba + 2 * heads
    return jnp.concatenate([w[:, :ba], w[:, rest:], w[:, ba:rest], jnp.zeros((d, BA_PAD - 2 * heads), w.dtype)], axis=1)


def _w_in_unlayout(dw, bw, heads, rank):
    ba = 4 * bw
    tail = 10 * bw + rank
    return jnp.concatenate([dw[:, :ba], dw[:, tail:tail + 2 * heads], dw[:, ba:tail]], axis=1)


def _lru_dense(w):
    nb, blk, _ = w.shape
    return jnp.einsum("nij,nm->nimj", w, jnp.eye(nb, dtype=w.dtype)).reshape(nb * blk, nb * blk)


def _w8(w):
    return jnp.concatenate([w, jnp.zeros((SUBLANES - CONV_WIDTH, w.shape[1]), w.dtype)], axis=0)


def _layer_fwd(x, mem, w_in_g, wts, prm, l, dn_comm):
    s, d = x.shape
    bw = d // N_BRANCH
    heads = bw // DN_HEAD_DIM
    rank = wts["w_gate"].shape[2]
    tag = f"l{l}_"
    sv = {"x": x}
    w_in = _w_in_layout(jnp.concatenate(list(w_in_g), axis=1), bw, heads, rank)
    sv["w_in"] = w_in
    h = _rms_fwd(x, prm["norm_w"], name=tag + "norm")
    p, rest = _mm(h, w_in, comm=_gather_rest(wts, l), name=tag + "in_proj")
    full = _full_rest(rest, wts)
    sv["h"], sv["p"] = h, p
    conv_a = _conv_fwd(p, 0, 3 * bw, _w8(full["dn_conv_w"]), jnp.zeros((1, 3 * bw), F32), name=tag + "dn_conv")
    ba_blk = (10 * bw + rank) // BA_PAD
    dn = _dn_fwd(p, conv_a, prm["dn_a_log"], prm["dn_dt_bias"], prm["dn_norm_w"], bw=bw, ba_blk=ba_blk,
                 comm=dn_comm, name=tag + "dn")
    (o_a, dn_states), dn_res = dn if dn_comm is not None else (dn, None)
    sv["conv_a"], sv["dn_states"] = conv_a, dn_states
    xc = _conv_fwd(p, 4, bw, _w8(full["lru_conv_w"]), prm["lru_conv_b"].reshape(1, bw), name=tag + "lru_conv")
    (wr, wi), lru_vjp = jax.vjp(lambda a, b: (_lru_dense(a), _lru_dense(b)), prm["lru_w_r"], prm["lru_w_i"])
    row = lambda v: v.reshape(1, bw)
    lru_args = (wr, row(prm["lru_b_r"]), wi, row(prm["lru_b_i"]), row(prm["lru_lambda"]))
    o_b, lru_saves = _lru_fwd(p, xc, *lru_args, bw=bw, z_blk=5, name=tag + "lru")
    sv["xc"], sv["lru_saves"], sv["lru_args"], sv["lru_vjp"] = xc, lru_saves, lru_args, lru_vjp
    prep, s5_vjp = jax.vjp(_s5_prep, prm["ssm_log_dt"], prm["ssm_a_re"], prm["ssm_a_im"], prm["ssm_b_re"],
                           prm["ssm_b_im"], prm["ssm_c_re"], prm["ssm_c_im"], prm["ssm_d"])
    u_blk0 = 6 * bw // LANES
    yg, s5_saves = _s5_fwd(p, prep, bw=bw, u_blk0=u_blk0, name=tag + "s5")
    pre = _mm(yg, full["ssm_w_glu"], bias=prm["ssm_b_glu"].reshape(1, 2 * bw), name=tag + "glu_proj")
    o_c = _glu_fwd(pre, p, bw=bw, z_blk=7, name=tag + "glu")
    sv["prep"], sv["s5_vjp"], sv["s5_saves"], sv["yg"], sv["pre"] = prep, s5_vjp, s5_saves, yg, pre
    mem_n = _rms_fwd(mem, prm["mem_norm_w"], name=tag + "mem_norm")
    kv = _mm(mem_n, full["w_kv"], name=tag + "kv_proj")
    o_d = _attn_fwd(p, kv, bw=bw, q_blk=8, z_blk=9, name=tag + "attn")
    sv["mem_n"], sv["kv"] = mem_n, kv
    o4 = jnp.stack([o_a, o_b, o_c, o_d], axis=0)
    bg = full["b_gate"].reshape(N_BRANCH, 1, d)
    g_blk = 10 * bw // rank
    merged = _merge_fwd(p, o4, full["w_gate"], bg, full["w_branch"], rank=rank, g_blk=g_blk, name=tag + "merge")
    sv["o4"], sv["bg"], sv["merged"] = o4, bg, merged
    return _mm(merged, full["w_out"], add=x, name=tag + "out_proj"), sv, full, dn_res


def _layer_bwd(dx_out, mem, sv, full, prm, l, dn_comm_of):
    x, p, h = sv["x"], sv["p"], sv["h"]
    s, d = x.shape
    bw = d // N_BRANCH
    heads = bw // DN_HEAD_DIM
    rank = full["w_gate"].shape[1]
    tag = f"l{l}b_"
    big, small = {}, {}
    dmerged = _mm(dx_out, full["w_out"], tb=True, out_dtype=BF16, name=tag + "out_dx")
    big["w_out"] = _mm(sv["merged"], dx_out, ta=True, out_dtype=BF16, name=tag + "out_dw")
    g_blk = 10 * bw // rank
    dpre, dbr, dbg = _merge_bwd(p, sv["o4"], full["w_gate"], sv["bg"], full["w_branch"], dmerged, rank=rank,
                                g_blk=g_blk, name=tag + "merge")
    small["b_gate"] = dbg.reshape(N_BRANCH, d)
    glow = p[:, 10 * bw:10 * bw + rank].astype(BF16)
    dglow = None
    dwg, dwb, d_o = [], [], []
    for n in range(N_BRANCH):
        dglow = _mm(dpre, full["w_gate"], la=n, lb=n, tb=True, add=dglow, name=tag + f"gate_dx{n}")
        dwg.append(_mm(glow, dpre, ta=True, lb=n, out_dtype=BF16, nsplit=4, name=tag + f"gate_dw{n}"))
        d_o.append(_mm(dbr, full["w_branch"], la=n, lb=n, tb=True, name=tag + f"branch_dx{n}"))
        dwb.append(_mm(sv["o4"], dbr, ta=True, la=n, lb=n, out_dtype=BF16, nsplit=4, name=tag + f"branch_dw{n}"))
    dn_comm = dn_comm_of([_chip_rows(big["w_out"]), *dwg, *dwb])
    ba_blk = (10 * bw + rank) // BA_PAD
    dn = _dn_bwd(p, sv["conv_a"], sv["dn_states"], d_o[0], prm["dn_a_log"], prm["dn_dt_bias"], prm["dn_norm_w"],
                 bw=bw, ba_blk=ba_blk, comm=dn_comm, name=tag + "dn")
    (dconv, dz_a, dba, dal, ddt, dnw), dn_res = dn if dn_comm is not None else (dn, None)
    small["dn_a_log"] = dal[0, heads:2 * heads]
    small["dn_dt_bias"] = ddt[0, heads:2 * heads]
    small["dn_norm_w"] = dnw[0]
    dqkv, dw8_a, _ = _conv_bwd(p, 0, 3 * bw, _w8(full["dn_conv_w"]), dconv, name=tag + "dn_conv")
    small["dn_conv_w"] = dw8_a[:CONV_WIDTH]
    dxc, dz_b, dwr, dwi, dbr_, dbi_, dlam = _lru_bwd(p, sv["xc"], sv["lru_saves"], d_o[1], *sv["lru_args"], bw=bw,
                                                     z_blk=5, name=tag + "lru")
    small["lru_w_r"], small["lru_w_i"] = sv["lru_vjp"]((dwr, dwi))
    small["lru_b_r"], small["lru_b_i"], small["lru_lambda"] = dbr_[0], dbi_[0], dlam[0]
    dlx, dw8_b, dcb = _conv_bwd(p, 4, bw, _w8(full["lru_conv_w"]), dxc, name=tag + "lru_conv")
    small["lru_conv_w"] = dw8_b[:CONV_WIDTH]
    small["lru_conv_b"] = dcb[0]
    dpre_glu, dz_c, dbglu = _glu_bwd(sv["pre"], p, d_o[2], bw=bw, z_blk=7, name=tag + "glu")
    small["ssm_b_glu"] = dbglu[0]
    dyg = _mm(dpre_glu, full["ssm_w_glu"], tb=True, name=tag + "glu_dx")
    big["ssm_w_glu"] = _mm(sv["yg"], dpre_glu, ta=True, out_dtype=BF16, nsplit=4, name=tag + "glu_dw")
    s5 = _s5_bwd(p, sv["prep"], sv["s5_saves"], dyg, bw=bw, u_blk0=6 * bw // LANES, name=tag + "s5")
    du = s5[0]
    (small["ssm_log_dt"], small["ssm_a_re"], small["ssm_a_im"], small["ssm_b_re"], small["ssm_b_im"],
     small["ssm_c_re"], small["ssm_c_im"], small["ssm_d"]) = sv["s5_vjp"](tuple(s5[1:]))
    dq, dz_d, dkv = _attn_bwd(p, sv["kv"], d_o[3], bw=bw, q_blk=8, z_blk=9, name=tag + "attn")
    big["w_kv"] = _mm(sv["mem_n"], dkv, ta=True, out_dtype=BF16, name=tag + "kv_dw")
    dmem_n = _mm(dkv, full["w_kv"], tb=True, name=tag + "kv_dx")
    _, dmnw = _rms_bwd(mem, prm["mem_norm_w"], dmem_n, None, name=tag + "mem_norm")
    small["mem_norm_w"] = dmnw[0]
    dp = jnp.concatenate([dqkv, dz_a, dlx, dz_b, du, dz_c, dq, dz_d, dglow, dba], axis=1).astype(BF16)
    dh = _mm(dp, sv["w_in"], tb=True, name=tag + "in_dx")
    dw_in = _w_in_unlayout(_mm(h, dp, ta=True, out_dtype=BF16, name=tag + "in_dw"), bw, heads, rank)
    dx, dnw_in = _rms_bwd(x, prm["norm_w"], dh, dx_out, name=tag + "norm")
    small["norm_w"] = dnw_in[0]
    late = [dw_in.reshape(d, 4, dw_in.shape[1] // 4).transpose(1, 0, 2), big["ssm_w_glu"], _chip_rows(big["w_kv"])]
    return dx, late, small, dn_res


def _step(wts, mom, vel, x, mem, target):
    depth = wts["norm_w"].shape[0]
    xi, yi, ci = lax.axis_index("x"), lax.axis_index("y"), lax.axis_index("c")
    c_idx = ci.astype(jnp.int32).reshape(1)
    chip = (2 * xi + yi).astype(jnp.int32)
    chip_idx = chip.reshape(1)
    x, mem, target = x[0], mem[0], target[0]

    prms = [{n: wts[n][l] for n in SMALL} for l in range(depth)]
    w_in_item = lambda l: [wts["w_in"][l].astype(BF16)]
    w_in_g = _run_comm(_ag4(w_in_item(0)), name="gather_w_in0")[0]
    saves, fulls = [], []
    act = x
    for l in range(depth):
        nxt = _ag4(w_in_item(l + 1)) if l + 1 < depth else None
        act, sv, full, res = _layer_fwd(act, mem, w_in_g, wts, prms[l], l, nxt)
        saves.append(sv)
        fulls.append(full)
        w_in_g = res[0] if nxt is not None else None
    loss_part, dx, dfw = _loss_head(act, wts["final_norm_w"], target, name="loss_head")
    loss = lax.psum(loss_part[0, 0], ("x", "y", "c"))

    big_g = [None] * depth
    small_g = [None] * depth
    pending = []
    done = {}
    for l in reversed(range(depth)):
        def dn_comm_of(early, l=l):
            pending.append((l, "early", _rs_pair(early, c_idx, name=f"scatter_g{l}a")))
            return _to_chips([t for _, _, h4 in pending for t in h4])

        dx, late, small_g[l], res = _layer_bwd(dx, mem, saves[l], fulls[l], prms[l], l, dn_comm_of)
        off = 0
        for ll, grp, h4 in pending:
            done[ll, grp] = _rs_close(h4, res[off:off + len(h4)], chip_idx, name=f"scatter_g{ll}{grp[0]}c")
            off += len(h4)
        pending = [(l, "late", _rs_pair(late, c_idx, name=f"scatter_g{l}b"))]
    (l, grp, h4), = pending
    done[l, grp] = _rs_close(h4, _run_comm(_to_chips(h4), name=f"scatter_g{l}b_ici"), chip_idx,
                             name=f"scatter_g{l}lc")
    for l in range(depth):
        e, t = done[l, "early"], done[l, "late"]
        big_g[l] = {"w_in": t[0], "ssm_w_glu": t[1], "w_kv": t[2], "w_out": e[0],
                    "w_gate": jnp.stack(e[1:1 + N_BRANCH], axis=0),
                    "w_branch": jnp.stack(e[1 + N_BRANCH:1 + 2 * N_BRANCH], axis=0)}

    names = SMALL + tuple(n for n, _ in SMALL_SHARDED)
    flat = jnp.concatenate([small_g[l][n].reshape(-1) for l in range(depth) for n in names] + [dfw.reshape(-1)])
    red = _all_reduce(_pack_rows(flat, (), 256), name="reduce_small").reshape(-1)
    grads, off = {n: [] for n in names}, 0
    for l in range(depth):
        for n in names:
            shp = small_g[l][n].shape
            sz = math.prod(shp)
            grads[n].append(red[off:off + sz].reshape(shp))
            off += sz
    grads = {n: jnp.stack(v, axis=0) for n, v in grads.items()}
    grads["final_norm_w"] = red[off:off + dfw.size].reshape(wts["final_norm_w"].shape)
    for n, ax in SMALL_SHARDED:
        width = wts[n].shape[-1]
        grads[n] = lax.dynamic_slice_in_dim(grads[n], chip * width, width, axis=ax + 1)
    for n, _ in BIG:
        grads[n] = jnp.stack([big_g[l][n] for l in range(depth)], axis=0)

    delta, new_m, new_v = {}, {}, {}
    for n, _ in BIG:
        shp = wts[n].shape
        two = lambda t: t.reshape(-1, shp[-1])
        dlt, nm, nv = _adamw(two(wts[n]), two(grads[n]), two(mom[n]), two(vel[n]), name="adamw_" + n)
        delta[n], new_m[n], new_v[n] = dlt.reshape(shp), nm.reshape(shp), nv.reshape(shp)
    rest = [n for n in WEIGHTS if n not in dict(BIG)]
    cat = lambda src: _pack_rows(jnp.concatenate([src[n].reshape(-1) for n in rest]), (), SUBLANES)
    dlt, nm, nv = _adamw(cat(wts), cat(grads), cat(mom), cat(vel), name="adamw_small")
    off = 0
    for n in rest:
        shp = wts[n].shape
        sz = math.prod(shp)
        for dst, src in ((delta, dlt), (new_m, nm), (new_v, nv)):
            dst[n] = src.reshape(-1)[off:off + sz].reshape(shp)
        off += sz
    return (loss, dx[None], *[grads[n] for n in WEIGHTS], *[delta[n] for n in WEIGHTS], *[new_m[n] for n in WEIGHTS],
            *[new_v[n] for n in WEIGHTS])


def kernel(x, mem, norm_w, w_in, dn_conv_w, dn_a_log, dn_dt_bias, dn_norm_w, lru_conv_w, lru_conv_b, lru_w_r, lru_b_r, lru_w_i, lru_b_i, lru_lambda, ssm_log_dt, ssm_a_re, ssm_a_im, ssm_b_re, ssm_b_im, ssm_c_re, ssm_c_im, ssm_d, ssm_w_glu, ssm_b_glu, mem_norm_w, w_kv, w_gate, b_gate, w_branch, w_out, final_norm_w, loss_target, m_norm_w, m_w_in, m_dn_conv_w, m_dn_a_log, m_dn_dt_bias, m_dn_norm_w, m_lru_conv_w, m_lru_conv_b, m_lru_w_r, m_lru_b_r, m_lru_w_i, m_lru_b_i, m_lru_lambda, m_ssm_log_dt, m_ssm_a_re, m_ssm_a_im, m_ssm_b_re, m_ssm_b_im, m_ssm_c_re, m_ssm_c_im, m_ssm_d, m_ssm_w_glu, m_ssm_b_glu, m_mem_norm_w, m_w_kv, m_w_gate, m_b_gate, m_w_branch, m_w_out, m_final_norm_w, v_norm_w, v_w_in, v_dn_conv_w, v_dn_a_log, v_dn_dt_bias, v_dn_norm_w, v_lru_conv_w, v_lru_conv_b, v_lru_w_r, v_lru_b_r, v_lru_w_i, v_lru_b_i, v_lru_lambda, v_ssm_log_dt, v_ssm_a_re, v_ssm_a_im, v_ssm_b_re, v_ssm_b_im, v_ssm_c_re, v_ssm_c_im, v_ssm_d, v_ssm_w_glu, v_ssm_b_glu, v_mem_norm_w, v_w_kv, v_w_gate, v_b_gate, v_w_branch, v_w_out, v_final_norm_w):
    given = dict(locals())
    wts = {n: given[n] for n in WEIGHTS}
    mom = {n: given["m_" + n] for n in WEIGHTS}
    vel = {n: given["v_" + n] for n in WEIGHTS}
    return _step(wts, mom, vel, x, mem, loss_target)
```

```python
import functools
import math

import jax
import jax.numpy as jnp
import numpy as np
from jax import lax
from jax.experimental import pallas as pl
from jax.experimental.pallas import tpu as pltpu

F32 = jnp.float32
BF16 = jnp.bfloat16
HIGHEST = lax.Precision.HIGHEST
MESH_ID = pl.DeviceIdType.MESH

NORM_EPS = 1e-6
CONV_WIDTH = 4
DN_HEAD_DIM = 128
DN_CHUNK = 64
LRU_C = 8.0
MEM_HEADS = 4
N_BRANCH = 4
ADAM_LR, ADAM_B1, ADAM_B2, ADAM_EPS, ADAM_WD, ADAM_STEP = 0.001, 0.9, 0.999, 1e-08, 0.01, 10

LANES = 128
SUBLANES = 8
VMEM_LIMIT = 56 * 2 ** 20
PACK_COLS = 1024
ELEMENTWISE_BLOCK_BYTES = 2 * 2 ** 20
BA_PAD = 256


def _tile(n, pref, align=LANES):
    if n <= pref:
        return n
    t = pref - pref % align
    while t > 0:
        if n % t == 0:
            return t
        t -= align
    return n


class _Comm:
    def __init__(self, inputs, out_shapes, n_sems, n_local, start, finish):
        self.inputs, self.out_shapes, self.n_sems, self.n_local = list(inputs), tuple(out_shapes), n_sems, n_local
        self.start, self.finish = start, finish

    def scratch(self):
        s = [pltpu.SemaphoreType.DMA((self.n_sems,)), pltpu.SemaphoreType.DMA((self.n_sems,))]
        return s + ([pltpu.SemaphoreType.DMA((self.n_local,))] if self.n_local else [])

    def split(self, refs):
        ni, no = len(self.inputs), len(self.out_shapes)
        sems = list(refs[ni + no:]) + ([] if self.n_local else [None])
        return (refs[:ni], refs[ni:ni + no], *sems)


def _pcall(body, *, name, out_shape, grid=(), in_specs=None, out_specs=None, scratch=(), sem=None,
           num_prefetch=0, comm=None):
    params = dict(vmem_limit_bytes=VMEM_LIMIT)
    if sem is not None:
        params["dimension_semantics"] = sem if comm is None else ("arbitrary",) * len(grid)
    scratch = list(scratch)
    if comm is None:
        run_body = body
    else:
        assert not num_prefetch
        single = not isinstance(out_shape, (tuple, list))
        outs = (out_shape,) if single else tuple(out_shape)
        ospecs = (out_specs,) if single else tuple(out_specs)
        n_in, n_out, n_scr = len(in_specs), len(outs), len(scratch)
        n_ci, n_co = len(comm.inputs), len(comm.out_shapes)
        in_specs = list(in_specs) + [HBM_SPEC] * n_ci
        out_shape = outs + comm.out_shapes
        out_specs = ospecs + (HBM_SPEC,) * n_co
        scratch = scratch + comm.scratch()

        def run_body(*refs):
            ins, rest = refs[:n_in], refs[n_in:]
            cins, rest = rest[:n_ci], rest[n_ci:]
            o, rest = rest[:n_out], rest[n_out:]
            couts, rest = rest[:n_co], rest[n_co:]
            cargs = comm.split((*cins, *couts, *rest[n_scr:]))
            first = functools.reduce(jnp.logical_and, [pl.program_id(a) == 0 for a in range(len(grid))])
            last = functools.reduce(jnp.logical_and, [pl.program_id(a) == grid[a] - 1 for a in range(len(grid))])

            @pl.when(first)
            def _():
                comm.start(*cargs)

            body(*ins, *o, *rest[:n_scr])

            @pl.when(last)
            def _():
                comm.finish(*cargs)

    if num_prefetch:
        call = pl.pallas_call(
            run_body, name=name, out_shape=out_shape,
            grid_spec=pltpu.PrefetchScalarGridSpec(num_scalar_prefetch=num_prefetch, grid=grid, in_specs=in_specs,
                                                   out_specs=out_specs, scratch_shapes=scratch),
            compiler_params=pltpu.CompilerParams(**params), interpret=False)
    else:
        call = pl.pallas_call(run_body, name=name, out_shape=out_shape, grid=grid, in_specs=in_specs,
                              out_specs=out_specs, scratch_shapes=scratch,
                              compiler_params=pltpu.CompilerParams(**params), interpret=False)
    if comm is None:
        return call

    def run(*operands):
        res = call(*operands, *comm.inputs)
        return (res[0] if single else tuple(res[:n_out])), tuple(res[n_out:])

    return run


def _run_comm(comm, *, name):
    def body(*refs):
        args = comm.split(refs)
        comm.start(*args)
        comm.finish(*args)

    return pl.pallas_call(body, name=name, out_shape=comm.out_shapes, in_specs=[HBM_SPEC] * len(comm.inputs),
                          out_specs=tuple(HBM_SPEC for _ in comm.out_shapes), scratch_shapes=comm.scratch(),
                          interpret=False)(*comm.inputs)


HBM_SPEC = pl.BlockSpec(memory_space=pl.ANY)


@functools.partial(jax.custom_vjp, nondiff_argnums=(2, 3))
def _bdot(a, b, ca, cb):
    return lax.dot_general(a.astype(BF16), b.astype(BF16), (((ca,), (cb,)), ((), ())), preferred_element_type=F32)


def _bdot_fwd(a, b, ca, cb):
    return _bdot(a, b, ca, cb), (a, b)


def _bdot_bwd(ca, cb, res, ct):
    a, b = res
    da = _bdot(ct, b, 1, 1 - cb) if ca == 1 else _bdot(b, ct, 1 - cb, 1)
    db = _bdot(a, ct, 1 - ca, 0) if cb == 0 else _bdot(ct, a, 0, 1 - ca)
    return da.astype(a.dtype), db.astype(b.dtype)


_bdot.defvjp(_bdot_fwd, _bdot_bwd)


def _split_bf16(a):
    hi = a.astype(BF16)
    return hi, (a - hi.astype(F32)).astype(BF16)


@functools.partial(jax.custom_vjp, nondiff_argnums=(2, 3))
def _xdot(a, b, ca, cb):
    dims = (((ca,), (cb,)), ((), ()))
    ah, al = _split_bf16(a)
    bh, bl = _split_bf16(b)
    dot = lambda p, q: lax.dot_general(p, q, dims, preferred_element_type=F32)
    return dot(ah, bh) + (dot(ah, bl) + dot(al, bh))


def _xdot_fwd(a, b, ca, cb):
    return _xdot(a, b, ca, cb), (a, b)


def _xdot_bwd(ca, cb, res, ct):
    a, b = res
    da = _xdot(ct, b, 1, 1 - cb) if ca == 1 else _xdot(b, ct, 1 - cb, 1)
    db = _xdot(a, ct, 1 - ca, 0) if cb == 0 else _xdot(ct, a, 0, 1 - ca)
    return da, db


_xdot.defvjp(_xdot_fwd, _xdot_bwd)


def _sigmoid(x):
    return 1.0 / (1.0 + jnp.exp(-x))


def _silu(x):
    return x * _sigmoid(x)


def _softplus(x):
    return jnp.maximum(x, 0.0) + jnp.log(1.0 + jnp.exp(-jnp.abs(x)))


def _expm1(x):
    small = x * (1.0 + x * (0.5 + x * (1.0 / 6.0 + x * (1.0 / 24.0 + x * (1.0 / 120.0 + x * (1.0 / 720.0))))))
    return jnp.where(jnp.abs(x) < 0.1, small, jnp.exp(x) - 1.0)


def _gelu(x):
    return 0.5 * x * (1.0 + jnp.tanh(math.sqrt(2.0 / math.pi) * (x + 0.044715 * x * x * x)))


def _rms(x, w):
    var = jnp.mean(x * x, axis=-1, keepdims=True)
    return x * lax.rsqrt(var + NORM_EPS) * w


def _pick_lane(v, idx):
    lane = lax.broadcasted_iota(jnp.int32, v.shape, 1)
    return jnp.sum(jnp.where(lane == idx, v, 0.0), axis=1, keepdims=True)


def _pick_row(v, idx):
    row = lax.broadcasted_iota(jnp.int32, v.shape, 0)
    return jnp.sum(jnp.where(row == idx, v, 0.0), axis=0, keepdims=True)


def _mm(a, b, *, name, ta=False, tb=False, out_dtype=F32, add=None, bias=None, la=None, lb=None, nsplit=None,
        comm=None, tm=1024, tn=1024, tk=2048):
    a2 = a.shape[-2:]
    b2 = b.shape[-2:]
    m, k = (a2[1], a2[0]) if ta else a2
    n = b2[0] if tb else b2[1]
    assert (b2[1] if tb else b2[0]) == k
    tm, tn, tk = _tile(m, tm), _tile(n // (nsplit or 1), tn), _tile(k, tk)
    nk = k // tk

    def a_map(i, j, kk):
        idx = (kk, i) if ta else (i, kk)
        return idx if la is None else (la,) + idx

    def b_map(i, j, kk):
        idx = (j, kk) if tb else (kk, j)
        return idx if lb is None else (lb,) + idx

    a_blk = (tk, tm) if ta else (tm, tk)
    b_blk = (tn, tk) if tb else (tk, tn)
    in_specs = [pl.BlockSpec(a_blk if la is None else (None,) + a_blk, a_map),
                pl.BlockSpec(b_blk if lb is None else (None,) + b_blk, b_map)]
    operands = [a, b]
    if add is not None:
        in_specs.append(pl.BlockSpec((tm, tn), lambda i, j, kk: (i, j)))
        operands.append(add)
    if bias is not None:
        in_specs.append(pl.BlockSpec((1, tn), lambda i, j, kk: (0, j)))
        operands.append(bias)
    dims = (((0 if ta else 1,), (1 if tb else 0,)), ((), ()))

    def body(*refs):
        a_ref, b_ref = refs[0], refs[1]
        rest = list(refs[2:])
        add_ref = rest.pop(0) if add is not None else None
        bias_ref = rest.pop(0) if bias is not None else None
        o_ref, acc_ref = rest
        kk = pl.program_id(2)

        @pl.when(kk == 0)
        def _():
            acc_ref[...] = jnp.zeros_like(acc_ref)

        acc_ref[...] += lax.dot_general(a_ref[...].astype(BF16), b_ref[...].astype(BF16), dims,
                                        preferred_element_type=F32)

        @pl.when(kk == nk - 1)
        def _():
            r = acc_ref[...]
            if add_ref is not None:
                r = r + add_ref[...].astype(F32)
            if bias_ref is not None:
                r = r + bias_ref[...]
            o_ref[...] = r.astype(out_dtype)

    if nsplit is None:
        out_shape = jax.ShapeDtypeStruct((m, n), out_dtype)
        out_spec = pl.BlockSpec((tm, tn), lambda i, j, kk: (i, j))
    else:
        per = n // nsplit // tn
        out_shape = jax.ShapeDtypeStruct((nsplit, m, n // nsplit), out_dtype)
        out_spec = pl.BlockSpec((None, tm, tn), lambda i, j, kk: (j // per, i, j % per))
    return _pcall(body, name=name, comm=comm, out_shape=out_shape, grid=(m // tm, n // tn, nk), in_specs=in_specs,
                  out_specs=out_spec, scratch=[pltpu.VMEM((tm, tn), F32)],
                  sem=("parallel", "parallel", "arbitrary"))(*operands)


def _rms_fwd(x, w, *, name):
    s, d = x.shape
    t = _tile(s, 256, SUBLANES)

    def body(x_ref, w_ref, o_ref):
        o_ref[...] = _rms(x_ref[...], w_ref[...]).astype(BF16)

    return _pcall(body, name=name, out_shape=jax.ShapeDtypeStruct((s, d), BF16), grid=(s // t,),
                  in_specs=[pl.BlockSpec((t, d), lambda i: (i, 0)), pl.BlockSpec((1, d), lambda i: (0, 0))],
                  out_specs=pl.BlockSpec((t, d), lambda i: (i, 0)), sem=("parallel",))(x, w.reshape(1, d))


def _rms_bwd(x, w, dh, res, *, name):
    s, d = x.shape
    t = _tile(s, 256, SUBLANES)

    def body(*refs):
        if res is None:
            x_ref, w_ref, dh_ref, dx_ref, dw_ref = refs
            res_ref = None
        else:
            x_ref, w_ref, dh_ref, res_ref, dx_ref, dw_ref = refs
        _, vjp = jax.vjp(_rms, x_ref[...], w_ref[...])
        dx, dw = vjp(dh_ref[...].astype(F32))
        if res_ref is not None:
            dx = dx + res_ref[...]
        dx_ref[...] = dx

        @pl.when(pl.program_id(0) == 0)
        def _():
            dw_ref[...] = jnp.zeros_like(dw_ref)

        dw_ref[...] += dw

    tok = pl.BlockSpec((t, d), lambda i: (i, 0))
    row = pl.BlockSpec((1, d), lambda i: (0, 0))
    operands = [x, w.reshape(1, d), dh] + ([] if res is None else [res])
    return _pcall(body, name=name,
                  out_shape=(jax.ShapeDtypeStruct((s, d), F32), jax.ShapeDtypeStruct((1, d), F32)), grid=(s // t,),
                  in_specs=[tok, row, tok] + ([] if res is None else [tok]), out_specs=(tok, row),
                  sem=("arbitrary",))(*operands)


def _loss_head(x, w, target, *, name):
    s, d = x.shape
    t = _tile(s, 256, SUBLANES)

    def body(x_ref, w_ref, t_ref, loss_ref, dx_ref, dw_ref):
        def f(xv, wv):
            err = _rms(xv, wv) - t_ref[...]
            return 0.5 * jnp.sum(jnp.mean(err * err, axis=-1))

        val, vjp = jax.vjp(f, x_ref[...], w_ref[...])
        dx, dw = vjp(jnp.ones((), F32))
        dx_ref[...] = dx

        @pl.when(pl.program_id(0) == 0)
        def _():
            dw_ref[...] = jnp.zeros_like(dw_ref)
            loss_ref[...] = jnp.zeros_like(loss_ref)

        dw_ref[...] += dw
        loss_ref[...] += jnp.full(loss_ref.shape, val, F32)

    tok = pl.BlockSpec((t, d), lambda i: (i, 0))
    row = pl.BlockSpec((1, d), lambda i: (0, 0))
    return _pcall(body, name=name,
                  out_shape=(jax.ShapeDtypeStruct((1, LANES), F32), jax.ShapeDtypeStruct((s, d), F32),
                             jax.ShapeDtypeStruct((1, d), F32)),
                  grid=(s // t,), in_specs=[tok, row, tok],
                  out_specs=(pl.BlockSpec((1, LANES), lambda i: (0, 0)), tok, row), sem=("arbitrary",))(
                      x, w.reshape(1, d), target)


def _conv_shifts(prev8, cur, t):
    xp = jnp.concatenate([prev8, cur], axis=0)
    out = []
    for j in range(CONV_WIDTH):
        k = CONV_WIDTH - 1 - j
        out.append(cur if k == 0 else pltpu.roll(xp, k, 0)[SUBLANES:SUBLANES + t])
    return out


def _conv_fwd(p, col_blk, width, w8, b, *, name):
    s = p.shape[0]
    t = _tile(s, 256, SUBLANES)
    r8 = t // SUBLANES

    def body(cur_ref, prev_ref, w_ref, b_ref, y_ref):
        i = pl.program_id(0)
        prev8 = jnp.where(i == 0, 0.0, prev_ref[...])
        sh = _conv_shifts(prev8, cur_ref[...], t)
        w = w_ref[...]
        y = b_ref[...] + sh[0] * w[0:1]
        for j in range(1, CONV_WIDTH):
            y = y + sh[j] * w[j:j + 1]
        y_ref[...] = y

    return _pcall(body, name=name, out_shape=jax.ShapeDtypeStruct((s, width), F32), grid=(s // t,),
                  in_specs=[pl.BlockSpec((t, width), lambda i: (i, col_blk)),
                            pl.BlockSpec((SUBLANES, width), lambda i: (jnp.maximum(i * r8 - 1, 0), col_blk)),
                            pl.BlockSpec((SUBLANES, width), lambda i: (0, 0)),
                            pl.BlockSpec((1, width), lambda i: (0, 0))],
                  out_specs=pl.BlockSpec((t, width), lambda i: (i, 0)), sem=("parallel",))(p, p, w8, b)


def _conv_bwd(p, col_blk, width, w8, dy, *, name):
    s = p.shape[0]
    t = _tile(s, 256, SUBLANES)
    r8 = t // SUBLANES
    nt = s // t

    def body(cur_ref, prev_ref, w_ref, dy_ref, dyn_ref, dx_ref, dw_ref, db_ref):
        i = pl.program_id(0)
        prev8 = jnp.where(i == 0, 0.0, prev_ref[...])
        sh = _conv_shifts(prev8, cur_ref[...], t)
        dy = dy_ref[...]
        next8 = jnp.where(i == nt - 1, 0.0, dyn_ref[...])
        dyp = jnp.concatenate([dy, next8], axis=0)
        w = w_ref[...]
        rows = lax.broadcasted_iota(jnp.int32, (SUBLANES, width), 0)
        dx = dy * w[CONV_WIDTH - 1:CONV_WIDTH]
        dw = jnp.zeros((SUBLANES, width), F32)
        for j in range(CONV_WIDTH):
            k = CONV_WIDTH - 1 - j
            if k:
                dx = dx + pltpu.roll(dyp, t + SUBLANES - k, 0)[0:t] * w[j:j + 1]
            dw = dw + jnp.where(rows == j, jnp.sum(dy * sh[j], axis=0, keepdims=True), 0.0)
        dx_ref[...] = dx

        @pl.when(i == 0)
        def _():
            dw_ref[...] = jnp.zeros_like(dw_ref)
            db_ref[...] = jnp.zeros_like(db_ref)

        dw_ref[...] += dw
        db_ref[...] += jnp.sum(dy, axis=0, keepdims=True)

    return _pcall(body, name=name,
                  out_shape=(jax.ShapeDtypeStruct((s, width), F32), jax.ShapeDtypeStruct((SUBLANES, width), F32),
                             jax.ShapeDtypeStruct((1, width), F32)),
                  grid=(nt,),
                  in_specs=[pl.BlockSpec((t, width), lambda i: (i, col_blk)),
                            pl.BlockSpec((SUBLANES, width), lambda i: (jnp.maximum(i * r8 - 1, 0), col_blk)),
                            pl.BlockSpec((SUBLANES, width), lambda i: (0, 0)),
                            pl.BlockSpec((t, width), lambda i: (i, 0)),
                            pl.BlockSpec((SUBLANES, width), lambda i: (jnp.minimum((i + 1) * r8, s // SUBLANES - 1), 0))],
                  out_specs=(pl.BlockSpec((t, width), lambda i: (i, 0)),
                             pl.BlockSpec((SUBLANES, width), lambda i: (0, 0)),
                             pl.BlockSpec((1, width), lambda i: (0, 0))),
                  sem=("arbitrary",))(p, p, w8, dy, dy)


def _dn_chunk(state, c, z, ba, alog_row, dt_row, nw_row, *, heads, bw):
    cs = c.shape[0]
    hd = DN_HEAD_DIM
    qkv = _silu(c)
    gfull = -jnp.exp(alog_row) * _softplus(ba + dt_row)
    beta_full = _sigmoid(ba)
    ri = lax.broadcasted_iota(jnp.int32, (cs, cs), 0)
    ci = lax.broadcasted_iota(jnp.int32, (cs, cs), 1)
    causal = ri >= ci
    strict = ri > ci
    tril = causal.astype(F32)
    eye = (ri == ci).astype(F32)
    gc = _xdot(tril, gfull, 1, 0)
    gct = _xdot(gfull, tril, 0, 1)
    outs, states = [], []
    for h in range(heads):
        q = qkv[:, h * hd:(h + 1) * hd]
        k = qkv[:, bw + h * hd:bw + (h + 1) * hd]
        v = qkv[:, 2 * bw + h * hd:2 * bw + (h + 1) * hd]
        q = q * lax.rsqrt(jnp.sum(q * q, axis=-1, keepdims=True) + NORM_EPS) * (hd ** -0.5)
        k = k * lax.rsqrt(jnp.sum(k * k, axis=-1, keepdims=True) + NORM_EPS)
        beta = _pick_lane(beta_full, h)
        g_col = _pick_lane(gc, heads + h)
        g_row = _pick_row(gct, heads + h)
        decay = jnp.exp(jnp.where(causal, g_col - g_row, -1e30))
        k_beta = k * beta
        v_beta = v * beta
        kk = _bdot(k_beta, k, 1, 1) * decay
        m = -jnp.where(strict, kk, 0.0)
        tinv = eye + m
        pw = m
        for _ in range(int(math.log2(cs)) - 1):
            pw = _xdot(pw, pw, 1, 0)
            tinv = tinv + _xdot(tinv, pw, 1, 0)
        rhs = jnp.concatenate([v_beta, k_beta * jnp.exp(g_col)], axis=-1)
        sol = _xdot(tinv, rhs, 1, 0)
        u, w = sol[:, :hd], sol[:, hd:]
        qk = jnp.where(causal, _bdot(q, k, 1, 1) * decay, 0.0)
        g_last = _pick_row(g_col, cs - 1)
        k_dec = k * jnp.exp(g_last - g_col)
        q_dec = q * jnp.exp(g_col)
        s_h = state[h]
        v_new = u - _bdot(w, s_h, 1, 0)
        o = _bdot(q_dec, s_h, 1, 0) + _bdot(qk, v_new, 1, 0)
        states.append(s_h * jnp.exp(g_last) + _bdot(k_dec, v_new, 0, 0))
        outs.append(_rms(o, nw_row) * _silu(z[:, h * hd:(h + 1) * hd]))
    return jnp.concatenate(outs, axis=-1), tuple(states)


def _dn_rows(a_log, dt_bias, heads):
    z = jnp.zeros((heads,), F32)
    pad = jnp.zeros((BA_PAD - 2 * heads,), F32)
    return (jnp.concatenate([z, a_log, pad]).reshape(1, BA_PAD), jnp.concatenate([z, dt_bias, pad]).reshape(1, BA_PAD))


def _dn_fwd(p, conv, a_log, dt_bias, norm_w, *, bw, ba_blk, name, comm=None):
    s = p.shape[0]
    heads = bw // DN_HEAD_DIM
    cs = min(DN_CHUNK, s)
    n = s // cs
    hd = DN_HEAD_DIM
    alog_row, dt_row = _dn_rows(a_log, dt_bias, heads)
    fn = functools.partial(_dn_chunk, heads=heads, bw=bw)

    def body(c_ref, z_ref, ba_ref, al_ref, dt_ref, nw_ref, o_ref, save_ref, st_ref):
        @pl.when(pl.program_id(0) == 0)
        def _():
            st_ref[...] = jnp.zeros_like(st_ref)

        save_ref[...] = st_ref[...]
        o, new = fn(tuple(st_ref[h] for h in range(heads)), c_ref[...], z_ref[...], ba_ref[...], al_ref[...],
                    dt_ref[...], nw_ref[...])
        o_ref[...] = o.astype(BF16)
        for h in range(heads):
            st_ref[h] = new[h]

    row = lambda wd: pl.BlockSpec((1, wd), lambda i: (0, 0))
    return _pcall(body, name=name, comm=comm,
                  out_shape=(jax.ShapeDtypeStruct((s, bw), BF16), jax.ShapeDtypeStruct((n, heads, hd, hd), F32)),
                  grid=(n,),
                  in_specs=[pl.BlockSpec((cs, 3 * bw), lambda i: (i, 0)), pl.BlockSpec((cs, bw), lambda i: (i, 3)),
                            pl.BlockSpec((cs, BA_PAD), lambda i: (i, ba_blk)), row(BA_PAD), row(BA_PAD), row(hd)],
                  out_specs=(pl.BlockSpec((cs, bw), lambda i: (i, 0)),
                             pl.BlockSpec((None, heads, hd, hd), lambda i: (i, 0, 0, 0))),
                  scratch=[pltpu.VMEM((heads, hd, hd), F32)], sem=("arbitrary",))(
                      conv, p, p, alog_row, dt_row, norm_w.reshape(1, hd))


def _dn_bwd(p, conv, states, d_o, a_log, dt_bias, norm_w, *, bw, ba_blk, name, comm=None):
    s = p.shape[0]
    heads = bw // DN_HEAD_DIM
    cs = min(DN_CHUNK, s)
    n = s // cs
    hd = DN_HEAD_DIM
    alog_row, dt_row = _dn_rows(a_log, dt_bias, heads)
    fn = functools.partial(_dn_chunk, heads=heads, bw=bw)

    def body(c_ref, z_ref, ba_ref, st_ref, do_ref, al_ref, dt_ref, nw_ref,
             dc_ref, dz_ref, dba_ref, dal_ref, ddt_ref, dnw_ref, dst_ref):
        @pl.when(pl.program_id(0) == 0)
        def _():
            dst_ref[...] = jnp.zeros_like(dst_ref)
            dal_ref[...] = jnp.zeros_like(dal_ref)
            ddt_ref[...] = jnp.zeros_like(ddt_ref)
            dnw_ref[...] = jnp.zeros_like(dnw_ref)

        _, vjp = jax.vjp(fn, tuple(st_ref[h] for h in range(heads)), c_ref[...], z_ref[...], ba_ref[...],
                         al_ref[...], dt_ref[...], nw_ref[...])
        dst, dc, dz, dba, dal, ddt, dnw = vjp((do_ref[...].astype(F32), tuple(dst_ref[h] for h in range(heads))))
        for h in range(heads):
            dst_ref[h] = dst[h]
        dc_ref[...] = dc
        dz_ref[...] = dz
        dba_ref[...] = dba
        dal_ref[...] += dal
        ddt_ref[...] += ddt
        dnw_ref[...] += dnw

    rev = lambda i: n - 1 - i
    row = lambda wd: pl.BlockSpec((1, wd), lambda i: (0, 0))
    return _pcall(body, name=name, comm=comm,
                  out_shape=(jax.ShapeDtypeStruct((s, 3 * bw), F32), jax.ShapeDtypeStruct((s, bw), F32),
                             jax.ShapeDtypeStruct((s, BA_PAD), F32), jax.ShapeDtypeStruct((1, BA_PAD), F32),
                             jax.ShapeDtypeStruct((1, BA_PAD), F32), jax.ShapeDtypeStruct((1, hd), F32)),
                  grid=(n,),
                  in_specs=[pl.BlockSpec((cs, 3 * bw), lambda i: (rev(i), 0)),
                            pl.BlockSpec((cs, bw), lambda i: (rev(i), 3)),
                            pl.BlockSpec((cs, BA_PAD), lambda i: (rev(i), ba_blk)),
                            pl.BlockSpec((None, heads, hd, hd), lambda i: (rev(i), 0, 0, 0)),
                            pl.BlockSpec((cs, bw), lambda i: (rev(i), 0)), row(BA_PAD), row(BA_PAD), row(hd)],
                  out_specs=(pl.BlockSpec((cs, 3 * bw), lambda i: (rev(i), 0)),
                             pl.BlockSpec((cs, bw), lambda i: (rev(i), 0)),
                             pl.BlockSpec((cs, BA_PAD), lambda i: (rev(i), 0)), row(BA_PAD), row(BA_PAD), row(hd)),
                  scratch=[pltpu.VMEM((heads, hd, hd), F32)], sem=("arbitrary",))(
                      conv, p, p, states, d_o, alog_row, dt_row, norm_w.reshape(1, hd))


def _lru_gates(xc, wr, br, wi, bi, lam):
    r = _sigmoid(_bdot(xc, wr, 1, 0) + br)
    i = _sigmoid(_bdot(xc, wi, 1, 0) + bi)
    log_a = -LRU_C * r * _softplus(-lam)
    return jnp.exp(log_a), jnp.sqrt(-_expm1(2.0 * log_a)) * (i * xc)


def _scan_rows(t, step, carry):
    def trip(g, cr):
        base = pl.multiple_of(g * SUBLANES, SUBLANES)
        for r in range(SUBLANES):
            cr = step(base + r, cr)
        return cr
    return lax.fori_loop(0, t // SUBLANES, trip, carry)


def _scan_rows_rev(t, step, carry):
    def trip(g, cr):
        base = pl.multiple_of((t // SUBLANES - 1 - g) * SUBLANES, SUBLANES)
        for r in range(SUBLANES - 1, -1, -1):
            cr = step(base + r, cr)
        return cr
    return lax.fori_loop(0, t // SUBLANES, trip, carry)


def _lru_fwd(p, xc, wr, br, wi, bi, lam, *, bw, z_blk, name):
    s = p.shape[0]
    t = _tile(s, 256, SUBLANES)
    nt = s // t

    def body(xc_ref, z_ref, wr_ref, br_ref, wi_ref, bi_ref, lam_ref, o_ref, save_ref, a_s, b_s, h_s, carry_s):
        @pl.when(pl.program_id(0) == 0)
        def _():
            carry_s[...] = jnp.zeros_like(carry_s)

        a, inp = _lru_gates(xc_ref[...], wr_ref[...], br_ref[...], wi_ref[...], bi_ref[...], lam_ref[...])
        a_s[...] = a
        b_s[...] = inp
        h0 = carry_s[...]
        save_ref[...] = h0

        def step(r, h):
            h = a_s[pl.ds(r, 1), :] * h + b_s[pl.ds(r, 1), :]
            h_s[pl.ds(r, 1), :] = h
            return h

        carry_s[...] = _scan_rows(t, step, h0)
        o_ref[...] = (h_s[...] * _silu(z_ref[...])).astype(BF16)

    tok = pl.BlockSpec((t, bw), lambda i: (i, 0))
    row = pl.BlockSpec((1, bw), lambda i: (0, 0))
    mat = pl.BlockSpec((bw, bw), lambda i: (0, 0))
    return _pcall(body, name=name,
                  out_shape=(jax.ShapeDtypeStruct((s, bw), BF16), jax.ShapeDtypeStruct((nt, 1, bw), F32)), grid=(nt,),
                  in_specs=[tok, pl.BlockSpec((t, bw), lambda i: (i, z_blk)), mat, row, mat, row, row],
                  out_specs=(tok, pl.BlockSpec((None, 1, bw), lambda i: (i, 0, 0))),
                  scratch=[pltpu.VMEM((t, bw), F32)] * 3 + [pltpu.VMEM((1, bw), F32)], sem=("arbitrary",))(
                      xc, p, wr, br, wi, bi, lam)


def _lru_bwd(p, xc, saves, d_o, wr, br, wi, bi, lam, *, bw, z_blk, name):
    s = p.shape[0]
    t = _tile(s, 256, SUBLANES)
    nt = s // t

    def body(xc_ref, z_ref, sv_ref, do_ref, wr_ref, br_ref, wi_ref, bi_ref, lam_ref,
             dxc_ref, dz_ref, dwr_ref, dwi_ref, dbr_ref, dbi_ref, dlam_ref, a_s, b_s, h_s, g_s, carry_s):
        @pl.when(pl.program_id(0) == 0)
        def _():
            carry_s[...] = jnp.zeros_like(carry_s)
            for r in (dwr_ref, dwi_ref, dbr_ref, dbi_ref, dlam_ref):
                r[...] = jnp.zeros_like(r)

        (a, inp), vjp_g = jax.vjp(_lru_gates, xc_ref[...], wr_ref[...], br_ref[...], wi_ref[...], bi_ref[...],
                                  lam_ref[...])
        a_s[...] = a
        b_s[...] = inp
        h0 = sv_ref[...]

        def fstep(r, h):
            h_s[pl.ds(r, 1), :] = h
            return a_s[pl.ds(r, 1), :] * h + b_s[pl.ds(r, 1), :]

        _scan_rows(t, fstep, h0)
        a = a_s[...]
        hs = a * h_s[...] + b_s[...]
        z = z_ref[...]
        d_o = do_ref[...].astype(F32)
        _, vjp_o = jax.vjp(lambda hv, zv: hv * _silu(zv), hs, z)
        dhs, dz = vjp_o(d_o)
        dz_ref[...] = dz
        g_s[...] = dhs

        def bstep(r, cr):
            g = g_s[pl.ds(r, 1), :] + cr
            g_s[pl.ds(r, 1), :] = g
            return a_s[pl.ds(r, 1), :] * g

        carry_s[...] = _scan_rows_rev(t, bstep, carry_s[...])
        g = g_s[...]
        dxc, dwr, dbr, dwi, dbi, dlam = vjp_g((g * h_s[...], g))
        dxc_ref[...] = dxc
        dwr_ref[...] += dwr
        dwi_ref[...] += dwi
        dbr_ref[...] += dbr
        dbi_ref[...] += dbi
        dlam_ref[...] += dlam

    rev = lambda i: nt - 1 - i
    tok = pl.BlockSpec((t, bw), lambda i: (rev(i), 0))
    row = pl.BlockSpec((1, bw), lambda i: (0, 0))
    mat = pl.BlockSpec((bw, bw), lambda i: (0, 0))
    sd = jax.ShapeDtypeStruct
    return _pcall(body, name=name,
                  out_shape=(sd((s, bw), F32), sd((s, bw), F32), sd((bw, bw), F32), sd((bw, bw), F32),
                             sd((1, bw), F32), sd((1, bw), F32), sd((1, bw), F32)),
                  grid=(nt,),
                  in_specs=[tok, pl.BlockSpec((t, bw), lambda i: (rev(i), z_blk)),
                            pl.BlockSpec((None, 1, bw), lambda i: (rev(i), 0, 0)), tok, mat, row, mat, row, row],
                  out_specs=(tok, tok, mat, mat, row, row, row),
                  scratch=[pltpu.VMEM((t, bw), F32)] * 4 + [pltpu.VMEM((1, bw), F32)], sem=("arbitrary",))(
                      xc, p, saves, d_o, wr, br, wi, bi, lam)


def _s5_prep(log_dt, a_re, a_im, b_re, b_im, c_re, c_im, d_skip):
    g, n = a_re.shape
    gs = d_skip.shape[1]
    gpb = LANES // gs
    nb = g // gpb
    dt = jnp.exp(log_dt)[:, None]
    mag = jnp.exp(dt * a_re)
    ab_re = mag * jnp.cos(dt * a_im)
    ab_im = mag * jnp.sin(dt * a_im)
    den = a_re * a_re + a_im * a_im
    f_re = ((ab_re - 1.0) * a_re + ab_im * a_im) / den
    f_im = (ab_im * a_re - (ab_re - 1.0) * a_im) / den
    bb_re = f_re[..., None] * b_re - f_im[..., None] * b_im
    bb_im = f_re[..., None] * b_im + f_im[..., None] * b_re
    eye = jnp.eye(gpb, dtype=F32)

    def b_dense(bb):
        t = bb.reshape(nb, gpb, n, gs)
        return jnp.einsum("bgnc,gh->bgchn", t, eye).reshape(nb, gpb * gs, gpb * n)

    def c_dense(cc):
        t = cc.reshape(nb, gpb, gs, n)
        return jnp.einsum("bgcn,gh->bgnhc", t, eye).reshape(nb, gpb * n, gpb * gs)

    lanes = gpb * n
    sub = lanes // LANES
    return (ab_re.reshape(nb, sub, LANES), ab_im.reshape(nb, sub, LANES), b_dense(bb_re), b_dense(bb_im),
            c_dense(c_re), c_dense(c_im), d_skip.reshape(1, g * gs))


def _s5_out(xre, xim, cre, cim, d, u):
    return _gelu(_bdot(xre, cre, 1, 0) - _bdot(xim, cim, 1, 0) + d * u)


def _s5_fwd(p, prep, *, bw, u_blk0, name):
    s = p.shape[0]
    are, aim, bre, bim, cre, cim, d = prep
    nb, sub, _ = are.shape
    lanes = sub * LANES
    t = _tile(s, 256, SUBLANES)
    nt = s // t

    def body(u_ref, are_ref, aim_ref, bre_ref, bim_ref, cre_ref, cim_ref, d_ref, y_ref, save_ref,
             bre_s, bim_s, xre_s, xim_s, carry_s):
        @pl.when(pl.program_id(1) == 0)
        def _():
            carry_s[...] = jnp.zeros_like(carry_s)

        u = u_ref[...]
        bre_s[...] = _bdot(u, bre_ref[...], 1, 0).reshape(t, sub, LANES)
        bim_s[...] = _bdot(u, bim_ref[...], 1, 0).reshape(t, sub, LANES)
        ar, ai = are_ref[...], aim_ref[...]
        save_ref[...] = carry_s[...]

        def step(r, cr):
            xr, xi = cr
            nr = ar * xr - ai * xi + bre_s[r]
            ni = ar * xi + ai * xr + bim_s[r]
            xre_s[r] = nr
            xim_s[r] = ni
            return nr, ni

        xr, xi = lax.fori_loop(0, t, step, (carry_s[0], carry_s[1]), unroll=SUBLANES)
        carry_s[0] = xr
        carry_s[1] = xi
        y_ref[...] = _s5_out(xre_s[...].reshape(t, lanes), xim_s[...].reshape(t, lanes), cre_ref[...], cim_ref[...],
                             d_ref[...], u).astype(BF16)

    vec = pl.BlockSpec((None, sub, LANES), lambda j, i: (j, 0, 0))
    bmat = pl.BlockSpec((None, LANES, lanes), lambda j, i: (j, 0, 0))
    cmat = pl.BlockSpec((None, lanes, LANES), lambda j, i: (j, 0, 0))
    return _pcall(body, name=name,
                  out_shape=(jax.ShapeDtypeStruct((s, bw), BF16), jax.ShapeDtypeStruct((nb, nt, 2, sub, LANES), F32)),
                  grid=(nb, nt),
                  in_specs=[pl.BlockSpec((t, LANES), lambda j, i: (i, u_blk0 + j)), vec, vec, bmat, bmat, cmat, cmat,
                            pl.BlockSpec((1, LANES), lambda j, i: (0, j))],
                  out_specs=(pl.BlockSpec((t, LANES), lambda j, i: (i, j)),
                             pl.BlockSpec((None, None, 2, sub, LANES), lambda j, i: (j, i, 0, 0, 0))),
                  scratch=[pltpu.VMEM((t, sub, LANES), F32)] * 4 + [pltpu.VMEM((2, sub, LANES), F32)],
                  sem=("parallel", "arbitrary"))(p, are, aim, bre, bim, cre, cim, d)


def _s5_bwd(p, prep, saves, dyg, *, bw, u_blk0, name):
    s = p.shape[0]
    are, aim, bre, bim, cre, cim, d = prep
    nb, sub, _ = are.shape
    lanes = sub * LANES
    t = _tile(s, 256, SUBLANES)
    nt = s // t

    def body(u_ref, dy_ref, sv_ref, are_ref, aim_ref, bre_ref, bim_ref, cre_ref, cim_ref, d_ref,
             du_ref, dar_ref, dai_ref, dbre_ref, dbim_ref, dcre_ref, dcim_ref, dd_ref,
             bre_s, bim_s, xre_s, xim_s, carry_s):
        @pl.when(pl.program_id(1) == 0)
        def _():
            carry_s[...] = jnp.zeros_like(carry_s)
            for r in (dar_ref, dai_ref, dbre_ref, dbim_ref, dcre_ref, dcim_ref, dd_ref):
                r[...] = jnp.zeros_like(r)

        u = u_ref[...]
        bre_s[...] = _bdot(u, bre_ref[...], 1, 0).reshape(t, sub, LANES)
        bim_s[...] = _bdot(u, bim_ref[...], 1, 0).reshape(t, sub, LANES)
        ar, ai = are_ref[...], aim_ref[...]

        def fstep(r, cr):
            xr, xi = cr
            nr = ar * xr - ai * xi + bre_s[r]
            ni = ar * xi + ai * xr + bim_s[r]
            xre_s[r] = nr
            xim_s[r] = ni
            return nr, ni

        lax.fori_loop(0, t, fstep, (sv_ref[0], sv_ref[1]), unroll=SUBLANES)
        _, vjp_o = jax.vjp(_s5_out, xre_s[...].reshape(t, lanes), xim_s[...].reshape(t, lanes), cre_ref[...],
                           cim_ref[...], d_ref[...], u)
        dxre, dxim, dcre, dcim, dd, du = vjp_o(dy_ref[...].astype(F32))
        dcre_ref[...] += dcre.astype(F32)
        dcim_ref[...] += dcim.astype(F32)
        dd_ref[...] += dd
        bre_s[...] = dxre.reshape(t, sub, LANES)
        bim_s[...] = dxim.reshape(t, sub, LANES)

        def bstep(k, cr):
            r = t - 1 - k
            gr, gi, dar, dai = cr
            gr = bre_s[r] + gr
            gi = bim_s[r] + gi
            bre_s[r] = gr
            bim_s[r] = gi
            pr = jnp.where(r == 0, sv_ref[0], xre_s[jnp.maximum(r - 1, 0)])
            pi = jnp.where(r == 0, sv_ref[1], xim_s[jnp.maximum(r - 1, 0)])
            dar = dar + gr * pr + gi * pi
            dai = dai + gi * pr - gr * pi
            return ar * gr + ai * gi, ar * gi - ai * gr, dar, dai

        zero = jnp.zeros((sub, LANES), F32)
        gr, gi, dar, dai = lax.fori_loop(0, t, bstep, (carry_s[0], carry_s[1], zero, zero), unroll=SUBLANES)
        carry_s[0] = gr
        carry_s[1] = gi
        dar_ref[...] += dar
        dai_ref[...] += dai
        dbu_re = bre_s[...].reshape(t, lanes)
        dbu_im = bim_s[...].reshape(t, lanes)
        du_ref[...] = du + _bdot(dbu_re, bre_ref[...], 1, 1) + _bdot(dbu_im, bim_ref[...], 1, 1)
        dbre_ref[...] += _bdot(u, dbu_re, 0, 0)
        dbim_ref[...] += _bdot(u, dbu_im, 0, 0)

    rev = lambda i: nt - 1 - i
    vec = pl.BlockSpec((None, sub, LANES), lambda j, i: (j, 0, 0))
    bmat = pl.BlockSpec((None, LANES, lanes), lambda j, i: (j, 0, 0))
    cmat = pl.BlockSpec((None, lanes, LANES), lambda j, i: (j, 0, 0))
    drow = pl.BlockSpec((1, LANES), lambda j, i: (0, j))
    sd = jax.ShapeDtypeStruct
    return _pcall(body, name=name,
                  out_shape=(sd((s, bw), F32), sd(are.shape, F32), sd(aim.shape, F32), sd(bre.shape, F32),
                             sd(bim.shape, F32), sd(cre.shape, F32), sd(cim.shape, F32), sd((1, bw), F32)),
                  grid=(nb, nt),
                  in_specs=[pl.BlockSpec((t, LANES), lambda j, i: (rev(i), u_blk0 + j)),
                            pl.BlockSpec((t, LANES), lambda j, i: (rev(i), j)),
                            pl.BlockSpec((None, None, 2, sub, LANES), lambda j, i: (j, rev(i), 0, 0, 0)),
                            vec, vec, bmat, bmat, cmat, cmat, drow],
                  out_specs=(pl.BlockSpec((t, LANES), lambda j, i: (rev(i), j)), vec, vec, bmat, bmat, cmat, cmat, drow),
                  scratch=[pltpu.VMEM((t, sub, LANES), F32)] * 4 + [pltpu.VMEM((2, sub, LANES), F32)],
                  sem=("parallel", "arbitrary"))(p, dyg, saves, are, aim, bre, bim, cre, cim, d)


def _glu_gate(pre, z, bw):
    return pre[:, :bw] * _sigmoid(pre[:, bw:]) * _silu(z)


def _glu_fwd(pre, p, *, bw, z_blk, name):
    s = pre.shape[0]
    t = _tile(s, 256, SUBLANES)

    def body(pre_ref, z_ref, o_ref):
        o_ref[...] = _glu_gate(pre_ref[...], z_ref[...], bw).astype(BF16)

    return _pcall(body, name=name, out_shape=jax.ShapeDtypeStruct((s, bw), BF16), grid=(s // t,),
                  in_specs=[pl.BlockSpec((t, 2 * bw), lambda i: (i, 0)), pl.BlockSpec((t, bw), lambda i: (i, z_blk))],
                  out_specs=pl.BlockSpec((t, bw), lambda i: (i, 0)), sem=("parallel",))(pre, p)


def _glu_bwd(pre, p, d_o, *, bw, z_blk, name):
    s = pre.shape[0]
    t = _tile(s, 256, SUBLANES)

    def body(pre_ref, z_ref, do_ref, dpre_ref, dz_ref, db_ref):
        _, vjp = jax.vjp(functools.partial(_glu_gate, bw=bw), pre_ref[...], z_ref[...])
        dpre, dz = vjp(do_ref[...].astype(F32))
        dpre_ref[...] = dpre.astype(BF16)
        dz_ref[...] = dz

        @pl.when(pl.program_id(0) == 0)
        def _():
            db_ref[...] = jnp.zeros_like(db_ref)

        db_ref[...] += jnp.sum(dpre, axis=0, keepdims=True)

    sd = jax.ShapeDtypeStruct
    return _pcall(body, name=name, out_shape=(sd((s, 2 * bw), BF16), sd((s, bw), F32), sd((1, 2 * bw), F32)),
                  grid=(s // t,),
                  in_specs=[pl.BlockSpec((t, 2 * bw), lambda i: (i, 0)), pl.BlockSpec((t, bw), lambda i: (i, z_blk)),
                            pl.BlockSpec((t, bw), lambda i: (i, 0))],
                  out_specs=(pl.BlockSpec((t, 2 * bw), lambda i: (i, 0)), pl.BlockSpec((t, bw), lambda i: (i, 0)),
                             pl.BlockSpec((1, 2 * bw), lambda i: (0, 0))), sem=("arbitrary",))(pre, p, d_o)


def _attn_tile(q, z, kv, *, bw):
    hd = bw // MEM_HEADS
    outs = []
    for h in range(MEM_HEADS):
        k = kv[:, h * hd:(h + 1) * hd]
        v = kv[:, bw + h * hd:bw + (h + 1) * hd]
        sc = _bdot(q[:, h * hd:(h + 1) * hd], k, 1, 1) * (hd ** -0.5)
        e = jnp.exp(sc - lax.stop_gradient(jnp.max(sc, axis=-1, keepdims=True)))
        prob = e / jnp.sum(e, axis=-1, keepdims=True)
        outs.append(_bdot(prob, v, 1, 0))
    return jnp.concatenate(outs, axis=-1) * _silu(z)


def _attn_fwd(p, kv, *, bw, q_blk, z_blk, name):
    s = p.shape[0]
    m = kv.shape[0]
    t = _tile(s, 256, SUBLANES)

    def body(q_ref, z_ref, kv_ref, o_ref):
        o_ref[...] = _attn_tile(q_ref[...], z_ref[...], kv_ref[...], bw=bw).astype(BF16)

    return _pcall(body, name=name, out_shape=jax.ShapeDtypeStruct((s, bw), BF16), grid=(s // t,),
                  in_specs=[pl.BlockSpec((t, bw), lambda i: (i, q_blk)), pl.BlockSpec((t, bw), lambda i: (i, z_blk)),
                            pl.BlockSpec((m, 2 * bw), lambda i: (0, 0))],
                  out_specs=pl.BlockSpec((t, bw), lambda i: (i, 0)), sem=("parallel",))(p, p, kv)


def _attn_bwd(p, kv, d_o, *, bw, q_blk, z_blk, name):
    s = p.shape[0]
    m = kv.shape[0]
    t = _tile(s, 256, SUBLANES)

    def body(q_ref, z_ref, kv_ref, do_ref, dq_ref, dz_ref, dkv_ref):
        _, vjp = jax.vjp(functools.partial(_attn_tile, bw=bw), q_ref[...], z_ref[...], kv_ref[...])
        dq, dz, dkv = vjp(do_ref[...].astype(F32))
        dq_ref[...] = dq
        dz_ref[...] = dz

        @pl.when(pl.program_id(0) == 0)
        def _():
            dkv_ref[...] = jnp.zeros_like(dkv_ref)

        dkv_ref[...] += dkv

    sd = jax.ShapeDtypeStruct
    tok = pl.BlockSpec((t, bw), lambda i: (i, 0))
    return _pcall(body, name=name, out_shape=(sd((s, bw), F32), sd((s, bw), F32), sd((m, 2 * bw), F32)),
                  grid=(s // t,),
                  in_specs=[pl.BlockSpec((t, bw), lambda i: (i, q_blk)), pl.BlockSpec((t, bw), lambda i: (i, z_blk)),
                            pl.BlockSpec((m, 2 * bw), lambda i: (0, 0)), tok],
                  out_specs=(tok, tok, pl.BlockSpec((m, 2 * bw), lambda i: (0, 0))), sem=("arbitrary",))(p, p, kv, d_o)


def _merge_fwd(p, o4, wg, bg, wb, *, rank, g_blk, name):
    s = p.shape[0]
    _, bw, d = wb.shape
    tm, tn = _tile(s, 512), _tile(d, 512)

    def body(g_ref, o_ref, wg_ref, bg_ref, wb_ref, out_ref):
        g = g_ref[...]
        acc = jnp.zeros((tm, tn), F32)
        for n in range(N_BRANCH):
            gate = _sigmoid(_bdot(g, wg_ref[n], 1, 0) + bg_ref[n])
            acc = acc + gate * _bdot(o_ref[n], wb_ref[n], 1, 0)
        out_ref[...] = acc.astype(BF16)

    return _pcall(body, name=name, out_shape=jax.ShapeDtypeStruct((s, d), BF16), grid=(s // tm, d // tn),
                  in_specs=[pl.BlockSpec((tm, rank), lambda i, j: (i, g_blk)),
                            pl.BlockSpec((N_BRANCH, tm, bw), lambda i, j: (0, i, 0)),
                            pl.BlockSpec((N_BRANCH, rank, tn), lambda i, j: (0, 0, j)),
                            pl.BlockSpec((N_BRANCH, 1, tn), lambda i, j: (0, 0, j)),
                            pl.BlockSpec((N_BRANCH, bw, tn), lambda i, j: (0, 0, j))],
                  out_specs=pl.BlockSpec((tm, tn), lambda i, j: (i, j)), sem=("parallel", "parallel"))(
                      p, o4, wg, bg, wb)


def _merge_bwd(p, o4, wg, bg, wb, dmerged, *, rank, g_blk, name):
    s = p.shape[0]
    _, bw, d = wb.shape
    tm, tn = _tile(s, 512), _tile(d, 512)

    def body(g_ref, o_ref, wg_ref, bg_ref, wb_ref, dm_ref, dpre_ref, dbr_ref, dbg_ref):
        g = g_ref[...]
        dm = dm_ref[...].astype(F32)

        @pl.when(pl.program_id(1) == 0)
        def _():
            dbg_ref[...] = jnp.zeros_like(dbg_ref)

        for n in range(N_BRANCH):
            gate = _sigmoid(_bdot(g, wg_ref[n], 1, 0) + bg_ref[n])
            br = _bdot(o_ref[n], wb_ref[n], 1, 0)
            dpre = dm * br * gate * (1.0 - gate)
            dpre_ref[n] = dpre.astype(BF16)
            dbr_ref[n] = (dm * gate).astype(BF16)
            dbg_ref[n] += jnp.sum(dpre, axis=0, keepdims=True)

    sd = jax.ShapeDtypeStruct
    big = pl.BlockSpec((N_BRANCH, tm, tn), lambda j, i: (0, i, j))
    return _pcall(body, name=name,
                  out_shape=(sd((N_BRANCH, s, d), BF16), sd((N_BRANCH, s, d), BF16), sd((N_BRANCH, 1, d), F32)),
                  grid=(d // tn, s // tm),
                  in_specs=[pl.BlockSpec((tm, rank), lambda j, i: (i, g_blk)),
                            pl.BlockSpec((N_BRANCH, tm, bw), lambda j, i: (0, i, 0)),
                            pl.BlockSpec((N_BRANCH, rank, tn), lambda j, i: (0, 0, j)),
                            pl.BlockSpec((N_BRANCH, 1, tn), lambda j, i: (0, 0, j)),
                            pl.BlockSpec((N_BRANCH, bw, tn), lambda j, i: (0, 0, j)),
                            pl.BlockSpec((tm, tn), lambda j, i: (i, j))],
                  out_specs=(big, big, pl.BlockSpec((N_BRANCH, 1, tn), lambda j, i: (0, 0, j))),
                  sem=("parallel", "arbitrary"))(p, o4, wg, bg, wb, dmerged)


def _adamw(w, g, m, v, *, name):
    r, cdim = w.shape
    tr = _tile(r, 128, SUBLANES)
    bc1 = 1.0 - ADAM_B1 ** ADAM_STEP
    bc2 = 1.0 - ADAM_B2 ** ADAM_STEP

    def body(w_ref, g_ref, m_ref, v_ref, d_ref, nm_ref, nv_ref):
        gv = g_ref[...]
        nm = ADAM_B1 * m_ref[...] + (1.0 - ADAM_B1) * gv
        nv = ADAM_B2 * v_ref[...] + (1.0 - ADAM_B2) * (gv * gv)
        d_ref[...] = -ADAM_LR * ((nm / bc1) / (jnp.sqrt(nv / bc2) + ADAM_EPS) + ADAM_WD * w_ref[...])
        nm_ref[...] = nm
        nv_ref[...] = nv

    blk = pl.BlockSpec((tr, cdim), lambda i: (i, 0))
    sd = jax.ShapeDtypeStruct((r, cdim), F32)
    return _pcall(body, name=name, out_shape=(sd, sd, sd), grid=(r // tr,), in_specs=[blk] * 4, out_specs=(blk,) * 3,
                  sem=("parallel",))(w, g, m, v)


HBM_SPEC = pl.BlockSpec(memory_space=pl.ANY)


def _place():
    x, y, c = lax.axis_index("x"), lax.axis_index("y"), lax.axis_index("c")
    return x, y, c, [(1 - x, y), (x, 1 - y), (1 - x, 1 - y)]


def _rcopy(src, dst, send_sem, recv_sem, device):
    return pltpu.make_async_remote_copy(src_ref=src, dst_ref=dst, send_sem=send_sem, recv_sem=recv_sem,
                                        device_id=device, device_id_type=MESH_ID)


def _comm_call(body, *, name, out_shape, n_in, n_sems, n_local=0):
    scratch = [pltpu.SemaphoreType.DMA((n_sems,)), pltpu.SemaphoreType.DMA((n_sems,))]
    if n_local:
        scratch.append(pltpu.SemaphoreType.DMA((n_local,)))
    multi = isinstance(out_shape, (tuple, list))
    return pl.pallas_call(body, name=name, out_shape=out_shape, in_specs=[HBM_SPEC] * n_in,
                          out_specs=tuple(HBM_SPEC for _ in out_shape) if multi else HBM_SPEC,
                          scratch_shapes=scratch, interpret=False)


def _ag4(items):
    n = len(items)

    def copies(ins, outs, send_sems, recv_sems, _, second_stage):
        x, y, c, chips = _place()
        sibling = (x, y, 1 - c)

        def part(i, px, py, h):
            half = ins[i].shape[0] // 2
            return outs[i].at[2 * px + py, pl.ds(h * half, half)]

        local = []
        first, landed, passed, arrived = [], [], [], []
        for i in range(n):
            half = ins[i].shape[0] // 2
            for j, chip in enumerate(chips):
                sems = (send_sems.at[6 * i + j], recv_sems.at[6 * i + j])
                sems2 = (send_sems.at[6 * i + 3 + j], recv_sems.at[6 * i + 3 + j])
                first.append(_rcopy(ins[i].at[pl.ds(c * half, half)], part(i, x, y, c), *sems, (*chip, c)))
                if second_stage:
                    landed.append(_rcopy(part(i, *chip, c), part(i, *chip, c), *sems, (*chip, c)))
                    passed.append(_rcopy(part(i, *chip, c), part(i, *chip, c), *sems2, sibling))
                    arrived.append(_rcopy(part(i, *chip, 1 - c), part(i, *chip, 1 - c), *sems2, sibling))
        return local, first, landed, passed, arrived

    def start(*refs):
        local, first, _, _, _ = copies(*refs, False)
        for cp in local + first:
            cp.start()

    def finish(*refs):
        local, first, landed, passed, arrived = copies(*refs, True)
        for k in range(3 * n):
            landed[k].wait_recv()
            passed[k].start()
        for cp in arrived:
            cp.wait_recv()
        for cp in first + passed:
            cp.wait_send()
        for cp in local:
            cp.wait()

    return _Comm(items, [jax.ShapeDtypeStruct((4,) + t.shape, t.dtype) for t in items], 6 * n, 0, start, finish)


def _own_slot(g, local, chip):
    return [jnp.where(chip == k, local, g[k]) for k in range(4)]


def _sib_halves(items, *, name):
    n = len(items)

    def body(*refs):
        ins, outs = refs[:n], refs[n:2 * n]
        send_sems, recv_sems = refs[2 * n:]
        x, y, c, _ = _place()
        cps = []
        for i in range(n):
            half = ins[i].shape[1] // 2
            cps.append(_rcopy(ins[i].at[:, pl.ds((1 - c) * half, half)], outs[i], send_sems.at[i], recv_sems.at[i],
                              (x, y, 1 - c)))
        for cp in cps:
            cp.start()
        for cp in cps:
            cp.wait()

    shapes = tuple(jax.ShapeDtypeStruct((4, t.shape[1] // 2) + t.shape[2:], t.dtype) for t in items)
    return _comm_call(body, name=name, out_shape=shapes, n_in=n, n_sems=n)(*items)


def _to_chips(items):
    n = len(items)

    def copies(ins, outs, send_sems, recv_sems, _):
        x, y, c, chips = _place()
        return [_rcopy(ins[i].at[2 * chip[0] + chip[1]], outs[i].at[j], send_sems.at[3 * i + j],
                       recv_sems.at[3 * i + j], (*chip, c)) for i in range(n) for j, chip in enumerate(chips)]

    def start(*refs):
        for cp in copies(*refs):
            cp.start()

    def finish(*refs):
        for cp in copies(*refs):
            cp.wait()

    return _Comm(items, [jax.ShapeDtypeStruct((3,) + t.shape[1:], t.dtype) for t in items], 3 * n, 0, start, finish)


def _join_halves(items, c, *, name):
    n = len(items)

    def body(*refs):
        ins, outs = refs[:n], refs[n:2 * n]
        send_sems, recv_sems = refs[2 * n:]
        x, y, cc, _ = _place()
        sibling = (x, y, 1 - cc)
        cps = []
        for i in range(n):
            a = ins[i].shape[0]
            cps.append(_rcopy(ins[i], outs[i].at[pl.ds(cc * a, a)], send_sems.at[i], recv_sems.at[i], sibling))
        for cp in cps:
            cp.start()
        for i in range(n):
            a = ins[i].shape[0]
            cps[i].wait_send()
            _rcopy(ins[i], outs[i].at[pl.ds((1 - cc) * a, a)], send_sems.at[i], recv_sems.at[i], sibling).wait_recv()

    shapes = tuple(jax.ShapeDtypeStruct((2 * t.shape[0],) + t.shape[1:], t.dtype) for t in items)
    got = _comm_call(body, name=name, out_shape=shapes, n_in=n, n_sems=n)(*items)
    out = []
    for q, g in zip(items, got):
        a = q.shape[0]
        out.append(jnp.concatenate([jnp.where(c == h, q, g[h * a:(h + 1) * a]) for h in range(2)], axis=0))
    return out


def _swap_sibling(buf, *, name):
    def body(b_ref, recv_ref, send_sems, recv_sems):
        x, y, c, _ = _place()
        cp = _rcopy(b_ref, recv_ref, send_sems.at[0], recv_sems.at[0], (x, y, 1 - c))
        cp.start()
        cp.wait()

    return _comm_call(body, name=name, out_shape=jax.ShapeDtypeStruct(buf.shape, buf.dtype), n_in=1, n_sems=1)(buf)


def _gather_chips(buf, *, name):
    n, cdim = buf.shape

    def body(b_ref, out_ref, send_sems, recv_sems, local_sems):
        x, y, c, chips = _place()
        mine = pltpu.make_async_copy(b_ref, out_ref.at[2 * x + y], local_sems.at[0])
        mine.start()
        cps = [_rcopy(b_ref, out_ref.at[2 * x + y], send_sems.at[j], recv_sems.at[j], (*chip, c))
               for j, chip in enumerate(chips)]
        for cp in cps:
            cp.start()
        for j, chip in enumerate(chips):
            slot = out_ref.at[2 * chip[0] + chip[1]]
            _rcopy(slot, slot, send_sems.at[j], recv_sems.at[j], (*chip, c)).wait_recv()
        for cp in cps:
            cp.wait_send()
        mine.wait()

    return _comm_call(body, name=name, out_shape=jax.ShapeDtypeStruct((4, n, cdim), buf.dtype), n_in=1, n_sems=3,
                      n_local=1)(buf)


def _rows_per_block(rows, cdim, itemsize, align):
    return _tile(rows, max(align, ELEMENTWISE_BLOCK_BYTES // (cdim * itemsize) // align * align), align)


def _sum_sib(g4, recv, c_idx, *, name):
    _, rows, cdim = g4.shape
    half = rows // 2
    tr = _rows_per_block(half, cdim, 2, 16)
    nh = half // tr

    def body(c_ref, g_ref, r_ref, o_ref):
        o_ref[...] = (g_ref[...].astype(F32) + r_ref[...].astype(F32)).astype(o_ref.dtype)

    return _pcall(body, name=name, out_shape=jax.ShapeDtypeStruct((4, half, cdim), g4.dtype), grid=(4, nh),
                  num_prefetch=1,
                  in_specs=[pl.BlockSpec((None, tr, cdim), lambda k, i, c_ref: (k, c_ref[0] * nh + i, 0)),
                            pl.BlockSpec((None, tr, cdim), lambda k, i, c_ref: (k, i, 0))],
                  out_specs=pl.BlockSpec((None, tr, cdim), lambda k, i, c_ref: (k, i, 0)),
                  sem=("parallel", "parallel"))(c_idx, g4, recv)


def _sum_chips(h4, recv3, chip_idx, *, name):
    _, n, cdim = h4.shape
    tr = _rows_per_block(n, cdim, 4, 16)

    def body(k_ref, h_ref, r_ref, o_ref):
        acc = h_ref[...].astype(F32)
        for j in range(3):
            acc = acc + r_ref[j].astype(F32)
        o_ref[...] = acc

    return _pcall(body, name=name, out_shape=jax.ShapeDtypeStruct((n, cdim), F32), grid=(n // tr,), num_prefetch=1,
                  in_specs=[pl.BlockSpec((None, tr, cdim), lambda i, k_ref: (k_ref[0], i, 0)),
                            pl.BlockSpec((3, tr, cdim), lambda i, k_ref: (0, i, 0))],
                  out_specs=pl.BlockSpec((tr, cdim), lambda i, k_ref: (i, 0)), sem=("parallel",))(chip_idx, h4, recv3)


def _add2(a, b, *, name):
    n, cdim = a.shape
    tr = _tile(n, 256, SUBLANES)

    def body(a_ref, b_ref, o_ref):
        o_ref[...] = a_ref[...] + b_ref[...]

    blk = pl.BlockSpec((tr, cdim), lambda i: (i, 0))
    return _pcall(body, name=name, out_shape=jax.ShapeDtypeStruct((n, cdim), F32), grid=(n // tr,), in_specs=[blk, blk],
                  out_specs=blk, sem=("parallel",))(a, b)


def _sum4(x4, *, name):
    _, n, cdim = x4.shape
    tr = _tile(n, 256, SUBLANES)

    def body(x_ref, o_ref):
        o_ref[...] = ((x_ref[0] + x_ref[1]) + x_ref[2]) + x_ref[3]

    return _pcall(body, name=name, out_shape=jax.ShapeDtypeStruct((n, cdim), F32), grid=(n // tr,),
                  in_specs=[pl.BlockSpec((4, tr, cdim), lambda i: (0, i, 0))],
                  out_specs=pl.BlockSpec((tr, cdim), lambda i: (i, 0)), sem=("parallel",))(x4)


def _rs_pair(items, c_idx, *, name):
    recv = _sib_halves(items, name=name + "_sib")
    return [_sum_sib(items[i], recv[i], c_idx, name=f"{name}_sum1_{i}") for i in range(len(items))]


def _rs_close(h4, recv3, c_idx, chip_idx, *, name):
    q = [_sum_chips(h4[i], recv3[i], chip_idx, name=f"{name}_sum2_{i}") for i in range(len(h4))]
    return _join_halves(q, c_idx[0], name=name + "_join")


def _all_reduce(buf, *, name):
    pair = _add2(buf, _swap_sibling(buf, name=name + "_sib"), name=name + "_add")
    return _sum4(_gather_chips(pair, name=name + "_ici"), name=name + "_sum")


def _pack_rows(flat, lead, align):
    n = flat.shape[-1]
    unit = PACK_COLS * align
    total = -(-n // unit) * unit
    flat = jnp.pad(flat, [(0, 0)] * len(lead) + [(0, total - n)])
    return flat.reshape(*lead, total // PACK_COLS, PACK_COLS)


BIG = (("w_in", 1), ("ssm_w_glu", 1), ("w_kv", 0), ("w_gate", 2), ("w_branch", 2), ("w_out", 0))
SMALL_SHARDED = (("dn_conv_w", 1), ("lru_conv_w", 1), ("b_gate", 1))
SMALL = ("norm_w", "dn_a_log", "dn_dt_bias", "dn_norm_w", "lru_conv_b", "lru_w_r", "lru_b_r", "lru_w_i", "lru_b_i",
         "lru_lambda", "ssm_log_dt", "ssm_a_re", "ssm_a_im", "ssm_b_re", "ssm_b_im", "ssm_c_re", "ssm_c_im", "ssm_d",
         "ssm_b_glu", "mem_norm_w")
WEIGHTS = ("norm_w", "w_in", "dn_conv_w", "dn_a_log", "dn_dt_bias", "dn_norm_w", "lru_conv_w", "lru_conv_b",
           "lru_w_r", "lru_b_r", "lru_w_i", "lru_b_i", "lru_lambda", "ssm_log_dt", "ssm_a_re", "ssm_a_im", "ssm_b_re",
           "ssm_b_im", "ssm_c_re", "ssm_c_im", "ssm_d", "ssm_w_glu", "ssm_b_glu", "mem_norm_w", "w_kv", "w_gate",
           "b_gate", "w_branch", "w_out", "final_norm_w")


REST = BIG[1:]


def _gather_rest(wts, l):
    small = _pack_rows(jnp.concatenate([wts[n][l].reshape(-1) for n, _ in SMALL_SHARDED]), (), 2 * SUBLANES)
    return _ag4([wts[n][l].astype(BF16) for n, _ in REST] + [small])


def _full_rest(g, local, wts, chip):
    out = {n: jnp.concatenate(_own_slot(g[i], local[i], chip), axis=ax) for i, (n, ax) in enumerate(REST)}
    flat, off = jnp.stack(_own_slot(g[-1], local[-1], chip), axis=0).reshape(4, -1), 0
    for n, ax in SMALL_SHARDED:
        shp = wts[n].shape[1:]
        sz = math.prod(shp)
        out[n] = jnp.concatenate(list(flat[:, off:off + sz].reshape(4, *shp)), axis=ax)
        off += sz
    return out


def _chip_rows(t):
    return t.reshape(4, t.shape[0] // 4, t.shape[1])


def _w_in_layout(w, bw, heads, rank):
    d = w.shape[0]
    ba = 4 * bw
    rest = ba + 2 * heads
    return jnp.concatenate([w[:, :ba], w[:, rest:], w[:, ba:rest], jnp.zeros((d, BA_PAD - 2 * heads), w.dtype)], axis=1)


def _w_in_unlayout(dw, bw, heads, rank):
    ba = 4 * bw
    tail = 10 * bw + rank
    return jnp.concatenate([dw[:, :ba], dw[:, tail:tail + 2 * heads], dw[:, ba:tail]], axis=1)


def _lru_dense(w):
    nb, blk, _ = w.shape
    return jnp.einsum("nij,nm->nimj", w, jnp.eye(nb, dtype=w.dtype)).reshape(nb * blk, nb * blk)


def _w8(w):
    return jnp.concatenate([w, jnp.zeros((SUBLANES - CONV_WIDTH, w.shape[1]), w.dtype)], axis=0)


def _layer_fwd(x, mem, w_in_slots, wts, prm, l, chip, dn_comm):
    s, d = x.shape
    bw = d // N_BRANCH
    heads = bw // DN_HEAD_DIM
    rank = wts["w_gate"].shape[2]
    tag = f"l{l}_"
    sv = {"x": x}
    w_in = _w_in_layout(jnp.concatenate(w_in_slots, axis=1), bw, heads, rank)
    sv["w_in"] = w_in
    h = _rms_fwd(x, prm["norm_w"], name=tag + "norm")
    gather = _gather_rest(wts, l)
    p, rest = _mm(h, w_in, comm=gather, name=tag + "in_proj")
    full = _full_rest(rest, gather.inputs, wts, chip)
    sv["h"], sv["p"] = h, p
    conv_a = _conv_fwd(p, 0, 3 * bw, _w8(full["dn_conv_w"]), jnp.zeros((1, 3 * bw), F32), name=tag + "dn_conv")
    ba_blk = (10 * bw + rank) // BA_PAD
    dn = _dn_fwd(p, conv_a, prm["dn_a_log"], prm["dn_dt_bias"], prm["dn_norm_w"], bw=bw, ba_blk=ba_blk,
                 comm=dn_comm, name=tag + "dn")
    (o_a, dn_states), dn_res = dn if dn_comm is not None else (dn, None)
    sv["conv_a"], sv["dn_states"] = conv_a, dn_states
    xc = _conv_fwd(p, 4, bw, _w8(full["lru_conv_w"]), prm["lru_conv_b"].reshape(1, bw), name=tag + "lru_conv")
    (wr, wi), lru_vjp = jax.vjp(lambda a, b: (_lru_dense(a), _lru_dense(b)), prm["lru_w_r"], prm["lru_w_i"])
    row = lambda v: v.reshape(1, bw)
    lru_args = (wr, row(prm["lru_b_r"]), wi, row(prm["lru_b_i"]), row(prm["lru_lambda"]))
    o_b, lru_saves = _lru_fwd(p, xc, *lru_args, bw=bw, z_blk=5, name=tag + "lru")
    sv["xc"], sv["lru_saves"], sv["lru_args"], sv["lru_vjp"] = xc, lru_saves, lru_args, lru_vjp
    prep, s5_vjp = jax.vjp(_s5_prep, prm["ssm_log_dt"], prm["ssm_a_re"], prm["ssm_a_im"], prm["ssm_b_re"],
                           prm["ssm_b_im"], prm["ssm_c_re"], prm["ssm_c_im"], prm["ssm_d"])
    u_blk0 = 6 * bw // LANES
    yg, s5_saves = _s5_fwd(p, prep, bw=bw, u_blk0=u_blk0, name=tag + "s5")
    pre = _mm(yg, full["ssm_w_glu"], bias=prm["ssm_b_glu"].reshape(1, 2 * bw), name=tag + "glu_proj")
    o_c = _glu_fwd(pre, p, bw=bw, z_blk=7, name=tag + "glu")
    sv["prep"], sv["s5_vjp"], sv["s5_saves"], sv["yg"], sv["pre"] = prep, s5_vjp, s5_saves, yg, pre
    mem_n = _rms_fwd(mem, prm["mem_norm_w"], name=tag + "mem_norm")
    kv = _mm(mem_n, full["w_kv"], name=tag + "kv_proj")
    o_d = _attn_fwd(p, kv, bw=bw, q_blk=8, z_blk=9, name=tag + "attn")
    sv["mem_n"], sv["kv"] = mem_n, kv
    o4 = jnp.stack([o_a, o_b, o_c, o_d], axis=0)
    bg = full["b_gate"].reshape(N_BRANCH, 1, d)
    g_blk = 10 * bw // rank
    merged = _merge_fwd(p, o4, full["w_gate"], bg, full["w_branch"], rank=rank, g_blk=g_blk, name=tag + "merge")
    sv["o4"], sv["bg"], sv["merged"] = o4, bg, merged
    return _mm(merged, full["w_out"], add=x, name=tag + "out_proj"), sv, full, dn_res


def _layer_bwd(dx_out, mem, sv, full, prm, l, dn_comm_of):
    x, p, h = sv["x"], sv["p"], sv["h"]
    s, d = x.shape
    bw = d // N_BRANCH
    heads = bw // DN_HEAD_DIM
    rank = full["w_gate"].shape[1]
    tag = f"l{l}b_"
    big, small = {}, {}
    dmerged = _mm(dx_out, full["w_out"], tb=True, out_dtype=BF16, name=tag + "out_dx")
    big["w_out"] = _mm(sv["merged"], dx_out, ta=True, out_dtype=BF16, name=tag + "out_dw")
    g_blk = 10 * bw // rank
    dpre, dbr, dbg = _merge_bwd(p, sv["o4"], full["w_gate"], sv["bg"], full["w_branch"], dmerged, rank=rank,
                                g_blk=g_blk, name=tag + "merge")
    small["b_gate"] = dbg.reshape(N_BRANCH, d)
    glow = p[:, 10 * bw:10 * bw + rank].astype(BF16)
    dglow = None
    dwg, dwb, d_o = [], [], []
    for n in range(N_BRANCH):
        dglow = _mm(dpre, full["w_gate"], la=n, lb=n, tb=True, add=dglow, name=tag + f"gate_dx{n}")
        dwg.append(_mm(glow, dpre, ta=True, lb=n, out_dtype=BF16, nsplit=4, name=tag + f"gate_dw{n}"))
        d_o.append(_mm(dbr, full["w_branch"], la=n, lb=n, tb=True, name=tag + f"branch_dx{n}"))
        dwb.append(_mm(sv["o4"], dbr, ta=True, la=n, lb=n, out_dtype=BF16, nsplit=4, name=tag + f"branch_dw{n}"))
    dn_comm = dn_comm_of([_chip_rows(big["w_out"]), *dwg, *dwb])
    ba_blk = (10 * bw + rank) // BA_PAD
    dn = _dn_bwd(p, sv["conv_a"], sv["dn_states"], d_o[0], prm["dn_a_log"], prm["dn_dt_bias"], prm["dn_norm_w"],
                 bw=bw, ba_blk=ba_blk, comm=dn_comm, name=tag + "dn")
    (dconv, dz_a, dba, dal, ddt, dnw), dn_res = dn if dn_comm is not None else (dn, None)
    small["dn_a_log"] = dal[0, heads:2 * heads]
    small["dn_dt_bias"] = ddt[0, heads:2 * heads]
    small["dn_norm_w"] = dnw[0]
    dqkv, dw8_a, _ = _conv_bwd(p, 0, 3 * bw, _w8(full["dn_conv_w"]), dconv, name=tag + "dn_conv")
    small["dn_conv_w"] = dw8_a[:CONV_WIDTH]
    dxc, dz_b, dwr, dwi, dbr_, dbi_, dlam = _lru_bwd(p, sv["xc"], sv["lru_saves"], d_o[1], *sv["lru_args"], bw=bw,
                                                     z_blk=5, name=tag + "lru")
    small["lru_w_r"], small["lru_w_i"] = sv["lru_vjp"]((dwr, dwi))
    small["lru_b_r"], small["lru_b_i"], small["lru_lambda"] = dbr_[0], dbi_[0], dlam[0]
    dlx, dw8_b, dcb = _conv_bwd(p, 4, bw, _w8(full["lru_conv_w"]), dxc, name=tag + "lru_conv")
    small["lru_conv_w"] = dw8_b[:CONV_WIDTH]
    small["lru_conv_b"] = dcb[0]
    dpre_glu, dz_c, dbglu = _glu_bwd(sv["pre"], p, d_o[2], bw=bw, z_blk=7, name=tag + "glu")
    small["ssm_b_glu"] = dbglu[0]
    dyg = _mm(dpre_glu, full["ssm_w_glu"], tb=True, name=tag + "glu_dx")
    big["ssm_w_glu"] = _mm(sv["yg"], dpre_glu, ta=True, out_dtype=BF16, nsplit=4, name=tag + "glu_dw")
    s5 = _s5_bwd(p, sv["prep"], sv["s5_saves"], dyg, bw=bw, u_blk0=6 * bw // LANES, name=tag + "s5")
    du = s5[0]
    (small["ssm_log_dt"], small["ssm_a_re"], small["ssm_a_im"], small["ssm_b_re"], small["ssm_b_im"],
     small["ssm_c_re"], small["ssm_c_im"], small["ssm_d"]) = sv["s5_vjp"](tuple(s5[1:]))
    dq, dz_d, dkv = _attn_bwd(p, sv["kv"], d_o[3], bw=bw, q_blk=8, z_blk=9, name=tag + "attn")
    big["w_kv"] = _mm(sv["mem_n"], dkv, ta=True, out_dtype=BF16, name=tag + "kv_dw")
    dmem_n = _mm(dkv, full["w_kv"], tb=True, name=tag + "kv_dx")
    _, dmnw = _rms_bwd(mem, prm["mem_norm_w"], dmem_n, None, name=tag + "mem_norm")
    small["mem_norm_w"] = dmnw[0]
    dp = jnp.concatenate([dqkv, dz_a, dlx, dz_b, du, dz_c, dq, dz_d, dglow, dba], axis=1).astype(BF16)
    dh = _mm(dp, sv["w_in"], tb=True, name=tag + "in_dx")
    dw_in = _w_in_unlayout(_mm(h, dp, ta=True, out_dtype=BF16, name=tag + "in_dw"), bw, heads, rank)
    dx, dnw_in = _rms_bwd(x, prm["norm_w"], dh, dx_out, name=tag + "norm")
    small["norm_w"] = dnw_in[0]
    late = [dw_in.reshape(d, 4, dw_in.shape[1] // 4).transpose(1, 0, 2), big["ssm_w_glu"], _chip_rows(big["w_kv"])]
    return dx, late, small, dn_res


def _step(wts, mom, vel, x, mem, target):
    depth = wts["norm_w"].shape[0]
    xi, yi, ci = lax.axis_index("x"), lax.axis_index("y"), lax.axis_index("c")
    c_idx = ci.astype(jnp.int32).reshape(1)
    chip = (2 * xi + yi).astype(jnp.int32)
    chip_idx = chip.reshape(1)
    x, mem, target = x[0], mem[0], target[0]

    prms = [{n: wts[n][l] for n in SMALL} for l in range(depth)]
    w_in_local = [wts["w_in"][l].astype(BF16) for l in range(depth)]
    w_in_g = _run_comm(_ag4([w_in_local[0]]), name="gather_w_in0")[0]
    saves, fulls = [], []
    act = x
    for l in range(depth):
        nxt = _ag4([w_in_local[l + 1]]) if l + 1 < depth else None
        act, sv, full, res = _layer_fwd(act, mem, _own_slot(w_in_g, w_in_local[l], chip), wts, prms[l], l, chip, nxt)
        saves.append(sv)
        fulls.append(full)
        w_in_g = res[0] if nxt is not None else None
    loss_part, dx, dfw = _loss_head(act, wts["final_norm_w"], target, name="loss_head")
    loss = lax.psum(loss_part[0, 0], ("x", "y", "c"))

    big_g = [None] * depth
    small_g = [None] * depth
    pending = []
    done = {}
    for l in reversed(range(depth)):
        def dn_comm_of(early, l=l):
            pending.append((l, "early", _rs_pair(early, c_idx, name=f"scatter_g{l}a")))
            return _to_chips([t for _, _, h4 in pending for t in h4])

        dx, late, small_g[l], res = _layer_bwd(dx, mem, saves[l], fulls[l], prms[l], l, dn_comm_of)
        off = 0
        for ll, grp, h4 in pending:
            done[ll, grp] = _rs_close(h4, res[off:off + len(h4)], c_idx, chip_idx, name=f"scatter_g{ll}{grp[0]}c")
            off += len(h4)
        pending = [(l, "late", _rs_pair(late, c_idx, name=f"scatter_g{l}b"))]
    (l, grp, h4), = pending
    done[l, grp] = _rs_close(h4, _run_comm(_to_chips(h4), name=f"scatter_g{l}b_ici"), c_idx, chip_idx,
                             name=f"scatter_g{l}lc")
    for l in range(depth):
        e, t = done[l, "early"], done[l, "late"]
        big_g[l] = {"w_in": t[0], "ssm_w_glu": t[1], "w_kv": t[2], "w_out": e[0],
                    "w_gate": jnp.stack(e[1:1 + N_BRANCH], axis=0),
                    "w_branch": jnp.stack(e[1 + N_BRANCH:1 + 2 * N_BRANCH], axis=0)}

    names = SMALL + tuple(n for n, _ in SMALL_SHARDED)
    flat = jnp.concatenate([small_g[l][n].reshape(-1) for l in range(depth) for n in names] + [dfw.reshape(-1)])
    red = _all_reduce(_pack_rows(flat, (), 256), name="reduce_small").reshape(-1)
    grads, off = {n: [] for n in names}, 0
    for l in range(depth):
        for n in names:
            shp = small_g[l][n].shape
            sz = math.prod(shp)
            grads[n].append(red[off:off + sz].reshape(shp))
            off += sz
    grads = {n: jnp.stack(v, axis=0) for n, v in grads.items()}
    grads["final_norm_w"] = red[off:off + dfw.size].reshape(wts["final_norm_w"].shape)
    for n, ax in SMALL_SHARDED:
        width = wts[n].shape[-1]
        grads[n] = lax.dynamic_slice_in_dim(grads[n], chip * width, width, axis=ax + 1)
    for n, _ in BIG:
        grads[n] = jnp.stack([big_g[l][n] for l in range(depth)], axis=0)

    delta, new_m, new_v = {}, {}, {}
    for n, _ in BIG:
        shp = wts[n].shape
        two = lambda t: t.reshape(-1, shp[-1])
        dlt, nm, nv = _adamw(two(wts[n]), two(grads[n]), two(mom[n]), two(vel[n]), name="adamw_" + n)
        delta[n], new_m[n], new_v[n] = dlt.reshape(shp), nm.reshape(shp), nv.reshape(shp)
    rest = [n for n in WEIGHTS if n not in dict(BIG)]
    cat = lambda src: _pack_rows(jnp.concatenate([src[n].reshape(-1) for n in rest]), (), SUBLANES)
    dlt, nm, nv = _adamw(cat(wts), cat(grads), cat(mom), cat(vel), name="adamw_small")
    off = 0
    for n in rest:
        shp = wts[n].shape
        sz = math.prod(shp)
        for dst, src in ((delta, dlt), (new_m, nm), (new_v, nv)):
            dst[n] = src.reshape(-1)[off:off + sz].reshape(shp)
        off += sz
    return (loss, dx[None], *[grads[n] for n in WEIGHTS], *[delta[n] for n in WEIGHTS], *[new_m[n] for n in WEIGHTS],
            *[new_v[n] for n in WEIGHTS])


def kernel(x, mem, norm_w, w_in, dn_conv_w, dn_a_log, dn_dt_bias, dn_norm_w, lru_conv_w, lru_conv_b, lru_w_r, lru_b_r, lru_w_i, lru_b_i, lru_lambda, ssm_log_dt, ssm_a_re, ssm_a_im, ssm_b_re, ssm_b_im, ssm_c_re, ssm_c_im, ssm_d, ssm_w_glu, ssm_b_glu, mem_norm_w, w_kv, w_gate, b_gate, w_branch, w_out, final_norm_w, loss_target, m_norm_w, m_w_in, m_dn_conv_w, m_dn_a_log, m_dn_dt_bias, m_dn_norm_w, m_lru_conv_w, m_lru_conv_b, m_lru_w_r, m_lru_b_r, m_lru_w_i, m_lru_b_i, m_lru_lambda, m_ssm_log_dt, m_ssm_a_re, m_ssm_a_im, m_ssm_b_re, m_ssm_b_im, m_ssm_c_re, m_ssm_c_im, m_ssm_d, m_ssm_w_glu, m_ssm_b_glu, m_mem_norm_w, m_w_kv, m_w_gate, m_b_gate, m_w_branch, m_w_out, m_final_norm_w, v_norm_w, v_w_in, v_dn_conv_w, v_dn_a_log, v_dn_dt_bias, v_dn_norm_w, v_lru_conv_w, v_lru_conv_b, v_lru_w_r, v_lru_b_r, v_lru_w_i, v_lru_b_i, v_lru_lambda, v_ssm_log_dt, v_ssm_a_re, v_ssm_a_im, v_ssm_b_re, v_ssm_b_im, v_ssm_c_re, v_ssm_c_im, v_ssm_d, v_ssm_w_glu, v_ssm_b_glu, v_mem_norm_w, v_w_kv, v_w_gate, v_b_gate, v_w_branch, v_w_out, v_final_norm_w):
    given = dict(locals())
    wts = {n: given[n] for n in WEIGHTS}
    mom = {n: given["m_" + n] for n in WEIGHTS}
    vel = {n: given["v_" + n] for n in WEIGHTS}
    return _step(wts, mom, vel, x, mem, loss_target)
```

```python
import functools
import math

import jax
import jax.numpy as jnp
import numpy as np
from jax import lax
from jax.experimental import pallas as pl
from jax.experimental.pallas import tpu as pltpu

F32 = jnp.float32
BF16 = jnp.bfloat16
HIGHEST = lax.Precision.HIGHEST
MESH_ID = pl.DeviceIdType.MESH

NORM_EPS = 1e-6
CONV_WIDTH = 4
DN_HEAD_DIM = 128
DN_CHUNK = 64
LRU_C = 8.0
MEM_HEADS = 4
N_BRANCH = 4
ADAM_LR, ADAM_B1, ADAM_B2, ADAM_EPS, ADAM_WD, ADAM_STEP = 0.001, 0.9, 0.999, 1e-08, 0.01, 10

LANES = 128
SUBLANES = 8
VMEM_LIMIT = 56 * 2 ** 20
PACK_COLS = 1024
ELEMENTWISE_BLOCK_BYTES = 2 * 2 ** 20
BA_PAD = 256


def _tile(n, pref, align=LANES):
    if n <= pref:
        return n
    t = pref - pref % align
    while t > 0:
        if n % t == 0:
            return t
        t -= align
    return n


class _Comm:
    def __init__(self, inputs, out_shapes, n_sems, n_local, start, finish):
        self.inputs, self.out_shapes, self.n_sems, self.n_local = list(inputs), tuple(out_shapes), n_sems, n_local
        self.start, self.finish = start, finish

    def scratch(self):
        s = [pltpu.SemaphoreType.DMA((self.n_sems,)), pltpu.SemaphoreType.DMA((self.n_sems,))]
        return s + ([pltpu.SemaphoreType.DMA((self.n_local,))] if self.n_local else [])

    def split(self, refs):
        ni, no = len(self.inputs), len(self.out_shapes)
        sems = list(refs[ni + no:]) + ([] if self.n_local else [None])
        return (refs[:ni], refs[ni:ni + no], *sems)


def _pcall(body, *, name, out_shape, grid=(), in_specs=None, out_specs=None, scratch=(), sem=None,
           num_prefetch=0, comm=None):
    params = dict(vmem_limit_bytes=VMEM_LIMIT)
    if sem is not None:
        params["dimension_semantics"] = sem if comm is None else ("arbitrary",) * len(grid)
    scratch = list(scratch)
    if comm is None:
        run_body = body
    else:
        assert not num_prefetch
        single = not isinstance(out_shape, (tuple, list))
        outs = (out_shape,) if single else tuple(out_shape)
        ospecs = (out_specs,) if single else tuple(out_specs)
        n_in, n_out, n_scr = len(in_specs), len(outs), len(scratch)
        n_ci, n_co = len(comm.inputs), len(comm.out_shapes)
        in_specs = list(in_specs) + [HBM_SPEC] * n_ci
        out_shape = outs + comm.out_shapes
        out_specs = ospecs + (HBM_SPEC,) * n_co
        scratch = scratch + comm.scratch()

        def run_body(*refs):
            ins, rest = refs[:n_in], refs[n_in:]
            cins, rest = rest[:n_ci], rest[n_ci:]
            o, rest = rest[:n_out], rest[n_out:]
            couts, rest = rest[:n_co], rest[n_co:]
            cargs = comm.split((*cins, *couts, *rest[n_scr:]))
            first = functools.reduce(jnp.logical_and, [pl.program_id(a) == 0 for a in range(len(grid))])
            last = functools.reduce(jnp.logical_and, [pl.program_id(a) == grid[a] - 1 for a in range(len(grid))])

            @pl.when(first)
            def _():
                comm.start(*cargs)

            body(*ins, *o, *rest[:n_scr])

            @pl.when(last)
            def _():
                comm.finish(*cargs)

    if num_prefetch:
        call = pl.pallas_call(
            run_body, name=name, out_shape=out_shape,
            grid_spec=pltpu.PrefetchScalarGridSpec(num_scalar_prefetch=num_prefetch, grid=grid, in_specs=in_specs,
                                                   out_specs=out_specs, scratch_shapes=scratch),
            compiler_params=pltpu.CompilerParams(**params), interpret=False)
    else:
        call = pl.pallas_call(run_body, name=name, out_shape=out_shape, grid=grid, in_specs=in_specs,
                              out_specs=out_specs, scratch_shapes=scratch,
                              compiler_params=pltpu.CompilerParams(**params), interpret=False)
    if comm is None:
        return call

    def run(*operands):
        res = call(*operands, *comm.inputs)
        return (res[0] if single else tuple(res[:n_out])), tuple(res[n_out:])

    return run


def _run_comm(comm, *, name):
    def body(*refs):
        args = comm.split(refs)
        comm.start(*args)
        comm.finish(*args)

    return pl.pallas_call(body, name=name, out_shape=comm.out_shapes, in_specs=[HBM_SPEC] * len(comm.inputs),
                          out_specs=tuple(HBM_SPEC for _ in comm.out_shapes), scratch_shapes=comm.scratch(),
                          interpret=False)(*comm.inputs)


HBM_SPEC = pl.BlockSpec(memory_space=pl.ANY)


@functools.partial(jax.custom_vjp, nondiff_argnums=(2, 3))
def _bdot(a, b, ca, cb):
    return lax.dot_general(a.astype(BF16), b.astype(BF16), (((ca,), (cb,)), ((), ())), preferred_element_type=F32)


def _bdot_fwd(a, b, ca, cb):
    return _bdot(a, b, ca, cb), (a, b)


def _bdot_bwd(ca, cb, res, ct):
    a, b = res
    da = _bdot(ct, b, 1, 1 - cb) if ca == 1 else _bdot(b, ct, 1 - cb, 1)
    db = _bdot(a, ct, 1 - ca, 0) if cb == 0 else _bdot(ct, a, 0, 1 - ca)
    return da.astype(a.dtype), db.astype(b.dtype)


_bdot.defvjp(_bdot_fwd, _bdot_bwd)


def _split_bf16(a):
    hi = a.astype(BF16)
    return hi, (a - hi.astype(F32)).astype(BF16)


@functools.partial(jax.custom_vjp, nondiff_argnums=(2, 3))
def _xdot(a, b, ca, cb):
    dims = (((ca,), (cb,)), ((), ()))
    ah, al = _split_bf16(a)
    bh, bl = _split_bf16(b)
    dot = lambda p, q: lax.dot_general(p, q, dims, preferred_element_type=F32)
    return dot(ah, bh) + (dot(ah, bl) + dot(al, bh))


def _xdot_fwd(a, b, ca, cb):
    return _xdot(a, b, ca, cb), (a, b)


def _xdot_bwd(ca, cb, res, ct):
    a, b = res
    da = _xdot(ct, b, 1, 1 - cb) if ca == 1 else _xdot(b, ct, 1 - cb, 1)
    db = _xdot(a, ct, 1 - ca, 0) if cb == 0 else _xdot(ct, a, 0, 1 - ca)
    return da, db


_xdot.defvjp(_xdot_fwd, _xdot_bwd)


def _sigmoid(x):
    return 1.0 / (1.0 + jnp.exp(-x))


def _silu(x):
    return x * _sigmoid(x)


def _softplus(x):
    return jnp.maximum(x, 0.0) + jnp.log(1.0 + jnp.exp(-jnp.abs(x)))


def _expm1(x):
    small = x * (1.0 + x * (0.5 + x * (1.0 / 6.0 + x * (1.0 / 24.0 + x * (1.0 / 120.0 + x * (1.0 / 720.0))))))
    return jnp.where(jnp.abs(x) < 0.1, small, jnp.exp(x) - 1.0)


def _gelu(x):
    return 0.5 * x * (1.0 + jnp.tanh(math.sqrt(2.0 / math.pi) * (x + 0.044715 * x * x * x)))


def _rms(x, w):
    var = jnp.mean(x * x, axis=-1, keepdims=True)
    return x * lax.rsqrt(var + NORM_EPS) * w


def _pick_lane(v, idx):
    lane = lax.broadcasted_iota(jnp.int32, v.shape, 1)
    return jnp.sum(jnp.where(lane == idx, v, 0.0), axis=1, keepdims=True)


def _pick_row(v, idx):
    row = lax.broadcasted_iota(jnp.int32, v.shape, 0)
    return jnp.sum(jnp.where(row == idx, v, 0.0), axis=0, keepdims=True)


def _mm(a, b, *, name, ta=False, tb=False, out_dtype=F32, add=None, bias=None, la=None, lb=None, nsplit=None,
        comm=None, tm=1024, tn=1024, tk=2048):
    a2 = a.shape[-2:]
    b2 = b.shape[-2:]
    m, k = (a2[1], a2[0]) if ta else a2
    n = b2[0] if tb else b2[1]
    assert (b2[1] if tb else b2[0]) == k
    tm, tn, tk = _tile(m, tm), _tile(n // (nsplit or 1), tn), _tile(k, tk)
    nk = k // tk

    def a_map(i, j, kk):
        idx = (kk, i) if ta else (i, kk)
        return idx if la is None else (la,) + idx

    def b_map(i, j, kk):
        idx = (j, kk) if tb else (kk, j)
        return idx if lb is None else (lb,) + idx

    a_blk = (tk, tm) if ta else (tm, tk)
    b_blk = (tn, tk) if tb else (tk, tn)
    in_specs = [pl.BlockSpec(a_blk if la is None else (None,) + a_blk, a_map),
                pl.BlockSpec(b_blk if lb is None else (None,) + b_blk, b_map)]
    operands = [a, b]
    if add is not None:
        in_specs.append(pl.BlockSpec((tm, tn), lambda i, j, kk: (i, j)))
        operands.append(add)
    if bias is not None:
        in_specs.append(pl.BlockSpec((1, tn), lambda i, j, kk: (0, j)))
        operands.append(bias)
    dims = (((0 if ta else 1,), (1 if tb else 0,)), ((), ()))

    def body(*refs):
        a_ref, b_ref = refs[0], refs[1]
        rest = list(refs[2:])
        add_ref = rest.pop(0) if add is not None else None
        bias_ref = rest.pop(0) if bias is not None else None
        o_ref, acc_ref = rest
        kk = pl.program_id(2)

        @pl.when(kk == 0)
        def _():
            acc_ref[...] = jnp.zeros_like(acc_ref)

        acc_ref[...] += lax.dot_general(a_ref[...].astype(BF16), b_ref[...].astype(BF16), dims,
                                        preferred_element_type=F32)

        @pl.when(kk == nk - 1)
        def _():
            r = acc_ref[...]
            if add_ref is not None:
                r = r + add_ref[...].astype(F32)
            if bias_ref is not None:
                r = r + bias_ref[...]
            o_ref[...] = r.astype(out_dtype)

    if nsplit is None:
        out_shape = jax.ShapeDtypeStruct((m, n), out_dtype)
        out_spec = pl.BlockSpec((tm, tn), lambda i, j, kk: (i, j))
    else:
        per = n // nsplit // tn
        out_shape = jax.ShapeDtypeStruct((nsplit, m, n // nsplit), out_dtype)
        out_spec = pl.BlockSpec((None, tm, tn), lambda i, j, kk: (j // per, i, j % per))
    return _pcall(body, name=name, comm=comm, out_shape=out_shape, grid=(m // tm, n // tn, nk), in_specs=in_specs,
                  out_specs=out_spec, scratch=[pltpu.VMEM((tm, tn), F32)],
                  sem=("parallel", "parallel", "arbitrary"))(*operands)


def _rms_fwd(x, w, *, name):
    s, d = x.shape
    t = _tile(s, 256, SUBLANES)

    def body(x_ref, w_ref, o_ref):
        o_ref[...] = _rms(x_ref[...], w_ref[...]).astype(BF16)

    return _pcall(body, name=name, out_shape=jax.ShapeDtypeStruct((s, d), BF16), grid=(s // t,),
                  in_specs=[pl.BlockSpec((t, d), lambda i: (i, 0)), pl.BlockSpec((1, d), lambda i: (0, 0))],
                  out_specs=pl.BlockSpec((t, d), lambda i: (i, 0)), sem=("parallel",))(x, w.reshape(1, d))


def _rms_bwd(x, w, dh, res, *, name):
    s, d = x.shape
    t = _tile(s, 256, SUBLANES)

    def body(*refs):
        if res is None:
            x_ref, w_ref, dh_ref, dx_ref, dw_ref = refs
            res_ref = None
        else:
            x_ref, w_ref, dh_ref, res_ref, dx_ref, dw_ref = refs
        _, vjp = jax.vjp(_rms, x_ref[...], w_ref[...])
        dx, dw = vjp(dh_ref[...].astype(F32))
        if res_ref is not None:
            dx = dx + res_ref[...]
        dx_ref[...] = dx

        @pl.when(pl.program_id(0) == 0)
        def _():
            dw_ref[...] = jnp.zeros_like(dw_ref)

        dw_ref[...] += dw

    tok = pl.BlockSpec((t, d), lambda i: (i, 0))
    row = pl.BlockSpec((1, d), lambda i: (0, 0))
    operands = [x, w.reshape(1, d), dh] + ([] if res is None else [res])
    return _pcall(body, name=name,
                  out_shape=(jax.ShapeDtypeStruct((s, d), F32), jax.ShapeDtypeStruct((1, d), F32)), grid=(s // t,),
                  in_specs=[tok, row, tok] + ([] if res is None else [tok]), out_specs=(tok, row),
                  sem=("arbitrary",))(*operands)


def _loss_head(x, w, target, *, name):
    s, d = x.shape
    t = _tile(s, 256, SUBLANES)

    def body(x_ref, w_ref, t_ref, loss_ref, dx_ref, dw_ref):
        def f(xv, wv):
            err = _rms(xv, wv) - t_ref[...]
            return 0.5 * jnp.sum(jnp.mean(err * err, axis=-1))

        val, vjp = jax.vjp(f, x_ref[...], w_ref[...])
        dx, dw = vjp(jnp.ones((), F32))
        dx_ref[...] = dx

        @pl.when(pl.program_id(0) == 0)
        def _():
            dw_ref[...] = jnp.zeros_like(dw_ref)
            loss_ref[...] = jnp.zeros_like(loss_ref)

        dw_ref[...] += dw
        loss_ref[...] += jnp.full(loss_ref.shape, val, F32)

    tok = pl.BlockSpec((t, d), lambda i: (i, 0))
    row = pl.BlockSpec((1, d), lambda i: (0, 0))
    return _pcall(body, name=name,
                  out_shape=(jax.ShapeDtypeStruct((1, LANES), F32), jax.ShapeDtypeStruct((s, d), F32),
                             jax.ShapeDtypeStruct((1, d), F32)),
                  grid=(s // t,), in_specs=[tok, row, tok],
                  out_specs=(pl.BlockSpec((1, LANES), lambda i: (0, 0)), tok, row), sem=("arbitrary",))(
                      x, w.reshape(1, d), target)


def _conv_shifts(prev8, cur, t):
    xp = jnp.concatenate([prev8, cur], axis=0)
    out = []
    for j in range(CONV_WIDTH):
        k = CONV_WIDTH - 1 - j
        out.append(cur if k == 0 else pltpu.roll(xp, k, 0)[SUBLANES:SUBLANES + t])
    return out


def _conv_fwd(p, col_blk, width, w8, b, *, name):
    s = p.shape[0]
    t = _tile(s, 256, SUBLANES)
    r8 = t // SUBLANES

    def body(cur_ref, prev_ref, w_ref, b_ref, y_ref):
        i = pl.program_id(0)
        prev8 = jnp.where(i == 0, 0.0, prev_ref[...])
        sh = _conv_shifts(prev8, cur_ref[...], t)
        w = w_ref[...]
        y = b_ref[...] + sh[0] * w[0:1]
        for j in range(1, CONV_WIDTH):
            y = y + sh[j] * w[j:j + 1]
        y_ref[...] = y

    return _pcall(body, name=name, out_shape=jax.ShapeDtypeStruct((s, width), F32), grid=(s // t,),
                  in_specs=[pl.BlockSpec((t, width), lambda i: (i, col_blk)),
                            pl.BlockSpec((SUBLANES, width), lambda i: (jnp.maximum(i * r8 - 1, 0), col_blk)),
                            pl.BlockSpec((SUBLANES, width), lambda i: (0, 0)),
                            pl.BlockSpec((1, width), lambda i: (0, 0))],
                  out_specs=pl.BlockSpec((t, width), lambda i: (i, 0)), sem=("parallel",))(p, p, w8, b)


def _conv_bwd(p, col_blk, width, w8, dy, *, name):
    s = p.shape[0]
    t = _tile(s, 256, SUBLANES)
    r8 = t // SUBLANES
    nt = s // t

    def body(cur_ref, prev_ref, w_ref, dy_ref, dyn_ref, dx_ref, dw_ref, db_ref):
        i = pl.program_id(0)
        prev8 = jnp.where(i == 0, 0.0, prev_ref[...])
        sh = _conv_shifts(prev8, cur_ref[...], t)
        dy = dy_ref[...]
        next8 = jnp.where(i == nt - 1, 0.0, dyn_ref[...])
        dyp = jnp.concatenate([dy, next8], axis=0)
        w = w_ref[...]
        rows = lax.broadcasted_iota(jnp.int32, (SUBLANES, width), 0)
        dx = dy * w[CONV_WIDTH - 1:CONV_WIDTH]
        dw = jnp.zeros((SUBLANES, width), F32)
        for j in range(CONV_WIDTH):
            k = CONV_WIDTH - 1 - j
            if k:
                dx = dx + pltpu.roll(dyp, t + SUBLANES - k, 0)[0:t] * w[j:j + 1]
            dw = dw + jnp.where(rows == j, jnp.sum(dy * sh[j], axis=0, keepdims=True), 0.0)
        dx_ref[...] = dx

        @pl.when(i == 0)
        def _():
            dw_ref[...] = jnp.zeros_like(dw_ref)
            db_ref[...] = jnp.zeros_like(db_ref)

        dw_ref[...] += dw
        db_ref[...] += jnp.sum(dy, axis=0, keepdims=True)

    return _pcall(body, name=name,
                  out_shape=(jax.ShapeDtypeStruct((s, width), F32), jax.ShapeDtypeStruct((SUBLANES, width), F32),
                             jax.ShapeDtypeStruct((1, width), F32)),
                  grid=(nt,),
                  in_specs=[pl.BlockSpec((t, width), lambda i: (i, col_blk)),
                            pl.BlockSpec((SUBLANES, width), lambda i: (jnp.maximum(i * r8 - 1, 0), col_blk)),
                            pl.BlockSpec((SUBLANES, width), lambda i: (0, 0)),
                            pl.BlockSpec((t, width), lambda i: (i, 0)),
                            pl.BlockSpec((SUBLANES, width), lambda i: (jnp.minimum((i + 1) * r8, s // SUBLANES - 1), 0))],
                  out_specs=(pl.BlockSpec((t, width), lambda i: (i, 0)),
                             pl.BlockSpec((SUBLANES, width), lambda i: (0, 0)),
                             pl.BlockSpec((1, width), lambda i: (0, 0))),
                  sem=("arbitrary",))(p, p, w8, dy, dy)


def _dn_chunk(state, c, z, ba, alog_row, dt_row, nw_row, *, heads, bw):
    cs = c.shape[0]
    hd = DN_HEAD_DIM
    qkv = _silu(c)
    gfull = -jnp.exp(alog_row) * _softplus(ba + dt_row)
    beta_full = _sigmoid(ba)
    ri = lax.broadcasted_iota(jnp.int32, (cs, cs), 0)
    ci = lax.broadcasted_iota(jnp.int32, (cs, cs), 1)
    causal = ri >= ci
    strict = ri > ci
    tril = causal.astype(F32)
    eye = (ri == ci).astype(F32)
    gc = _xdot(tril, gfull, 1, 0)
    gct = _xdot(gfull, tril, 0, 1)
    outs, states = [], []
    for h in range(heads):
        q = qkv[:, h * hd:(h + 1) * hd]
        k = qkv[:, bw + h * hd:bw + (h + 1) * hd]
        v = qkv[:, 2 * bw + h * hd:2 * bw + (h + 1) * hd]
        q = q * lax.rsqrt(jnp.sum(q * q, axis=-1, keepdims=True) + NORM_EPS) * (hd ** -0.5)
        k = k * lax.rsqrt(jnp.sum(k * k, axis=-1, keepdims=True) + NORM_EPS)
        beta = _pick_lane(beta_full, h)
        g_col = _pick_lane(gc, heads + h)
        g_row = _pick_row(gct, heads + h)
        decay = jnp.exp(jnp.where(causal, g_col - g_row, -1e30))
        k_beta = k * beta
        v_beta = v * beta
        kk = _bdot(k_beta, k, 1, 1) * decay
        m = -jnp.where(strict, kk, 0.0)
        tinv = eye + m
        pw = m
        for _ in range(int(math.log2(cs)) - 1):
            pw = _xdot(pw, pw, 1, 0)
            tinv = tinv + _xdot(tinv, pw, 1, 0)
        rhs = jnp.concatenate([v_beta, k_beta * jnp.exp(g_col)], axis=-1)
        sol = _xdot(tinv, rhs, 1, 0)
        u, w = sol[:, :hd], sol[:, hd:]
        qk = jnp.where(causal, _bdot(q, k, 1, 1) * decay, 0.0)
        g_last = _pick_row(g_col, cs - 1)
        k_dec = k * jnp.exp(g_last - g_col)
        q_dec = q * jnp.exp(g_col)
        s_h = state[h]
        v_new = u - _bdot(w, s_h, 1, 0)
        o = _bdot(q_dec, s_h, 1, 0) + _bdot(qk, v_new, 1, 0)
        states.append(s_h * jnp.exp(g_last) + _bdot(k_dec, v_new, 0, 0))
        outs.append(_rms(o, nw_row) * _silu(z[:, h * hd:(h + 1) * hd]))
    return jnp.concatenate(outs, axis=-1), tuple(states)


def _dn_rows(a_log, dt_bias, heads):
    z = jnp.zeros((heads,), F32)
    pad = jnp.zeros((BA_PAD - 2 * heads,), F32)
    return (jnp.concatenate([z, a_log, pad]).reshape(1, BA_PAD), jnp.concatenate([z, dt_bias, pad]).reshape(1, BA_PAD))


def _dn_fwd(p, conv, a_log, dt_bias, norm_w, *, bw, ba_blk, name, comm=None):
    s = p.shape[0]
    heads = bw // DN_HEAD_DIM
    cs = min(DN_CHUNK, s)
    n = s // cs
    hd = DN_HEAD_DIM
    alog_row, dt_row = _dn_rows(a_log, dt_bias, heads)
    fn = functools.partial(_dn_chunk, heads=heads, bw=bw)

    def body(c_ref, z_ref, ba_ref, al_ref, dt_ref, nw_ref, o_ref, save_ref, st_ref):
        @pl.when(pl.program_id(0) == 0)
        def _():
            st_ref[...] = jnp.zeros_like(st_ref)

        save_ref[...] = st_ref[...]
        o, new = fn(tuple(st_ref[h] for h in range(heads)), c_ref[...], z_ref[...], ba_ref[...], al_ref[...],
                    dt_ref[...], nw_ref[...])
        o_ref[...] = o.astype(BF16)
        for h in range(heads):
            st_ref[h] = new[h]

    row = lambda wd: pl.BlockSpec((1, wd), lambda i: (0, 0))
    return _pcall(body, name=name, comm=comm,
                  out_shape=(jax.ShapeDtypeStruct((s, bw), BF16), jax.ShapeDtypeStruct((n, heads, hd, hd), F32)),
                  grid=(n,),
                  in_specs=[pl.BlockSpec((cs, 3 * bw), lambda i: (i, 0)), pl.BlockSpec((cs, bw), lambda i: (i, 3)),
                            pl.BlockSpec((cs, BA_PAD), lambda i: (i, ba_blk)), row(BA_PAD), row(BA_PAD), row(hd)],
                  out_specs=(pl.BlockSpec((cs, bw), lambda i: (i, 0)),
                             pl.BlockSpec((None, heads, hd, hd), lambda i: (i, 0, 0, 0))),
                  scratch=[pltpu.VMEM((heads, hd, hd), F32)], sem=("arbitrary",))(
                      conv, p, p, alog_row, dt_row, norm_w.reshape(1, hd))


def _dn_bwd(p, conv, states, d_o, a_log, dt_bias, norm_w, *, bw, ba_blk, name, comm=None):
    s = p.shape[0]
    heads = bw // DN_HEAD_DIM
    cs = min(DN_CHUNK, s)
    n = s // cs
    hd = DN_HEAD_DIM
    alog_row, dt_row = _dn_rows(a_log, dt_bias, heads)
    fn = functools.partial(_dn_chunk, heads=heads, bw=bw)

    def body(c_ref, z_ref, ba_ref, st_ref, do_ref, al_ref, dt_ref, nw_ref,
             dc_ref, dz_ref, dba_ref, dal_ref, ddt_ref, dnw_ref, dst_ref):
        @pl.when(pl.program_id(0) == 0)
        def _():
            dst_ref[...] = jnp.zeros_like(dst_ref)
            dal_ref[...] = jnp.zeros_like(dal_ref)
            ddt_ref[...] = jnp.zeros_like(ddt_ref)
            dnw_ref[...] = jnp.zeros_like(dnw_ref)

        _, vjp = jax.vjp(fn, tuple(st_ref[h] for h in range(heads)), c_ref[...], z_ref[...], ba_ref[...],
                         al_ref[...], dt_ref[...], nw_ref[...])
        dst, dc, dz, dba, dal, ddt, dnw = vjp((do_ref[...].astype(F32), tuple(dst_ref[h] for h in range(heads))))
        for h in range(heads):
            dst_ref[h] = dst[h]
        dc_ref[...] = dc
        dz_ref[...] = dz
        dba_ref[...] = dba
        dal_ref[...] += dal
        ddt_ref[...] += ddt
        dnw_ref[...] += dnw

    rev = lambda i: n - 1 - i
    row = lambda wd: pl.BlockSpec((1, wd), lambda i: (0, 0))
    return _pcall(body, name=name, comm=comm,
                  out_shape=(jax.ShapeDtypeStruct((s, 3 * bw), F32), jax.ShapeDtypeStruct((s, bw), F32),
                             jax.ShapeDtypeStruct((s, BA_PAD), F32), jax.ShapeDtypeStruct((1, BA_PAD), F32),
                             jax.ShapeDtypeStruct((1, BA_PAD), F32), jax.ShapeDtypeStruct((1, hd), F32)),
                  grid=(n,),
                  in_specs=[pl.BlockSpec((cs, 3 * bw), lambda i: (rev(i), 0)),
                            pl.BlockSpec((cs, bw), lambda i: (rev(i), 3)),
                            pl.BlockSpec((cs, BA_PAD), lambda i: (rev(i), ba_blk)),
                            pl.BlockSpec((None, heads, hd, hd), lambda i: (rev(i), 0, 0, 0)),
                            pl.BlockSpec((cs, bw), lambda i: (rev(i), 0)), row(BA_PAD), row(BA_PAD), row(hd)],
                  out_specs=(pl.BlockSpec((cs, 3 * bw), lambda i: (rev(i), 0)),
                             pl.BlockSpec((cs, bw), lambda i: (rev(i), 0)),
                             pl.BlockSpec((cs, BA_PAD), lambda i: (rev(i), 0)), row(BA_PAD), row(BA_PAD), row(hd)),
                  scratch=[pltpu.VMEM((heads, hd, hd), F32)], sem=("arbitrary",))(
                      conv, p, p, states, d_o, alog_row, dt_row, norm_w.reshape(1, hd))


def _lru_gates(xc, wr, br, wi, bi, lam):
    r = _sigmoid(_bdot(xc, wr, 1, 0) + br)
    i = _sigmoid(_bdot(xc, wi, 1, 0) + bi)
    log_a = -LRU_C * r * _softplus(-lam)
    return jnp.exp(log_a), jnp.sqrt(-_expm1(2.0 * log_a)) * (i * xc)


def _scan_rows(t, step, carry):
    def trip(g, cr):
        base = pl.multiple_of(g * SUBLANES, SUBLANES)
        for r in range(SUBLANES):
            cr = step(base + r, cr)
        return cr
    return lax.fori_loop(0, t // SUBLANES, trip, carry)


def _scan_rows_rev(t, step, carry):
    def trip(g, cr):
        base = pl.multiple_of((t // SUBLANES - 1 - g) * SUBLANES, SUBLANES)
        for r in range(SUBLANES - 1, -1, -1):
            cr = step(base + r, cr)
        return cr
    return lax.fori_loop(0, t // SUBLANES, trip, carry)


def _lru_fwd(p, xc, wr, br, wi, bi, lam, *, bw, z_blk, name):
    s = p.shape[0]
    t = _tile(s, 256, SUBLANES)
    nt = s // t

    def body(xc_ref, z_ref, wr_ref, br_ref, wi_ref, bi_ref, lam_ref, o_ref, save_ref, a_s, b_s, h_s, carry_s):
        @pl.when(pl.program_id(0) == 0)
        def _():
            carry_s[...] = jnp.zeros_like(carry_s)

        a, inp = _lru_gates(xc_ref[...], wr_ref[...], br_ref[...], wi_ref[...], bi_ref[...], lam_ref[...])
        a_s[...] = a
        b_s[...] = inp
        h0 = carry_s[...]
        save_ref[...] = h0

        def step(r, h):
            h = a_s[pl.ds(r, 1), :] * h + b_s[pl.ds(r, 1), :]
            h_s[pl.ds(r, 1), :] = h
            return h

        carry_s[...] = _scan_rows(t, step, h0)
        o_ref[...] = (h_s[...] * _silu(z_ref[...])).astype(BF16)

    tok = pl.BlockSpec((t, bw), lambda i: (i, 0))
    row = pl.BlockSpec((1, bw), lambda i: (0, 0))
    mat = pl.BlockSpec((bw, bw), lambda i: (0, 0))
    return _pcall(body, name=name,
                  out_shape=(jax.ShapeDtypeStruct((s, bw), BF16), jax.ShapeDtypeStruct((nt, 1, bw), F32)), grid=(nt,),
                  in_specs=[tok, pl.BlockSpec((t, bw), lambda i: (i, z_blk)), mat, row, mat, row, row],
                  out_specs=(tok, pl.BlockSpec((None, 1, bw), lambda i: (i, 0, 0))),
                  scratch=[pltpu.VMEM((t, bw), F32)] * 3 + [pltpu.VMEM((1, bw), F32)], sem=("arbitrary",))(
                      xc, p, wr, br, wi, bi, lam)


def _lru_bwd(p, xc, saves, d_o, wr, br, wi, bi, lam, *, bw, z_blk, name):
    s = p.shape[0]
    t = _tile(s, 256, SUBLANES)
    nt = s // t

    def body(xc_ref, z_ref, sv_ref, do_ref, wr_ref, br_ref, wi_ref, bi_ref, lam_ref,
             dxc_ref, dz_ref, dwr_ref, dwi_ref, dbr_ref, dbi_ref, dlam_ref, a_s, b_s, h_s, g_s, carry_s):
        @pl.when(pl.program_id(0) == 0)
        def _():
            carry_s[...] = jnp.zeros_like(carry_s)
            for r in (dwr_ref, dwi_ref, dbr_ref, dbi_ref, dlam_ref):
                r[...] = jnp.zeros_like(r)

        (a, inp), vjp_g = jax.vjp(_lru_gates, xc_ref[...], wr_ref[...], br_ref[...], wi_ref[...], bi_ref[...],
                                  lam_ref[...])
        a_s[...] = a
        b_s[...] = inp
        h0 = sv_ref[...]

        def fstep(r, h):
            h_s[pl.ds(r, 1), :] = h
            return a_s[pl.ds(r, 1), :] * h + b_s[pl.ds(r, 1), :]

        _scan_rows(t, fstep, h0)
        a = a_s[...]
        hs = a * h_s[...] + b_s[...]
        z = z_ref[...]
        d_o = do_ref[...].astype(F32)
        _, vjp_o = jax.vjp(lambda hv, zv: hv * _silu(zv), hs, z)
        dhs, dz = vjp_o(d_o)
        dz_ref[...] = dz
        g_s[...] = dhs

        def bstep(r, cr):
            g = g_s[pl.ds(r, 1), :] + cr
            g_s[pl.ds(r, 1), :] = g
            return a_s[pl.ds(r, 1), :] * g

        carry_s[...] = _scan_rows_rev(t, bstep, carry_s[...])
        g = g_s[...]
        dxc, dwr, dbr, dwi, dbi, dlam = vjp_g((g * h_s[...], g))
        dxc_ref[...] = dxc
        dwr_ref[...] += dwr
        dwi_ref[...] += dwi
        dbr_ref[...] += dbr
        dbi_ref[...] += dbi
        dlam_ref[...] += dlam

    rev = lambda i: nt - 1 - i
    tok = pl.BlockSpec((t, bw), lambda i: (rev(i), 0))
    row = pl.BlockSpec((1, bw), lambda i: (0, 0))
    mat = pl.BlockSpec((bw, bw), lambda i: (0, 0))
    sd = jax.ShapeDtypeStruct
    return _pcall(body, name=name,
                  out_shape=(sd((s, bw), F32), sd((s, bw), F32), sd((bw, bw), F32), sd((bw, bw), F32),
                             sd((1, bw), F32), sd((1, bw), F32), sd((1, bw), F32)),
                  grid=(nt,),
                  in_specs=[tok, pl.BlockSpec((t, bw), lambda i: (rev(i), z_blk)),
                            pl.BlockSpec((None, 1, bw), lambda i: (rev(i), 0, 0)), tok, mat, row, mat, row, row],
                  out_specs=(tok, tok, mat, mat, row, row, row),
                  scratch=[pltpu.VMEM((t, bw), F32)] * 4 + [pltpu.VMEM((1, bw), F32)], sem=("arbitrary",))(
                      xc, p, saves, d_o, wr, br, wi, bi, lam)


def _s5_prep(log_dt, a_re, a_im, b_re, b_im, c_re, c_im, d_skip):
    g, n = a_re.shape
    gs = d_skip.shape[1]
    gpb = LANES // gs
    nb = g // gpb
    dt = jnp.exp(log_dt)[:, None]
    mag = jnp.exp(dt * a_re)
    ab_re = mag * jnp.cos(dt * a_im)
    ab_im = mag * jnp.sin(dt * a_im)
    den = a_re * a_re + a_im * a_im
    f_re = ((ab_re - 1.0) * a_re + ab_im * a_im) / den
    f_im = (ab_im * a_re - (ab_re - 1.0) * a_im) / den
    bb_re = f_re[..., None] * b_re - f_im[..., None] * b_im
    bb_im = f_re[..., None] * b_im + f_im[..., None] * b_re
    eye = jnp.eye(gpb, dtype=F32)

    def b_dense(bb):
        t = bb.reshape(nb, gpb, n, gs)
        return jnp.einsum("bgnc,gh->bgchn", t, eye).reshape(nb, gpb * gs, gpb * n)

    def c_dense(cc):
        t = cc.reshape(nb, gpb, gs, n)
        return jnp.einsum("bgcn,gh->bgnhc", t, eye).reshape(nb, gpb * n, gpb * gs)

    lanes = gpb * n
    sub = lanes // LANES
    return (ab_re.reshape(nb, sub, LANES), ab_im.reshape(nb, sub, LANES), b_dense(bb_re), b_dense(bb_im),
            c_dense(c_re), c_dense(c_im), d_skip.reshape(1, g * gs))


def _s5_out(xre, xim, cre, cim, d, u):
    return _gelu(_bdot(xre, cre, 1, 0) - _bdot(xim, cim, 1, 0) + d * u)


S5_LOOKAHEAD = 4


def _cmul(ar, ai, xr, xi):
    return ar * xr - ai * xi, ar * xi + ai * xr


def _s5_scan(ar, ai, b_re, b_im, t_re, t_im, t, reverse):
    k = S5_LOOKAHEAD
    a2 = _cmul(ar, ai, ar, ai)
    a4 = _cmul(*a2, *a2)
    lo = lambda n, off=0: pl.ds(off, n)
    for (src_re, src_im, dst_re, dst_im, pw, d) in ((b_re, b_im, t_re, t_im, (ar, ai), 1),
                                                     (t_re, t_im, b_re, b_im, a2, 2)):
        keep = lo(d, t - d) if reverse else lo(d)
        cur = lo(t - d) if reverse else lo(t - d, d)
        nbr = lo(t - d, d) if reverse else lo(t - d)
        dst_re[keep] = src_re[keep]
        dst_im[keep] = src_im[keep]
        mr, mi = _cmul(*pw, src_re[nbr], src_im[nbr])
        dst_re[cur] = src_re[cur] + mr
        dst_im[cur] = src_im[cur] + mi

    def step(g, cr):
        row = pl.multiple_of(((t // k - 1 - g) if reverse else g) * k, k)
        mr, mi = _cmul(*a4, *cr)
        nr = mr + b_re[pl.ds(row, k)]
        ni = mi + b_im[pl.ds(row, k)]
        b_re[pl.ds(row, k)] = nr
        b_im[pl.ds(row, k)] = ni
        return nr, ni

    zero = jnp.zeros((k,) + ar.shape, F32)
    xr, xi = lax.fori_loop(0, t // k, step, (zero, zero), unroll=2)
    return (xr[0], xi[0]) if reverse else (xr[k - 1], xi[k - 1])


def _s5_fwd(p, prep, *, bw, u_blk0, name):
    s = p.shape[0]
    are, aim, bre, bim, cre, cim, d = prep
    nb, sub, _ = are.shape
    lanes = sub * LANES
    t = _tile(s, 256, SUBLANES)
    nt = s // t

    def body(u_ref, are_ref, aim_ref, bre_ref, bim_ref, cre_ref, cim_ref, d_ref, y_ref, save_ref,
             bre_s, bim_s, xre_s, xim_s, carry_s):
        @pl.when(pl.program_id(1) == 0)
        def _():
            carry_s[...] = jnp.zeros_like(carry_s)

        u = u_ref[...]
        xre_s[...] = _bdot(u, bre_ref[...], 1, 0).reshape(t, sub, LANES)
        xim_s[...] = _bdot(u, bim_ref[...], 1, 0).reshape(t, sub, LANES)
        ar, ai = are_ref[...], aim_ref[...]
        save_ref[...] = carry_s[...]
        er, ei = _cmul(ar, ai, carry_s[0], carry_s[1])
        xre_s[0] += er
        xim_s[0] += ei
        xr, xi = _s5_scan(ar, ai, xre_s, xim_s, bre_s, bim_s, t, False)
        carry_s[0] = xr
        carry_s[1] = xi
        y_ref[...] = _s5_out(xre_s[...].reshape(t, lanes), xim_s[...].reshape(t, lanes), cre_ref[...], cim_ref[...],
                             d_ref[...], u).astype(BF16)

    vec = pl.BlockSpec((None, sub, LANES), lambda j, i: (j, 0, 0))
    bmat = pl.BlockSpec((None, LANES, lanes), lambda j, i: (j, 0, 0))
    cmat = pl.BlockSpec((None, lanes, LANES), lambda j, i: (j, 0, 0))
    return _pcall(body, name=name,
                  out_shape=(jax.ShapeDtypeStruct((s, bw), BF16), jax.ShapeDtypeStruct((nb, nt, 2, sub, LANES), F32)),
                  grid=(nb, nt),
                  in_specs=[pl.BlockSpec((t, LANES), lambda j, i: (i, u_blk0 + j)), vec, vec, bmat, bmat, cmat, cmat,
                            pl.BlockSpec((1, LANES), lambda j, i: (0, j))],
                  out_specs=(pl.BlockSpec((t, LANES), lambda j, i: (i, j)),
                             pl.BlockSpec((None, None, 2, sub, LANES), lambda j, i: (j, i, 0, 0, 0))),
                  scratch=[pltpu.VMEM((t, sub, LANES), F32)] * 4 + [pltpu.VMEM((2, sub, LANES), F32)],
                  sem=("parallel", "arbitrary"))(p, are, aim, bre, bim, cre, cim, d)


def _s5_bwd(p, prep, saves, dyg, *, bw, u_blk0, name):
    s = p.shape[0]
    are, aim, bre, bim, cre, cim, d = prep
    nb, sub, _ = are.shape
    lanes = sub * LANES
    t = _tile(s, 256, SUBLANES)
    nt = s // t

    def body(u_ref, dy_ref, sv_ref, are_ref, aim_ref, bre_ref, bim_ref, cre_ref, cim_ref, d_ref,
             du_ref, dar_ref, dai_ref, dbre_ref, dbim_ref, dcre_ref, dcim_ref, dd_ref,
             bre_s, bim_s, xre_s, xim_s, tre_s, tim_s, carry_s):
        @pl.when(pl.program_id(1) == 0)
        def _():
            carry_s[...] = jnp.zeros_like(carry_s)
            for r in (dar_ref, dai_ref, dbre_ref, dbim_ref, dcre_ref, dcim_ref, dd_ref):
                r[...] = jnp.zeros_like(r)

        u = u_ref[...]
        xre_s[...] = _bdot(u, bre_ref[...], 1, 0).reshape(t, sub, LANES)
        xim_s[...] = _bdot(u, bim_ref[...], 1, 0).reshape(t, sub, LANES)
        ar, ai = are_ref[...], aim_ref[...]
        er, ei = _cmul(ar, ai, sv_ref[0], sv_ref[1])
        xre_s[0] += er
        xim_s[0] += ei
        _s5_scan(ar, ai, xre_s, xim_s, tre_s, tim_s, t, False)
        _, vjp_o = jax.vjp(_s5_out, xre_s[...].reshape(t, lanes), xim_s[...].reshape(t, lanes), cre_ref[...],
                           cim_ref[...], d_ref[...], u)
        dxre, dxim, dcre, dcim, dd, du = vjp_o(dy_ref[...].astype(F32))
        dcre_ref[...] += dcre.astype(F32)
        dcim_ref[...] += dcim.astype(F32)
        dd_ref[...] += dd
        bre_s[...] = dxre.reshape(t, sub, LANES)
        bim_s[...] = dxim.reshape(t, sub, LANES)
        bre_s[t - 1] += carry_s[0]
        bim_s[t - 1] += carry_s[1]
        g0r, g0i = _s5_scan(ar, -ai, bre_s, bim_s, tre_s, tim_s, t, True)
        carry_s[0], carry_s[1] = _cmul(ar, -ai, g0r, g0i)
        gr, gi = bre_s[pl.ds(1, t - 1)], bim_s[pl.ds(1, t - 1)]
        pr, pi = xre_s[pl.ds(0, t - 1)], xim_s[pl.ds(0, t - 1)]
        dar_ref[...] += jnp.sum(gr * pr + gi * pi, axis=0) + bre_s[0] * sv_ref[0] + bim_s[0] * sv_ref[1]
        dai_ref[...] += jnp.sum(gi * pr - gr * pi, axis=0) + bim_s[0] * sv_ref[0] - bre_s[0] * sv_ref[1]
        dbu_re = bre_s[...].reshape(t, lanes)
        dbu_im = bim_s[...].reshape(t, lanes)
        du_ref[...] = du + _bdot(dbu_re, bre_ref[...], 1, 1) + _bdot(dbu_im, bim_ref[...], 1, 1)
        dbre_ref[...] += _bdot(u, dbu_re, 0, 0)
        dbim_ref[...] += _bdot(u, dbu_im, 0, 0)

    rev = lambda i: nt - 1 - i
    vec = pl.BlockSpec((None, sub, LANES), lambda j, i: (j, 0, 0))
    bmat = pl.BlockSpec((None, LANES, lanes), lambda j, i: (j, 0, 0))
    cmat = pl.BlockSpec((None, lanes, LANES), lambda j, i: (j, 0, 0))
    drow = pl.BlockSpec((1, LANES), lambda j, i: (0, j))
    sd = jax.ShapeDtypeStruct
    return _pcall(body, name=name,
                  out_shape=(sd((s, bw), F32), sd(are.shape, F32), sd(aim.shape, F32), sd(bre.shape, F32),
                             sd(bim.shape, F32), sd(cre.shape, F32), sd(cim.shape, F32), sd((1, bw), F32)),
                  grid=(nb, nt),
                  in_specs=[pl.BlockSpec((t, LANES), lambda j, i: (rev(i), u_blk0 + j)),
                            pl.BlockSpec((t, LANES), lambda j, i: (rev(i), j)),
                            pl.BlockSpec((None, None, 2, sub, LANES), lambda j, i: (j, rev(i), 0, 0, 0)),
                            vec, vec, bmat, bmat, cmat, cmat, drow],
                  out_specs=(pl.BlockSpec((t, LANES), lambda j, i: (rev(i), j)), vec, vec, bmat, bmat, cmat, cmat, drow),
                  scratch=[pltpu.VMEM((t, sub, LANES), F32)] * 6 + [pltpu.VMEM((2, sub, LANES), F32)],
                  sem=("parallel", "arbitrary"))(p, dyg, saves, are, aim, bre, bim, cre, cim, d)


def _glu_gate(pre, z, bw):
    return pre[:, :bw] * _sigmoid(pre[:, bw:]) * _silu(z)


def _glu_fwd(pre, p, *, bw, z_blk, name):
    s = pre.shape[0]
    t = _tile(s, 256, SUBLANES)

    def body(pre_ref, z_ref, o_ref):
        o_ref[...] = _glu_gate(pre_ref[...], z_ref[...], bw).astype(BF16)

    return _pcall(body, name=name, out_shape=jax.ShapeDtypeStruct((s, bw), BF16), grid=(s // t,),
                  in_specs=[pl.BlockSpec((t, 2 * bw), lambda i: (i, 0)), pl.BlockSpec((t, bw), lambda i: (i, z_blk))],
                  out_specs=pl.BlockSpec((t, bw), lambda i: (i, 0)), sem=("parallel",))(pre, p)


def _glu_bwd(pre, p, d_o, *, bw, z_blk, name):
    s = pre.shape[0]
    t = _tile(s, 256, SUBLANES)

    def body(pre_ref, z_ref, do_ref, dpre_ref, dz_ref, db_ref):
        _, vjp = jax.vjp(functools.partial(_glu_gate, bw=bw), pre_ref[...], z_ref[...])
        dpre, dz = vjp(do_ref[...].astype(F32))
        dpre_ref[...] = dpre.astype(BF16)
        dz_ref[...] = dz

        @pl.when(pl.program_id(0) == 0)
        def _():
            db_ref[...] = jnp.zeros_like(db_ref)

        db_ref[...] += jnp.sum(dpre, axis=0, keepdims=True)

    sd = jax.ShapeDtypeStruct
    return _pcall(body, name=name, out_shape=(sd((s, 2 * bw), BF16), sd((s, bw), F32), sd((1, 2 * bw), F32)),
                  grid=(s // t,),
                  in_specs=[pl.BlockSpec((t, 2 * bw), lambda i: (i, 0)), pl.BlockSpec((t, bw), lambda i: (i, z_blk)),
                            pl.BlockSpec((t, bw), lambda i: (i, 0))],
                  out_specs=(pl.BlockSpec((t, 2 * bw), lambda i: (i, 0)), pl.BlockSpec((t, bw), lambda i: (i, 0)),
                             pl.BlockSpec((1, 2 * bw), lambda i: (0, 0))), sem=("arbitrary",))(pre, p, d_o)


def _attn_tile(q, z, kv, *, bw):
    hd = bw // MEM_HEADS
    outs = []
    for h in range(MEM_HEADS):
        k = kv[:, h * hd:(h + 1) * hd]
        v = kv[:, bw + h * hd:bw + (h + 1) * hd]
        sc = _bdot(q[:, h * hd:(h + 1) * hd], k, 1, 1) * (hd ** -0.5)
        e = jnp.exp(sc - lax.stop_gradient(jnp.max(sc, axis=-1, keepdims=True)))
        prob = e / jnp.sum(e, axis=-1, keepdims=True)
        outs.append(_bdot(prob, v, 1, 0))
    return jnp.concatenate(outs, axis=-1) * _silu(z)


def _attn_fwd(p, kv, *, bw, q_blk, z_blk, name):
    s = p.shape[0]
    m = kv.shape[0]
    t = _tile(s, 256, SUBLANES)

    def body(q_ref, z_ref, kv_ref, o_ref):
        o_ref[...] = _attn_tile(q_ref[...], z_ref[...], kv_ref[...], bw=bw).astype(BF16)

    return _pcall(body, name=name, out_shape=jax.ShapeDtypeStruct((s, bw), BF16), grid=(s // t,),
                  in_specs=[pl.BlockSpec((t, bw), lambda i: (i, q_blk)), pl.BlockSpec((t, bw), lambda i: (i, z_blk)),
                            pl.BlockSpec((m, 2 * bw), lambda i: (0, 0))],
                  out_specs=pl.BlockSpec((t, bw), lambda i: (i, 0)), sem=("parallel",))(p, p, kv)


def _attn_bwd(p, kv, d_o, *, bw, q_blk, z_blk, name):
    s = p.shape[0]
    m = kv.shape[0]
    t = _tile(s, 256, SUBLANES)

    def body(q_ref, z_ref, kv_ref, do_ref, dq_ref, dz_ref, dkv_ref):
        _, vjp = jax.vjp(functools.partial(_attn_tile, bw=bw), q_ref[...], z_ref[...], kv_ref[...])
        dq, dz, dkv = vjp(do_ref[...].astype(F32))
        dq_ref[...] = dq
        dz_ref[...] = dz

        @pl.when(pl.program_id(0) == 0)
        def _():
            dkv_ref[...] = jnp.zeros_like(dkv_ref)

        dkv_ref[...] += dkv

    sd = jax.ShapeDtypeStruct
    tok = pl.BlockSpec((t, bw), lambda i: (i, 0))
    return _pcall(body, name=name, out_shape=(sd((s, bw), F32), sd((s, bw), F32), sd((m, 2 * bw), F32)),
                  grid=(s // t,),
                  in_specs=[pl.BlockSpec((t, bw), lambda i: (i, q_blk)), pl.BlockSpec((t, bw), lambda i: (i, z_blk)),
                            pl.BlockSpec((m, 2 * bw), lambda i: (0, 0)), tok],
                  out_specs=(tok, tok, pl.BlockSpec((m, 2 * bw), lambda i: (0, 0))), sem=("arbitrary",))(p, p, kv, d_o)


def _merge_fwd(p, o4, wg, bg, wb, *, rank, g_blk, name):
    s = p.shape[0]
    _, bw, d = wb.shape
    tm, tn = _tile(s, 512), _tile(d, 512)

    def body(g_ref, o_ref, wg_ref, bg_ref, wb_ref, out_ref):
        g = g_ref[...]
        acc = jnp.zeros((tm, tn), F32)
        for n in range(N_BRANCH):
            gate = _sigmoid(_bdot(g, wg_ref[n], 1, 0) + bg_ref[n])
            acc = acc + gate * _bdot(o_ref[n], wb_ref[n], 1, 0)
        out_ref[...] = acc.astype(BF16)

    return _pcall(body, name=name, out_shape=jax.ShapeDtypeStruct((s, d), BF16), grid=(s // tm, d // tn),
                  in_specs=[pl.BlockSpec((tm, rank), lambda i, j: (i, g_blk)),
                            pl.BlockSpec((N_BRANCH, tm, bw), lambda i, j: (0, i, 0)),
                            pl.BlockSpec((N_BRANCH, rank, tn), lambda i, j: (0, 0, j)),
                            pl.BlockSpec((N_BRANCH, 1, tn), lambda i, j: (0, 0, j)),
                            pl.BlockSpec((N_BRANCH, bw, tn), lambda i, j: (0, 0, j))],
                  out_specs=pl.BlockSpec((tm, tn), lambda i, j: (i, j)), sem=("parallel", "parallel"))(
                      p, o4, wg, bg, wb)


def _merge_bwd(p, o4, wg, bg, wb, dmerged, *, rank, g_blk, name):
    s = p.shape[0]
    _, bw, d = wb.shape
    tm, tn = _tile(s, 512), _tile(d, 512)

    def body(g_ref, o_ref, wg_ref, bg_ref, wb_ref, dm_ref, dpre_ref, dbr_ref, dbg_ref):
        g = g_ref[...]
        dm = dm_ref[...].astype(F32)

        @pl.when(pl.program_id(1) == 0)
        def _():
            dbg_ref[...] = jnp.zeros_like(dbg_ref)

        for n in range(N_BRANCH):
            gate = _sigmoid(_bdot(g, wg_ref[n], 1, 0) + bg_ref[n])
            br = _bdot(o_ref[n], wb_ref[n], 1, 0)
            dpre = dm * br * gate * (1.0 - gate)
            dpre_ref[n] = dpre.astype(BF16)
            dbr_ref[n] = (dm * gate).astype(BF16)
            dbg_ref[n] += jnp.sum(dpre, axis=0, keepdims=True)

    sd = jax.ShapeDtypeStruct
    big = pl.BlockSpec((N_BRANCH, tm, tn), lambda j, i: (0, i, j))
    return _pcall(body, name=name,
                  out_shape=(sd((N_BRANCH, s, d), BF16), sd((N_BRANCH, s, d), BF16), sd((N_BRANCH, 1, d), F32)),
                  grid=(d // tn, s // tm),
                  in_specs=[pl.BlockSpec((tm, rank), lambda j, i: (i, g_blk)),
                            pl.BlockSpec((N_BRANCH, tm, bw), lambda j, i: (0, i, 0)),
                            pl.BlockSpec((N_BRANCH, rank, tn), lambda j, i: (0, 0, j)),
                            pl.BlockSpec((N_BRANCH, 1, tn), lambda j, i: (0, 0, j)),
                            pl.BlockSpec((N_BRANCH, bw, tn), lambda j, i: (0, 0, j)),
                            pl.BlockSpec((tm, tn), lambda j, i: (i, j))],
                  out_specs=(big, big, pl.BlockSpec((N_BRANCH, 1, tn), lambda j, i: (0, 0, j))),
                  sem=("parallel", "arbitrary"))(p, o4, wg, bg, wb, dmerged)


def _adamw(w, g, m, v, *, name):
    lead, r, cdim = w.shape
    tr = _tile(r, 128, SUBLANES)
    bc1 = 1.0 - ADAM_B1 ** ADAM_STEP
    bc2 = 1.0 - ADAM_B2 ** ADAM_STEP

    def body(w_ref, g_ref, m_ref, v_ref, d_ref, nm_ref, nv_ref):
        gv = g_ref[...]
        nm = ADAM_B1 * m_ref[...] + (1.0 - ADAM_B1) * gv
        nv = ADAM_B2 * v_ref[...] + (1.0 - ADAM_B2) * (gv * gv)
        d_ref[...] = -ADAM_LR * ((nm / bc1) / (jnp.sqrt(nv / bc2) + ADAM_EPS) + ADAM_WD * w_ref[...])
        nm_ref[...] = nm
        nv_ref[...] = nv

    blk = pl.BlockSpec((None, tr, cdim), lambda l, i: (l, i, 0))
    sd = jax.ShapeDtypeStruct((lead, r, cdim), F32)
    return _pcall(body, name=name, out_shape=(sd, sd, sd), grid=(lead, r // tr), in_specs=[blk] * 4,
                  out_specs=(blk,) * 3, sem=("parallel", "parallel"))(w, g, m, v)


HBM_SPEC = pl.BlockSpec(memory_space=pl.ANY)


def _place():
    x, y, c = lax.axis_index("x"), lax.axis_index("y"), lax.axis_index("c")
    return x, y, c, [(1 - x, y), (x, 1 - y), (1 - x, 1 - y)]


def _rcopy(src, dst, send_sem, recv_sem, device):
    return pltpu.make_async_remote_copy(src_ref=src, dst_ref=dst, send_sem=send_sem, recv_sem=recv_sem,
                                        device_id=device, device_id_type=MESH_ID)


def _comm_call(body, *, name, out_shape, n_in, n_sems, n_local=0):
    scratch = [pltpu.SemaphoreType.DMA((n_sems,)), pltpu.SemaphoreType.DMA((n_sems,))]
    if n_local:
        scratch.append(pltpu.SemaphoreType.DMA((n_local,)))
    multi = isinstance(out_shape, (tuple, list))
    return pl.pallas_call(body, name=name, out_shape=out_shape, in_specs=[HBM_SPEC] * n_in,
                          out_specs=tuple(HBM_SPEC for _ in out_shape) if multi else HBM_SPEC,
                          scratch_shapes=scratch, interpret=False)


def _ag4(items):
    n = len(items)

    def copies(ins, outs, send_sems, recv_sems, _, second_stage):
        x, y, c, chips = _place()
        sibling = (x, y, 1 - c)

        def part(i, px, py, h):
            half = ins[i].shape[0] // 2
            return outs[i].at[2 * px + py, pl.ds(h * half, half)]

        local = []
        first, landed, passed, arrived = [], [], [], []
        for i in range(n):
            half = ins[i].shape[0] // 2
            for j, chip in enumerate(chips):
                sems = (send_sems.at[6 * i + j], recv_sems.at[6 * i + j])
                sems2 = (send_sems.at[6 * i + 3 + j], recv_sems.at[6 * i + 3 + j])
                first.append(_rcopy(ins[i].at[pl.ds(c * half, half)], part(i, x, y, c), *sems, (*chip, c)))
                if second_stage:
                    landed.append(_rcopy(part(i, *chip, c), part(i, *chip, c), *sems, (*chip, c)))
                    passed.append(_rcopy(part(i, *chip, c), part(i, *chip, c), *sems2, sibling))
                    arrived.append(_rcopy(part(i, *chip, 1 - c), part(i, *chip, 1 - c), *sems2, sibling))
        return local, first, landed, passed, arrived

    def start(*refs):
        local, first, _, _, _ = copies(*refs, False)
        for cp in local + first:
            cp.start()

    def finish(*refs):
        local, first, landed, passed, arrived = copies(*refs, True)
        for k in range(3 * n):
            landed[k].wait_recv()
            passed[k].start()
        for cp in arrived:
            cp.wait_recv()
        for cp in first + passed:
            cp.wait_send()
        for cp in local:
            cp.wait()

    return _Comm(items, [jax.ShapeDtypeStruct((4,) + t.shape, t.dtype) for t in items], 6 * n, 0, start, finish)


def _own_slot(g, local, chip):
    return [jnp.where(chip == k, local, g[k]) for k in range(4)]


def _sib_halves(items, *, name):
    n = len(items)

    def body(*refs):
        ins, outs = refs[:n], refs[n:2 * n]
        send_sems, recv_sems = refs[2 * n:]
        x, y, c, _ = _place()
        cps = []
        for i in range(n):
            half = ins[i].shape[1] // 2
            cps.append(_rcopy(ins[i].at[:, pl.ds((1 - c) * half, half)], outs[i], send_sems.at[i], recv_sems.at[i],
                              (x, y, 1 - c)))
        for cp in cps:
            cp.start()
        for cp in cps:
            cp.wait()

    shapes = tuple(jax.ShapeDtypeStruct((4, t.shape[1] // 2) + t.shape[2:], t.dtype) for t in items)
    return _comm_call(body, name=name, out_shape=shapes, n_in=n, n_sems=n)(*items)


def _to_chips(items):
    n = len(items)

    def copies(ins, outs, send_sems, recv_sems, _):
        x, y, c, chips = _place()
        return [_rcopy(ins[i].at[2 * chip[0] + chip[1]], outs[i].at[j], send_sems.at[3 * i + j],
                       recv_sems.at[3 * i + j], (*chip, c)) for i in range(n) for j, chip in enumerate(chips)]

    def start(*refs):
        for cp in copies(*refs):
            cp.start()

    def finish(*refs):
        for cp in copies(*refs):
            cp.wait()

    return _Comm(items, [jax.ShapeDtypeStruct((3,) + t.shape[1:], t.dtype) for t in items], 3 * n, 0, start, finish)


def _join_halves(items, c, *, name):
    n = len(items)

    def body(*refs):
        ins, outs = refs[:n], refs[n:2 * n]
        send_sems, recv_sems = refs[2 * n:]
        x, y, cc, _ = _place()
        sibling = (x, y, 1 - cc)
        cps = []
        for i in range(n):
            a = ins[i].shape[0]
            cps.append(_rcopy(ins[i], outs[i].at[pl.ds(cc * a, a)], send_sems.at[i], recv_sems.at[i], sibling))
        for cp in cps:
            cp.start()
        for i in range(n):
            a = ins[i].shape[0]
            cps[i].wait_send()
            _rcopy(ins[i], outs[i].at[pl.ds((1 - cc) * a, a)], send_sems.at[i], recv_sems.at[i], sibling).wait_recv()

    shapes = tuple(jax.ShapeDtypeStruct((2 * t.shape[0],) + t.shape[1:], t.dtype) for t in items)
    got = _comm_call(body, name=name, out_shape=shapes, n_in=n, n_sems=n)(*items)
    out = []
    for q, g in zip(items, got):
        a = q.shape[0]
        out.append(jnp.concatenate([jnp.where(c == h, q, g[h * a:(h + 1) * a]) for h in range(2)], axis=0))
    return out


def _swap_sibling(buf, *, name):
    def body(b_ref, recv_ref, send_sems, recv_sems):
        x, y, c, _ = _place()
        cp = _rcopy(b_ref, recv_ref, send_sems.at[0], recv_sems.at[0], (x, y, 1 - c))
        cp.start()
        cp.wait()

    return _comm_call(body, name=name, out_shape=jax.ShapeDtypeStruct(buf.shape, buf.dtype), n_in=1, n_sems=1)(buf)


def _gather_chips(buf, *, name):
    n, cdim = buf.shape

    def body(b_ref, out_ref, send_sems, recv_sems, local_sems):
        x, y, c, chips = _place()
        mine = pltpu.make_async_copy(b_ref, out_ref.at[2 * x + y], local_sems.at[0])
        mine.start()
        cps = [_rcopy(b_ref, out_ref.at[2 * x + y], send_sems.at[j], recv_sems.at[j], (*chip, c))
               for j, chip in enumerate(chips)]
        for cp in cps:
            cp.start()
        for j, chip in enumerate(chips):
            slot = out_ref.at[2 * chip[0] + chip[1]]
            _rcopy(slot, slot, send_sems.at[j], recv_sems.at[j], (*chip, c)).wait_recv()
        for cp in cps:
            cp.wait_send()
        mine.wait()

    return _comm_call(body, name=name, out_shape=jax.ShapeDtypeStruct((4, n, cdim), buf.dtype), n_in=1, n_sems=3,
                      n_local=1)(buf)


def _rows_per_block(rows, cdim, itemsize, align):
    return _tile(rows, max(align, ELEMENTWISE_BLOCK_BYTES // (cdim * itemsize) // align * align), align)


def _sum_sib(g4, recv, c_idx, *, name):
    _, rows, cdim = g4.shape
    half = rows // 2
    tr = _rows_per_block(half, cdim, 2, 16)
    nh = half // tr

    def body(c_ref, g_ref, r_ref, o_ref):
        o_ref[...] = (g_ref[...].astype(F32) + r_ref[...].astype(F32)).astype(o_ref.dtype)

    return _pcall(body, name=name, out_shape=jax.ShapeDtypeStruct((4, half, cdim), g4.dtype), grid=(4, nh),
                  num_prefetch=1,
                  in_specs=[pl.BlockSpec((None, tr, cdim), lambda k, i, c_ref: (k, c_ref[0] * nh + i, 0)),
                            pl.BlockSpec((None, tr, cdim), lambda k, i, c_ref: (k, i, 0))],
                  out_specs=pl.BlockSpec((None, tr, cdim), lambda k, i, c_ref: (k, i, 0)),
                  sem=("parallel", "parallel"))(c_idx, g4, recv)


def _sum_chips(h4, recv3, chip_idx, *, name):
    _, n, cdim = h4.shape
    tr = _rows_per_block(n, cdim, 4, 16)

    def body(k_ref, h_ref, r_ref, o_ref):
        acc = h_ref[...].astype(F32)
        for j in range(3):
            acc = acc + r_ref[j].astype(F32)
        o_ref[...] = acc

    return _pcall(body, name=name, out_shape=jax.ShapeDtypeStruct((n, cdim), F32), grid=(n // tr,), num_prefetch=1,
                  in_specs=[pl.BlockSpec((None, tr, cdim), lambda i, k_ref: (k_ref[0], i, 0)),
                            pl.BlockSpec((3, tr, cdim), lambda i, k_ref: (0, i, 0))],
                  out_specs=pl.BlockSpec((tr, cdim), lambda i, k_ref: (i, 0)), sem=("parallel",))(chip_idx, h4, recv3)


def _add2(a, b, *, name):
    n, cdim = a.shape
    tr = _tile(n, 256, SUBLANES)

    def body(a_ref, b_ref, o_ref):
        o_ref[...] = a_ref[...] + b_ref[...]

    blk = pl.BlockSpec((tr, cdim), lambda i: (i, 0))
    return _pcall(body, name=name, out_shape=jax.ShapeDtypeStruct((n, cdim), F32), grid=(n // tr,), in_specs=[blk, blk],
                  out_specs=blk, sem=("parallel",))(a, b)


def _sum4(x4, *, name):
    _, n, cdim = x4.shape
    tr = _tile(n, 256, SUBLANES)

    def body(x_ref, o_ref):
        o_ref[...] = ((x_ref[0] + x_ref[1]) + x_ref[2]) + x_ref[3]

    return _pcall(body, name=name, out_shape=jax.ShapeDtypeStruct((n, cdim), F32), grid=(n // tr,),
                  in_specs=[pl.BlockSpec((4, tr, cdim), lambda i: (0, i, 0))],
                  out_specs=pl.BlockSpec((tr, cdim), lambda i: (i, 0)), sem=("parallel",))(x4)


def _rs_pair(items, c_idx, *, name):
    recv = _sib_halves(items, name=name + "_sib")
    return [_sum_sib(items[i], recv[i], c_idx, name=f"{name}_sum1_{i}") for i in range(len(items))]


def _rs_close(h4, recv3, c_idx, chip_idx, *, name):
    q = [_sum_chips(h4[i], recv3[i], chip_idx, name=f"{name}_sum2_{i}") for i in range(len(h4))]
    return _join_halves(q, c_idx[0], name=name + "_join")


def _all_reduce(buf, *, name):
    pair = _add2(buf, _swap_sibling(buf, name=name + "_sib"), name=name + "_add")
    return _sum4(_gather_chips(pair, name=name + "_ici"), name=name + "_sum")


def _pack_rows(flat, lead, align):
    n = flat.shape[-1]
    unit = PACK_COLS * align
    total = -(-n // unit) * unit
    flat = jnp.pad(flat, [(0, 0)] * len(lead) + [(0, total - n)])
    return flat.reshape(*lead, total // PACK_COLS, PACK_COLS)


BIG = (("w_in", 1), ("ssm_w_glu", 1), ("w_kv", 0), ("w_gate", 2), ("w_branch", 2), ("w_out", 0))
SMALL_SHARDED = (("dn_conv_w", 1), ("lru_conv_w", 1), ("b_gate", 1))
SMALL = ("norm_w", "dn_a_log", "dn_dt_bias", "dn_norm_w", "lru_conv_b", "lru_w_r", "lru_b_r", "lru_w_i", "lru_b_i",
         "lru_lambda", "ssm_log_dt", "ssm_a_re", "ssm_a_im", "ssm_b_re", "ssm_b_im", "ssm_c_re", "ssm_c_im", "ssm_d",
         "ssm_b_glu", "mem_norm_w")
WEIGHTS = ("norm_w", "w_in", "dn_conv_w", "dn_a_log", "dn_dt_bias", "dn_norm_w", "lru_conv_w", "lru_conv_b",
           "lru_w_r", "lru_b_r", "lru_w_i", "lru_b_i", "lru_lambda", "ssm_log_dt", "ssm_a_re", "ssm_a_im", "ssm_b_re",
           "ssm_b_im", "ssm_c_re", "ssm_c_im", "ssm_d", "ssm_w_glu", "ssm_b_glu", "mem_norm_w", "w_kv", "w_gate",
           "b_gate", "w_branch", "w_out", "final_norm_w")


REST = BIG[1:]


def _gather_rest(wts, l):
    small = _pack_rows(jnp.concatenate([wts[n][l].reshape(-1) for n, _ in SMALL_SHARDED]), (), 2 * SUBLANES)
    return _ag4([wts[n][l].astype(BF16) for n, _ in REST] + [small])


def _full_rest(g, local, wts, chip):
    out = {n: jnp.concatenate(_own_slot(g[i], local[i], chip), axis=ax) for i, (n, ax) in enumerate(REST)}
    flat, off = jnp.stack(_own_slot(g[-1], local[-1], chip), axis=0).reshape(4, -1), 0
    for n, ax in SMALL_SHARDED:
        shp = wts[n].shape[1:]
        sz = math.prod(shp)
        out[n] = jnp.concatenate(list(flat[:, off:off + sz].reshape(4, *shp)), axis=ax)
        off += sz
    return out


def _chip_rows(t):
    return t.reshape(4, t.shape[0] // 4, t.shape[1])


def _w_in_layout(w, bw, heads, rank):
    d = w.shape[0]
    ba = 4 * bw
    rest = ba + 2 * heads
    return jnp.concatenate([w[:, :ba], w[:, rest:], w[:, ba:rest], jnp.zeros((d, BA_PAD - 2 * heads), w.dtype)], axis=1)


def _w_in_unlayout(dw, bw, heads, rank):
    ba = 4 * bw
    tail = 10 * bw + rank
    return jnp.concatenate([dw[:, :ba], dw[:, tail:tail + 2 * heads], dw[:, ba:tail]], axis=1)


def _lru_dense(w):
    nb, blk, _ = w.shape
    return jnp.einsum("nij,nm->nimj", w, jnp.eye(nb, dtype=w.dtype)).reshape(nb * blk, nb * blk)


def _w8(w):
    return jnp.concatenate([w, jnp.zeros((SUBLANES - CONV_WIDTH, w.shape[1]), w.dtype)], axis=0)


def _layer_fwd(x, mem, w_in_slots, wts, prm, l, chip, dn_comm):
    s, d = x.shape
    bw = d // N_BRANCH
    heads = bw // DN_HEAD_DIM
    rank = wts["w_gate"].shape[2]
    tag = f"l{l}_"
    sv = {"x": x}
    w_in = _w_in_layout(jnp.concatenate(w_in_slots, axis=1), bw, heads, rank)
    sv["w_in"] = w_in
    h = _rms_fwd(x, prm["norm_w"], name=tag + "norm")
    gather = _gather_rest(wts, l)
    p, rest = _mm(h, w_in, comm=gather, name=tag + "in_proj")
    full = _full_rest(rest, gather.inputs, wts, chip)
    sv["h"], sv["p"] = h, p
    conv_a = _conv_fwd(p, 0, 3 * bw, _w8(full["dn_conv_w"]), jnp.zeros((1, 3 * bw), F32), name=tag + "dn_conv")
    ba_blk = (10 * bw + rank) // BA_PAD
    dn = _dn_fwd(p, conv_a, prm["dn_a_log"], prm["dn_dt_bias"], prm["dn_norm_w"], bw=bw, ba_blk=ba_blk,
                 comm=dn_comm, name=tag + "dn")
    (o_a, dn_states), dn_res = dn if dn_comm is not None else (dn, None)
    sv["conv_a"], sv["dn_states"] = conv_a, dn_states
    xc = _conv_fwd(p, 4, bw, _w8(full["lru_conv_w"]), prm["lru_conv_b"].reshape(1, bw), name=tag + "lru_conv")
    (wr, wi), lru_vjp = jax.vjp(lambda a, b: (_lru_dense(a), _lru_dense(b)), prm["lru_w_r"], prm["lru_w_i"])
    row = lambda v: v.reshape(1, bw)
    lru_args = (wr, row(prm["lru_b_r"]), wi, row(prm["lru_b_i"]), row(prm["lru_lambda"]))
    o_b, lru_saves = _lru_fwd(p, xc, *lru_args, bw=bw, z_blk=5, name=tag + "lru")
    sv["xc"], sv["lru_saves"], sv["lru_args"], sv["lru_vjp"] = xc, lru_saves, lru_args, lru_vjp
    prep, s5_vjp = jax.vjp(_s5_prep, prm["ssm_log_dt"], prm["ssm_a_re"], prm["ssm_a_im"], prm["ssm_b_re"],
                           prm["ssm_b_im"], prm["ssm_c_re"], prm["ssm_c_im"], prm["ssm_d"])
    u_blk0 = 6 * bw // LANES
    yg, s5_saves = _s5_fwd(p, prep, bw=bw, u_blk0=u_blk0, name=tag + "s5")
    pre = _mm(yg, full["ssm_w_glu"], bias=prm["ssm_b_glu"].reshape(1, 2 * bw), name=tag + "glu_proj")
    o_c = _glu_fwd(pre, p, bw=bw, z_blk=7, name=tag + "glu")
    sv["prep"], sv["s5_vjp"], sv["s5_saves"], sv["yg"], sv["pre"] = prep, s5_vjp, s5_saves, yg, pre
    mem_n = _rms_fwd(mem, prm["mem_norm_w"], name=tag + "mem_norm")
    kv = _mm(mem_n, full["w_kv"], name=tag + "kv_proj")
    o_d = _attn_fwd(p, kv, bw=bw, q_blk=8, z_blk=9, name=tag + "attn")
    sv["mem_n"], sv["kv"] = mem_n, kv
    o4 = jnp.stack([o_a, o_b, o_c, o_d], axis=0)
    bg = full["b_gate"].reshape(N_BRANCH, 1, d)
    g_blk = 10 * bw // rank
    merged = _merge_fwd(p, o4, full["w_gate"], bg, full["w_branch"], rank=rank, g_blk=g_blk, name=tag + "merge")
    sv["o4"], sv["bg"], sv["merged"] = o4, bg, merged
    return _mm(merged, full["w_out"], add=x, name=tag + "out_proj"), sv, full, dn_res


def _layer_bwd(dx_out, mem, sv, full, prm, l, dn_comm_of):
    x, p, h = sv["x"], sv["p"], sv["h"]
    s, d = x.shape
    bw = d // N_BRANCH
    heads = bw // DN_HEAD_DIM
    rank = full["w_gate"].shape[1]
    tag = f"l{l}b_"
    big, small = {}, {}
    dmerged = _mm(dx_out, full["w_out"], tb=True, out_dtype=BF16, name=tag + "out_dx")
    big["w_out"] = _mm(sv["merged"], dx_out, ta=True, out_dtype=BF16, name=tag + "out_dw")
    g_blk = 10 * bw // rank
    dpre, dbr, dbg = _merge_bwd(p, sv["o4"], full["w_gate"], sv["bg"], full["w_branch"], dmerged, rank=rank,
                                g_blk=g_blk, name=tag + "merge")
    small["b_gate"] = dbg.reshape(N_BRANCH, d)
    glow = p[:, 10 * bw:10 * bw + rank].astype(BF16)
    dglow = None
    dwg, dwb, d_o = [], [], []
    for n in range(N_BRANCH):
        dglow = _mm(dpre, full["w_gate"], la=n, lb=n, tb=True, add=dglow, name=tag + f"gate_dx{n}")
        dwg.append(_mm(glow, dpre, ta=True, lb=n, out_dtype=BF16, nsplit=4, name=tag + f"gate_dw{n}"))
        d_o.append(_mm(dbr, full["w_branch"], la=n, lb=n, tb=True, name=tag + f"branch_dx{n}"))
        dwb.append(_mm(sv["o4"], dbr, ta=True, la=n, lb=n, out_dtype=BF16, nsplit=4, name=tag + f"branch_dw{n}"))
    dn_comm = dn_comm_of([_chip_rows(big["w_out"]), *dwg, *dwb])
    ba_blk = (10 * bw + rank) // BA_PAD
    dn = _dn_bwd(p, sv["conv_a"], sv["dn_states"], d_o[0], prm["dn_a_log"], prm["dn_dt_bias"], prm["dn_norm_w"],
                 bw=bw, ba_blk=ba_blk, comm=dn_comm, name=tag + "dn")
    (dconv, dz_a, dba, dal, ddt, dnw), dn_res = dn if dn_comm is not None else (dn, None)
    small["dn_a_log"] = dal[0, heads:2 * heads]
    small["dn_dt_bias"] = ddt[0, heads:2 * heads]
    small["dn_norm_w"] = dnw[0]
    dqkv, dw8_a, _ = _conv_bwd(p, 0, 3 * bw, _w8(full["dn_conv_w"]), dconv, name=tag + "dn_conv")
    small["dn_conv_w"] = dw8_a[:CONV_WIDTH]
    dxc, dz_b, dwr, dwi, dbr_, dbi_, dlam = _lru_bwd(p, sv["xc"], sv["lru_saves"], d_o[1], *sv["lru_args"], bw=bw,
                                                     z_blk=5, name=tag + "lru")
    small["lru_w_r"], small["lru_w_i"] = sv["lru_vjp"]((dwr, dwi))
    small["lru_b_r"], small["lru_b_i"], small["lru_lambda"] = dbr_[0], dbi_[0], dlam[0]
    dlx, dw8_b, dcb = _conv_bwd(p, 4, bw, _w8(full["lru_conv_w"]), dxc, name=tag + "lru_conv")
    small["lru_conv_w"] = dw8_b[:CONV_WIDTH]
    small["lru_conv_b"] = dcb[0]
    dpre_glu, dz_c, dbglu = _glu_bwd(sv["pre"], p, d_o[2], bw=bw, z_blk=7, name=tag + "glu")
    small["ssm_b_glu"] = dbglu[0]
    dyg = _mm(dpre_glu, full["ssm_w_glu"], tb=True, name=tag + "glu_dx")
    big["ssm_w_glu"] = _mm(sv["yg"], dpre_glu, ta=True, out_dtype=BF16, nsplit=4, name=tag + "glu_dw")
    s5 = _s5_bwd(p, sv["prep"], sv["s5_saves"], dyg, bw=bw, u_blk0=6 * bw // LANES, name=tag + "s5")
    du = s5[0]
    (small["ssm_log_dt"], small["ssm_a_re"], small["ssm_a_im"], small["ssm_b_re"], small["ssm_b_im"],
     small["ssm_c_re"], small["ssm_c_im"], small["ssm_d"]) = sv["s5_vjp"](tuple(s5[1:]))
    dq, dz_d, dkv = _attn_bwd(p, sv["kv"], d_o[3], bw=bw, q_blk=8, z_blk=9, name=tag + "attn")
    big["w_kv"] = _mm(sv["mem_n"], dkv, ta=True, out_dtype=BF16, name=tag + "kv_dw")
    dmem_n = _mm(dkv, full["w_kv"], tb=True, name=tag + "kv_dx")
    _, dmnw = _rms_bwd(mem, prm["mem_norm_w"], dmem_n, None, name=tag + "mem_norm")
    small["mem_norm_w"] = dmnw[0]
    dp = jnp.concatenate([dqkv, dz_a, dlx, dz_b, du, dz_c, dq, dz_d, dglow, dba], axis=1).astype(BF16)
    dh = _mm(dp, sv["w_in"], tb=True, name=tag + "in_dx")
    dw_in = _w_in_unlayout(_mm(h, dp, ta=True, out_dtype=BF16, name=tag + "in_dw"), bw, heads, rank)
    dx, dnw_in = _rms_bwd(x, prm["norm_w"], dh, dx_out, name=tag + "norm")
    small["norm_w"] = dnw_in[0]
    late = [dw_in.reshape(d, 4, dw_in.shape[1] // 4).transpose(1, 0, 2), big["ssm_w_glu"], _chip_rows(big["w_kv"])]
    return dx, late, small, dn_res


def _step(wts, mom, vel, x, mem, target):
    depth = wts["norm_w"].shape[0]
    xi, yi, ci = lax.axis_index("x"), lax.axis_index("y"), lax.axis_index("c")
    c_idx = ci.astype(jnp.int32).reshape(1)
    chip = (2 * xi + yi).astype(jnp.int32)
    chip_idx = chip.reshape(1)
    x, mem, target = x[0], mem[0], target[0]

    prms = [{n: wts[n][l] for n in SMALL} for l in range(depth)]
    w_in_local = [wts["w_in"][l].astype(BF16) for l in range(depth)]
    w_in_g = _run_comm(_ag4([w_in_local[0]]), name="gather_w_in0")[0]
    saves, fulls = [], []
    act = x
    for l in range(depth):
        nxt = _ag4([w_in_local[l + 1]]) if l + 1 < depth else None
        act, sv, full, res = _layer_fwd(act, mem, _own_slot(w_in_g, w_in_local[l], chip), wts, prms[l], l, chip, nxt)
        saves.append(sv)
        fulls.append(full)
        w_in_g = res[0] if nxt is not None else None
    loss_part, dx, dfw = _loss_head(act, wts["final_norm_w"], target, name="loss_head")
    loss = lax.psum(loss_part[0, 0], ("x", "y", "c"))

    big_g = [None] * depth
    small_g = [None] * depth
    pending = []
    done = {}
    for l in reversed(range(depth)):
        def dn_comm_of(early, l=l):
            pending.append((l, "early", _rs_pair(early, c_idx, name=f"scatter_g{l}a")))
            return _to_chips([t for _, _, h4 in pending for t in h4])

        dx, late, small_g[l], res = _layer_bwd(dx, mem, saves[l], fulls[l], prms[l], l, dn_comm_of)
        off = 0
        for ll, grp, h4 in pending:
            done[ll, grp] = _rs_close(h4, res[off:off + len(h4)], c_idx, chip_idx, name=f"scatter_g{ll}{grp[0]}c")
            off += len(h4)
        pending = [(l, "late", _rs_pair(late, c_idx, name=f"scatter_g{l}b"))]
    (l, grp, h4), = pending
    done[l, grp] = _rs_close(h4, _run_comm(_to_chips(h4), name=f"scatter_g{l}b_ici"), c_idx, chip_idx,
                             name=f"scatter_g{l}lc")
    for l in range(depth):
        e, t = done[l, "early"], done[l, "late"]
        big_g[l] = {"w_in": t[0], "ssm_w_glu": t[1], "w_kv": t[2], "w_out": e[0],
                    "w_gate": jnp.stack(e[1:1 + N_BRANCH], axis=0),
                    "w_branch": jnp.stack(e[1 + N_BRANCH:1 + 2 * N_BRANCH], axis=0)}

    names = SMALL + tuple(n for n, _ in SMALL_SHARDED)
    flat = jnp.concatenate([small_g[l][n].reshape(-1) for l in range(depth) for n in names] + [dfw.reshape(-1)])
    red = _all_reduce(_pack_rows(flat, (), 256), name="reduce_small").reshape(-1)
    grads, off = {n: [] for n in names}, 0
    for l in range(depth):
        for n in names:
            shp = small_g[l][n].shape
            sz = math.prod(shp)
            grads[n].append(red[off:off + sz].reshape(shp))
            off += sz
    grads = {n: jnp.stack(v, axis=0) for n, v in grads.items()}
    grads["final_norm_w"] = red[off:off + dfw.size].reshape(wts["final_norm_w"].shape)
    for n, ax in SMALL_SHARDED:
        width = wts[n].shape[-1]
        grads[n] = lax.dynamic_slice_in_dim(grads[n], chip * width, width, axis=ax + 1)
    for n, _ in BIG:
        grads[n] = jnp.stack([big_g[l][n] for l in range(depth)], axis=0)

    delta, new_m, new_v = {}, {}, {}
    for n, _ in BIG:
        shp = wts[n].shape
        three = lambda t: t.reshape(-1, *shp[-2:])
        dlt, nm, nv = _adamw(three(wts[n]), three(grads[n]), three(mom[n]), three(vel[n]), name="adamw_" + n)
        delta[n], new_m[n], new_v[n] = dlt.reshape(shp), nm.reshape(shp), nv.reshape(shp)
    rest = [n for n in WEIGHTS if n not in dict(BIG)]
    cat = lambda src: _pack_rows(jnp.concatenate([src[n].reshape(-1) for n in rest])[None], (1,), SUBLANES)
    dlt, nm, nv = _adamw(cat(wts), cat(grads), cat(mom), cat(vel), name="adamw_small")
    off = 0
    for n in rest:
        shp = wts[n].shape
        sz = math.prod(shp)
        for dst, src in ((delta, dlt), (new_m, nm), (new_v, nv)):
            dst[n] = src.reshape(-1)[off:off + sz].reshape(shp)
        off += sz
    return (loss, dx[None], *[grads[n] for n in WEIGHTS], *[delta[n] for n in WEIGHTS], *[new_m[n] for n in WEIGHTS],
            *[new_v[n] for n in WEIGHTS])


def kernel(x, mem, norm_w, w_in, dn_conv_w, dn_a_log, dn_dt_bias, dn_norm_w, lru_conv_w, lru_conv_b, lru_w_r, lru_b_r, lru_w_i, lru_b_i, lru_lambda, ssm_log_dt, ssm_a_re, ssm_a_im, ssm_b_re, ssm_b_im, ssm_c_re, ssm_c_im, ssm_d, ssm_w_glu, ssm_b_glu, mem_norm_w, w_kv, w_gate, b_gate, w_branch, w_out, final_norm_w, loss_target, m_norm_w, m_w_in, m_dn_conv_w, m_dn_a_log, m_dn_dt_bias, m_dn_norm_w, m_lru_conv_w, m_lru_conv_b, m_lru_w_r, m_lru_b_r, m_lru_w_i, m_lru_b_i, m_lru_lambda, m_ssm_log_dt, m_ssm_a_re, m_ssm_a_im, m_ssm_b_re, m_ssm_b_im, m_ssm_c_re, m_ssm_c_im, m_ssm_d, m_ssm_w_glu, m_ssm_b_glu, m_mem_norm_w, m_w_kv, m_w_gate, m_b_gate, m_w_branch, m_w_out, m_final_norm_w, v_norm_w, v_w_in, v_dn_conv_w, v_dn_a_log, v_dn_dt_bias, v_dn_norm_w, v_lru_conv_w, v_lru_conv_b, v_lru_w_r, v_lru_b_r, v_lru_w_i, v_lru_b_i, v_lru_lambda, v_ssm_log_dt, v_ssm_a_re, v_ssm_a_im, v_ssm_b_re, v_ssm_b_im, v_ssm_c_re, v_ssm_c_im, v_ssm_d, v_ssm_w_glu, v_ssm_b_glu, v_mem_norm_w, v_w_kv, v_w_gate, v_b_gate, v_w_branch, v_w_out, v_final_norm_w):
    given = dict(locals())
    wts = {n: given[n] for n in WEIGHTS}
    mom = {n: given["m_" + n] for n in WEIGHTS}
    vel = {n: given["v_" + n] for n in WEIGHTS}
    return _step(wts, mom, vel, x, mem, loss_target)
```

```python
import functools
import math

import jax
import jax.numpy as jnp
import numpy as np
from jax import lax
from jax.experimental import pallas as pl
from jax.experimental.pallas import tpu as pltpu

F32 = jnp.float32
BF16 = jnp.bfloat16
HIGHEST = lax.Precision.HIGHEST
MESH_ID = pl.DeviceIdType.MESH

NORM_EPS = 1e-6
CONV_WIDTH = 4
DN_HEAD_DIM = 128
DN_CHUNK = 64
LRU_C = 8.0
MEM_HEADS = 4
N_BRANCH = 4
ADAM_LR, ADAM_B1, ADAM_B2, ADAM_EPS, ADAM_WD, ADAM_STEP = 0.001, 0.9, 0.999, 1e-08, 0.01, 10

LANES = 128
SUBLANES = 8
VMEM_LIMIT = 56 * 2 ** 20
PACK_COLS = 1024
ELEMENTWISE_BLOCK_BYTES = 2 * 2 ** 20
BA_PAD = 256


def _tile(n, pref, align=LANES):
    if n <= pref:
        return n
    t = pref - pref % align
    while t > 0:
        if n % t == 0:
            return t
        t -= align
    return n


class _Comm:
    def __init__(self, inputs, out_shapes, n_sems, n_local, start, finish):
        self.inputs, self.out_shapes, self.n_sems, self.n_local = list(inputs), tuple(out_shapes), n_sems, n_local
        self.start, self.finish = start, finish

    def scratch(self):
        s = [pltpu.SemaphoreType.DMA((self.n_sems,)), pltpu.SemaphoreType.DMA((self.n_sems,))]
        return s + ([pltpu.SemaphoreType.DMA((self.n_local,))] if self.n_local else [])

    def split(self, refs):
        ni, no = len(self.inputs), len(self.out_shapes)
        sems = list(refs[ni + no:]) + ([] if self.n_local else [None])
        return (refs[:ni], refs[ni:ni + no], *sems)


def _pcall(body, *, name, out_shape, grid=(), in_specs=None, out_specs=None, scratch=(), sem=None,
           num_prefetch=0, comm=None):
    params = dict(vmem_limit_bytes=VMEM_LIMIT)
    if sem is not None:
        params["dimension_semantics"] = sem if comm is None else ("arbitrary",) * len(grid)
    scratch = list(scratch)
    if comm is None:
        run_body = body
    else:
        assert not num_prefetch
        single = not isinstance(out_shape, (tuple, list))
        outs = (out_shape,) if single else tuple(out_shape)
        ospecs = (out_specs,) if single else tuple(out_specs)
        n_in, n_out, n_scr = len(in_specs), len(outs), len(scratch)
        n_ci, n_co = len(comm.inputs), len(comm.out_shapes)
        in_specs = list(in_specs) + [HBM_SPEC] * n_ci
        out_shape = outs + comm.out_shapes
        out_specs = ospecs + (HBM_SPEC,) * n_co
        scratch = scratch + comm.scratch()

        def run_body(*refs):
            ins, rest = refs[:n_in], refs[n_in:]
            cins, rest = rest[:n_ci], rest[n_ci:]
            o, rest = rest[:n_out], rest[n_out:]
            couts, rest = rest[:n_co], rest[n_co:]
            cargs = comm.split((*cins, *couts, *rest[n_scr:]))
            first = functools.reduce(jnp.logical_and, [pl.program_id(a) == 0 for a in range(len(grid))])
            last = functools.reduce(jnp.logical_and, [pl.program_id(a) == grid[a] - 1 for a in range(len(grid))])

            @pl.when(first)
            def _():
                comm.start(*cargs)

            body(*ins, *o, *rest[:n_scr])

            @pl.when(last)
            def _():
                comm.finish(*cargs)

    if num_prefetch:
        call = pl.pallas_call(
            run_body, name=name, out_shape=out_shape,
            grid_spec=pltpu.PrefetchScalarGridSpec(num_scalar_prefetch=num_prefetch, grid=grid, in_specs=in_specs,
                                                   out_specs=out_specs, scratch_shapes=scratch),
            compiler_params=pltpu.CompilerParams(**params), interpret=False)
    else:
        call = pl.pallas_call(run_body, name=name, out_shape=out_shape, grid=grid, in_specs=in_specs,
                              out_specs=out_specs, scratch_shapes=scratch,
                              compiler_params=pltpu.CompilerParams(**params), interpret=False)
    if comm is None:
        return call

    def run(*operands):
        res = call(*operands, *comm.inputs)
        return (res[0] if single else tuple(res[:n_out])), tuple(res[n_out:])

    return run


def _run_comm(comm, *, name):
    def body(*refs):
        args = comm.split(refs)
        comm.start(*args)
        comm.finish(*args)

    return pl.pallas_call(body, name=name, out_shape=comm.out_shapes, in_specs=[HBM_SPEC] * len(comm.inputs),
                          out_specs=tuple(HBM_SPEC for _ in comm.out_shapes), scratch_shapes=comm.scratch(),
                          interpret=False)(*comm.inputs)


HBM_SPEC = pl.BlockSpec(memory_space=pl.ANY)


@functools.partial(jax.custom_vjp, nondiff_argnums=(2, 3))
def _bdot(a, b, ca, cb):
    return lax.dot_general(a.astype(BF16), b.astype(BF16), (((ca,), (cb,)), ((), ())), preferred_element_type=F32)


def _bdot_fwd(a, b, ca, cb):
    return _bdot(a, b, ca, cb), (a, b)


def _bdot_bwd(ca, cb, res, ct):
    a, b = res
    da = _bdot(ct, b, 1, 1 - cb) if ca == 1 else _bdot(b, ct, 1 - cb, 1)
    db = _bdot(a, ct, 1 - ca, 0) if cb == 0 else _bdot(ct, a, 0, 1 - ca)
    return da.astype(a.dtype), db.astype(b.dtype)


_bdot.defvjp(_bdot_fwd, _bdot_bwd)


def _split_bf16(a):
    hi = a.astype(BF16)
    return hi, (a - hi.astype(F32)).astype(BF16)


@functools.partial(jax.custom_vjp, nondiff_argnums=(2, 3))
def _xdot(a, b, ca, cb):
    dims = (((ca,), (cb,)), ((), ()))
    ah, al = _split_bf16(a)
    bh, bl = _split_bf16(b)
    dot = lambda p, q: lax.dot_general(p, q, dims, preferred_element_type=F32)
    return dot(ah, bh) + (dot(ah, bl) + dot(al, bh))


def _xdot_fwd(a, b, ca, cb):
    return _xdot(a, b, ca, cb), (a, b)


def _xdot_bwd(ca, cb, res, ct):
    a, b = res
    da = _xdot(ct, b, 1, 1 - cb) if ca == 1 else _xdot(b, ct, 1 - cb, 1)
    db = _xdot(a, ct, 1 - ca, 0) if cb == 0 else _xdot(ct, a, 0, 1 - ca)
    return da, db


_xdot.defvjp(_xdot_fwd, _xdot_bwd)


def _sigmoid(x):
    return 1.0 / (1.0 + jnp.exp(-x))


def _silu(x):
    return x * _sigmoid(x)


def _softplus(x):
    return jnp.maximum(x, 0.0) + jnp.log(1.0 + jnp.exp(-jnp.abs(x)))


def _expm1(x):
    small = x * (1.0 + x * (0.5 + x * (1.0 / 6.0 + x * (1.0 / 24.0 + x * (1.0 / 120.0 + x * (1.0 / 720.0))))))
    return jnp.where(jnp.abs(x) < 0.1, small, jnp.exp(x) - 1.0)


def _gelu(x):
    return 0.5 * x * (1.0 + jnp.tanh(math.sqrt(2.0 / math.pi) * (x + 0.044715 * x * x * x)))


def _rms(x, w):
    var = jnp.mean(x * x, axis=-1, keepdims=True)
    return x * lax.rsqrt(var + NORM_EPS) * w


def _pick_lane(v, idx):
    lane = lax.broadcasted_iota(jnp.int32, v.shape, 1)
    return jnp.sum(jnp.where(lane == idx, v, 0.0), axis=1, keepdims=True)


def _pick_row(v, idx):
    row = lax.broadcasted_iota(jnp.int32, v.shape, 0)
    return jnp.sum(jnp.where(row == idx, v, 0.0), axis=0, keepdims=True)


def _mm(a, b, *, name, ta=False, tb=False, out_dtype=F32, add=None, bias=None, la=None, lb=None, nsplit=None,
        comm=None, tm=1024, tn=1024, tk=2048):
    a2 = a.shape[-2:]
    b2 = b.shape[-2:]
    m, k = (a2[1], a2[0]) if ta else a2
    n = b2[0] if tb else b2[1]
    assert (b2[1] if tb else b2[0]) == k
    tm, tn, tk = _tile(m, tm), _tile(n // (nsplit or 1), tn), _tile(k, tk)
    nk = k // tk

    def a_map(i, j, kk):
        idx = (kk, i) if ta else (i, kk)
        return idx if la is None else (la,) + idx

    def b_map(i, j, kk):
        idx = (j, kk) if tb else (kk, j)
        return idx if lb is None else (lb,) + idx

    a_blk = (tk, tm) if ta else (tm, tk)
    b_blk = (tn, tk) if tb else (tk, tn)
    in_specs = [pl.BlockSpec(a_blk if la is None else (None,) + a_blk, a_map),
                pl.BlockSpec(b_blk if lb is None else (None,) + b_blk, b_map)]
    operands = [a, b]
    if add is not None:
        in_specs.append(pl.BlockSpec((tm, tn), lambda i, j, kk: (i, j)))
        operands.append(add)
    if bias is not None:
        in_specs.append(pl.BlockSpec((1, tn), lambda i, j, kk: (0, j)))
        operands.append(bias)
    dims = (((0 if ta else 1,), (1 if tb else 0,)), ((), ()))

    def body(*refs):
        a_ref, b_ref = refs[0], refs[1]
        rest = list(refs[2:])
        add_ref = rest.pop(0) if add is not None else None
        bias_ref = rest.pop(0) if bias is not None else None
        o_ref, acc_ref = rest
        kk = pl.program_id(2)

        @pl.when(kk == 0)
        def _():
            acc_ref[...] = jnp.zeros_like(acc_ref)

        acc_ref[...] += lax.dot_general(a_ref[...].astype(BF16), b_ref[...].astype(BF16), dims,
                                        preferred_element_type=F32)

        @pl.when(kk == nk - 1)
        def _():
            r = acc_ref[...]
            if add_ref is not None:
                r = r + add_ref[...].astype(F32)
            if bias_ref is not None:
                r = r + bias_ref[...]
            o_ref[...] = r.astype(out_dtype)

    if nsplit is None:
        out_shape = jax.ShapeDtypeStruct((m, n), out_dtype)
        out_spec = pl.BlockSpec((tm, tn), lambda i, j, kk: (i, j))
    else:
        per = n // nsplit // tn
        out_shape = jax.ShapeDtypeStruct((nsplit, m, n // nsplit), out_dtype)
        out_spec = pl.BlockSpec((None, tm, tn), lambda i, j, kk: (j // per, i, j % per))
    return _pcall(body, name=name, comm=comm, out_shape=out_shape, grid=(m // tm, n // tn, nk), in_specs=in_specs,
                  out_specs=out_spec, scratch=[pltpu.VMEM((tm, tn), F32)],
                  sem=("parallel", "parallel", "arbitrary"))(*operands)


def _rms_fwd(x, w, *, name):
    s, d = x.shape
    t = _tile(s, 256, SUBLANES)

    def body(x_ref, w_ref, o_ref):
        o_ref[...] = _rms(x_ref[...], w_ref[...]).astype(BF16)

    return _pcall(body, name=name, out_shape=jax.ShapeDtypeStruct((s, d), BF16), grid=(s // t,),
                  in_specs=[pl.BlockSpec((t, d), lambda i: (i, 0)), pl.BlockSpec((1, d), lambda i: (0, 0))],
                  out_specs=pl.BlockSpec((t, d), lambda i: (i, 0)), sem=("parallel",))(x, w.reshape(1, d))


def _rms_bwd(x, w, dh, res, *, name):
    s, d = x.shape
    t = _tile(s, 256, SUBLANES)

    def body(*refs):
        if res is None:
            x_ref, w_ref, dh_ref, dx_ref, dw_ref = refs
            res_ref = None
        else:
            x_ref, w_ref, dh_ref, res_ref, dx_ref, dw_ref = refs
        _, vjp = jax.vjp(_rms, x_ref[...], w_ref[...])
        dx, dw = vjp(dh_ref[...].astype(F32))
        if res_ref is not None:
            dx = dx + res_ref[...]
        dx_ref[...] = dx

        @pl.when(pl.program_id(0) == 0)
        def _():
            dw_ref[...] = jnp.zeros_like(dw_ref)

        dw_ref[...] += dw

    tok = pl.BlockSpec((t, d), lambda i: (i, 0))
    row = pl.BlockSpec((1, d), lambda i: (0, 0))
    operands = [x, w.reshape(1, d), dh] + ([] if res is None else [res])
    return _pcall(body, name=name,
                  out_shape=(jax.ShapeDtypeStruct((s, d), F32), jax.ShapeDtypeStruct((1, d), F32)), grid=(s // t,),
                  in_specs=[tok, row, tok] + ([] if res is None else [tok]), out_specs=(tok, row),
                  sem=("arbitrary",))(*operands)


def _loss_head(x, w, target, *, name):
    s, d = x.shape
    t = _tile(s, 256, SUBLANES)

    def body(x_ref, w_ref, t_ref, loss_ref, dx_ref, dw_ref):
        def f(xv, wv):
            err = _rms(xv, wv) - t_ref[...]
            return 0.5 * jnp.sum(jnp.mean(err * err, axis=-1))

        val, vjp = jax.vjp(f, x_ref[...], w_ref[...])
        dx, dw = vjp(jnp.ones((), F32))
        dx_ref[...] = dx

        @pl.when(pl.program_id(0) == 0)
        def _():
            dw_ref[...] = jnp.zeros_like(dw_ref)
            loss_ref[...] = jnp.zeros_like(loss_ref)

        dw_ref[...] += dw
        loss_ref[...] += jnp.full(loss_ref.shape, val, F32)

    tok = pl.BlockSpec((t, d), lambda i: (i, 0))
    row = pl.BlockSpec((1, d), lambda i: (0, 0))
    return _pcall(body, name=name,
                  out_shape=(jax.ShapeDtypeStruct((1, LANES), F32), jax.ShapeDtypeStruct((s, d), F32),
                             jax.ShapeDtypeStruct((1, d), F32)),
                  grid=(s // t,), in_specs=[tok, row, tok],
                  out_specs=(pl.BlockSpec((1, LANES), lambda i: (0, 0)), tok, row), sem=("arbitrary",))(
                      x, w.reshape(1, d), target)


def _conv_shifts(prev8, cur, t):
    xp = jnp.concatenate([prev8, cur], axis=0)
    out = []
    for j in range(CONV_WIDTH):
        k = CONV_WIDTH - 1 - j
        out.append(cur if k == 0 else pltpu.roll(xp, k, 0)[SUBLANES:SUBLANES + t])
    return out


def _conv_fwd(p, col_blk, width, w8, b, *, name):
    s = p.shape[0]
    t = _tile(s, 256, SUBLANES)
    r8 = t // SUBLANES

    def body(cur_ref, prev_ref, w_ref, b_ref, y_ref):
        i = pl.program_id(0)
        prev8 = jnp.where(i == 0, 0.0, prev_ref[...])
        sh = _conv_shifts(prev8, cur_ref[...], t)
        w = w_ref[...]
        y = b_ref[...] + sh[0] * w[0:1]
        for j in range(1, CONV_WIDTH):
            y = y + sh[j] * w[j:j + 1]
        y_ref[...] = y

    return _pcall(body, name=name, out_shape=jax.ShapeDtypeStruct((s, width), F32), grid=(s // t,),
                  in_specs=[pl.BlockSpec((t, width), lambda i: (i, col_blk)),
                            pl.BlockSpec((SUBLANES, width), lambda i: (jnp.maximum(i * r8 - 1, 0), col_blk)),
                            pl.BlockSpec((SUBLANES, width), lambda i: (0, 0)),
                            pl.BlockSpec((1, width), lambda i: (0, 0))],
                  out_specs=pl.BlockSpec((t, width), lambda i: (i, 0)), sem=("parallel",))(p, p, w8, b)


def _conv_bwd(p, col_blk, width, w8, dy, *, name):
    s = p.shape[0]
    t = _tile(s, 256, SUBLANES)
    r8 = t // SUBLANES
    nt = s // t

    def body(cur_ref, prev_ref, w_ref, dy_ref, dyn_ref, dx_ref, dw_ref, db_ref):
        i = pl.program_id(0)
        prev8 = jnp.where(i == 0, 0.0, prev_ref[...])
        sh = _conv_shifts(prev8, cur_ref[...], t)
        dy = dy_ref[...]
        next8 = jnp.where(i == nt - 1, 0.0, dyn_ref[...])
        dyp = jnp.concatenate([dy, next8], axis=0)
        w = w_ref[...]
        rows = lax.broadcasted_iota(jnp.int32, (SUBLANES, width), 0)
        dx = dy * w[CONV_WIDTH - 1:CONV_WIDTH]
        dw = jnp.zeros((SUBLANES, width), F32)
        for j in range(CONV_WIDTH):
            k = CONV_WIDTH - 1 - j
            if k:
                dx = dx + pltpu.roll(dyp, t + SUBLANES - k, 0)[0:t] * w[j:j + 1]
            dw = dw + jnp.where(rows == j, jnp.sum(dy * sh[j], axis=0, keepdims=True), 0.0)
        dx_ref[...] = dx

        @pl.when(i == 0)
        def _():
            dw_ref[...] = jnp.zeros_like(dw_ref)
            db_ref[...] = jnp.zeros_like(db_ref)

        dw_ref[...] += dw
        db_ref[...] += jnp.sum(dy, axis=0, keepdims=True)

    return _pcall(body, name=name,
                  out_shape=(jax.ShapeDtypeStruct((s, width), F32), jax.ShapeDtypeStruct((SUBLANES, width), F32),
                             jax.ShapeDtypeStruct((1, width), F32)),
                  grid=(nt,),
                  in_specs=[pl.BlockSpec((t, width), lambda i: (i, col_blk)),
                            pl.BlockSpec((SUBLANES, width), lambda i: (jnp.maximum(i * r8 - 1, 0), col_blk)),
                            pl.BlockSpec((SUBLANES, width), lambda i: (0, 0)),
                            pl.BlockSpec((t, width), lambda i: (i, 0)),
                            pl.BlockSpec((SUBLANES, width), lambda i: (jnp.minimum((i + 1) * r8, s // SUBLANES - 1), 0))],
                  out_specs=(pl.BlockSpec((t, width), lambda i: (i, 0)),
                             pl.BlockSpec((SUBLANES, width), lambda i: (0, 0)),
                             pl.BlockSpec((1, width), lambda i: (0, 0))),
                  sem=("arbitrary",))(p, p, w8, dy, dy)


def _dn_chunk(state, c, z, ba, alog_row, dt_row, nw_row, *, heads, bw):
    cs = c.shape[0]
    hd = DN_HEAD_DIM
    qkv = _silu(c)
    gfull = -jnp.exp(alog_row) * _softplus(ba + dt_row)
    beta_full = _sigmoid(ba)
    ri = lax.broadcasted_iota(jnp.int32, (cs, cs), 0)
    ci = lax.broadcasted_iota(jnp.int32, (cs, cs), 1)
    causal = ri >= ci
    strict = ri > ci
    tril = causal.astype(F32)
    eye = (ri == ci).astype(F32)
    gc = _xdot(tril, gfull, 1, 0)
    gct = _xdot(gfull, tril, 0, 1)
    outs, states = [], []
    for h in range(heads):
        q = qkv[:, h * hd:(h + 1) * hd]
        k = qkv[:, bw + h * hd:bw + (h + 1) * hd]
        v = qkv[:, 2 * bw + h * hd:2 * bw + (h + 1) * hd]
        q = q * lax.rsqrt(jnp.sum(q * q, axis=-1, keepdims=True) + NORM_EPS) * (hd ** -0.5)
        k = k * lax.rsqrt(jnp.sum(k * k, axis=-1, keepdims=True) + NORM_EPS)
        beta = _pick_lane(beta_full, h)
        g_col = _pick_lane(gc, heads + h)
        g_row = _pick_row(gct, heads + h)
        decay = jnp.exp(jnp.where(causal, g_col - g_row, -1e30))
        k_beta = k * beta
        v_beta = v * beta
        kk = _bdot(k_beta, k, 1, 1) * decay
        m = -jnp.where(strict, kk, 0.0)
        tinv = eye + m
        pw = m
        for _ in range(int(math.log2(cs)) - 1):
            pw = _xdot(pw, pw, 1, 0)
            tinv = tinv + _xdot(tinv, pw, 1, 0)
        rhs = jnp.concatenate([v_beta, k_beta * jnp.exp(g_col)], axis=-1)
        sol = _xdot(tinv, rhs, 1, 0)
        u, w = sol[:, :hd], sol[:, hd:]
        qk = jnp.where(causal, _bdot(q, k, 1, 1) * decay, 0.0)
        g_last = _pick_row(g_col, cs - 1)
        k_dec = k * jnp.exp(g_last - g_col)
        q_dec = q * jnp.exp(g_col)
        s_h = state[h]
        v_new = u - _bdot(w, s_h, 1, 0)
        o = _bdot(q_dec, s_h, 1, 0) + _bdot(qk, v_new, 1, 0)
        states.append(s_h * jnp.exp(g_last) + _bdot(k_dec, v_new, 0, 0))
        outs.append(_rms(o, nw_row) * _silu(z[:, h * hd:(h + 1) * hd]))
    return jnp.concatenate(outs, axis=-1), tuple(states)


def _dn_rows(a_log, dt_bias, heads):
    z = jnp.zeros((heads,), F32)
    pad = jnp.zeros((BA_PAD - 2 * heads,), F32)
    return (jnp.concatenate([z, a_log, pad]).reshape(1, BA_PAD), jnp.concatenate([z, dt_bias, pad]).reshape(1, BA_PAD))


def _dn_fwd(p, conv, a_log, dt_bias, norm_w, *, bw, ba_blk, name, comm=None):
    s = p.shape[0]
    heads = bw // DN_HEAD_DIM
    cs = min(DN_CHUNK, s)
    n = s // cs
    hd = DN_HEAD_DIM
    alog_row, dt_row = _dn_rows(a_log, dt_bias, heads)
    fn = functools.partial(_dn_chunk, heads=heads, bw=bw)

    def body(c_ref, z_ref, ba_ref, al_ref, dt_ref, nw_ref, o_ref, save_ref, st_ref):
        @pl.when(pl.program_id(0) == 0)
        def _():
            st_ref[...] = jnp.zeros_like(st_ref)

        save_ref[...] = st_ref[...]
        o, new = fn(tuple(st_ref[h] for h in range(heads)), c_ref[...], z_ref[...], ba_ref[...], al_ref[...],
                    dt_ref[...], nw_ref[...])
        o_ref[...] = o.astype(BF16)
        for h in range(heads):
            st_ref[h] = new[h]

    row = lambda wd: pl.BlockSpec((1, wd), lambda i: (0, 0))
    return _pcall(body, name=name, comm=comm,
                  out_shape=(jax.ShapeDtypeStruct((s, bw), BF16), jax.ShapeDtypeStruct((n, heads, hd, hd), F32)),
                  grid=(n,),
                  in_specs=[pl.BlockSpec((cs, 3 * bw), lambda i: (i, 0)), pl.BlockSpec((cs, bw), lambda i: (i, 3)),
                            pl.BlockSpec((cs, BA_PAD), lambda i: (i, ba_blk)), row(BA_PAD), row(BA_PAD), row(hd)],
                  out_specs=(pl.BlockSpec((cs, bw), lambda i: (i, 0)),
                             pl.BlockSpec((None, heads, hd, hd), lambda i: (i, 0, 0, 0))),
                  scratch=[pltpu.VMEM((heads, hd, hd), F32)], sem=("arbitrary",))(
                      conv, p, p, alog_row, dt_row, norm_w.reshape(1, hd))


def _dn_bwd(p, conv, states, d_o, a_log, dt_bias, norm_w, *, bw, ba_blk, name, comm=None):
    s = p.shape[0]
    heads = bw // DN_HEAD_DIM
    cs = min(DN_CHUNK, s)
    n = s // cs
    hd = DN_HEAD_DIM
    alog_row, dt_row = _dn_rows(a_log, dt_bias, heads)
    fn = functools.partial(_dn_chunk, heads=heads, bw=bw)

    def body(c_ref, z_ref, ba_ref, st_ref, do_ref, al_ref, dt_ref, nw_ref,
             dc_ref, dz_ref, dba_ref, dal_ref, ddt_ref, dnw_ref, dst_ref):
        @pl.when(pl.program_id(0) == 0)
        def _():
            dst_ref[...] = jnp.zeros_like(dst_ref)
            dal_ref[...] = jnp.zeros_like(dal_ref)
            ddt_ref[...] = jnp.zeros_like(ddt_ref)
            dnw_ref[...] = jnp.zeros_like(dnw_ref)

        _, vjp = jax.vjp(fn, tuple(st_ref[h] for h in range(heads)), c_ref[...], z_ref[...], ba_ref[...],
                         al_ref[...], dt_ref[...], nw_ref[...])
        dst, dc, dz, dba, dal, ddt, dnw = vjp((do_ref[...].astype(F32), tuple(dst_ref[h] for h in range(heads))))
        for h in range(heads):
            dst_ref[h] = dst[h]
        dc_ref[...] = dc
        dz_ref[...] = dz
        dba_ref[...] = dba
        dal_ref[...] += dal
        ddt_ref[...] += ddt
        dnw_ref[...] += dnw

    rev = lambda i: n - 1 - i
    row = lambda wd: pl.BlockSpec((1, wd), lambda i: (0, 0))
    return _pcall(body, name=name, comm=comm,
                  out_shape=(jax.ShapeDtypeStruct((s, 3 * bw), F32), jax.ShapeDtypeStruct((s, bw), F32),
                             jax.ShapeDtypeStruct((s, BA_PAD), F32), jax.ShapeDtypeStruct((1, BA_PAD), F32),
                             jax.ShapeDtypeStruct((1, BA_PAD), F32), jax.ShapeDtypeStruct((1, hd), F32)),
                  grid=(n,),
                  in_specs=[pl.BlockSpec((cs, 3 * bw), lambda i: (rev(i), 0)),
                            pl.BlockSpec((cs, bw), lambda i: (rev(i), 3)),
                            pl.BlockSpec((cs, BA_PAD), lambda i: (rev(i), ba_blk)),
                            pl.BlockSpec((None, heads, hd, hd), lambda i: (rev(i), 0, 0, 0)),
                            pl.BlockSpec((cs, bw), lambda i: (rev(i), 0)), row(BA_PAD), row(BA_PAD), row(hd)],
                  out_specs=(pl.BlockSpec((cs, 3 * bw), lambda i: (rev(i), 0)),
                             pl.BlockSpec((cs, bw), lambda i: (rev(i), 0)),
                             pl.BlockSpec((cs, BA_PAD), lambda i: (rev(i), 0)), row(BA_PAD), row(BA_PAD), row(hd)),
                  scratch=[pltpu.VMEM((heads, hd, hd), F32)], sem=("arbitrary",))(
                      conv, p, p, states, d_o, alog_row, dt_row, norm_w.reshape(1, hd))


def _lru_gates(xc, wr, br, wi, bi, lam):
    r = _sigmoid(_bdot(xc, wr, 1, 0) + br)
    i = _sigmoid(_bdot(xc, wi, 1, 0) + bi)
    log_a = -LRU_C * r * _softplus(-lam)
    return jnp.exp(log_a), jnp.sqrt(-_expm1(2.0 * log_a)) * (i * xc)


def _scan_rows(t, step, carry):
    def trip(g, cr):
        base = pl.multiple_of(g * SUBLANES, SUBLANES)
        for r in range(SUBLANES):
            cr = step(base + r, cr)
        return cr
    return lax.fori_loop(0, t // SUBLANES, trip, carry)


def _scan_rows_rev(t, step, carry):
    def trip(g, cr):
        base = pl.multiple_of((t // SUBLANES - 1 - g) * SUBLANES, SUBLANES)
        for r in range(SUBLANES - 1, -1, -1):
            cr = step(base + r, cr)
        return cr
    return lax.fori_loop(0, t // SUBLANES, trip, carry)


def _lru_fwd(p, xc, wr, br, wi, bi, lam, *, bw, z_blk, name):
    s = p.shape[0]
    t = _tile(s, 256, SUBLANES)
    nt = s // t

    def body(xc_ref, z_ref, wr_ref, br_ref, wi_ref, bi_ref, lam_ref, o_ref, save_ref, a_s, b_s, h_s, carry_s):
        @pl.when(pl.program_id(0) == 0)
        def _():
            carry_s[...] = jnp.zeros_like(carry_s)

        a, inp = _lru_gates(xc_ref[...], wr_ref[...], br_ref[...], wi_ref[...], bi_ref[...], lam_ref[...])
        a_s[...] = a
        b_s[...] = inp
        h0 = carry_s[...]
        save_ref[...] = h0

        def step(r, h):
            h = a_s[pl.ds(r, 1), :] * h + b_s[pl.ds(r, 1), :]
            h_s[pl.ds(r, 1), :] = h
            return h

        carry_s[...] = _scan_rows(t, step, h0)
        o_ref[...] = (h_s[...] * _silu(z_ref[...])).astype(BF16)

    tok = pl.BlockSpec((t, bw), lambda i: (i, 0))
    row = pl.BlockSpec((1, bw), lambda i: (0, 0))
    mat = pl.BlockSpec((bw, bw), lambda i: (0, 0))
    return _pcall(body, name=name,
                  out_shape=(jax.ShapeDtypeStruct((s, bw), BF16), jax.ShapeDtypeStruct((nt, 1, bw), F32)), grid=(nt,),
                  in_specs=[tok, pl.BlockSpec((t, bw), lambda i: (i, z_blk)), mat, row, mat, row, row],
                  out_specs=(tok, pl.BlockSpec((None, 1, bw), lambda i: (i, 0, 0))),
                  scratch=[pltpu.VMEM((t, bw), F32)] * 3 + [pltpu.VMEM((1, bw), F32)], sem=("arbitrary",))(
                      xc, p, wr, br, wi, bi, lam)


def _lru_bwd(p, xc, saves, d_o, wr, br, wi, bi, lam, *, bw, z_blk, name):
    s = p.shape[0]
    t = _tile(s, 256, SUBLANES)
    nt = s // t

    def body(xc_ref, z_ref, sv_ref, do_ref, wr_ref, br_ref, wi_ref, bi_ref, lam_ref,
             dxc_ref, dz_ref, dwr_ref, dwi_ref, dbr_ref, dbi_ref, dlam_ref, a_s, b_s, h_s, g_s, carry_s):
        @pl.when(pl.program_id(0) == 0)
        def _():
            carry_s[...] = jnp.zeros_like(carry_s)
            for r in (dwr_ref, dwi_ref, dbr_ref, dbi_ref, dlam_ref):
                r[...] = jnp.zeros_like(r)

        (a, inp), vjp_g = jax.vjp(_lru_gates, xc_ref[...], wr_ref[...], br_ref[...], wi_ref[...], bi_ref[...],
                                  lam_ref[...])
        a_s[...] = a
        b_s[...] = inp
        h0 = sv_ref[...]

        def fstep(r, h):
            h_s[pl.ds(r, 1), :] = h
            return a_s[pl.ds(r, 1), :] * h + b_s[pl.ds(r, 1), :]

        _scan_rows(t, fstep, h0)
        a = a_s[...]
        hs = a * h_s[...] + b_s[...]
        z = z_ref[...]
        d_o = do_ref[...].astype(F32)
        _, vjp_o = jax.vjp(lambda hv, zv: hv * _silu(zv), hs, z)
        dhs, dz = vjp_o(d_o)
        dz_ref[...] = dz
        g_s[...] = dhs

        def bstep(r, cr):
            g = g_s[pl.ds(r, 1), :] + cr
            g_s[pl.ds(r, 1), :] = g
            return a_s[pl.ds(r, 1), :] * g

        carry_s[...] = _scan_rows_rev(t, bstep, carry_s[...])
        g = g_s[...]
        dxc, dwr, dbr, dwi, dbi, dlam = vjp_g((g * h_s[...], g))
        dxc_ref[...] = dxc
        dwr_ref[...] += dwr
        dwi_ref[...] += dwi
        dbr_ref[...] += dbr
        dbi_ref[...] += dbi
        dlam_ref[...] += dlam

    rev = lambda i: nt - 1 - i
    tok = pl.BlockSpec((t, bw), lambda i: (rev(i), 0))
    row = pl.BlockSpec((1, bw), lambda i: (0, 0))
    mat = pl.BlockSpec((bw, bw), lambda i: (0, 0))
    sd = jax.ShapeDtypeStruct
    return _pcall(body, name=name,
                  out_shape=(sd((s, bw), F32), sd((s, bw), F32), sd((bw, bw), F32), sd((bw, bw), F32),
                             sd((1, bw), F32), sd((1, bw), F32), sd((1, bw), F32)),
                  grid=(nt,),
                  in_specs=[tok, pl.BlockSpec((t, bw), lambda i: (rev(i), z_blk)),
                            pl.BlockSpec((None, 1, bw), lambda i: (rev(i), 0, 0)), tok, mat, row, mat, row, row],
                  out_specs=(tok, tok, mat, mat, row, row, row),
                  scratch=[pltpu.VMEM((t, bw), F32)] * 4 + [pltpu.VMEM((1, bw), F32)], sem=("arbitrary",))(
                      xc, p, saves, d_o, wr, br, wi, bi, lam)


def _s5_prep(log_dt, a_re, a_im, b_re, b_im, c_re, c_im, d_skip):
    g, n = a_re.shape
    gs = d_skip.shape[1]
    gpb = LANES // gs
    nb = g // gpb
    dt = jnp.exp(log_dt)[:, None]
    mag = jnp.exp(dt * a_re)
    ab_re = mag * jnp.cos(dt * a_im)
    ab_im = mag * jnp.sin(dt * a_im)
    den = a_re * a_re + a_im * a_im
    f_re = ((ab_re - 1.0) * a_re + ab_im * a_im) / den
    f_im = (ab_im * a_re - (ab_re - 1.0) * a_im) / den
    bb_re = f_re[..., None] * b_re - f_im[..., None] * b_im
    bb_im = f_re[..., None] * b_im + f_im[..., None] * b_re
    eye = jnp.eye(gpb, dtype=F32)

    def b_dense(bb):
        t = bb.reshape(nb, gpb, n, gs)
        return jnp.einsum("bgnc,gh->bgchn", t, eye).reshape(nb, gpb * gs, gpb * n)

    def c_dense(cc):
        t = cc.reshape(nb, gpb, gs, n)
        return jnp.einsum("bgcn,gh->bgnhc", t, eye).reshape(nb, gpb * n, gpb * gs)

    lanes = gpb * n
    sub = lanes // LANES
    return (ab_re.reshape(nb, sub, LANES), ab_im.reshape(nb, sub, LANES), b_dense(bb_re), b_dense(bb_im),
            c_dense(c_re), c_dense(c_im), d_skip.reshape(1, g * gs))


def _s5_out(xre, xim, cre, cim, d, u):
    return _gelu(_bdot(xre, cre, 1, 0) - _bdot(xim, cim, 1, 0) + d * u)


S5_LOOKAHEAD = 4


def _cmul(ar, ai, xr, xi):
    return ar * xr - ai * xi, ar * xi + ai * xr


def _s5_scan(ar, ai, b_re, b_im, t_re, t_im, t, reverse):
    k = S5_LOOKAHEAD
    a2 = _cmul(ar, ai, ar, ai)
    a4 = _cmul(*a2, *a2)
    lo = lambda n, off=0: pl.ds(off, n)
    for (src_re, src_im, dst_re, dst_im, pw, d) in ((b_re, b_im, t_re, t_im, (ar, ai), 1),
                                                     (t_re, t_im, b_re, b_im, a2, 2)):
        keep = lo(d, t - d) if reverse else lo(d)
        cur = lo(t - d) if reverse else lo(t - d, d)
        nbr = lo(t - d, d) if reverse else lo(t - d)
        dst_re[keep] = src_re[keep]
        dst_im[keep] = src_im[keep]
        mr, mi = _cmul(*pw, src_re[nbr], src_im[nbr])
        dst_re[cur] = src_re[cur] + mr
        dst_im[cur] = src_im[cur] + mi

    def step(g, cr):
        row = pl.multiple_of(((t // k - 1 - g) if reverse else g) * k, k)
        mr, mi = _cmul(*a4, *cr)
        nr = mr + b_re[pl.ds(row, k)]
        ni = mi + b_im[pl.ds(row, k)]
        b_re[pl.ds(row, k)] = nr
        b_im[pl.ds(row, k)] = ni
        return nr, ni

    zero = jnp.zeros((k,) + ar.shape, F32)
    xr, xi = lax.fori_loop(0, t // k, step, (zero, zero), unroll=2)
    return (xr[0], xi[0]) if reverse else (xr[k - 1], xi[k - 1])


def _s5_fwd(p, prep, *, bw, u_blk0, name):
    s = p.shape[0]
    are, aim, bre, bim, cre, cim, d = prep
    nb, sub, _ = are.shape
    lanes = sub * LANES
    t = _tile(s, 256, SUBLANES)
    nt = s // t

    def body(u_ref, are_ref, aim_ref, bre_ref, bim_ref, cre_ref, cim_ref, d_ref, y_ref, save_ref,
             bre_s, bim_s, xre_s, xim_s, carry_s):
        @pl.when(pl.program_id(1) == 0)
        def _():
            carry_s[...] = jnp.zeros_like(carry_s)

        u = u_ref[...]
        xre_s[...] = _bdot(u, bre_ref[...], 1, 0).reshape(t, sub, LANES)
        xim_s[...] = _bdot(u, bim_ref[...], 1, 0).reshape(t, sub, LANES)
        ar, ai = are_ref[...], aim_ref[...]
        save_ref[...] = carry_s[...]
        er, ei = _cmul(ar, ai, carry_s[0], carry_s[1])
        xre_s[0] += er
        xim_s[0] += ei
        xr, xi = _s5_scan(ar, ai, xre_s, xim_s, bre_s, bim_s, t, False)
        carry_s[0] = xr
        carry_s[1] = xi
        y_ref[...] = _s5_out(xre_s[...].reshape(t, lanes), xim_s[...].reshape(t, lanes), cre_ref[...], cim_ref[...],
                             d_ref[...], u).astype(BF16)

    vec = pl.BlockSpec((None, sub, LANES), lambda j, i: (j, 0, 0))
    bmat = pl.BlockSpec((None, LANES, lanes), lambda j, i: (j, 0, 0))
    cmat = pl.BlockSpec((None, lanes, LANES), lambda j, i: (j, 0, 0))
    return _pcall(body, name=name,
                  out_shape=(jax.ShapeDtypeStruct((s, bw), BF16), jax.ShapeDtypeStruct((nb, nt, 2, sub, LANES), F32)),
                  grid=(nb, nt),
                  in_specs=[pl.BlockSpec((t, LANES), lambda j, i: (i, u_blk0 + j)), vec, vec, bmat, bmat, cmat, cmat,
                            pl.BlockSpec((1, LANES), lambda j, i: (0, j))],
                  out_specs=(pl.BlockSpec((t, LANES), lambda j, i: (i, j)),
                             pl.BlockSpec((None, None, 2, sub, LANES), lambda j, i: (j, i, 0, 0, 0))),
                  scratch=[pltpu.VMEM((t, sub, LANES), F32)] * 4 + [pltpu.VMEM((2, sub, LANES), F32)],
                  sem=("parallel", "arbitrary"))(p, are, aim, bre, bim, cre, cim, d)


def _s5_bwd(p, prep, saves, dyg, *, bw, u_blk0, name):
    s = p.shape[0]
    are, aim, bre, bim, cre, cim, d = prep
    nb, sub, _ = are.shape
    lanes = sub * LANES
    t = _tile(s, 256, SUBLANES)
    nt = s // t

    def body(u_ref, dy_ref, sv_ref, are_ref, aim_ref, bre_ref, bim_ref, cre_ref, cim_ref, d_ref,
             du_ref, dar_ref, dai_ref, dbre_ref, dbim_ref, dcre_ref, dcim_ref, dd_ref,
             bre_s, bim_s, xre_s, xim_s, tre_s, tim_s, carry_s):
        @pl.when(pl.program_id(1) == 0)
        def _():
            carry_s[...] = jnp.zeros_like(carry_s)
            for r in (dar_ref, dai_ref, dbre_ref, dbim_ref, dcre_ref, dcim_ref, dd_ref):
                r[...] = jnp.zeros_like(r)

        u = u_ref[...]
        xre_s[...] = _bdot(u, bre_ref[...], 1, 0).reshape(t, sub, LANES)
        xim_s[...] = _bdot(u, bim_ref[...], 1, 0).reshape(t, sub, LANES)
        ar, ai = are_ref[...], aim_ref[...]
        er, ei = _cmul(ar, ai, sv_ref[0], sv_ref[1])
        xre_s[0] += er
        xim_s[0] += ei
        _s5_scan(ar, ai, xre_s, xim_s, tre_s, tim_s, t, False)
        _, vjp_o = jax.vjp(_s5_out, xre_s[...].reshape(t, lanes), xim_s[...].reshape(t, lanes), cre_ref[...],
                           cim_ref[...], d_ref[...], u)
        dxre, dxim, dcre, dcim, dd, du = vjp_o(dy_ref[...].astype(F32))
        dcre_ref[...] += dcre.astype(F32)
        dcim_ref[...] += dcim.astype(F32)
        dd_ref[...] += dd
        bre_s[...] = dxre.reshape(t, sub, LANES)
        bim_s[...] = dxim.reshape(t, sub, LANES)
        bre_s[t - 1] += carry_s[0]
        bim_s[t - 1] += carry_s[1]
        g0r, g0i = _s5_scan(ar, -ai, bre_s, bim_s, tre_s, tim_s, t, True)
        carry_s[0], carry_s[1] = _cmul(ar, -ai, g0r, g0i)
        gr, gi = bre_s[pl.ds(1, t - 1)], bim_s[pl.ds(1, t - 1)]
        pr, pi = xre_s[pl.ds(0, t - 1)], xim_s[pl.ds(0, t - 1)]
        dar_ref[...] += jnp.sum(gr * pr + gi * pi, axis=0) + bre_s[0] * sv_ref[0] + bim_s[0] * sv_ref[1]
        dai_ref[...] += jnp.sum(gi * pr - gr * pi, axis=0) + bim_s[0] * sv_ref[0] - bre_s[0] * sv_ref[1]
        dbu_re = bre_s[...].reshape(t, lanes)
        dbu_im = bim_s[...].reshape(t, lanes)
        du_ref[...] = du + _bdot(dbu_re, bre_ref[...], 1, 1) + _bdot(dbu_im, bim_ref[...], 1, 1)
        dbre_ref[...] += _bdot(u, dbu_re, 0, 0)
        dbim_ref[...] += _bdot(u, dbu_im, 0, 0)

    rev = lambda i: nt - 1 - i
    vec = pl.BlockSpec((None, sub, LANES), lambda j, i: (j, 0, 0))
    bmat = pl.BlockSpec((None, LANES, lanes), lambda j, i: (j, 0, 0))
    cmat = pl.BlockSpec((None, lanes, LANES), lambda j, i: (j, 0, 0))
    drow = pl.BlockSpec((1, LANES), lambda j, i: (0, j))
    sd = jax.ShapeDtypeStruct
    return _pcall(body, name=name,
                  out_shape=(sd((s, bw), F32), sd(are.shape, F32), sd(aim.shape, F32), sd(bre.shape, F32),
                             sd(bim.shape, F32), sd(cre.shape, F32), sd(cim.shape, F32), sd((1, bw), F32)),
                  grid=(nb, nt),
                  in_specs=[pl.BlockSpec((t, LANES), lambda j, i: (rev(i), u_blk0 + j)),
                            pl.BlockSpec((t, LANES), lambda j, i: (rev(i), j)),
                            pl.BlockSpec((None, None, 2, sub, LANES), lambda j, i: (j, rev(i), 0, 0, 0)),
                            vec, vec, bmat, bmat, cmat, cmat, drow],
                  out_specs=(pl.BlockSpec((t, LANES), lambda j, i: (rev(i), j)), vec, vec, bmat, bmat, cmat, cmat, drow),
                  scratch=[pltpu.VMEM((t, sub, LANES), F32)] * 6 + [pltpu.VMEM((2, sub, LANES), F32)],
                  sem=("parallel", "arbitrary"))(p, dyg, saves, are, aim, bre, bim, cre, cim, d)


def _glu_gate(pre, z, bw):
    return pre[:, :bw] * _sigmoid(pre[:, bw:]) * _silu(z)


def _glu_fwd(pre, p, *, bw, z_blk, name):
    s = pre.shape[0]
    t = _tile(s, 256, SUBLANES)

    def body(pre_ref, z_ref, o_ref):
        o_ref[...] = _glu_gate(pre_ref[...], z_ref[...], bw).astype(BF16)

    return _pcall(body, name=name, out_shape=jax.ShapeDtypeStruct((s, bw), BF16), grid=(s // t,),
                  in_specs=[pl.BlockSpec((t, 2 * bw), lambda i: (i, 0)), pl.BlockSpec((t, bw), lambda i: (i, z_blk))],
                  out_specs=pl.BlockSpec((t, bw), lambda i: (i, 0)), sem=("parallel",))(pre, p)


def _glu_bwd(pre, p, d_o, *, bw, z_blk, name):
    s = pre.shape[0]
    t = _tile(s, 256, SUBLANES)

    def body(pre_ref, z_ref, do_ref, dpre_ref, dz_ref, db_ref):
        _, vjp = jax.vjp(functools.partial(_glu_gate, bw=bw), pre_ref[...], z_ref[...])
        dpre, dz = vjp(do_ref[...].astype(F32))
        dpre_ref[...] = dpre.astype(BF16)
        dz_ref[...] = dz

        @pl.when(pl.program_id(0) == 0)
        def _():
            db_ref[...] = jnp.zeros_like(db_ref)

        db_ref[...] += jnp.sum(dpre, axis=0, keepdims=True)

    sd = jax.ShapeDtypeStruct
    return _pcall(body, name=name, out_shape=(sd((s, 2 * bw), BF16), sd((s, bw), F32), sd((1, 2 * bw), F32)),
                  grid=(s // t,),
                  in_specs=[pl.BlockSpec((t, 2 * bw), lambda i: (i, 0)), pl.BlockSpec((t, bw), lambda i: (i, z_blk)),
                            pl.BlockSpec((t, bw), lambda i: (i, 0))],
                  out_specs=(pl.BlockSpec((t, 2 * bw), lambda i: (i, 0)), pl.BlockSpec((t, bw), lambda i: (i, 0)),
                             pl.BlockSpec((1, 2 * bw), lambda i: (0, 0))), sem=("arbitrary",))(pre, p, d_o)


def _attn_tile(q, z, kv, *, bw):
    hd = bw // MEM_HEADS
    outs = []
    for h in range(MEM_HEADS):
        k = kv[:, h * hd:(h + 1) * hd]
        v = kv[:, bw + h * hd:bw + (h + 1) * hd]
        sc = _bdot(q[:, h * hd:(h + 1) * hd], k, 1, 1) * (hd ** -0.5)
        e = jnp.exp(sc - lax.stop_gradient(jnp.max(sc, axis=-1, keepdims=True)))
        prob = e / jnp.sum(e, axis=-1, keepdims=True)
        outs.append(_bdot(prob, v, 1, 0))
    return jnp.concatenate(outs, axis=-1) * _silu(z)


def _attn_fwd(p, kv, *, bw, q_blk, z_blk, name):
    s = p.shape[0]
    m = kv.shape[0]
    t = _tile(s, 256, SUBLANES)

    def body(q_ref, z_ref, kv_ref, o_ref):
        o_ref[...] = _attn_tile(q_ref[...], z_ref[...], kv_ref[...], bw=bw).astype(BF16)

    return _pcall(body, name=name, out_shape=jax.ShapeDtypeStruct((s, bw), BF16), grid=(s // t,),
                  in_specs=[pl.BlockSpec((t, bw), lambda i: (i, q_blk)), pl.BlockSpec((t, bw), lambda i: (i, z_blk)),
                            pl.BlockSpec((m, 2 * bw), lambda i: (0, 0))],
                  out_specs=pl.BlockSpec((t, bw), lambda i: (i, 0)), sem=("parallel",))(p, p, kv)


def _attn_bwd(p, kv, d_o, *, bw, q_blk, z_blk, name):
    s = p.shape[0]
    m = kv.shape[0]
    t = _tile(s, 256, SUBLANES)

    def body(q_ref, z_ref, kv_ref, do_ref, dq_ref, dz_ref, dkv_ref):
        _, vjp = jax.vjp(functools.partial(_attn_tile, bw=bw), q_ref[...], z_ref[...], kv_ref[...])
        dq, dz, dkv = vjp(do_ref[...].astype(F32))
        dq_ref[...] = dq
        dz_ref[...] = dz

        @pl.when(pl.program_id(0) == 0)
        def _():
            dkv_ref[...] = jnp.zeros_like(dkv_ref)

        dkv_ref[...] += dkv

    sd = jax.ShapeDtypeStruct
    tok = pl.BlockSpec((t, bw), lambda i: (i, 0))
    return _pcall(body, name=name, out_shape=(sd((s, bw), F32), sd((s, bw), F32), sd((m, 2 * bw), F32)),
                  grid=(s // t,),
                  in_specs=[pl.BlockSpec((t, bw), lambda i: (i, q_blk)), pl.BlockSpec((t, bw), lambda i: (i, z_blk)),
                            pl.BlockSpec((m, 2 * bw), lambda i: (0, 0)), tok],
                  out_specs=(tok, tok, pl.BlockSpec((m, 2 * bw), lambda i: (0, 0))), sem=("arbitrary",))(p, p, kv, d_o)


def _merge_fwd(p, o4, wg, bg, wb, *, rank, g_blk, name):
    s = p.shape[0]
    _, bw, d = wb.shape
    tm, tn = _tile(s, 512), _tile(d, 512)

    def body(g_ref, o_ref, wg_ref, bg_ref, wb_ref, out_ref):
        g = g_ref[...]
        acc = jnp.zeros((tm, tn), F32)
        for n in range(N_BRANCH):
            gate = _sigmoid(_bdot(g, wg_ref[n], 1, 0) + bg_ref[n])
            acc = acc + gate * _bdot(o_ref[n], wb_ref[n], 1, 0)
        out_ref[...] = acc.astype(BF16)

    return _pcall(body, name=name, out_shape=jax.ShapeDtypeStruct((s, d), BF16), grid=(s // tm, d // tn),
                  in_specs=[pl.BlockSpec((tm, rank), lambda i, j: (i, g_blk)),
                            pl.BlockSpec((N_BRANCH, tm, bw), lambda i, j: (0, i, 0)),
                            pl.BlockSpec((N_BRANCH, rank, tn), lambda i, j: (0, 0, j)),
                            pl.BlockSpec((N_BRANCH, 1, tn), lambda i, j: (0, 0, j)),
                            pl.BlockSpec((N_BRANCH, bw, tn), lambda i, j: (0, 0, j))],
                  out_specs=pl.BlockSpec((tm, tn), lambda i, j: (i, j)), sem=("parallel", "parallel"))(
                      p, o4, wg, bg, wb)


def _merge_bwd(p, o4, wg, bg, wb, dmerged, *, rank, g_blk, name):
    s = p.shape[0]
    _, bw, d = wb.shape
    tm, tn = _tile(s, 512), _tile(d, 512)

    def body(g_ref, o_ref, wg_ref, bg_ref, wb_ref, dm_ref, dpre_ref, dbr_ref, dbg_ref):
        g = g_ref[...]
        dm = dm_ref[...].astype(F32)

        @pl.when(pl.program_id(1) == 0)
        def _():
            dbg_ref[...] = jnp.zeros_like(dbg_ref)

        for n in range(N_BRANCH):
            gate = _sigmoid(_bdot(g, wg_ref[n], 1, 0) + bg_ref[n])
            br = _bdot(o_ref[n], wb_ref[n], 1, 0)
            dpre = dm * br * gate * (1.0 - gate)
            dpre_ref[n] = dpre.astype(BF16)
            dbr_ref[n] = (dm * gate).astype(BF16)
            dbg_ref[n] += jnp.sum(dpre, axis=0, keepdims=True)

    sd = jax.ShapeDtypeStruct
    big = pl.BlockSpec((N_BRANCH, tm, tn), lambda j, i: (0, i, j))
    return _pcall(body, name=name,
                  out_shape=(sd((N_BRANCH, s, d), BF16), sd((N_BRANCH, s, d), BF16), sd((N_BRANCH, 1, d), F32)),
                  grid=(d // tn, s // tm),
                  in_specs=[pl.BlockSpec((tm, rank), lambda j, i: (i, g_blk)),
                            pl.BlockSpec((N_BRANCH, tm, bw), lambda j, i: (0, i, 0)),
                            pl.BlockSpec((N_BRANCH, rank, tn), lambda j, i: (0, 0, j)),
                            pl.BlockSpec((N_BRANCH, 1, tn), lambda j, i: (0, 0, j)),
                            pl.BlockSpec((N_BRANCH, bw, tn), lambda j, i: (0, 0, j)),
                            pl.BlockSpec((tm, tn), lambda j, i: (i, j))],
                  out_specs=(big, big, pl.BlockSpec((N_BRANCH, 1, tn), lambda j, i: (0, 0, j))),
                  sem=("parallel", "arbitrary"))(p, o4, wg, bg, wb, dmerged)


def _adamw(w, g, m, v, *, name):
    r, cdim = w.shape
    tr = _tile(r, 128, SUBLANES)
    bc1 = 1.0 - ADAM_B1 ** ADAM_STEP
    bc2 = 1.0 - ADAM_B2 ** ADAM_STEP

    def body(w_ref, g_ref, m_ref, v_ref, d_ref, nm_ref, nv_ref):
        gv = g_ref[...]
        nm = ADAM_B1 * m_ref[...] + (1.0 - ADAM_B1) * gv
        nv = ADAM_B2 * v_ref[...] + (1.0 - ADAM_B2) * (gv * gv)
        d_ref[...] = -ADAM_LR * ((nm / bc1) / (jnp.sqrt(nv / bc2) + ADAM_EPS) + ADAM_WD * w_ref[...])
        nm_ref[...] = nm
        nv_ref[...] = nv

    blk = pl.BlockSpec((tr, cdim), lambda i: (i, 0))
    sd = jax.ShapeDtypeStruct((r, cdim), F32)
    return _pcall(body, name=name, out_shape=(sd, sd, sd), grid=(r // tr,), in_specs=[blk] * 4, out_specs=(blk,) * 3,
                  sem=("parallel",))(w, g, m, v)


HBM_SPEC = pl.BlockSpec(memory_space=pl.ANY)


def _place():
    x, y, c = lax.axis_index("x"), lax.axis_index("y"), lax.axis_index("c")
    return x, y, c, [(1 - x, y), (x, 1 - y), (1 - x, 1 - y)]


def _rcopy(src, dst, send_sem, recv_sem, device):
    return pltpu.make_async_remote_copy(src_ref=src, dst_ref=dst, send_sem=send_sem, recv_sem=recv_sem,
                                        device_id=device, device_id_type=MESH_ID)


def _comm_call(body, *, name, out_shape, n_in, n_sems, n_local=0):
    scratch = [pltpu.SemaphoreType.DMA((n_sems,)), pltpu.SemaphoreType.DMA((n_sems,))]
    if n_local:
        scratch.append(pltpu.SemaphoreType.DMA((n_local,)))
    multi = isinstance(out_shape, (tuple, list))
    return pl.pallas_call(body, name=name, out_shape=out_shape, in_specs=[HBM_SPEC] * n_in,
                          out_specs=tuple(HBM_SPEC for _ in out_shape) if multi else HBM_SPEC,
                          scratch_shapes=scratch, interpret=False)


def _ag4(items):
    n = len(items)

    def copies(ins, outs, send_sems, recv_sems, _, second_stage):
        x, y, c, chips = _place()
        sibling = (x, y, 1 - c)

        def part(i, px, py, h):
            half = ins[i].shape[0] // 2
            return outs[i].at[2 * px + py, pl.ds(h * half, half)]

        local = []
        first, landed, passed, arrived = [], [], [], []
        for i in range(n):
            half = ins[i].shape[0] // 2
            for j, chip in enumerate(chips):
                sems = (send_sems.at[6 * i + j], recv_sems.at[6 * i + j])
                sems2 = (send_sems.at[6 * i + 3 + j], recv_sems.at[6 * i + 3 + j])
                first.append(_rcopy(ins[i].at[pl.ds(c * half, half)], part(i, x, y, c), *sems, (*chip, c)))
                if second_stage:
                    landed.append(_rcopy(part(i, *chip, c), part(i, *chip, c), *sems, (*chip, c)))
                    passed.append(_rcopy(part(i, *chip, c), part(i, *chip, c), *sems2, sibling))
                    arrived.append(_rcopy(part(i, *chip, 1 - c), part(i, *chip, 1 - c), *sems2, sibling))
        return local, first, landed, passed, arrived

    def start(*refs):
        local, first, _, _, _ = copies(*refs, False)
        for cp in local + first:
            cp.start()

    def finish(*refs):
        local, first, landed, passed, arrived = copies(*refs, True)
        for k in range(3 * n):
            landed[k].wait_recv()
            passed[k].start()
        for cp in arrived:
            cp.wait_recv()
        for cp in first + passed:
            cp.wait_send()
        for cp in local:
            cp.wait()

    return _Comm(items, [jax.ShapeDtypeStruct((4,) + t.shape, t.dtype) for t in items], 6 * n, 0, start, finish)


def _own_slot(g, local, chip):
    return [jnp.where(chip == k, local, g[k]) for k in range(4)]


def _sib_halves(items, *, name):
    n = len(items)

    def body(*refs):
        ins, outs = refs[:n], refs[n:2 * n]
        send_sems, recv_sems = refs[2 * n:]
        x, y, c, _ = _place()
        cps = []
        for i in range(n):
            half = ins[i].shape[1] // 2
            cps.append(_rcopy(ins[i].at[:, pl.ds((1 - c) * half, half)], outs[i], send_sems.at[i], recv_sems.at[i],
                              (x, y, 1 - c)))
        for cp in cps:
            cp.start()
        for cp in cps:
            cp.wait()

    shapes = tuple(jax.ShapeDtypeStruct((4, t.shape[1] // 2) + t.shape[2:], t.dtype) for t in items)
    return _comm_call(body, name=name, out_shape=shapes, n_in=n, n_sems=n)(*items)


def _to_chips(items):
    n = len(items)

    def copies(ins, outs, send_sems, recv_sems, _):
        x, y, c, chips = _place()
        return [_rcopy(ins[i].at[2 * chip[0] + chip[1]], outs[i].at[j], send_sems.at[3 * i + j],
                       recv_sems.at[3 * i + j], (*chip, c)) for i in range(n) for j, chip in enumerate(chips)]

    def start(*refs):
        for cp in copies(*refs):
            cp.start()

    def finish(*refs):
        for cp in copies(*refs):
            cp.wait()

    return _Comm(items, [jax.ShapeDtypeStruct((3,) + t.shape[1:], t.dtype) for t in items], 3 * n, 0, start, finish)


def _join_halves(items, c, *, name):
    n = len(items)

    def body(*refs):
        ins, outs = refs[:n], refs[n:2 * n]
        send_sems, recv_sems = refs[2 * n:]
        x, y, cc, _ = _place()
        sibling = (x, y, 1 - cc)
        cps = []
        for i in range(n):
            a = ins[i].shape[0]
            cps.append(_rcopy(ins[i], outs[i].at[pl.ds(cc * a, a)], send_sems.at[i], recv_sems.at[i], sibling))
        for cp in cps:
            cp.start()
        for i in range(n):
            a = ins[i].shape[0]
            cps[i].wait_send()
            _rcopy(ins[i], outs[i].at[pl.ds((1 - cc) * a, a)], send_sems.at[i], recv_sems.at[i], sibling).wait_recv()

    shapes = tuple(jax.ShapeDtypeStruct((2 * t.shape[0],) + t.shape[1:], t.dtype) for t in items)
    got = _comm_call(body, name=name, out_shape=shapes, n_in=n, n_sems=n)(*items)
    out = []
    for q, g in zip(items, got):
        a = q.shape[0]
        out.append(jnp.concatenate([jnp.where(c == h, q, g[h * a:(h + 1) * a]) for h in range(2)], axis=0))
    return out


def _swap_sibling(buf, *, name):
    def body(b_ref, recv_ref, send_sems, recv_sems):
        x, y, c, _ = _place()
        cp = _rcopy(b_ref, recv_ref, send_sems.at[0], recv_sems.at[0], (x, y, 1 - c))
        cp.start()
        cp.wait()

    return _comm_call(body, name=name, out_shape=jax.ShapeDtypeStruct(buf.shape, buf.dtype), n_in=1, n_sems=1)(buf)


def _gather_chips(buf, *, name):
    n, cdim = buf.shape

    def body(b_ref, out_ref, send_sems, recv_sems):
        x, y, c, chips = _place()
        cps = [_rcopy(b_ref, out_ref.at[2 * x + y], send_sems.at[j], recv_sems.at[j], (*chip, c))
               for j, chip in enumerate(chips)]
        for cp in cps:
            cp.start()
        for j, chip in enumerate(chips):
            slot = out_ref.at[2 * chip[0] + chip[1]]
            _rcopy(slot, slot, send_sems.at[j], recv_sems.at[j], (*chip, c)).wait_recv()
        for cp in cps:
            cp.wait_send()

    return _comm_call(body, name=name, out_shape=jax.ShapeDtypeStruct((4, n, cdim), buf.dtype), n_in=1, n_sems=3)(buf)


def _rows_per_block(rows, cdim, itemsize, align):
    return _tile(rows, max(align, ELEMENTWISE_BLOCK_BYTES // (cdim * itemsize) // align * align), align)


def _sum_sib(g4, recv, c_idx, *, name):
    _, rows, cdim = g4.shape
    half = rows // 2
    tr = _rows_per_block(half, cdim, 2, 16)
    nh = half // tr

    def body(c_ref, g_ref, r_ref, o_ref):
        o_ref[...] = (g_ref[...].astype(F32) + r_ref[...].astype(F32)).astype(o_ref.dtype)

    return _pcall(body, name=name, out_shape=jax.ShapeDtypeStruct((4, half, cdim), g4.dtype), grid=(4, nh),
                  num_prefetch=1,
                  in_specs=[pl.BlockSpec((None, tr, cdim), lambda k, i, c_ref: (k, c_ref[0] * nh + i, 0)),
                            pl.BlockSpec((None, tr, cdim), lambda k, i, c_ref: (k, i, 0))],
                  out_specs=pl.BlockSpec((None, tr, cdim), lambda k, i, c_ref: (k, i, 0)),
                  sem=("parallel", "parallel"))(c_idx, g4, recv)


def _sum_chips(h4, recv3, chip_idx, *, name):
    _, n, cdim = h4.shape
    tr = _rows_per_block(n, cdim, 4, 16)

    def body(k_ref, h_ref, r_ref, o_ref):
        acc = h_ref[...].astype(F32)
        for j in range(3):
            acc = acc + r_ref[j].astype(F32)
        o_ref[...] = acc

    return _pcall(body, name=name, out_shape=jax.ShapeDtypeStruct((n, cdim), F32), grid=(n // tr,), num_prefetch=1,
                  in_specs=[pl.BlockSpec((None, tr, cdim), lambda i, k_ref: (k_ref[0], i, 0)),
                            pl.BlockSpec((3, tr, cdim), lambda i, k_ref: (0, i, 0))],
                  out_specs=pl.BlockSpec((tr, cdim), lambda i, k_ref: (i, 0)), sem=("parallel",))(chip_idx, h4, recv3)


def _add2(a, b, *, name):
    n, cdim = a.shape
    tr = _tile(n, 256, SUBLANES)

    def body(a_ref, b_ref, o_ref):
        o_ref[...] = a_ref[...] + b_ref[...]

    blk = pl.BlockSpec((tr, cdim), lambda i: (i, 0))
    return _pcall(body, name=name, out_shape=jax.ShapeDtypeStruct((n, cdim), F32), grid=(n // tr,), in_specs=[blk, blk],
                  out_specs=blk, sem=("parallel",))(a, b)


def _sum4(x4, *, name):
    _, n, cdim = x4.shape
    tr = _tile(n, 256, SUBLANES)

    def body(x_ref, o_ref):
        o_ref[...] = ((x_ref[0] + x_ref[1]) + x_ref[2]) + x_ref[3]

    return _pcall(body, name=name, out_shape=jax.ShapeDtypeStruct((n, cdim), F32), grid=(n // tr,),
                  in_specs=[pl.BlockSpec((4, tr, cdim), lambda i: (0, i, 0))],
                  out_specs=pl.BlockSpec((tr, cdim), lambda i: (i, 0)), sem=("parallel",))(x4)


def _rs_pair(items, c_idx, *, name):
    recv = _sib_halves(items, name=name + "_sib")
    return [_sum_sib(items[i], recv[i], c_idx, name=f"{name}_sum1_{i}") for i in range(len(items))]


def _rs_close(h4, recv3, c_idx, chip_idx, *, name):
    q = [_sum_chips(h4[i], recv3[i], chip_idx, name=f"{name}_sum2_{i}") for i in range(len(h4))]
    return _join_halves(q, c_idx[0], name=name + "_join")


def _all_reduce(buf, chip, *, name):
    pair = _add2(buf, _swap_sibling(buf, name=name + "_sib"), name=name + "_add")
    slots = jnp.stack(_own_slot(_gather_chips(pair, name=name + "_ici"), pair, chip), axis=0)
    return _sum4(slots, name=name + "_sum")


def _pack_rows(flat, lead, align):
    n = flat.shape[-1]
    unit = PACK_COLS * align
    total = -(-n // unit) * unit
    flat = jnp.pad(flat, [(0, 0)] * len(lead) + [(0, total - n)])
    return flat.reshape(*lead, total // PACK_COLS, PACK_COLS)


BIG = (("w_in", 1), ("ssm_w_glu", 1), ("w_kv", 0), ("w_gate", 2), ("w_branch", 2), ("w_out", 0))
SMALL_SHARDED = (("dn_conv_w", 1), ("lru_conv_w", 1), ("b_gate", 1))
SMALL = ("norm_w", "dn_a_log", "dn_dt_bias", "dn_norm_w", "lru_conv_b", "lru_w_r", "lru_b_r", "lru_w_i", "lru_b_i",
         "lru_lambda", "ssm_log_dt", "ssm_a_re", "ssm_a_im", "ssm_b_re", "ssm_b_im", "ssm_c_re", "ssm_c_im", "ssm_d",
         "ssm_b_glu", "mem_norm_w")
WEIGHTS = ("norm_w", "w_in", "dn_conv_w", "dn_a_log", "dn_dt_bias", "dn_norm_w", "lru_conv_w", "lru_conv_b",
           "lru_w_r", "lru_b_r", "lru_w_i", "lru_b_i", "lru_lambda", "ssm_log_dt", "ssm_a_re", "ssm_a_im", "ssm_b_re",
           "ssm_b_im", "ssm_c_re", "ssm_c_im", "ssm_d", "ssm_w_glu", "ssm_b_glu", "mem_norm_w", "w_kv", "w_gate",
           "b_gate", "w_branch", "w_out", "final_norm_w")


REST = BIG[1:]


def _gather_rest(wts, l):
    small = _pack_rows(jnp.concatenate([wts[n][l].reshape(-1) for n, _ in SMALL_SHARDED]), (), 2 * SUBLANES)
    return _ag4([wts[n][l].astype(BF16) for n, _ in REST] + [small])


def _full_rest(g, local, wts, chip):
    out = {n: jnp.concatenate(_own_slot(g[i], local[i], chip), axis=ax) for i, (n, ax) in enumerate(REST)}
    flat, off = jnp.stack(_own_slot(g[-1], local[-1], chip), axis=0).reshape(4, -1), 0
    for n, ax in SMALL_SHARDED:
        shp = wts[n].shape[1:]
        sz = math.prod(shp)
        out[n] = jnp.concatenate(list(flat[:, off:off + sz].reshape(4, *shp)), axis=ax)
        off += sz
    return out


def _chip_rows(t):
    return t.reshape(4, t.shape[0] // 4, t.shape[1])


def _w_in_layout(w, bw, heads, rank):
    d = w.shape[0]
    ba = 4 * bw
    rest = ba + 2 * heads
    return jnp.concatenate([w[:, :ba], w[:, rest:], w[:, ba:rest], jnp.zeros((d, BA_PAD - 2 * heads), w.dtype)], axis=1)


def _w_in_unlayout(dw, bw, heads, rank):
    ba = 4 * bw
    tail = 10 * bw + rank
    return jnp.concatenate([dw[:, :ba], dw[:, tail:tail + 2 * heads], dw[:, ba:tail]], axis=1)


def _lru_dense(w):
    nb, blk, _ = w.shape
    return jnp.einsum("nij,nm->nimj", w, jnp.eye(nb, dtype=w.dtype)).reshape(nb * blk, nb * blk)


def _w8(w):
    return jnp.concatenate([w, jnp.zeros((SUBLANES - CONV_WIDTH, w.shape[1]), w.dtype)], axis=0)


def _layer_fwd(x, mem, w_in_slots, wts, prm, l, chip, dn_comm):
    s, d = x.shape
    bw = d // N_BRANCH
    heads = bw // DN_HEAD_DIM
    rank = wts["w_gate"].shape[2]
    tag = f"l{l}_"
    sv = {"x": x}
    w_in = _w_in_layout(jnp.concatenate(w_in_slots, axis=1), bw, heads, rank)
    sv["w_in"] = w_in
    h = _rms_fwd(x, prm["norm_w"], name=tag + "norm")
    gather = _gather_rest(wts, l)
    p, rest = _mm(h, w_in, comm=gather, name=tag + "in_proj")
    full = _full_rest(rest, gather.inputs, wts, chip)
    sv["h"], sv["p"] = h, p
    conv_a = _conv_fwd(p, 0, 3 * bw, _w8(full["dn_conv_w"]), jnp.zeros((1, 3 * bw), F32), name=tag + "dn_conv")
    ba_blk = (10 * bw + rank) // BA_PAD
    dn = _dn_fwd(p, conv_a, prm["dn_a_log"], prm["dn_dt_bias"], prm["dn_norm_w"], bw=bw, ba_blk=ba_blk,
                 comm=dn_comm, name=tag + "dn")
    (o_a, dn_states), dn_res = dn if dn_comm is not None else (dn, None)
    sv["conv_a"], sv["dn_states"] = conv_a, dn_states
    xc = _conv_fwd(p, 4, bw, _w8(full["lru_conv_w"]), prm["lru_conv_b"].reshape(1, bw), name=tag + "lru_conv")
    (wr, wi), lru_vjp = jax.vjp(lambda a, b: (_lru_dense(a), _lru_dense(b)), prm["lru_w_r"], prm["lru_w_i"])
    row = lambda v: v.reshape(1, bw)
    lru_args = (wr, row(prm["lru_b_r"]), wi, row(prm["lru_b_i"]), row(prm["lru_lambda"]))
    o_b, lru_saves = _lru_fwd(p, xc, *lru_args, bw=bw, z_blk=5, name=tag + "lru")
    sv["xc"], sv["lru_saves"], sv["lru_args"], sv["lru_vjp"] = xc, lru_saves, lru_args, lru_vjp
    prep, s5_vjp = jax.vjp(_s5_prep, prm["ssm_log_dt"], prm["ssm_a_re"], prm["ssm_a_im"], prm["ssm_b_re"],
                           prm["ssm_b_im"], prm["ssm_c_re"], prm["ssm_c_im"], prm["ssm_d"])
    u_blk0 = 6 * bw // LANES
    yg, s5_saves = _s5_fwd(p, prep, bw=bw, u_blk0=u_blk0, name=tag + "s5")
    pre = _mm(yg, full["ssm_w_glu"], bias=prm["ssm_b_glu"].reshape(1, 2 * bw), name=tag + "glu_proj")
    o_c = _glu_fwd(pre, p, bw=bw, z_blk=7, name=tag + "glu")
    sv["prep"], sv["s5_vjp"], sv["s5_saves"], sv["yg"], sv["pre"] = prep, s5_vjp, s5_saves, yg, pre
    mem_n = _rms_fwd(mem, prm["mem_norm_w"], name=tag + "mem_norm")
    kv = _mm(mem_n, full["w_kv"], name=tag + "kv_proj")
    o_d = _attn_fwd(p, kv, bw=bw, q_blk=8, z_blk=9, name=tag + "attn")
    sv["mem_n"], sv["kv"] = mem_n, kv
    o4 = jnp.stack([o_a, o_b, o_c, o_d], axis=0)
    bg = full["b_gate"].reshape(N_BRANCH, 1, d)
    g_blk = 10 * bw // rank
    merged = _merge_fwd(p, o4, full["w_gate"], bg, full["w_branch"], rank=rank, g_blk=g_blk, name=tag + "merge")
    sv["o4"], sv["bg"], sv["merged"] = o4, bg, merged
    return _mm(merged, full["w_out"], add=x, name=tag + "out_proj"), sv, full, dn_res


def _layer_bwd(dx_out, mem, sv, full, prm, l, dn_comm_of):
    x, p, h = sv["x"], sv["p"], sv["h"]
    s, d = x.shape
    bw = d // N_BRANCH
    heads = bw // DN_HEAD_DIM
    rank = full["w_gate"].shape[1]
    tag = f"l{l}b_"
    big, small = {}, {}
    dmerged = _mm(dx_out, full["w_out"], tb=True, out_dtype=BF16, name=tag + "out_dx")
    big["w_out"] = _mm(sv["merged"], dx_out, ta=True, out_dtype=BF16, name=tag + "out_dw")
    g_blk = 10 * bw // rank
    dpre, dbr, dbg = _merge_bwd(p, sv["o4"], full["w_gate"], sv["bg"], full["w_branch"], dmerged, rank=rank,
                                g_blk=g_blk, name=tag + "merge")
    small["b_gate"] = dbg.reshape(N_BRANCH, d)
    glow = p[:, 10 * bw:10 * bw + rank].astype(BF16)
    dglow = None
    dwg, dwb, d_o = [], [], []
    for n in range(N_BRANCH):
        dglow = _mm(dpre, full["w_gate"], la=n, lb=n, tb=True, add=dglow, name=tag + f"gate_dx{n}")
        dwg.append(_mm(glow, dpre, ta=True, lb=n, out_dtype=BF16, nsplit=4, name=tag + f"gate_dw{n}"))
        d_o.append(_mm(dbr, full["w_branch"], la=n, lb=n, tb=True, name=tag + f"branch_dx{n}"))
        dwb.append(_mm(sv["o4"], dbr, ta=True, la=n, lb=n, out_dtype=BF16, nsplit=4, name=tag + f"branch_dw{n}"))
    dn_comm = dn_comm_of([_chip_rows(big["w_out"]), *dwg, *dwb])
    ba_blk = (10 * bw + rank) // BA_PAD
    dn = _dn_bwd(p, sv["conv_a"], sv["dn_states"], d_o[0], prm["dn_a_log"], prm["dn_dt_bias"], prm["dn_norm_w"],
                 bw=bw, ba_blk=ba_blk, comm=dn_comm, name=tag + "dn")
    (dconv, dz_a, dba, dal, ddt, dnw), dn_res = dn if dn_comm is not None else (dn, None)
    small["dn_a_log"] = dal[0, heads:2 * heads]
    small["dn_dt_bias"] = ddt[0, heads:2 * heads]
    small["dn_norm_w"] = dnw[0]
    dqkv, dw8_a, _ = _conv_bwd(p, 0, 3 * bw, _w8(full["dn_conv_w"]), dconv, name=tag + "dn_conv")
    small["dn_conv_w"] = dw8_a[:CONV_WIDTH]
    dxc, dz_b, dwr, dwi, dbr_, dbi_, dlam = _lru_bwd(p, sv["xc"], sv["lru_saves"], d_o[1], *sv["lru_args"], bw=bw,
                                                     z_blk=5, name=tag + "lru")
    small["lru_w_r"], small["lru_w_i"] = sv["lru_vjp"]((dwr, dwi))
    small["lru_b_r"], small["lru_b_i"], small["lru_lambda"] = dbr_[0], dbi_[0], dlam[0]
    dlx, dw8_b, dcb = _conv_bwd(p, 4, bw, _w8(full["lru_conv_w"]), dxc, name=tag + "lru_conv")
    small["lru_conv_w"] = dw8_b[:CONV_WIDTH]
    small["lru_conv_b"] = dcb[0]
    dpre_glu, dz_c, dbglu = _glu_bwd(sv["pre"], p, d_o[2], bw=bw, z_blk=7, name=tag + "glu")
    small["ssm_b_glu"] = dbglu[0]
    dyg = _mm(dpre_glu, full["ssm_w_glu"], tb=True, name=tag + "glu_dx")
    big["ssm_w_glu"] = _mm(sv["yg"], dpre_glu, ta=True, out_dtype=BF16, nsplit=4, name=tag + "glu_dw")
    s5 = _s5_bwd(p, sv["prep"], sv["s5_saves"], dyg, bw=bw, u_blk0=6 * bw // LANES, name=tag + "s5")
    du = s5[0]
    (small["ssm_log_dt"], small["ssm_a_re"], small["ssm_a_im"], small["ssm_b_re"], small["ssm_b_im"],
     small["ssm_c_re"], small["ssm_c_im"], small["ssm_d"]) = sv["s5_vjp"](tuple(s5[1:]))
    dq, dz_d, dkv = _attn_bwd(p, sv["kv"], d_o[3], bw=bw, q_blk=8, z_blk=9, name=tag + "attn")
    big["w_kv"] = _mm(sv["mem_n"], dkv, ta=True, out_dtype=BF16, name=tag + "kv_dw")
    dmem_n = _mm(dkv, full["w_kv"], tb=True, name=tag + "kv_dx")
    _, dmnw = _rms_bwd(mem, prm["mem_norm_w"], dmem_n, None, name=tag + "mem_norm")
    small["mem_norm_w"] = dmnw[0]
    dp = jnp.concatenate([dqkv, dz_a, dlx, dz_b, du, dz_c, dq, dz_d, dglow, dba], axis=1).astype(BF16)
    dh = _mm(dp, sv["w_in"], tb=True, name=tag + "in_dx")
    dw_in = _w_in_unlayout(_mm(h, dp, ta=True, out_dtype=BF16, name=tag + "in_dw"), bw, heads, rank)
    dx, dnw_in = _rms_bwd(x, prm["norm_w"], dh, dx_out, name=tag + "norm")
    small["norm_w"] = dnw_in[0]
    late = [dw_in.reshape(d, 4, dw_in.shape[1] // 4).transpose(1, 0, 2), big["ssm_w_glu"], _chip_rows(big["w_kv"])]
    return dx, late, small, dn_res


def _step(wts, mom, vel, x, mem, target):
    depth = wts["norm_w"].shape[0]
    xi, yi, ci = lax.axis_index("x"), lax.axis_index("y"), lax.axis_index("c")
    c_idx = ci.astype(jnp.int32).reshape(1)
    chip = (2 * xi + yi).astype(jnp.int32)
    chip_idx = chip.reshape(1)
    x, mem, target = x[0], mem[0], target[0]

    prms = [{n: wts[n][l] for n in SMALL} for l in range(depth)]
    w_in_local = [wts["w_in"][l].astype(BF16) for l in range(depth)]
    w_in_g = _run_comm(_ag4([w_in_local[0]]), name="gather_w_in0")[0]
    saves, fulls = [], []
    act = x
    for l in range(depth):
        nxt = _ag4([w_in_local[l + 1]]) if l + 1 < depth else None
        act, sv, full, res = _layer_fwd(act, mem, _own_slot(w_in_g, w_in_local[l], chip), wts, prms[l], l, chip, nxt)
        saves.append(sv)
        fulls.append(full)
        w_in_g = res[0] if nxt is not None else None
    loss_part, dx, dfw = _loss_head(act, wts["final_norm_w"], target, name="loss_head")
    loss = lax.psum(loss_part[0, 0], ("x", "y", "c"))

    big_g = [None] * depth
    small_g = [None] * depth
    pending = []
    done = {}
    for l in reversed(range(depth)):
        def dn_comm_of(early, l=l):
            pending.append((l, "early", _rs_pair(early, c_idx, name=f"scatter_g{l}a")))
            return _to_chips([t for _, _, h4 in pending for t in h4])

        dx, late, small_g[l], res = _layer_bwd(dx, mem, saves[l], fulls[l], prms[l], l, dn_comm_of)
        off = 0
        for ll, grp, h4 in pending:
            done[ll, grp] = _rs_close(h4, res[off:off + len(h4)], c_idx, chip_idx, name=f"scatter_g{ll}{grp[0]}c")
            off += len(h4)
        pending = [(l, "late", _rs_pair(late, c_idx, name=f"scatter_g{l}b"))]
    (l, grp, h4), = pending
    done[l, grp] = _rs_close(h4, _run_comm(_to_chips(h4), name=f"scatter_g{l}b_ici"), c_idx, chip_idx,
                             name=f"scatter_g{l}lc")
    for l in range(depth):
        e, t = done[l, "early"], done[l, "late"]
        big_g[l] = {"w_in": t[0], "ssm_w_glu": t[1], "w_kv": t[2], "w_out": e[0],
                    "w_gate": jnp.stack(e[1:1 + N_BRANCH], axis=0),
                    "w_branch": jnp.stack(e[1 + N_BRANCH:1 + 2 * N_BRANCH], axis=0)}

    names = SMALL + tuple(n for n, _ in SMALL_SHARDED)
    flat = jnp.concatenate([small_g[l][n].reshape(-1) for l in range(depth) for n in names] + [dfw.reshape(-1)])
    red = _all_reduce(_pack_rows(flat, (), 256), chip, name="reduce_small").reshape(-1)
    grads, off = {n: [] for n in names}, 0
    for l in range(depth):
        for n in names:
            shp = small_g[l][n].shape
            sz = math.prod(shp)
            grads[n].append(red[off:off + sz].reshape(shp))
            off += sz
    grads = {n: jnp.stack(v, axis=0) for n, v in grads.items()}
    grads["final_norm_w"] = red[off:off + dfw.size].reshape(wts["final_norm_w"].shape)
    for n, ax in SMALL_SHARDED:
        width = wts[n].shape[-1]
        grads[n] = lax.dynamic_slice_in_dim(grads[n], chip * width, width, axis=ax + 1)
    for n, _ in BIG:
        grads[n] = jnp.stack([big_g[l][n] for l in range(depth)], axis=0)

    delta, new_m, new_v = {}, {}, {}
    for n, _ in BIG:
        shp = wts[n].shape
        two = lambda t: t.reshape(-1, shp[-1])
        dlt, nm, nv = _adamw(two(wts[n]), two(grads[n]), two(mom[n]), two(vel[n]), name="adamw_" + n)
        delta[n], new_m[n], new_v[n] = dlt.reshape(shp), nm.reshape(shp), nv.reshape(shp)
    rest = [n for n in WEIGHTS if n not in dict(BIG)]
    cat = lambda src: _pack_rows(jnp.concatenate([src[n].reshape(-1) for n in rest]), (), SUBLANES)
    dlt, nm, nv = _adamw(cat(wts), cat(grads), cat(mom), cat(vel), name="adamw_small")
    off = 0
    for n in rest:
        shp = wts[n].shape
        sz = math.prod(shp)
        for dst, src in ((delta, dlt), (new_m, nm), (new_v, nv)):
            dst[n] = src.reshape(-1)[off:off + sz].reshape(shp)
        off += sz
    return (loss, dx[None], *[grads[n] for n in WEIGHTS], *[delta[n] for n in WEIGHTS], *[new_m[n] for n in WEIGHTS],
            *[new_v[n] for n in WEIGHTS])


def kernel(x, mem, norm_w, w_in, dn_conv_w, dn_a_log, dn_dt_bias, dn_norm_w, lru_conv_w, lru_conv_b, lru_w_r, lru_b_r, lru_w_i, lru_b_i, lru_lambda, ssm_log_dt, ssm_a_re, ssm_a_im, ssm_b_re, ssm_b_im, ssm_c_re, ssm_c_im, ssm_d, ssm_w_glu, ssm_b_glu, mem_norm_w, w_kv, w_gate, b_gate, w_branch, w_out, final_norm_w, loss_target, m_norm_w, m_w_in, m_dn_conv_w, m_dn_a_log, m_dn_dt_bias, m_dn_norm_w, m_lru_conv_w, m_lru_conv_b, m_lru_w_r, m_lru_b_r, m_lru_w_i, m_lru_b_i, m_lru_lambda, m_ssm_log_dt, m_ssm_a_re, m_ssm_a_im, m_ssm_b_re, m_ssm_b_im, m_ssm_c_re, m_ssm_c_im, m_ssm_d, m_ssm_w_glu, m_ssm_b_glu, m_mem_norm_w, m_w_kv, m_w_gate, m_b_gate, m_w_branch, m_w_out, m_final_norm_w, v_norm_w, v_w_in, v_dn_conv_w, v_dn_a_log, v_dn_dt_bias, v_dn_norm_w, v_lru_conv_w, v_lru_conv_b, v_lru_w_r, v_lru_b_r, v_lru_w_i, v_lru_b_i, v_lru_lambda, v_ssm_log_dt, v_ssm_a_re, v_ssm_a_im, v_ssm_b_re, v_ssm_b_im, v_ssm_c_re, v_ssm_c_im, v_ssm_d, v_ssm_w_glu, v_ssm_b_glu, v_mem_norm_w, v_w_kv, v_w_gate, v_b_gate, v_w_branch, v_w_out, v_final_norm_w):
    given = dict(locals())
    wts = {n: given[n] for n in WEIGHTS}
    mom = {n: given["m_" + n] for n in WEIGHTS}
    vel = {n: given["v_" + n] for n in WEIGHTS}
    return _step(wts, mom, vel, x, mem, loss_target)
```

```python
import functools
import math

import jax
import jax.numpy as jnp
import numpy as np
from jax import lax
from jax.experimental import pallas as pl
from jax.experimental.pallas import tpu as pltpu

F32 = jnp.float32
BF16 = jnp.bfloat16
HIGHEST = lax.Precision.HIGHEST
MESH_ID = pl.DeviceIdType.MESH

NORM_EPS = 1e-6
CONV_WIDTH = 4
DN_HEAD_DIM = 128
DN_CHUNK = 64
LRU_C = 8.0
MEM_HEADS = 4
N_BRANCH = 4
ADAM_LR, ADAM_B1, ADAM_B2, ADAM_EPS, ADAM_WD, ADAM_STEP = 0.001, 0.9, 0.999, 1e-08, 0.01, 10

LANES = 128
SUBLANES = 8
VMEM_LIMIT = 56 * 2 ** 20
PACK_COLS = 1024
ELEMENTWISE_BLOCK_BYTES = 2 * 2 ** 20
BA_PAD = 256


def _tile(n, pref, align=LANES):
    if n <= pref:
        return n
    t = pref - pref % align
    while t > 0:
        if n % t == 0:
            return t
        t -= align
    return n


class _Comm:
    def __init__(self, inputs, out_shapes, n_sems, n_local, start, finish):
        self.inputs, self.out_shapes, self.n_sems, self.n_local = list(inputs), tuple(out_shapes), n_sems, n_local
        self.start, self.finish = start, finish

    def scratch(self):
        s = [pltpu.SemaphoreType.DMA((self.n_sems,)), pltpu.SemaphoreType.DMA((self.n_sems,))]
        return s + ([pltpu.SemaphoreType.DMA((self.n_local,))] if self.n_local else [])

    def split(self, refs):
        ni, no = len(self.inputs), len(self.out_shapes)
        sems = list(refs[ni + no:]) + ([] if self.n_local else [None])
        return (refs[:ni], refs[ni:ni + no], *sems)


def _pcall(body, *, name, out_shape, grid=(), in_specs=None, out_specs=None, scratch=(), sem=None,
           num_prefetch=0, comm=None):
    params = dict(vmem_limit_bytes=VMEM_LIMIT)
    if sem is not None:
        params["dimension_semantics"] = sem if comm is None else ("arbitrary",) * len(grid)
    scratch = list(scratch)
    if comm is None:
        run_body = body
    else:
        assert not num_prefetch
        single = not isinstance(out_shape, (tuple, list))
        outs = (out_shape,) if single else tuple(out_shape)
        ospecs = (out_specs,) if single else tuple(out_specs)
        n_in, n_out, n_scr = len(in_specs), len(outs), len(scratch)
        n_ci, n_co = len(comm.inputs), len(comm.out_shapes)
        in_specs = list(in_specs) + [HBM_SPEC] * n_ci
        out_shape = outs + comm.out_shapes
        out_specs = ospecs + (HBM_SPEC,) * n_co
        scratch = scratch + comm.scratch()

        def run_body(*refs):
            ins, rest = refs[:n_in], refs[n_in:]
            cins, rest = rest[:n_ci], rest[n_ci:]
            o, rest = rest[:n_out], rest[n_out:]
            couts, rest = rest[:n_co], rest[n_co:]
            cargs = comm.split((*cins, *couts, *rest[n_scr:]))
            first = functools.reduce(jnp.logical_and, [pl.program_id(a) == 0 for a in range(len(grid))])
            last = functools.reduce(jnp.logical_and, [pl.program_id(a) == grid[a] - 1 for a in range(len(grid))])

            @pl.when(first)
            def _():
                comm.start(*cargs)

            body(*ins, *o, *rest[:n_scr])

            @pl.when(last)
            def _():
                comm.finish(*cargs)

    if num_prefetch:
        call = pl.pallas_call(
            run_body, name=name, out_shape=out_shape,
            grid_spec=pltpu.PrefetchScalarGridSpec(num_scalar_prefetch=num_prefetch, grid=grid, in_specs=in_specs,
                                                   out_specs=out_specs, scratch_shapes=scratch),
            compiler_params=pltpu.CompilerParams(**params), interpret=False)
    else:
        call = pl.pallas_call(run_body, name=name, out_shape=out_shape, grid=grid, in_specs=in_specs,
                              out_specs=out_specs, scratch_shapes=scratch,
                              compiler_params=pltpu.CompilerParams(**params), interpret=False)
    if comm is None:
        return call

    def run(*operands):
        res = call(*operands, *comm.inputs)
        return (res[0] if single else tuple(res[:n_out])), tuple(res[n_out:])

    return run


def _run_comm(comm, *, name):
    def body(*refs):
        args = comm.split(refs)
        comm.start(*args)
        comm.finish(*args)

    return pl.pallas_call(body, name=name, out_shape=comm.out_shapes, in_specs=[HBM_SPEC] * len(comm.inputs),
                          out_specs=tuple(HBM_SPEC for _ in comm.out_shapes), scratch_shapes=comm.scratch(),
                          interpret=False)(*comm.inputs)


HBM_SPEC = pl.BlockSpec(memory_space=pl.ANY)


@functools.partial(jax.custom_vjp, nondiff_argnums=(2, 3))
def _bdot(a, b, ca, cb):
    return lax.dot_general(a.astype(BF16), b.astype(BF16), (((ca,), (cb,)), ((), ())), preferred_element_type=F32)


def _bdot_fwd(a, b, ca, cb):
    return _bdot(a, b, ca, cb), (a, b)


def _bdot_bwd(ca, cb, res, ct):
    a, b = res
    da = _bdot(ct, b, 1, 1 - cb) if ca == 1 else _bdot(b, ct, 1 - cb, 1)
    db = _bdot(a, ct, 1 - ca, 0) if cb == 0 else _bdot(ct, a, 0, 1 - ca)
    return da.astype(a.dtype), db.astype(b.dtype)


_bdot.defvjp(_bdot_fwd, _bdot_bwd)


def _split_bf16(a):
    hi = a.astype(BF16)
    return hi, (a - hi.astype(F32)).astype(BF16)


@functools.partial(jax.custom_vjp, nondiff_argnums=(2, 3))
def _xdot(a, b, ca, cb):
    dims = (((ca,), (cb,)), ((), ()))
    ah, al = _split_bf16(a)
    bh, bl = _split_bf16(b)
    dot = lambda p, q: lax.dot_general(p, q, dims, preferred_element_type=F32)
    return dot(ah, bh) + (dot(ah, bl) + dot(al, bh))


def _xdot_fwd(a, b, ca, cb):
    return _xdot(a, b, ca, cb), (a, b)


def _xdot_bwd(ca, cb, res, ct):
    a, b = res
    da = _xdot(ct, b, 1, 1 - cb) if ca == 1 else _xdot(b, ct, 1 - cb, 1)
    db = _xdot(a, ct, 1 - ca, 0) if cb == 0 else _xdot(ct, a, 0, 1 - ca)
    return da, db


_xdot.defvjp(_xdot_fwd, _xdot_bwd)


def _sigmoid(x):
    return 1.0 / (1.0 + jnp.exp(-x))


def _silu(x):
    return x * _sigmoid(x)


def _softplus(x):
    return jnp.maximum(x, 0.0) + jnp.log(1.0 + jnp.exp(-jnp.abs(x)))


def _expm1(x):
    small = x * (1.0 + x * (0.5 + x * (1.0 / 6.0 + x * (1.0 / 24.0 + x * (1.0 / 120.0 + x * (1.0 / 720.0))))))
    return jnp.where(jnp.abs(x) < 0.1, small, jnp.exp(x) - 1.0)


def _gelu(x):
    return 0.5 * x * (1.0 + jnp.tanh(math.sqrt(2.0 / math.pi) * (x + 0.044715 * x * x * x)))


def _rms(x, w):
    var = jnp.mean(x * x, axis=-1, keepdims=True)
    return x * lax.rsqrt(var + NORM_EPS) * w


def _pick_lane(v, idx):
    lane = lax.broadcasted_iota(jnp.int32, v.shape, 1)
    return jnp.sum(jnp.where(lane == idx, v, 0.0), axis=1, keepdims=True)


def _pick_row(v, idx):
    row = lax.broadcasted_iota(jnp.int32, v.shape, 0)
    return jnp.sum(jnp.where(row == idx, v, 0.0), axis=0, keepdims=True)


def _mm(a, b, *, name, ta=False, tb=False, out_dtype=F32, add=None, bias=None, la=None, lb=None, nsplit=None,
        comm=None, tm=1024, tn=1024, tk=2048):
    a2 = a.shape[-2:]
    b2 = b.shape[-2:]
    m, k = (a2[1], a2[0]) if ta else a2
    n = b2[0] if tb else b2[1]
    assert (b2[1] if tb else b2[0]) == k
    tm, tn, tk = _tile(m, tm), _tile(n // (nsplit or 1), tn), _tile(k, tk)
    nk = k // tk

    def a_map(i, j, kk):
        idx = (kk, i) if ta else (i, kk)
        return idx if la is None else (la,) + idx

    def b_map(i, j, kk):
        idx = (j, kk) if tb else (kk, j)
        return idx if lb is None else (lb,) + idx

    a_blk = (tk, tm) if ta else (tm, tk)
    b_blk = (tn, tk) if tb else (tk, tn)
    in_specs = [pl.BlockSpec(a_blk if la is None else (None,) + a_blk, a_map),
                pl.BlockSpec(b_blk if lb is None else (None,) + b_blk, b_map)]
    operands = [a, b]
    if add is not None:
        in_specs.append(pl.BlockSpec((tm, tn), lambda i, j, kk: (i, j)))
        operands.append(add)
    if bias is not None:
        in_specs.append(pl.BlockSpec((1, tn), lambda i, j, kk: (0, j)))
        operands.append(bias)
    dims = (((0 if ta else 1,), (1 if tb else 0,)), ((), ()))

    def body(*refs):
        a_ref, b_ref = refs[0], refs[1]
        rest = list(refs[2:])
        add_ref = rest.pop(0) if add is not None else None
        bias_ref = rest.pop(0) if bias is not None else None
        o_ref, acc_ref = rest
        kk = pl.program_id(2)

        @pl.when(kk == 0)
        def _():
            acc_ref[...] = jnp.zeros_like(acc_ref)

        acc_ref[...] += lax.dot_general(a_ref[...].astype(BF16), b_ref[...].astype(BF16), dims,
                                        preferred_element_type=F32)

        @pl.when(kk == nk - 1)
        def _():
            r = acc_ref[...]
            if add_ref is not None:
                r = r + add_ref[...].astype(F32)
            if bias_ref is not None:
                r = r + bias_ref[...]
            o_ref[...] = r.astype(out_dtype)

    if nsplit is None:
        out_shape = jax.ShapeDtypeStruct((m, n), out_dtype)
        out_spec = pl.BlockSpec((tm, tn), lambda i, j, kk: (i, j))
    else:
        per = n // nsplit // tn
        out_shape = jax.ShapeDtypeStruct((nsplit, m, n // nsplit), out_dtype)
        out_spec = pl.BlockSpec((None, tm, tn), lambda i, j, kk: (j // per, i, j % per))
    return _pcall(body, name=name, comm=comm, out_shape=out_shape, grid=(m // tm, n // tn, nk), in_specs=in_specs,
                  out_specs=out_spec, scratch=[pltpu.VMEM((tm, tn), F32)],
                  sem=("parallel", "parallel", "arbitrary"))(*operands)


def _rms_fwd(x, w, *, name):
    s, d = x.shape
    t = _tile(s, 256, SUBLANES)

    def body(x_ref, w_ref, o_ref):
        o_ref[...] = _rms(x_ref[...], w_ref[...]).astype(BF16)

    return _pcall(body, name=name, out_shape=jax.ShapeDtypeStruct((s, d), BF16), grid=(s // t,),
                  in_specs=[pl.BlockSpec((t, d), lambda i: (i, 0)), pl.BlockSpec((1, d), lambda i: (0, 0))],
                  out_specs=pl.BlockSpec((t, d), lambda i: (i, 0)), sem=("parallel",))(x, w.reshape(1, d))


def _rms_bwd(x, w, dh, res, *, name):
    s, d = x.shape
    t = _tile(s, 256, SUBLANES)

    def body(*refs):
        if res is None:
            x_ref, w_ref, dh_ref, dx_ref, dw_ref = refs
            res_ref = None
        else:
            x_ref, w_ref, dh_ref, res_ref, dx_ref, dw_ref = refs
        _, vjp = jax.vjp(_rms, x_ref[...], w_ref[...])
        dx, dw = vjp(dh_ref[...].astype(F32))
        if res_ref is not None:
            dx = dx + res_ref[...]
        dx_ref[...] = dx

        @pl.when(pl.program_id(0) == 0)
        def _():
            dw_ref[...] = jnp.zeros_like(dw_ref)

        dw_ref[...] += dw

    tok = pl.BlockSpec((t, d), lambda i: (i, 0))
    row = pl.BlockSpec((1, d), lambda i: (0, 0))
    operands = [x, w.reshape(1, d), dh] + ([] if res is None else [res])
    return _pcall(body, name=name,
                  out_shape=(jax.ShapeDtypeStruct((s, d), F32), jax.ShapeDtypeStruct((1, d), F32)), grid=(s // t,),
                  in_specs=[tok, row, tok] + ([] if res is None else [tok]), out_specs=(tok, row),
                  sem=("arbitrary",))(*operands)


def _loss_head(x, w, target, *, name):
    s, d = x.shape
    t = _tile(s, 256, SUBLANES)

    def body(x_ref, w_ref, t_ref, loss_ref, dx_ref, dw_ref):
        def f(xv, wv):
            err = _rms(xv, wv) - t_ref[...]
            return 0.5 * jnp.sum(jnp.mean(err * err, axis=-1))

        val, vjp = jax.vjp(f, x_ref[...], w_ref[...])
        dx, dw = vjp(jnp.ones((), F32))
        dx_ref[...] = dx

        @pl.when(pl.program_id(0) == 0)
        def _():
            dw_ref[...] = jnp.zeros_like(dw_ref)
            loss_ref[...] = jnp.zeros_like(loss_ref)

        dw_ref[...] += dw
        loss_ref[...] += jnp.full(loss_ref.shape, val, F32)

    tok = pl.BlockSpec((t, d), lambda i: (i, 0))
    row = pl.BlockSpec((1, d), lambda i: (0, 0))
    return _pcall(body, name=name,
                  out_shape=(jax.ShapeDtypeStruct((1, LANES), F32), jax.ShapeDtypeStruct((s, d), F32),
                             jax.ShapeDtypeStruct((1, d), F32)),
                  grid=(s // t,), in_specs=[tok, row, tok],
                  out_specs=(pl.BlockSpec((1, LANES), lambda i: (0, 0)), tok, row), sem=("arbitrary",))(
                      x, w.reshape(1, d), target)


def _conv_shifts(prev8, cur, t):
    xp = jnp.concatenate([prev8, cur], axis=0)
    out = []
    for j in range(CONV_WIDTH):
        k = CONV_WIDTH - 1 - j
        out.append(cur if k == 0 else pltpu.roll(xp, k, 0)[SUBLANES:SUBLANES + t])
    return out


def _conv_fwd(p, col_blk, width, w8, b, *, name):
    s = p.shape[0]
    t = _tile(s, 256, SUBLANES)
    r8 = t // SUBLANES

    def body(cur_ref, prev_ref, w_ref, b_ref, y_ref):
        i = pl.program_id(0)
        prev8 = jnp.where(i == 0, 0.0, prev_ref[...])
        sh = _conv_shifts(prev8, cur_ref[...], t)
        w = w_ref[...]
        y = b_ref[...] + sh[0] * w[0:1]
        for j in range(1, CONV_WIDTH):
            y = y + sh[j] * w[j:j + 1]
        y_ref[...] = y

    return _pcall(body, name=name, out_shape=jax.ShapeDtypeStruct((s, width), F32), grid=(s // t,),
                  in_specs=[pl.BlockSpec((t, width), lambda i: (i, col_blk)),
                            pl.BlockSpec((SUBLANES, width), lambda i: (jnp.maximum(i * r8 - 1, 0), col_blk)),
                            pl.BlockSpec((SUBLANES, width), lambda i: (0, 0)),
                            pl.BlockSpec((1, width), lambda i: (0, 0))],
                  out_specs=pl.BlockSpec((t, width), lambda i: (i, 0)), sem=("parallel",))(p, p, w8, b)


def _conv_bwd(p, col_blk, width, w8, dy, *, name):
    s = p.shape[0]
    t = _tile(s, 256, SUBLANES)
    r8 = t // SUBLANES
    nt = s // t

    def body(cur_ref, prev_ref, w_ref, dy_ref, dyn_ref, dx_ref, dw_ref, db_ref):
        i = pl.program_id(0)
        prev8 = jnp.where(i == 0, 0.0, prev_ref[...])
        sh = _conv_shifts(prev8, cur_ref[...], t)
        dy = dy_ref[...]
        next8 = jnp.where(i == nt - 1, 0.0, dyn_ref[...])
        dyp = jnp.concatenate([dy, next8], axis=0)
        w = w_ref[...]
        rows = lax.broadcasted_iota(jnp.int32, (SUBLANES, width), 0)
        dx = dy * w[CONV_WIDTH - 1:CONV_WIDTH]
        dw = jnp.zeros((SUBLANES, width), F32)
        for j in range(CONV_WIDTH):
            k = CONV_WIDTH - 1 - j
            if k:
                dx = dx + pltpu.roll(dyp, t + SUBLANES - k, 0)[0:t] * w[j:j + 1]
            dw = dw + jnp.where(rows == j, jnp.sum(dy * sh[j], axis=0, keepdims=True), 0.0)
        dx_ref[...] = dx.astype(BF16)

        @pl.when(i == 0)
        def _():
            dw_ref[...] = jnp.zeros_like(dw_ref)
            db_ref[...] = jnp.zeros_like(db_ref)

        dw_ref[...] += dw
        db_ref[...] += jnp.sum(dy, axis=0, keepdims=True)

    return _pcall(body, name=name,
                  out_shape=(jax.ShapeDtypeStruct((s, width), BF16), jax.ShapeDtypeStruct((SUBLANES, width), F32),
                             jax.ShapeDtypeStruct((1, width), F32)),
                  grid=(nt,),
                  in_specs=[pl.BlockSpec((t, width), lambda i: (i, col_blk)),
                            pl.BlockSpec((SUBLANES, width), lambda i: (jnp.maximum(i * r8 - 1, 0), col_blk)),
                            pl.BlockSpec((SUBLANES, width), lambda i: (0, 0)),
                            pl.BlockSpec((t, width), lambda i: (i, 0)),
                            pl.BlockSpec((SUBLANES, width), lambda i: (jnp.minimum((i + 1) * r8, s // SUBLANES - 1), 0))],
                  out_specs=(pl.BlockSpec((t, width), lambda i: (i, 0)),
                             pl.BlockSpec((SUBLANES, width), lambda i: (0, 0)),
                             pl.BlockSpec((1, width), lambda i: (0, 0))),
                  sem=("arbitrary",))(p, p, w8, dy, dy)


def _dn_chunk(state, c, z, ba, alog_row, dt_row, nw_row, *, heads, bw):
    cs = c.shape[0]
    hd = DN_HEAD_DIM
    qkv = _silu(c)
    gfull = -jnp.exp(alog_row) * _softplus(ba + dt_row)
    beta_full = _sigmoid(ba)
    ri = lax.broadcasted_iota(jnp.int32, (cs, cs), 0)
    ci = lax.broadcasted_iota(jnp.int32, (cs, cs), 1)
    causal = ri >= ci
    strict = ri > ci
    tril = causal.astype(F32)
    eye = (ri == ci).astype(F32)
    gc = _xdot(tril, gfull, 1, 0)
    gct = _xdot(gfull, tril, 0, 1)
    outs, states = [], []
    for h in range(heads):
        q = qkv[:, h * hd:(h + 1) * hd]
        k = qkv[:, bw + h * hd:bw + (h + 1) * hd]
        v = qkv[:, 2 * bw + h * hd:2 * bw + (h + 1) * hd]
        q = q * lax.rsqrt(jnp.sum(q * q, axis=-1, keepdims=True) + NORM_EPS) * (hd ** -0.5)
        k = k * lax.rsqrt(jnp.sum(k * k, axis=-1, keepdims=True) + NORM_EPS)
        beta = _pick_lane(beta_full, h)
        g_col = _pick_lane(gc, heads + h)
        g_row = _pick_row(gct, heads + h)
        decay = jnp.exp(jnp.where(causal, g_col - g_row, -1e30))
        k_beta = k * beta
        v_beta = v * beta
        kk = _bdot(k_beta, k, 1, 1) * decay
        m = -jnp.where(strict, kk, 0.0)
        tinv = eye + m
        pw = m
        for _ in range(int(math.log2(cs)) - 1):
            pw = _xdot(pw, pw, 1, 0)
            tinv = tinv + _xdot(tinv, pw, 1, 0)
        rhs = jnp.concatenate([v_beta, k_beta * jnp.exp(g_col)], axis=-1)
        sol = _xdot(tinv, rhs, 1, 0)
        u, w = sol[:, :hd], sol[:, hd:]
        qk = jnp.where(causal, _bdot(q, k, 1, 1) * decay, 0.0)
        g_last = _pick_row(g_col, cs - 1)
        k_dec = k * jnp.exp(g_last - g_col)
        q_dec = q * jnp.exp(g_col)
        s_h = state[h]
        v_new = u - _bdot(w, s_h, 1, 0)
        o = _bdot(q_dec, s_h, 1, 0) + _bdot(qk, v_new, 1, 0)
        states.append(s_h * jnp.exp(g_last) + _bdot(k_dec, v_new, 0, 0))
        outs.append(_rms(o, nw_row) * _silu(z[:, h * hd:(h + 1) * hd]))
    return jnp.concatenate(outs, axis=-1), tuple(states)


def _dn_rows(a_log, dt_bias, heads):
    z = jnp.zeros((heads,), F32)
    pad = jnp.zeros((BA_PAD - 2 * heads,), F32)
    return (jnp.concatenate([z, a_log, pad]).reshape(1, BA_PAD), jnp.concatenate([z, dt_bias, pad]).reshape(1, BA_PAD))


def _dn_fwd(p, conv, a_log, dt_bias, norm_w, *, bw, ba_blk, name, comm=None):
    s = p.shape[0]
    heads = bw // DN_HEAD_DIM
    cs = min(DN_CHUNK, s)
    n = s // cs
    hd = DN_HEAD_DIM
    alog_row, dt_row = _dn_rows(a_log, dt_bias, heads)
    fn = functools.partial(_dn_chunk, heads=heads, bw=bw)

    def body(c_ref, z_ref, ba_ref, al_ref, dt_ref, nw_ref, o_ref, save_ref, st_ref):
        @pl.when(pl.program_id(0) == 0)
        def _():
            st_ref[...] = jnp.zeros_like(st_ref)

        save_ref[...] = st_ref[...]
        o, new = fn(tuple(st_ref[h] for h in range(heads)), c_ref[...], z_ref[...], ba_ref[...], al_ref[...],
                    dt_ref[...], nw_ref[...])
        o_ref[...] = o.astype(BF16)
        for h in range(heads):
            st_ref[h] = new[h]

    row = lambda wd: pl.BlockSpec((1, wd), lambda i: (0, 0))
    return _pcall(body, name=name, comm=comm,
                  out_shape=(jax.ShapeDtypeStruct((s, bw), BF16), jax.ShapeDtypeStruct((n, heads, hd, hd), F32)),
                  grid=(n,),
                  in_specs=[pl.BlockSpec((cs, 3 * bw), lambda i: (i, 0)), pl.BlockSpec((cs, bw), lambda i: (i, 3)),
                            pl.BlockSpec((cs, BA_PAD), lambda i: (i, ba_blk)), row(BA_PAD), row(BA_PAD), row(hd)],
                  out_specs=(pl.BlockSpec((cs, bw), lambda i: (i, 0)),
                             pl.BlockSpec((None, heads, hd, hd), lambda i: (i, 0, 0, 0))),
                  scratch=[pltpu.VMEM((heads, hd, hd), F32)], sem=("arbitrary",))(
                      conv, p, p, alog_row, dt_row, norm_w.reshape(1, hd))


def _dn_bwd(p, conv, states, d_o, a_log, dt_bias, norm_w, *, bw, ba_blk, name, comm=None):
    s = p.shape[0]
    heads = bw // DN_HEAD_DIM
    cs = min(DN_CHUNK, s)
    n = s // cs
    hd = DN_HEAD_DIM
    alog_row, dt_row = _dn_rows(a_log, dt_bias, heads)
    fn = functools.partial(_dn_chunk, heads=heads, bw=bw)

    def body(c_ref, z_ref, ba_ref, st_ref, do_ref, al_ref, dt_ref, nw_ref,
             dc_ref, dz_ref, dba_ref, dal_ref, ddt_ref, dnw_ref, dst_ref):
        @pl.when(pl.program_id(0) == 0)
        def _():
            dst_ref[...] = jnp.zeros_like(dst_ref)
            dal_ref[...] = jnp.zeros_like(dal_ref)
            ddt_ref[...] = jnp.zeros_like(ddt_ref)
            dnw_ref[...] = jnp.zeros_like(dnw_ref)

        _, vjp = jax.vjp(fn, tuple(st_ref[h] for h in range(heads)), c_ref[...], z_ref[...], ba_ref[...],
                         al_ref[...], dt_ref[...], nw_ref[...])
        dst, dc, dz, dba, dal, ddt, dnw = vjp((do_ref[...].astype(F32), tuple(dst_ref[h] for h in range(heads))))
        for h in range(heads):
            dst_ref[h] = dst[h]
        dc_ref[...] = dc
        dz_ref[...] = dz.astype(BF16)
        dba_ref[...] = dba.astype(BF16)
        dal_ref[...] += dal
        ddt_ref[...] += ddt
        dnw_ref[...] += dnw

    rev = lambda i: n - 1 - i
    row = lambda wd: pl.BlockSpec((1, wd), lambda i: (0, 0))
    return _pcall(body, name=name, comm=comm,
                  out_shape=(jax.ShapeDtypeStruct((s, 3 * bw), F32), jax.ShapeDtypeStruct((s, bw), BF16),
                             jax.ShapeDtypeStruct((s, BA_PAD), BF16), jax.ShapeDtypeStruct((1, BA_PAD), F32),
                             jax.ShapeDtypeStruct((1, BA_PAD), F32), jax.ShapeDtypeStruct((1, hd), F32)),
                  grid=(n,),
                  in_specs=[pl.BlockSpec((cs, 3 * bw), lambda i: (rev(i), 0)),
                            pl.BlockSpec((cs, bw), lambda i: (rev(i), 3)),
                            pl.BlockSpec((cs, BA_PAD), lambda i: (rev(i), ba_blk)),
                            pl.BlockSpec((None, heads, hd, hd), lambda i: (rev(i), 0, 0, 0)),
                            pl.BlockSpec((cs, bw), lambda i: (rev(i), 0)), row(BA_PAD), row(BA_PAD), row(hd)],
                  out_specs=(pl.BlockSpec((cs, 3 * bw), lambda i: (rev(i), 0)),
                             pl.BlockSpec((cs, bw), lambda i: (rev(i), 0)),
                             pl.BlockSpec((cs, BA_PAD), lambda i: (rev(i), 0)), row(BA_PAD), row(BA_PAD), row(hd)),
                  scratch=[pltpu.VMEM((heads, hd, hd), F32)], sem=("arbitrary",))(
                      conv, p, p, states, d_o, alog_row, dt_row, norm_w.reshape(1, hd))


def _lru_gates(xc, wr, br, wi, bi, lam):
    r = _sigmoid(_bdot(xc, wr, 1, 0) + br)
    i = _sigmoid(_bdot(xc, wi, 1, 0) + bi)
    log_a = -LRU_C * r * _softplus(-lam)
    return jnp.exp(log_a), jnp.sqrt(-_expm1(2.0 * log_a)) * (i * xc)


def _scan_rows(t, step, carry):
    def trip(g, cr):
        base = pl.multiple_of(g * SUBLANES, SUBLANES)
        for r in range(SUBLANES):
            cr = step(base + r, cr)
        return cr
    return lax.fori_loop(0, t // SUBLANES, trip, carry)


def _scan_rows_rev(t, step, carry):
    def trip(g, cr):
        base = pl.multiple_of((t // SUBLANES - 1 - g) * SUBLANES, SUBLANES)
        for r in range(SUBLANES - 1, -1, -1):
            cr = step(base + r, cr)
        return cr
    return lax.fori_loop(0, t // SUBLANES, trip, carry)


def _lru_fwd(p, xc, wr, br, wi, bi, lam, *, bw, z_blk, name):
    s = p.shape[0]
    t = _tile(s, 256, SUBLANES)
    nt = s // t

    def body(xc_ref, z_ref, wr_ref, br_ref, wi_ref, bi_ref, lam_ref, o_ref, save_ref, a_s, b_s, h_s, carry_s):
        @pl.when(pl.program_id(0) == 0)
        def _():
            carry_s[...] = jnp.zeros_like(carry_s)

        a, inp = _lru_gates(xc_ref[...], wr_ref[...], br_ref[...], wi_ref[...], bi_ref[...], lam_ref[...])
        a_s[...] = a
        b_s[...] = inp
        h0 = carry_s[...]
        save_ref[...] = h0

        def step(r, h):
            h = a_s[pl.ds(r, 1), :] * h + b_s[pl.ds(r, 1), :]
            h_s[pl.ds(r, 1), :] = h
            return h

        carry_s[...] = _scan_rows(t, step, h0)
        o_ref[...] = (h_s[...] * _silu(z_ref[...])).astype(BF16)

    tok = pl.BlockSpec((t, bw), lambda i: (i, 0))
    row = pl.BlockSpec((1, bw), lambda i: (0, 0))
    mat = pl.BlockSpec((bw, bw), lambda i: (0, 0))
    return _pcall(body, name=name,
                  out_shape=(jax.ShapeDtypeStruct((s, bw), BF16), jax.ShapeDtypeStruct((nt, 1, bw), F32)), grid=(nt,),
                  in_specs=[tok, pl.BlockSpec((t, bw), lambda i: (i, z_blk)), mat, row, mat, row, row],
                  out_specs=(tok, pl.BlockSpec((None, 1, bw), lambda i: (i, 0, 0))),
                  scratch=[pltpu.VMEM((t, bw), F32)] * 3 + [pltpu.VMEM((1, bw), F32)], sem=("arbitrary",))(
                      xc, p, wr, br, wi, bi, lam)


def _lru_bwd(p, xc, saves, d_o, wr, br, wi, bi, lam, *, bw, z_blk, name):
    s = p.shape[0]
    t = _tile(s, 256, SUBLANES)
    nt = s // t

    def body(xc_ref, z_ref, sv_ref, do_ref, wr_ref, br_ref, wi_ref, bi_ref, lam_ref,
             dxc_ref, dz_ref, dwr_ref, dwi_ref, dbr_ref, dbi_ref, dlam_ref, a_s, b_s, h_s, g_s, carry_s):
        @pl.when(pl.program_id(0) == 0)
        def _():
            carry_s[...] = jnp.zeros_like(carry_s)
            for r in (dwr_ref, dwi_ref, dbr_ref, dbi_ref, dlam_ref):
                r[...] = jnp.zeros_like(r)

        (a, inp), vjp_g = jax.vjp(_lru_gates, xc_ref[...], wr_ref[...], br_ref[...], wi_ref[...], bi_ref[...],
                                  lam_ref[...])
        a_s[...] = a
        b_s[...] = inp
        h0 = sv_ref[...]

        def fstep(r, h):
            h_s[pl.ds(r, 1), :] = h
            return a_s[pl.ds(r, 1), :] * h + b_s[pl.ds(r, 1), :]

        _scan_rows(t, fstep, h0)
        a = a_s[...]
        hs = a * h_s[...] + b_s[...]
        z = z_ref[...]
        d_o = do_ref[...].astype(F32)
        _, vjp_o = jax.vjp(lambda hv, zv: hv * _silu(zv), hs, z)
        dhs, dz = vjp_o(d_o)
        dz_ref[...] = dz.astype(BF16)
        g_s[...] = dhs

        def bstep(r, cr):
            g = g_s[pl.ds(r, 1), :] + cr
            g_s[pl.ds(r, 1), :] = g
            return a_s[pl.ds(r, 1), :] * g

        carry_s[...] = _scan_rows_rev(t, bstep, carry_s[...])
        g = g_s[...]
        dxc, dwr, dbr, dwi, dbi, dlam = vjp_g((g * h_s[...], g))
        dxc_ref[...] = dxc
        dwr_ref[...] += dwr
        dwi_ref[...] += dwi
        dbr_ref[...] += dbr
        dbi_ref[...] += dbi
        dlam_ref[...] += dlam

    rev = lambda i: nt - 1 - i
    tok = pl.BlockSpec((t, bw), lambda i: (rev(i), 0))
    row = pl.BlockSpec((1, bw), lambda i: (0, 0))
    mat = pl.BlockSpec((bw, bw), lambda i: (0, 0))
    sd = jax.ShapeDtypeStruct
    return _pcall(body, name=name,
                  out_shape=(sd((s, bw), F32), sd((s, bw), BF16), sd((bw, bw), F32), sd((bw, bw), F32),
                             sd((1, bw), F32), sd((1, bw), F32), sd((1, bw), F32)),
                  grid=(nt,),
                  in_specs=[tok, pl.BlockSpec((t, bw), lambda i: (rev(i), z_blk)),
                            pl.BlockSpec((None, 1, bw), lambda i: (rev(i), 0, 0)), tok, mat, row, mat, row, row],
                  out_specs=(tok, tok, mat, mat, row, row, row),
                  scratch=[pltpu.VMEM((t, bw), F32)] * 4 + [pltpu.VMEM((1, bw), F32)], sem=("arbitrary",))(
                      xc, p, saves, d_o, wr, br, wi, bi, lam)


def _s5_prep(log_dt, a_re, a_im, b_re, b_im, c_re, c_im, d_skip):
    g, n = a_re.shape
    gs = d_skip.shape[1]
    gpb = LANES // gs
    nb = g // gpb
    dt = jnp.exp(log_dt)[:, None]
    mag = jnp.exp(dt * a_re)
    ab_re = mag * jnp.cos(dt * a_im)
    ab_im = mag * jnp.sin(dt * a_im)
    den = a_re * a_re + a_im * a_im
    f_re = ((ab_re - 1.0) * a_re + ab_im * a_im) / den
    f_im = (ab_im * a_re - (ab_re - 1.0) * a_im) / den
    bb_re = f_re[..., None] * b_re - f_im[..., None] * b_im
    bb_im = f_re[..., None] * b_im + f_im[..., None] * b_re
    eye = jnp.eye(gpb, dtype=F32)

    def b_dense(bb):
        t = bb.reshape(nb, gpb, n, gs)
        return jnp.einsum("bgnc,gh->bgchn", t, eye).reshape(nb, gpb * gs, gpb * n)

    def c_dense(cc):
        t = cc.reshape(nb, gpb, gs, n)
        return jnp.einsum("bgcn,gh->bgnhc", t, eye).reshape(nb, gpb * n, gpb * gs)

    lanes = gpb * n
    sub = lanes // LANES
    return (ab_re.reshape(nb, sub, LANES), ab_im.reshape(nb, sub, LANES), b_dense(bb_re), b_dense(bb_im),
            c_dense(c_re), c_dense(c_im), d_skip.reshape(1, g * gs))


def _s5_out(xre, xim, cre, cim, d, u):
    return _gelu(_bdot(xre, cre, 1, 0) - _bdot(xim, cim, 1, 0) + d * u)


S5_LOOKAHEAD = 4


def _cmul(ar, ai, xr, xi):
    return ar * xr - ai * xi, ar * xi + ai * xr


def _s5_scan(ar, ai, b_re, b_im, t_re, t_im, t, reverse):
    k = S5_LOOKAHEAD
    a2 = _cmul(ar, ai, ar, ai)
    a4 = _cmul(*a2, *a2)
    lo = lambda n, off=0: pl.ds(off, n)
    for (src_re, src_im, dst_re, dst_im, pw, d) in ((b_re, b_im, t_re, t_im, (ar, ai), 1),
                                                     (t_re, t_im, b_re, b_im, a2, 2)):
        keep = lo(d, t - d) if reverse else lo(d)
        cur = lo(t - d) if reverse else lo(t - d, d)
        nbr = lo(t - d, d) if reverse else lo(t - d)
        dst_re[keep] = src_re[keep]
        dst_im[keep] = src_im[keep]
        mr, mi = _cmul(*pw, src_re[nbr], src_im[nbr])
        dst_re[cur] = src_re[cur] + mr
        dst_im[cur] = src_im[cur] + mi

    def step(g, cr):
        row = pl.multiple_of(((t // k - 1 - g) if reverse else g) * k, k)
        mr, mi = _cmul(*a4, *cr)
        nr = mr + b_re[pl.ds(row, k)]
        ni = mi + b_im[pl.ds(row, k)]
        b_re[pl.ds(row, k)] = nr
        b_im[pl.ds(row, k)] = ni
        return nr, ni

    zero = jnp.zeros((k,) + ar.shape, F32)
    xr, xi = lax.fori_loop(0, t // k, step, (zero, zero), unroll=2)
    return (xr[0], xi[0]) if reverse else (xr[k - 1], xi[k - 1])


def _s5_fwd(p, prep, *, bw, u_blk0, name):
    s = p.shape[0]
    are, aim, bre, bim, cre, cim, d = prep
    nb, sub, _ = are.shape
    lanes = sub * LANES
    t = _tile(s, 256, SUBLANES)
    nt = s // t

    def body(u_ref, are_ref, aim_ref, bre_ref, bim_ref, cre_ref, cim_ref, d_ref, y_ref, save_ref,
             bre_s, bim_s, xre_s, xim_s, carry_s):
        @pl.when(pl.program_id(1) == 0)
        def _():
            carry_s[...] = jnp.zeros_like(carry_s)

        u = u_ref[...]
        xre_s[...] = _bdot(u, bre_ref[...], 1, 0).reshape(t, sub, LANES)
        xim_s[...] = _bdot(u, bim_ref[...], 1, 0).reshape(t, sub, LANES)
        ar, ai = are_ref[...], aim_ref[...]
        save_ref[...] = carry_s[...]
        er, ei = _cmul(ar, ai, carry_s[0], carry_s[1])
        xre_s[0] += er
        xim_s[0] += ei
        xr, xi = _s5_scan(ar, ai, xre_s, xim_s, bre_s, bim_s, t, False)
        carry_s[0] = xr
        carry_s[1] = xi
        y_ref[...] = _s5_out(xre_s[...].reshape(t, lanes), xim_s[...].reshape(t, lanes), cre_ref[...], cim_ref[...],
                             d_ref[...], u).astype(BF16)

    vec = pl.BlockSpec((None, sub, LANES), lambda j, i: (j, 0, 0))
    bmat = pl.BlockSpec((None, LANES, lanes), lambda j, i: (j, 0, 0))
    cmat = pl.BlockSpec((None, lanes, LANES), lambda j, i: (j, 0, 0))
    return _pcall(body, name=name,
                  out_shape=(jax.ShapeDtypeStruct((s, bw), BF16), jax.ShapeDtypeStruct((nb, nt, 2, sub, LANES), F32)),
                  grid=(nb, nt),
                  in_specs=[pl.BlockSpec((t, LANES), lambda j, i: (i, u_blk0 + j)), vec, vec, bmat, bmat, cmat, cmat,
                            pl.BlockSpec((1, LANES), lambda j, i: (0, j))],
                  out_specs=(pl.BlockSpec((t, LANES), lambda j, i: (i, j)),
                             pl.BlockSpec((None, None, 2, sub, LANES), lambda j, i: (j, i, 0, 0, 0))),
                  scratch=[pltpu.VMEM((t, sub, LANES), F32)] * 4 + [pltpu.VMEM((2, sub, LANES), F32)],
                  sem=("parallel", "arbitrary"))(p, are, aim, bre, bim, cre, cim, d)


def _s5_bwd(p, prep, saves, dyg, *, bw, u_blk0, name):
    s = p.shape[0]
    are, aim, bre, bim, cre, cim, d = prep
    nb, sub, _ = are.shape
    lanes = sub * LANES
    t = _tile(s, 256, SUBLANES)
    nt = s // t

    def body(u_ref, dy_ref, sv_ref, are_ref, aim_ref, bre_ref, bim_ref, cre_ref, cim_ref, d_ref,
             du_ref, dar_ref, dai_ref, dbre_ref, dbim_ref, dcre_ref, dcim_ref, dd_ref,
             bre_s, bim_s, xre_s, xim_s, tre_s, tim_s, carry_s):
        @pl.when(pl.program_id(1) == 0)
        def _():
            carry_s[...] = jnp.zeros_like(carry_s)
            for r in (dar_ref, dai_ref, dbre_ref, dbim_ref, dcre_ref, dcim_ref, dd_ref):
                r[...] = jnp.zeros_like(r)

        u = u_ref[...]
        xre_s[...] = _bdot(u, bre_ref[...], 1, 0).reshape(t, sub, LANES)
        xim_s[...] = _bdot(u, bim_ref[...], 1, 0).reshape(t, sub, LANES)
        ar, ai = are_ref[...], aim_ref[...]
        er, ei = _cmul(ar, ai, sv_ref[0], sv_ref[1])
        xre_s[0] += er
        xim_s[0] += ei
        _s5_scan(ar, ai, xre_s, xim_s, tre_s, tim_s, t, False)
        _, vjp_o = jax.vjp(_s5_out, xre_s[...].reshape(t, lanes), xim_s[...].reshape(t, lanes), cre_ref[...],
                           cim_ref[...], d_ref[...], u)
        dxre, dxim, dcre, dcim, dd, du = vjp_o(dy_ref[...].astype(F32))
        dcre_ref[...] += dcre.astype(F32)
        dcim_ref[...] += dcim.astype(F32)
        dd_ref[...] += dd
        bre_s[...] = dxre.reshape(t, sub, LANES)
        bim_s[...] = dxim.reshape(t, sub, LANES)
        bre_s[t - 1] += carry_s[0]
        bim_s[t - 1] += carry_s[1]
        g0r, g0i = _s5_scan(ar, -ai, bre_s, bim_s, tre_s, tim_s, t, True)
        carry_s[0], carry_s[1] = _cmul(ar, -ai, g0r, g0i)
        gr, gi = bre_s[pl.ds(1, t - 1)], bim_s[pl.ds(1, t - 1)]
        pr, pi = xre_s[pl.ds(0, t - 1)], xim_s[pl.ds(0, t - 1)]
        dar_ref[...] += jnp.sum(gr * pr + gi * pi, axis=0) + bre_s[0] * sv_ref[0] + bim_s[0] * sv_ref[1]
        dai_ref[...] += jnp.sum(gi * pr - gr * pi, axis=0) + bim_s[0] * sv_ref[0] - bre_s[0] * sv_ref[1]
        dbu_re = bre_s[...].reshape(t, lanes)
        dbu_im = bim_s[...].reshape(t, lanes)
        du_ref[...] = (du + _bdot(dbu_re, bre_ref[...], 1, 1) + _bdot(dbu_im, bim_ref[...], 1, 1)).astype(BF16)
        dbre_ref[...] += _bdot(u, dbu_re, 0, 0)
        dbim_ref[...] += _bdot(u, dbu_im, 0, 0)

    rev = lambda i: nt - 1 - i
    vec = pl.BlockSpec((None, sub, LANES), lambda j, i: (j, 0, 0))
    bmat = pl.BlockSpec((None, LANES, lanes), lambda j, i: (j, 0, 0))
    cmat = pl.BlockSpec((None, lanes, LANES), lambda j, i: (j, 0, 0))
    drow = pl.BlockSpec((1, LANES), lambda j, i: (0, j))
    sd = jax.ShapeDtypeStruct
    return _pcall(body, name=name,
                  out_shape=(sd((s, bw), BF16), sd(are.shape, F32), sd(aim.shape, F32), sd(bre.shape, F32),
                             sd(bim.shape, F32), sd(cre.shape, F32), sd(cim.shape, F32), sd((1, bw), F32)),
                  grid=(nb, nt),
                  in_specs=[pl.BlockSpec((t, LANES), lambda j, i: (rev(i), u_blk0 + j)),
                            pl.BlockSpec((t, LANES), lambda j, i: (rev(i), j)),
                            pl.BlockSpec((None, None, 2, sub, LANES), lambda j, i: (j, rev(i), 0, 0, 0)),
                            vec, vec, bmat, bmat, cmat, cmat, drow],
                  out_specs=(pl.BlockSpec((t, LANES), lambda j, i: (rev(i), j)), vec, vec, bmat, bmat, cmat, cmat, drow),
                  scratch=[pltpu.VMEM((t, sub, LANES), F32)] * 6 + [pltpu.VMEM((2, sub, LANES), F32)],
                  sem=("parallel", "arbitrary"))(p, dyg, saves, are, aim, bre, bim, cre, cim, d)


def _glu_gate(pre, z, bw):
    return pre[:, :bw] * _sigmoid(pre[:, bw:]) * _silu(z)


def _glu_fwd(pre, p, *, bw, z_blk, name):
    s = pre.shape[0]
    t = _tile(s, 256, SUBLANES)

    def body(pre_ref, z_ref, o_ref):
        o_ref[...] = _glu_gate(pre_ref[...], z_ref[...], bw).astype(BF16)

    return _pcall(body, name=name, out_shape=jax.ShapeDtypeStruct((s, bw), BF16), grid=(s // t,),
                  in_specs=[pl.BlockSpec((t, 2 * bw), lambda i: (i, 0)), pl.BlockSpec((t, bw), lambda i: (i, z_blk))],
                  out_specs=pl.BlockSpec((t, bw), lambda i: (i, 0)), sem=("parallel",))(pre, p)


def _glu_bwd(pre, p, d_o, *, bw, z_blk, name):
    s = pre.shape[0]
    t = _tile(s, 256, SUBLANES)

    def body(pre_ref, z_ref, do_ref, dpre_ref, dz_ref, db_ref):
        _, vjp = jax.vjp(functools.partial(_glu_gate, bw=bw), pre_ref[...], z_ref[...])
        dpre, dz = vjp(do_ref[...].astype(F32))
        dpre_ref[...] = dpre.astype(BF16)
        dz_ref[...] = dz.astype(BF16)

        @pl.when(pl.program_id(0) == 0)
        def _():
            db_ref[...] = jnp.zeros_like(db_ref)

        db_ref[...] += jnp.sum(dpre, axis=0, keepdims=True)

    sd = jax.ShapeDtypeStruct
    return _pcall(body, name=name, out_shape=(sd((s, 2 * bw), BF16), sd((s, bw), BF16), sd((1, 2 * bw), F32)),
                  grid=(s // t,),
                  in_specs=[pl.BlockSpec((t, 2 * bw), lambda i: (i, 0)), pl.BlockSpec((t, bw), lambda i: (i, z_blk)),
                            pl.BlockSpec((t, bw), lambda i: (i, 0))],
                  out_specs=(pl.BlockSpec((t, 2 * bw), lambda i: (i, 0)), pl.BlockSpec((t, bw), lambda i: (i, 0)),
                             pl.BlockSpec((1, 2 * bw), lambda i: (0, 0))), sem=("arbitrary",))(pre, p, d_o)


def _attn_tile(q, z, kv, *, bw):
    hd = bw // MEM_HEADS
    outs = []
    for h in range(MEM_HEADS):
        k = kv[:, h * hd:(h + 1) * hd]
        v = kv[:, bw + h * hd:bw + (h + 1) * hd]
        sc = _bdot(q[:, h * hd:(h + 1) * hd], k, 1, 1) * (hd ** -0.5)
        e = jnp.exp(sc - lax.stop_gradient(jnp.max(sc, axis=-1, keepdims=True)))
        prob = e / jnp.sum(e, axis=-1, keepdims=True)
        outs.append(_bdot(prob, v, 1, 0))
    return jnp.concatenate(outs, axis=-1) * _silu(z)


def _attn_fwd(p, kv, *, bw, q_blk, z_blk, name):
    s = p.shape[0]
    m = kv.shape[0]
    t = _tile(s, 256, SUBLANES)

    def body(q_ref, z_ref, kv_ref, o_ref):
        o_ref[...] = _attn_tile(q_ref[...], z_ref[...], kv_ref[...], bw=bw).astype(BF16)

    return _pcall(body, name=name, out_shape=jax.ShapeDtypeStruct((s, bw), BF16), grid=(s // t,),
                  in_specs=[pl.BlockSpec((t, bw), lambda i: (i, q_blk)), pl.BlockSpec((t, bw), lambda i: (i, z_blk)),
                            pl.BlockSpec((m, 2 * bw), lambda i: (0, 0))],
                  out_specs=pl.BlockSpec((t, bw), lambda i: (i, 0)), sem=("parallel",))(p, p, kv)


def _attn_bwd(p, kv, d_o, *, bw, q_blk, z_blk, name):
    s = p.shape[0]
    m = kv.shape[0]
    t = _tile(s, 256, SUBLANES)

    def body(q_ref, z_ref, kv_ref, do_ref, dq_ref, dz_ref, dkv_ref):
        _, vjp = jax.vjp(functools.partial(_attn_tile, bw=bw), q_ref[...], z_ref[...], kv_ref[...])
        dq, dz, dkv = vjp(do_ref[...].astype(F32))
        dq_ref[...] = dq.astype(BF16)
        dz_ref[...] = dz.astype(BF16)

        @pl.when(pl.program_id(0) == 0)
        def _():
            dkv_ref[...] = jnp.zeros_like(dkv_ref)

        dkv_ref[...] += dkv

    sd = jax.ShapeDtypeStruct
    tok = pl.BlockSpec((t, bw), lambda i: (i, 0))
    return _pcall(body, name=name, out_shape=(sd((s, bw), BF16), sd((s, bw), BF16), sd((m, 2 * bw), F32)),
                  grid=(s // t,),
                  in_specs=[pl.BlockSpec((t, bw), lambda i: (i, q_blk)), pl.BlockSpec((t, bw), lambda i: (i, z_blk)),
                            pl.BlockSpec((m, 2 * bw), lambda i: (0, 0)), tok],
                  out_specs=(tok, tok, pl.BlockSpec((m, 2 * bw), lambda i: (0, 0))), sem=("arbitrary",))(p, p, kv, d_o)


def _merge_fwd(p, o4, wg, bg, wb, *, rank, g_blk, name):
    s = p.shape[0]
    _, bw, d = wb.shape
    tm, tn = _tile(s, 512), _tile(d, 512)

    def body(g_ref, o_ref, wg_ref, bg_ref, wb_ref, out_ref):
        g = g_ref[...]
        acc = jnp.zeros((tm, tn), F32)
        for n in range(N_BRANCH):
            gate = _sigmoid(_bdot(g, wg_ref[n], 1, 0) + bg_ref[n])
            acc = acc + gate * _bdot(o_ref[n], wb_ref[n], 1, 0)
        out_ref[...] = acc.astype(BF16)

    return _pcall(body, name=name, out_shape=jax.ShapeDtypeStruct((s, d), BF16), grid=(s // tm, d // tn),
                  in_specs=[pl.BlockSpec((tm, rank), lambda i, j: (i, g_blk)),
                            pl.BlockSpec((N_BRANCH, tm, bw), lambda i, j: (0, i, 0)),
                            pl.BlockSpec((N_BRANCH, rank, tn), lambda i, j: (0, 0, j)),
                            pl.BlockSpec((N_BRANCH, 1, tn), lambda i, j: (0, 0, j)),
                            pl.BlockSpec((N_BRANCH, bw, tn), lambda i, j: (0, 0, j))],
                  out_specs=pl.BlockSpec((tm, tn), lambda i, j: (i, j)), sem=("parallel", "parallel"))(
                      p, o4, wg, bg, wb)


def _merge_bwd(p, o4, wg, bg, wb, dmerged, *, rank, g_blk, name):
    s = p.shape[0]
    _, bw, d = wb.shape
    tm, tn = _tile(s, 512), _tile(d, 512)

    def body(g_ref, o_ref, wg_ref, bg_ref, wb_ref, dm_ref, dpre_ref, dbr_ref, dbg_ref):
        g = g_ref[...]
        dm = dm_ref[...].astype(F32)

        @pl.when(pl.program_id(1) == 0)
        def _():
            dbg_ref[...] = jnp.zeros_like(dbg_ref)

        for n in range(N_BRANCH):
            gate = _sigmoid(_bdot(g, wg_ref[n], 1, 0) + bg_ref[n])
            br = _bdot(o_ref[n], wb_ref[n], 1, 0)
            dpre = dm * br * gate * (1.0 - gate)
            dpre_ref[n] = dpre.astype(BF16)
            dbr_ref[n] = (dm * gate).astype(BF16)
            dbg_ref[n] += jnp.sum(dpre, axis=0, keepdims=True)

    sd = jax.ShapeDtypeStruct
    big = pl.BlockSpec((N_BRANCH, tm, tn), lambda j, i: (0, i, j))
    return _pcall(body, name=name,
                  out_shape=(sd((N_BRANCH, s, d), BF16), sd((N_BRANCH, s, d), BF16), sd((N_BRANCH, 1, d), F32)),
                  grid=(d // tn, s // tm),
                  in_specs=[pl.BlockSpec((tm, rank), lambda j, i: (i, g_blk)),
                            pl.BlockSpec((N_BRANCH, tm, bw), lambda j, i: (0, i, 0)),
                            pl.BlockSpec((N_BRANCH, rank, tn), lambda j, i: (0, 0, j)),
                            pl.BlockSpec((N_BRANCH, 1, tn), lambda j, i: (0, 0, j)),
                            pl.BlockSpec((N_BRANCH, bw, tn), lambda j, i: (0, 0, j)),
                            pl.BlockSpec((tm, tn), lambda j, i: (i, j))],
                  out_specs=(big, big, pl.BlockSpec((N_BRANCH, 1, tn), lambda j, i: (0, 0, j))),
                  sem=("parallel", "arbitrary"))(p, o4, wg, bg, wb, dmerged)


def _adamw(w, g, m, v, *, name):
    r, cdim = w.shape
    tr = _tile(r, 128, SUBLANES)
    bc1 = 1.0 - ADAM_B1 ** ADAM_STEP
    bc2 = 1.0 - ADAM_B2 ** ADAM_STEP

    def body(w_ref, g_ref, m_ref, v_ref, d_ref, nm_ref, nv_ref):
        gv = g_ref[...]
        nm = ADAM_B1 * m_ref[...] + (1.0 - ADAM_B1) * gv
        nv = ADAM_B2 * v_ref[...] + (1.0 - ADAM_B2) * (gv * gv)
        d_ref[...] = -ADAM_LR * ((nm / bc1) / (jnp.sqrt(nv / bc2) + ADAM_EPS) + ADAM_WD * w_ref[...])
        nm_ref[...] = nm
        nv_ref[...] = nv

    blk = pl.BlockSpec((tr, cdim), lambda i: (i, 0))
    sd = jax.ShapeDtypeStruct((r, cdim), F32)
    return _pcall(body, name=name, out_shape=(sd, sd, sd), grid=(r // tr,), in_specs=[blk] * 4, out_specs=(blk,) * 3,
                  sem=("parallel",))(w, g, m, v)


HBM_SPEC = pl.BlockSpec(memory_space=pl.ANY)


def _place():
    x, y, c = lax.axis_index("x"), lax.axis_index("y"), lax.axis_index("c")
    return x, y, c, [(1 - x, y), (x, 1 - y), (1 - x, 1 - y)]


def _rcopy(src, dst, send_sem, recv_sem, device):
    return pltpu.make_async_remote_copy(src_ref=src, dst_ref=dst, send_sem=send_sem, recv_sem=recv_sem,
                                        device_id=device, device_id_type=MESH_ID)


def _comm_call(body, *, name, out_shape, n_in, n_sems, n_local=0):
    scratch = [pltpu.SemaphoreType.DMA((n_sems,)), pltpu.SemaphoreType.DMA((n_sems,))]
    if n_local:
        scratch.append(pltpu.SemaphoreType.DMA((n_local,)))
    multi = isinstance(out_shape, (tuple, list))
    return pl.pallas_call(body, name=name, out_shape=out_shape, in_specs=[HBM_SPEC] * n_in,
                          out_specs=tuple(HBM_SPEC for _ in out_shape) if multi else HBM_SPEC,
                          scratch_shapes=scratch, interpret=False)


def _ag4(items):
    n = len(items)

    def copies(ins, outs, send_sems, recv_sems, _, second_stage):
        x, y, c, chips = _place()
        sibling = (x, y, 1 - c)

        def part(i, px, py, h):
            half = ins[i].shape[0] // 2
            return outs[i].at[2 * px + py, pl.ds(h * half, half)]

        local = []
        first, landed, passed, arrived = [], [], [], []
        for i in range(n):
            half = ins[i].shape[0] // 2
            for j, chip in enumerate(chips):
                sems = (send_sems.at[6 * i + j], recv_sems.at[6 * i + j])
                sems2 = (send_sems.at[6 * i + 3 + j], recv_sems.at[6 * i + 3 + j])
                first.append(_rcopy(ins[i].at[pl.ds(c * half, half)], part(i, x, y, c), *sems, (*chip, c)))
                if second_stage:
                    landed.append(_rcopy(part(i, *chip, c), part(i, *chip, c), *sems, (*chip, c)))
                    passed.append(_rcopy(part(i, *chip, c), part(i, *chip, c), *sems2, sibling))
                    arrived.append(_rcopy(part(i, *chip, 1 - c), part(i, *chip, 1 - c), *sems2, sibling))
        return local, first, landed, passed, arrived

    def start(*refs):
        local, first, _, _, _ = copies(*refs, False)
        for cp in local + first:
            cp.start()

    def finish(*refs):
        local, first, landed, passed, arrived = copies(*refs, True)
        for k in range(3 * n):
            landed[k].wait_recv()
            passed[k].start()
        for cp in arrived:
            cp.wait_recv()
        for cp in first + passed:
            cp.wait_send()
        for cp in local:
            cp.wait()

    return _Comm(items, [jax.ShapeDtypeStruct((4,) + t.shape, t.dtype) for t in items], 6 * n, 0, start, finish)


def _own_slot(g, local, chip):
    return [jnp.where(chip == k, local, g[k]) for k in range(4)]


def _sib_halves(items, *, name):
    n = len(items)

    def body(*refs):
        ins, outs = refs[:n], refs[n:2 * n]
        send_sems, recv_sems = refs[2 * n:]
        x, y, c, _ = _place()
        cps = []
        for i in range(n):
            half = ins[i].shape[1] // 2
            cps.append(_rcopy(ins[i].at[:, pl.ds((1 - c) * half, half)], outs[i], send_sems.at[i], recv_sems.at[i],
                              (x, y, 1 - c)))
        for cp in cps:
            cp.start()
        for cp in cps:
            cp.wait()

    shapes = tuple(jax.ShapeDtypeStruct((4, t.shape[1] // 2) + t.shape[2:], t.dtype) for t in items)
    return _comm_call(body, name=name, out_shape=shapes, n_in=n, n_sems=n)(*items)


def _to_chips(items):
    n = len(items)

    def copies(ins, outs, send_sems, recv_sems, _):
        x, y, c, chips = _place()
        return [_rcopy(ins[i].at[2 * chip[0] + chip[1]], outs[i].at[j], send_sems.at[3 * i + j],
                       recv_sems.at[3 * i + j], (*chip, c)) for i in range(n) for j, chip in enumerate(chips)]

    def start(*refs):
        for cp in copies(*refs):
            cp.start()

    def finish(*refs):
        for cp in copies(*refs):
            cp.wait()

    return _Comm(items, [jax.ShapeDtypeStruct((3,) + t.shape[1:], t.dtype) for t in items], 3 * n, 0, start, finish)


def _join_halves(items, c, *, name):
    n = len(items)

    def body(*refs):
        ins, outs = refs[:n], refs[n:2 * n]
        send_sems, recv_sems = refs[2 * n:]
        x, y, cc, _ = _place()
        sibling = (x, y, 1 - cc)
        cps = []
        for i in range(n):
            a = ins[i].shape[0]
            cps.append(_rcopy(ins[i], outs[i].at[pl.ds(cc * a, a)], send_sems.at[i], recv_sems.at[i], sibling))
        for cp in cps:
            cp.start()
        for i in range(n):
            a = ins[i].shape[0]
            cps[i].wait_send()
            _rcopy(ins[i], outs[i].at[pl.ds((1 - cc) * a, a)], send_sems.at[i], recv_sems.at[i], sibling).wait_recv()

    shapes = tuple(jax.ShapeDtypeStruct((2 * t.shape[0],) + t.shape[1:], t.dtype) for t in items)
    got = _comm_call(body, name=name, out_shape=shapes, n_in=n, n_sems=n)(*items)
    out = []
    for q, g in zip(items, got):
        a = q.shape[0]
        out.append(jnp.concatenate([jnp.where(c == h, q, g[h * a:(h + 1) * a]) for h in range(2)], axis=0))
    return out


def _swap_sibling(buf, *, name):
    def body(b_ref, recv_ref, send_sems, recv_sems):
        x, y, c, _ = _place()
        cp = _rcopy(b_ref, recv_ref, send_sems.at[0], recv_sems.at[0], (x, y, 1 - c))
        cp.start()
        cp.wait()

    return _comm_call(body, name=name, out_shape=jax.ShapeDtypeStruct(buf.shape, buf.dtype), n_in=1, n_sems=1)(buf)


def _gather_chips(buf, *, name):
    n, cdim = buf.shape

    def body(b_ref, out_ref, send_sems, recv_sems):
        x, y, c, chips = _place()
        cps = [_rcopy(b_ref, out_ref.at[2 * x + y], send_sems.at[j], recv_sems.at[j], (*chip, c))
               for j, chip in enumerate(chips)]
        for cp in cps:
            cp.start()
        for j, chip in enumerate(chips):
            slot = out_ref.at[2 * chip[0] + chip[1]]
            _rcopy(slot, slot, send_sems.at[j], recv_sems.at[j], (*chip, c)).wait_recv()
        for cp in cps:
            cp.wait_send()

    return _comm_call(body, name=name, out_shape=jax.ShapeDtypeStruct((4, n, cdim), buf.dtype), n_in=1, n_sems=3)(buf)


def _rows_per_block(rows, cdim, itemsize, align):
    return _tile(rows, max(align, ELEMENTWISE_BLOCK_BYTES // (cdim * itemsize) // align * align), align)


def _sum_sib(g4, recv, c_idx, *, name):
    _, rows, cdim = g4.shape
    half = rows // 2
    tr = _rows_per_block(half, cdim, 2, 16)
    nh = half // tr

    def body(c_ref, g_ref, r_ref, o_ref):
        o_ref[...] = (g_ref[...].astype(F32) + r_ref[...].astype(F32)).astype(o_ref.dtype)

    return _pcall(body, name=name, out_shape=jax.ShapeDtypeStruct((4, half, cdim), g4.dtype), grid=(4, nh),
                  num_prefetch=1,
                  in_specs=[pl.BlockSpec((None, tr, cdim), lambda k, i, c_ref: (k, c_ref[0] * nh + i, 0)),
                            pl.BlockSpec((None, tr, cdim), lambda k, i, c_ref: (k, i, 0))],
                  out_specs=pl.BlockSpec((None, tr, cdim), lambda k, i, c_ref: (k, i, 0)),
                  sem=("parallel", "parallel"))(c_idx, g4, recv)


def _sum_chips(h4, recv3, chip_idx, *, name):
    _, n, cdim = h4.shape
    tr = _rows_per_block(n, cdim, 4, 16)

    def body(k_ref, h_ref, r_ref, o_ref):
        acc = h_ref[...].astype(F32)
        for j in range(3):
            acc = acc + r_ref[j].astype(F32)
        o_ref[...] = acc

    return _pcall(body, name=name, out_shape=jax.ShapeDtypeStruct((n, cdim), F32), grid=(n // tr,), num_prefetch=1,
                  in_specs=[pl.BlockSpec((None, tr, cdim), lambda i, k_ref: (k_ref[0], i, 0)),
                            pl.BlockSpec((3, tr, cdim), lambda i, k_ref: (0, i, 0))],
                  out_specs=pl.BlockSpec((tr, cdim), lambda i, k_ref: (i, 0)), sem=("parallel",))(chip_idx, h4, recv3)


def _add2(a, b, *, name):
    n, cdim = a.shape
    tr = _tile(n, 256, SUBLANES)

    def body(a_ref, b_ref, o_ref):
        o_ref[...] = a_ref[...] + b_ref[...]

    blk = pl.BlockSpec((tr, cdim), lambda i: (i, 0))
    return _pcall(body, name=name, out_shape=jax.ShapeDtypeStruct((n, cdim), F32), grid=(n // tr,), in_specs=[blk, blk],
                  out_specs=blk, sem=("parallel",))(a, b)


def _sum4(x4, *, name):
    _, n, cdim = x4.shape
    tr = _tile(n, 256, SUBLANES)

    def body(x_ref, o_ref):
        o_ref[...] = ((x_ref[0] + x_ref[1]) + x_ref[2]) + x_ref[3]

    return _pcall(body, name=name, out_shape=jax.ShapeDtypeStruct((n, cdim), F32), grid=(n // tr,),
                  in_specs=[pl.BlockSpec((4, tr, cdim), lambda i: (0, i, 0))],
                  out_specs=pl.BlockSpec((tr, cdim), lambda i: (i, 0)), sem=("parallel",))(x4)


def _rs_pair(items, c_idx, *, name):
    recv = _sib_halves(items, name=name + "_sib")
    return [_sum_sib(items[i], recv[i], c_idx, name=f"{name}_sum1_{i}") for i in range(len(items))]


def _rs_close(h4, recv3, c_idx, chip_idx, *, name):
    q = [_sum_chips(h4[i], recv3[i], chip_idx, name=f"{name}_sum2_{i}") for i in range(len(h4))]
    return _join_halves(q, c_idx[0], name=name + "_join")


def _all_reduce(buf, chip, *, name):
    pair = _add2(buf, _swap_sibling(buf, name=name + "_sib"), name=name + "_add")
    slots = jnp.stack(_own_slot(_gather_chips(pair, name=name + "_ici"), pair, chip), axis=0)
    return _sum4(slots, name=name + "_sum")


def _pack_rows(flat, lead, align):
    n = flat.shape[-1]
    unit = PACK_COLS * align
    total = -(-n // unit) * unit
    flat = jnp.pad(flat, [(0, 0)] * len(lead) + [(0, total - n)])
    return flat.reshape(*lead, total // PACK_COLS, PACK_COLS)


BIG = (("w_in", 1), ("ssm_w_glu", 1), ("w_kv", 0), ("w_gate", 2), ("w_branch", 2), ("w_out", 0))
SMALL_SHARDED = (("dn_conv_w", 1), ("lru_conv_w", 1), ("b_gate", 1))
SMALL = ("norm_w", "dn_a_log", "dn_dt_bias", "dn_norm_w", "lru_conv_b", "lru_w_r", "lru_b_r", "lru_w_i", "lru_b_i",
         "lru_lambda", "ssm_log_dt", "ssm_a_re", "ssm_a_im", "ssm_b_re", "ssm_b_im", "ssm_c_re", "ssm_c_im", "ssm_d",
         "ssm_b_glu", "mem_norm_w")
WEIGHTS = ("norm_w", "w_in", "dn_conv_w", "dn_a_log", "dn_dt_bias", "dn_norm_w", "lru_conv_w", "lru_conv_b",
           "lru_w_r", "lru_b_r", "lru_w_i", "lru_b_i", "lru_lambda", "ssm_log_dt", "ssm_a_re", "ssm_a_im", "ssm_b_re",
           "ssm_b_im", "ssm_c_re", "ssm_c_im", "ssm_d", "ssm_w_glu", "ssm_b_glu", "mem_norm_w", "w_kv", "w_gate",
           "b_gate", "w_branch", "w_out", "final_norm_w")


REST = BIG[1:]


def _gather_rest(wts, l):
    small = _pack_rows(jnp.concatenate([wts[n][l].reshape(-1) for n, _ in SMALL_SHARDED]), (), 2 * SUBLANES)
    return _ag4([wts[n][l].astype(BF16) for n, _ in REST] + [small])


def _full_rest(g, local, wts, chip):
    out = {n: jnp.concatenate(_own_slot(g[i], local[i], chip), axis=ax) for i, (n, ax) in enumerate(REST)}
    flat, off = jnp.stack(_own_slot(g[-1], local[-1], chip), axis=0).reshape(4, -1), 0
    for n, ax in SMALL_SHARDED:
        shp = wts[n].shape[1:]
        sz = math.prod(shp)
        out[n] = jnp.concatenate(list(flat[:, off:off + sz].reshape(4, *shp)), axis=ax)
        off += sz
    return out


def _chip_rows(t):
    return t.reshape(4, t.shape[0] // 4, t.shape[1])


def _w_in_layout(w, bw, heads, rank):
    d = w.shape[0]
    ba = 4 * bw
    rest = ba + 2 * heads
    return jnp.concatenate([w[:, :ba], w[:, rest:], w[:, ba:rest], jnp.zeros((d, BA_PAD - 2 * heads), w.dtype)], axis=1)


def _w_in_unlayout(dw, bw, heads, rank):
    ba = 4 * bw
    tail = 10 * bw + rank
    return jnp.concatenate([dw[:, :ba], dw[:, tail:tail + 2 * heads], dw[:, ba:tail]], axis=1)


def _lru_dense(w):
    nb, blk, _ = w.shape
    return jnp.einsum("nij,nm->nimj", w, jnp.eye(nb, dtype=w.dtype)).reshape(nb * blk, nb * blk)


def _w8(w):
    return jnp.concatenate([w, jnp.zeros((SUBLANES - CONV_WIDTH, w.shape[1]), w.dtype)], axis=0)


def _layer_fwd(x, mem, w_in_slots, wts, prm, l, chip, dn_comm):
    s, d = x.shape
    bw = d // N_BRANCH
    heads = bw // DN_HEAD_DIM
    rank = wts["w_gate"].shape[2]
    tag = f"l{l}_"
    sv = {"x": x}
    w_in = _w_in_layout(jnp.concatenate(w_in_slots, axis=1), bw, heads, rank)
    sv["w_in"] = w_in
    h = _rms_fwd(x, prm["norm_w"], name=tag + "norm")
    gather = _gather_rest(wts, l)
    p, rest = _mm(h, w_in, comm=gather, name=tag + "in_proj")
    full = _full_rest(rest, gather.inputs, wts, chip)
    sv["h"], sv["p"] = h, p
    conv_a = _conv_fwd(p, 0, 3 * bw, _w8(full["dn_conv_w"]), jnp.zeros((1, 3 * bw), F32), name=tag + "dn_conv")
    ba_blk = (10 * bw + rank) // BA_PAD
    dn = _dn_fwd(p, conv_a, prm["dn_a_log"], prm["dn_dt_bias"], prm["dn_norm_w"], bw=bw, ba_blk=ba_blk,
                 comm=dn_comm, name=tag + "dn")
    (o_a, dn_states), dn_res = dn if dn_comm is not None else (dn, None)
    sv["conv_a"], sv["dn_states"] = conv_a, dn_states
    xc = _conv_fwd(p, 4, bw, _w8(full["lru_conv_w"]), prm["lru_conv_b"].reshape(1, bw), name=tag + "lru_conv")
    (wr, wi), lru_vjp = jax.vjp(lambda a, b: (_lru_dense(a), _lru_dense(b)), prm["lru_w_r"], prm["lru_w_i"])
    row = lambda v: v.reshape(1, bw)
    lru_args = (wr, row(prm["lru_b_r"]), wi, row(prm["lru_b_i"]), row(prm["lru_lambda"]))
    o_b, lru_saves = _lru_fwd(p, xc, *lru_args, bw=bw, z_blk=5, name=tag + "lru")
    sv["xc"], sv["lru_saves"], sv["lru_args"], sv["lru_vjp"] = xc, lru_saves, lru_args, lru_vjp
    prep, s5_vjp = jax.vjp(_s5_prep, prm["ssm_log_dt"], prm["ssm_a_re"], prm["ssm_a_im"], prm["ssm_b_re"],
                           prm["ssm_b_im"], prm["ssm_c_re"], prm["ssm_c_im"], prm["ssm_d"])
    u_blk0 = 6 * bw // LANES
    yg, s5_saves = _s5_fwd(p, prep, bw=bw, u_blk0=u_blk0, name=tag + "s5")
    pre = _mm(yg, full["ssm_w_glu"], bias=prm["ssm_b_glu"].reshape(1, 2 * bw), name=tag + "glu_proj")
    o_c = _glu_fwd(pre, p, bw=bw, z_blk=7, name=tag + "glu")
    sv["prep"], sv["s5_vjp"], sv["s5_saves"], sv["yg"], sv["pre"] = prep, s5_vjp, s5_saves, yg, pre
    mem_n = _rms_fwd(mem, prm["mem_norm_w"], name=tag + "mem_norm")
    kv = _mm(mem_n, full["w_kv"], name=tag + "kv_proj")
    o_d = _attn_fwd(p, kv, bw=bw, q_blk=8, z_blk=9, name=tag + "attn")
    sv["mem_n"], sv["kv"] = mem_n, kv
    o4 = jnp.stack([o_a, o_b, o_c, o_d], axis=0)
    bg = full["b_gate"].reshape(N_BRANCH, 1, d)
    g_blk = 10 * bw // rank
    merged = _merge_fwd(p, o4, full["w_gate"], bg, full["w_branch"], rank=rank, g_blk=g_blk, name=tag + "merge")
    sv["o4"], sv["bg"], sv["merged"] = o4, bg, merged
    return _mm(merged, full["w_out"], add=x, name=tag + "out_proj"), sv, full, dn_res


def _layer_bwd(dx_out, mem, sv, full, prm, l, dn_comm_of):
    x, p, h = sv["x"], sv["p"], sv["h"]
    s, d = x.shape
    bw = d // N_BRANCH
    heads = bw // DN_HEAD_DIM
    rank = full["w_gate"].shape[1]
    tag = f"l{l}b_"
    big, small = {}, {}
    dmerged = _mm(dx_out, full["w_out"], tb=True, out_dtype=BF16, name=tag + "out_dx")
    big["w_out"] = _mm(sv["merged"], dx_out, ta=True, out_dtype=BF16, name=tag + "out_dw")
    g_blk = 10 * bw // rank
    dpre, dbr, dbg = _merge_bwd(p, sv["o4"], full["w_gate"], sv["bg"], full["w_branch"], dmerged, rank=rank,
                                g_blk=g_blk, name=tag + "merge")
    small["b_gate"] = dbg.reshape(N_BRANCH, d)
    glow = p[:, 10 * bw:10 * bw + rank].astype(BF16)
    dglow = None
    dwg, dwb, d_o = [], [], []
    for n in range(N_BRANCH):
        dglow = _mm(dpre, full["w_gate"], la=n, lb=n, tb=True, add=dglow, name=tag + f"gate_dx{n}",
                    out_dtype=BF16 if n == N_BRANCH - 1 else F32)
        dwg.append(_mm(glow, dpre, ta=True, lb=n, out_dtype=BF16, nsplit=4, name=tag + f"gate_dw{n}"))
        d_o.append(_mm(dbr, full["w_branch"], la=n, lb=n, tb=True, name=tag + f"branch_dx{n}"))
        dwb.append(_mm(sv["o4"], dbr, ta=True, la=n, lb=n, out_dtype=BF16, nsplit=4, name=tag + f"branch_dw{n}"))
    dn_comm = dn_comm_of([_chip_rows(big["w_out"]), *dwg, *dwb])
    ba_blk = (10 * bw + rank) // BA_PAD
    dn = _dn_bwd(p, sv["conv_a"], sv["dn_states"], d_o[0], prm["dn_a_log"], prm["dn_dt_bias"], prm["dn_norm_w"],
                 bw=bw, ba_blk=ba_blk, comm=dn_comm, name=tag + "dn")
    (dconv, dz_a, dba, dal, ddt, dnw), dn_res = dn if dn_comm is not None else (dn, None)
    small["dn_a_log"] = dal[0, heads:2 * heads]
    small["dn_dt_bias"] = ddt[0, heads:2 * heads]
    small["dn_norm_w"] = dnw[0]
    dqkv, dw8_a, _ = _conv_bwd(p, 0, 3 * bw, _w8(full["dn_conv_w"]), dconv, name=tag + "dn_conv")
    small["dn_conv_w"] = dw8_a[:CONV_WIDTH]
    dxc, dz_b, dwr, dwi, dbr_, dbi_, dlam = _lru_bwd(p, sv["xc"], sv["lru_saves"], d_o[1], *sv["lru_args"], bw=bw,
                                                     z_blk=5, name=tag + "lru")
    small["lru_w_r"], small["lru_w_i"] = sv["lru_vjp"]((dwr, dwi))
    small["lru_b_r"], small["lru_b_i"], small["lru_lambda"] = dbr_[0], dbi_[0], dlam[0]
    dlx, dw8_b, dcb = _conv_bwd(p, 4, bw, _w8(full["lru_conv_w"]), dxc, name=tag + "lru_conv")
    small["lru_conv_w"] = dw8_b[:CONV_WIDTH]
    small["lru_conv_b"] = dcb[0]
    dpre_glu, dz_c, dbglu = _glu_bwd(sv["pre"], p, d_o[2], bw=bw, z_blk=7, name=tag + "glu")
    small["ssm_b_glu"] = dbglu[0]
    dyg = _mm(dpre_glu, full["ssm_w_glu"], tb=True, name=tag + "glu_dx")
    big["ssm_w_glu"] = _mm(sv["yg"], dpre_glu, ta=True, out_dtype=BF16, nsplit=4, name=tag + "glu_dw")
    s5 = _s5_bwd(p, sv["prep"], sv["s5_saves"], dyg, bw=bw, u_blk0=6 * bw // LANES, name=tag + "s5")
    du = s5[0]
    (small["ssm_log_dt"], small["ssm_a_re"], small["ssm_a_im"], small["ssm_b_re"], small["ssm_b_im"],
     small["ssm_c_re"], small["ssm_c_im"], small["ssm_d"]) = sv["s5_vjp"](tuple(s5[1:]))
    dq, dz_d, dkv = _attn_bwd(p, sv["kv"], d_o[3], bw=bw, q_blk=8, z_blk=9, name=tag + "attn")
    big["w_kv"] = _mm(sv["mem_n"], dkv, ta=True, out_dtype=BF16, name=tag + "kv_dw")
    dmem_n = _mm(dkv, full["w_kv"], tb=True, name=tag + "kv_dx")
    _, dmnw = _rms_bwd(mem, prm["mem_norm_w"], dmem_n, None, name=tag + "mem_norm")
    small["mem_norm_w"] = dmnw[0]
    dp = jnp.concatenate([dqkv, dz_a, dlx, dz_b, du, dz_c, dq, dz_d, dglow, dba], axis=1).astype(BF16)
    dh = _mm(dp, sv["w_in"], tb=True, name=tag + "in_dx")
    dw_in = _w_in_unlayout(_mm(h, dp, ta=True, out_dtype=BF16, name=tag + "in_dw"), bw, heads, rank)
    dx, dnw_in = _rms_bwd(x, prm["norm_w"], dh, dx_out, name=tag + "norm")
    small["norm_w"] = dnw_in[0]
    late = [dw_in.reshape(d, 4, dw_in.shape[1] // 4).transpose(1, 0, 2), big["ssm_w_glu"], _chip_rows(big["w_kv"])]
    return dx, late, small, dn_res


def _step(wts, mom, vel, x, mem, target):
    depth = wts["norm_w"].shape[0]
    xi, yi, ci = lax.axis_index("x"), lax.axis_index("y"), lax.axis_index("c")
    c_idx = ci.astype(jnp.int32).reshape(1)
    chip = (2 * xi + yi).astype(jnp.int32)
    chip_idx = chip.reshape(1)
    x, mem, target = x[0], mem[0], target[0]

    prms = [{n: wts[n][l] for n in SMALL} for l in range(depth)]
    w_in_local = [wts["w_in"][l].astype(BF16) for l in range(depth)]
    w_in_g = _run_comm(_ag4([w_in_local[0]]), name="gather_w_in0")[0]
    saves, fulls = [], []
    act = x
    for l in range(depth):
        nxt = _ag4([w_in_local[l + 1]]) if l + 1 < depth else None
        act, sv, full, res = _layer_fwd(act, mem, _own_slot(w_in_g, w_in_local[l], chip), wts, prms[l], l, chip, nxt)
        saves.append(sv)
        fulls.append(full)
        w_in_g = res[0] if nxt is not None else None
    loss_part, dx, dfw = _loss_head(act, wts["final_norm_w"], target, name="loss_head")
    loss = lax.psum(loss_part[0, 0], ("x", "y", "c"))

    big_g = [None] * depth
    small_g = [None] * depth
    pending = []
    done = {}
    for l in reversed(range(depth)):
        def dn_comm_of(early, l=l):
            pending.append((l, "early", _rs_pair(early, c_idx, name=f"scatter_g{l}a")))
            return _to_chips([t for _, _, h4 in pending for t in h4])

        dx, late, small_g[l], res = _layer_bwd(dx, mem, saves[l], fulls[l], prms[l], l, dn_comm_of)
        off = 0
        for ll, grp, h4 in pending:
            done[ll, grp] = _rs_close(h4, res[off:off + len(h4)], c_idx, chip_idx, name=f"scatter_g{ll}{grp[0]}c")
            off += len(h4)
        pending = [(l, "late", _rs_pair(late, c_idx, name=f"scatter_g{l}b"))]
    (l, grp, h4), = pending
    done[l, grp] = _rs_close(h4, _run_comm(_to_chips(h4), name=f"scatter_g{l}b_ici"), c_idx, chip_idx,
                             name=f"scatter_g{l}lc")
    for l in range(depth):
        e, t = done[l, "early"], done[l, "late"]
        big_g[l] = {"w_in": t[0], "ssm_w_glu": t[1], "w_kv": t[2], "w_out": e[0],
                    "w_gate": jnp.stack(e[1:1 + N_BRANCH], axis=0),
                    "w_branch": jnp.stack(e[1 + N_BRANCH:1 + 2 * N_BRANCH], axis=0)}

    names = SMALL + tuple(n for n, _ in SMALL_SHARDED)
    flat = jnp.concatenate([small_g[l][n].reshape(-1) for l in range(depth) for n in names] + [dfw.reshape(-1)])
    red = _all_reduce(_pack_rows(flat, (), 256), chip, name="reduce_small").reshape(-1)
    grads, off = {n: [] for n in names}, 0
    for l in range(depth):
        for n in names:
            shp = small_g[l][n].shape
            sz = math.prod(shp)
            grads[n].append(red[off:off + sz].reshape(shp))
            off += sz
    grads = {n: jnp.stack(v, axis=0) for n, v in grads.items()}
    grads["final_norm_w"] = red[off:off + dfw.size].reshape(wts["final_norm_w"].shape)
    for n, ax in SMALL_SHARDED:
        width = wts[n].shape[-1]
        grads[n] = lax.dynamic_slice_in_dim(grads[n], chip * width, width, axis=ax + 1)
    for n, _ in BIG:
        grads[n] = jnp.stack([big_g[l][n] for l in range(depth)], axis=0)

    delta, new_m, new_v = {}, {}, {}
    for n, _ in BIG:
        shp = wts[n].shape
        two = lambda t: t.reshape(-1, shp[-1])
        dlt, nm, nv = _adamw(two(wts[n]), two(grads[n]), two(mom[n]), two(vel[n]), name="adamw_" + n)
        delta[n], new_m[n], new_v[n] = dlt.reshape(shp), nm.reshape(shp), nv.reshape(shp)
    rest = [n for n in WEIGHTS if n not in dict(BIG)]
    cat = lambda src: _pack_rows(jnp.concatenate([src[n].reshape(-1) for n in rest]), (), SUBLANES)
    dlt, nm, nv = _adamw(cat(wts), cat(grads), cat(mom), cat(vel), name="adamw_small")
    off = 0
    for n in rest:
        shp = wts[n].shape
        sz = math.prod(shp)
        for dst, src in ((delta, dlt), (new_m, nm), (new_v, nv)):
            dst[n] = src.reshape(-1)[off:off + sz].reshape(shp)
        off += sz
    return (loss, dx[None], *[grads[n] for n in WEIGHTS], *[delta[n] for n in WEIGHTS], *[new_m[n] for n in WEIGHTS],
            *[new_v[n] for n in WEIGHTS])


def kernel(x, mem, norm_w, w_in, dn_conv_w, dn_a_log, dn_dt_bias, dn_norm_w, lru_conv_w, lru_conv_b, lru_w_r, lru_b_r, lru_w_i, lru_b_i, lru_lambda, ssm_log_dt, ssm_a_re, ssm_a_im, ssm_b_re, ssm_b_im, ssm_c_re, ssm_c_im, ssm_d, ssm_w_glu, ssm_b_glu, mem_norm_w, w_kv, w_gate, b_gate, w_branch, w_out, final_norm_w, loss_target, m_norm_w, m_w_in, m_dn_conv_w, m_dn_a_log, m_dn_dt_bias, m_dn_norm_w, m_lru_conv_w, m_lru_conv_b, m_lru_w_r, m_lru_b_r, m_lru_w_i, m_lru_b_i, m_lru_lambda, m_ssm_log_dt, m_ssm_a_re, m_ssm_a_im, m_ssm_b_re, m_ssm_b_im, m_ssm_c_re, m_ssm_c_im, m_ssm_d, m_ssm_w_glu, m_ssm_b_glu, m_mem_norm_w, m_w_kv, m_w_gate, m_b_gate, m_w_branch, m_w_out, m_final_norm_w, v_norm_w, v_w_in, v_dn_conv_w, v_dn_a_log, v_dn_dt_bias, v_dn_norm_w, v_lru_conv_w, v_lru_conv_b, v_lru_w_r, v_lru_b_r, v_lru_w_i, v_lru_b_i, v_lru_lambda, v_ssm_log_dt, v_ssm_a_re, v_ssm_a_im, v_ssm_b_re, v_ssm_b_im, v_ssm_c_re, v_ssm_c_im, v_ssm_d, v_ssm_w_glu, v_ssm_b_glu, v_mem_norm_w, v_w_kv, v_w_gate, v_b_gate, v_w_branch, v_w_out, v_final_norm_w):
    given = dict(locals())
    wts = {n: given[n] for n in WEIGHTS}
    mom = {n: given["m_" + n] for n in WEIGHTS}
    vel = {n: given["v_" + n] for n in WEIGHTS}
    return _step(wts, mom, vel, x, mem, loss_target)
```

```python
import functools
import math

import jax
import jax.numpy as jnp
import numpy as np
from jax import lax
from jax.experimental import pallas as pl
from jax.experimental.pallas import tpu as pltpu

F32 = jnp.float32
BF16 = jnp.bfloat16
HIGHEST = lax.Precision.HIGHEST
MESH_ID = pl.DeviceIdType.MESH

NORM_EPS = 1e-6
CONV_WIDTH = 4
DN_HEAD_DIM = 128
DN_CHUNK = 64
LRU_C = 8.0
MEM_HEADS = 4
N_BRANCH = 4
ADAM_LR, ADAM_B1, ADAM_B2, ADAM_EPS, ADAM_WD, ADAM_STEP = 0.001, 0.9, 0.999, 1e-08, 0.01, 10

LANES = 128
SUBLANES = 8
VMEM_LIMIT = 56 * 2 ** 20
PACK_COLS = 1024
ELEMENTWISE_BLOCK_BYTES = 2 * 2 ** 20
BA_PAD = 256


def _tile(n, pref, align=LANES):
    if n <= pref:
        return n
    t = pref - pref % align
    while t > 0:
        if n % t == 0:
            return t
        t -= align
    return n


class _Comm:
    def __init__(self, inputs, out_shapes, n_sems, n_local, start, finish):
        self.inputs, self.out_shapes, self.n_sems, self.n_local = list(inputs), tuple(out_shapes), n_sems, n_local
        self.start, self.finish = start, finish

    def scratch(self):
        s = [pltpu.SemaphoreType.DMA((self.n_sems,)), pltpu.SemaphoreType.DMA((self.n_sems,))]
        return s + ([pltpu.SemaphoreType.DMA((self.n_local,))] if self.n_local else [])

    def split(self, refs):
        ni, no = len(self.inputs), len(self.out_shapes)
        sems = list(refs[ni + no:]) + ([] if self.n_local else [None])
        return (refs[:ni], refs[ni:ni + no], *sems)


def _pcall(body, *, name, out_shape, grid=(), in_specs=None, out_specs=None, scratch=(), sem=None,
           num_prefetch=0, comm=None):
    params = dict(vmem_limit_bytes=VMEM_LIMIT)
    if sem is not None:
        params["dimension_semantics"] = sem if comm is None else ("arbitrary",) * len(grid)
    scratch = list(scratch)
    if comm is None:
        run_body = body
    else:
        assert not num_prefetch
        single = not isinstance(out_shape, (tuple, list))
        outs = (out_shape,) if single else tuple(out_shape)
        ospecs = (out_specs,) if single else tuple(out_specs)
        n_in, n_out, n_scr = len(in_specs), len(outs), len(scratch)
        n_ci, n_co = len(comm.inputs), len(comm.out_shapes)
        in_specs = list(in_specs) + [HBM_SPEC] * n_ci
        out_shape = outs + comm.out_shapes
        out_specs = ospecs + (HBM_SPEC,) * n_co
        scratch = scratch + comm.scratch()

        def run_body(*refs):
            ins, rest = refs[:n_in], refs[n_in:]
            cins, rest = rest[:n_ci], rest[n_ci:]
            o, rest = rest[:n_out], rest[n_out:]
            couts, rest = rest[:n_co], rest[n_co:]
            cargs = comm.split((*cins, *couts, *rest[n_scr:]))
            first = functools.reduce(jnp.logical_and, [pl.program_id(a) == 0 for a in range(len(grid))])
            last = functools.reduce(jnp.logical_and, [pl.program_id(a) == grid[a] - 1 for a in range(len(grid))])

            @pl.when(first)
            def _():
                comm.start(*cargs)

            body(*ins, *o, *rest[:n_scr])

            @pl.when(last)
            def _():
                comm.finish(*cargs)

    if num_prefetch:
        call = pl.pallas_call(
            run_body, name=name, out_shape=out_shape,
            grid_spec=pltpu.PrefetchScalarGridSpec(num_scalar_prefetch=num_prefetch, grid=grid, in_specs=in_specs,
                                                   out_specs=out_specs, scratch_shapes=scratch),
            compiler_params=pltpu.CompilerParams(**params), interpret=False)
    else:
        call = pl.pallas_call(run_body, name=name, out_shape=out_shape, grid=grid, in_specs=in_specs,
                              out_specs=out_specs, scratch_shapes=scratch,
                              compiler_params=pltpu.CompilerParams(**params), interpret=False)
    if comm is None:
        return call

    def run(*operands):
        res = call(*operands, *comm.inputs)
        return (res[0] if single else tuple(res[:n_out])), tuple(res[n_out:])

    return run


def _run_comm(comm, *, name):
    def body(*refs):
        args = comm.split(refs)
        comm.start(*args)
        comm.finish(*args)

    return pl.pallas_call(body, name=name, out_shape=comm.out_shapes, in_specs=[HBM_SPEC] * len(comm.inputs),
                          out_specs=tuple(HBM_SPEC for _ in comm.out_shapes), scratch_shapes=comm.scratch(),
                          interpret=False)(*comm.inputs)


HBM_SPEC = pl.BlockSpec(memory_space=pl.ANY)


@functools.partial(jax.custom_vjp, nondiff_argnums=(2, 3))
def _bdot(a, b, ca, cb):
    return lax.dot_general(a.astype(BF16), b.astype(BF16), (((ca,), (cb,)), ((), ())), preferred_element_type=F32)


def _bdot_fwd(a, b, ca, cb):
    return _bdot(a, b, ca, cb), (a, b)


def _bdot_bwd(ca, cb, res, ct):
    a, b = res
    da = _bdot(ct, b, 1, 1 - cb) if ca == 1 else _bdot(b, ct, 1 - cb, 1)
    db = _bdot(a, ct, 1 - ca, 0) if cb == 0 else _bdot(ct, a, 0, 1 - ca)
    return da.astype(a.dtype), db.astype(b.dtype)


_bdot.defvjp(_bdot_fwd, _bdot_bwd)


def _split_bf16(a):
    hi = a.astype(BF16)
    return hi, (a - hi.astype(F32)).astype(BF16)


@functools.partial(jax.custom_vjp, nondiff_argnums=(2, 3))
def _xdot(a, b, ca, cb):
    dims = (((ca,), (cb,)), ((), ()))
    ah, al = _split_bf16(a)
    bh, bl = _split_bf16(b)
    dot = lambda p, q: lax.dot_general(p, q, dims, preferred_element_type=F32)
    return dot(ah, bh) + (dot(ah, bl) + dot(al, bh))


def _xdot_fwd(a, b, ca, cb):
    return _xdot(a, b, ca, cb), (a, b)


def _xdot_bwd(ca, cb, res, ct):
    a, b = res
    da = _xdot(ct, b, 1, 1 - cb) if ca == 1 else _xdot(b, ct, 1 - cb, 1)
    db = _xdot(a, ct, 1 - ca, 0) if cb == 0 else _xdot(ct, a, 0, 1 - ca)
    return da, db


_xdot.defvjp(_xdot_fwd, _xdot_bwd)


def _sigmoid(x):
    return 1.0 / (1.0 + jnp.exp(-x))


def _silu(x):
    return x * _sigmoid(x)


def _softplus(x):
    return jnp.maximum(x, 0.0) + jnp.log(1.0 + jnp.exp(-jnp.abs(x)))


def _expm1(x):
    small = x * (1.0 + x * (0.5 + x * (1.0 / 6.0 + x * (1.0 / 24.0 + x * (1.0 / 120.0 + x * (1.0 / 720.0))))))
    return jnp.where(jnp.abs(x) < 0.1, small, jnp.exp(x) - 1.0)


def _gelu(x):
    return 0.5 * x * (1.0 + jnp.tanh(math.sqrt(2.0 / math.pi) * (x + 0.044715 * x * x * x)))


def _rms(x, w):
    var = jnp.mean(x * x, axis=-1, keepdims=True)
    return x * lax.rsqrt(var + NORM_EPS) * w


def _pick_lane(v, idx):
    lane = lax.broadcasted_iota(jnp.int32, v.shape, 1)
    return jnp.sum(jnp.where(lane == idx, v, 0.0), axis=1, keepdims=True)


def _pick_row(v, idx):
    row = lax.broadcasted_iota(jnp.int32, v.shape, 0)
    return jnp.sum(jnp.where(row == idx, v, 0.0), axis=0, keepdims=True)


def _mm(a, b, *, name, ta=False, tb=False, out_dtype=F32, add=None, bias=None, la=None, lb=None, nsplit=None,
        comm=None, tm=1024, tn=1024, tk=2048):
    a2 = a.shape[-2:]
    b2 = b.shape[-2:]
    m, k = (a2[1], a2[0]) if ta else a2
    n = b2[0] if tb else b2[1]
    assert (b2[1] if tb else b2[0]) == k
    tm, tn, tk = _tile(m, tm), _tile(n // (nsplit or 1), tn), _tile(k, tk)
    nk = k // tk

    def a_map(i, j, kk):
        idx = (kk, i) if ta else (i, kk)
        return idx if la is None else (la,) + idx

    def b_map(i, j, kk):
        idx = (j, kk) if tb else (kk, j)
        return idx if lb is None else (lb,) + idx

    a_blk = (tk, tm) if ta else (tm, tk)
    b_blk = (tn, tk) if tb else (tk, tn)
    in_specs = [pl.BlockSpec(a_blk if la is None else (None,) + a_blk, a_map),
                pl.BlockSpec(b_blk if lb is None else (None,) + b_blk, b_map)]
    operands = [a, b]
    if add is not None:
        in_specs.append(pl.BlockSpec((tm, tn), lambda i, j, kk: (i, j)))
        operands.append(add)
    if bias is not None:
        in_specs.append(pl.BlockSpec((1, tn), lambda i, j, kk: (0, j)))
        operands.append(bias)
    dims = (((0 if ta else 1,), (1 if tb else 0,)), ((), ()))

    def body(*refs):
        a_ref, b_ref = refs[0], refs[1]
        rest = list(refs[2:])
        add_ref = rest.pop(0) if add is not None else None
        bias_ref = rest.pop(0) if bias is not None else None
        o_ref, acc_ref = rest
        kk = pl.program_id(2)

        @pl.when(kk == 0)
        def _():
            acc_ref[...] = jnp.zeros_like(acc_ref)

        acc_ref[...] += lax.dot_general(a_ref[...].astype(BF16), b_ref[...].astype(BF16), dims,
                                        preferred_element_type=F32)

        @pl.when(kk == nk - 1)
        def _():
            r = acc_ref[...]
            if add_ref is not None:
                r = r + add_ref[...].astype(F32)
            if bias_ref is not None:
                r = r + bias_ref[...]
            o_ref[...] = r.astype(out_dtype)

    if nsplit is None:
        out_shape = jax.ShapeDtypeStruct((m, n), out_dtype)
        out_spec = pl.BlockSpec((tm, tn), lambda i, j, kk: (i, j))
    else:
        per = n // nsplit // tn
        out_shape = jax.ShapeDtypeStruct((nsplit, m, n // nsplit), out_dtype)
        out_spec = pl.BlockSpec((None, tm, tn), lambda i, j, kk: (j // per, i, j % per))
    return _pcall(body, name=name, comm=comm, out_shape=out_shape, grid=(m // tm, n // tn, nk), in_specs=in_specs,
                  out_specs=out_spec, scratch=[pltpu.VMEM((tm, tn), F32)],
                  sem=("parallel", "parallel", "arbitrary"))(*operands)


def _rms_fwd(x, w, *, name):
    s, d = x.shape
    t = _tile(s, 256, SUBLANES)

    def body(x_ref, w_ref, o_ref):
        o_ref[...] = _rms(x_ref[...], w_ref[...]).astype(BF16)

    return _pcall(body, name=name, out_shape=jax.ShapeDtypeStruct((s, d), BF16), grid=(s // t,),
                  in_specs=[pl.BlockSpec((t, d), lambda i: (i, 0)), pl.BlockSpec((1, d), lambda i: (0, 0))],
                  out_specs=pl.BlockSpec((t, d), lambda i: (i, 0)), sem=("parallel",))(x, w.reshape(1, d))


def _rms_bwd(x, w, dh, res, *, name):
    s, d = x.shape
    t = _tile(s, 256, SUBLANES)

    def body(*refs):
        if res is None:
            x_ref, w_ref, dh_ref, dx_ref, dw_ref = refs
            res_ref = None
        else:
            x_ref, w_ref, dh_ref, res_ref, dx_ref, dw_ref = refs
        _, vjp = jax.vjp(_rms, x_ref[...], w_ref[...])
        dx, dw = vjp(dh_ref[...].astype(F32))
        if res_ref is not None:
            dx = dx + res_ref[...]
        dx_ref[...] = dx

        @pl.when(pl.program_id(0) == 0)
        def _():
            dw_ref[...] = jnp.zeros_like(dw_ref)

        dw_ref[...] += dw

    tok = pl.BlockSpec((t, d), lambda i: (i, 0))
    row = pl.BlockSpec((1, d), lambda i: (0, 0))
    operands = [x, w.reshape(1, d), dh] + ([] if res is None else [res])
    return _pcall(body, name=name,
                  out_shape=(jax.ShapeDtypeStruct((s, d), F32), jax.ShapeDtypeStruct((1, d), F32)), grid=(s // t,),
                  in_specs=[tok, row, tok] + ([] if res is None else [tok]), out_specs=(tok, row),
                  sem=("arbitrary",))(*operands)


def _loss_head(x, w, target, *, name):
    s, d = x.shape
    t = _tile(s, 256, SUBLANES)

    def body(x_ref, w_ref, t_ref, loss_ref, dx_ref, dw_ref):
        def f(xv, wv):
            err = _rms(xv, wv) - t_ref[...]
            return 0.5 * jnp.sum(jnp.mean(err * err, axis=-1))

        val, vjp = jax.vjp(f, x_ref[...], w_ref[...])
        dx, dw = vjp(jnp.ones((), F32))
        dx_ref[...] = dx

        @pl.when(pl.program_id(0) == 0)
        def _():
            dw_ref[...] = jnp.zeros_like(dw_ref)
            loss_ref[...] = jnp.zeros_like(loss_ref)

        dw_ref[...] += dw
        loss_ref[...] += jnp.full(loss_ref.shape, val, F32)

    tok = pl.BlockSpec((t, d), lambda i: (i, 0))
    row = pl.BlockSpec((1, d), lambda i: (0, 0))
    return _pcall(body, name=name,
                  out_shape=(jax.ShapeDtypeStruct((1, LANES), F32), jax.ShapeDtypeStruct((s, d), F32),
                             jax.ShapeDtypeStruct((1, d), F32)),
                  grid=(s // t,), in_specs=[tok, row, tok],
                  out_specs=(pl.BlockSpec((1, LANES), lambda i: (0, 0)), tok, row), sem=("arbitrary",))(
                      x, w.reshape(1, d), target)


def _conv_shifts(prev8, cur, t):
    xp = jnp.concatenate([prev8, cur], axis=0)
    out = []
    for j in range(CONV_WIDTH):
        k = CONV_WIDTH - 1 - j
        out.append(cur if k == 0 else pltpu.roll(xp, k, 0)[SUBLANES:SUBLANES + t])
    return out


def _conv_fwd(p, col_blk, width, w8, b, *, name):
    s = p.shape[0]
    t = _tile(s, 256, SUBLANES)
    r8 = t // SUBLANES

    def body(cur_ref, prev_ref, w_ref, b_ref, y_ref):
        i = pl.program_id(0)
        prev8 = jnp.where(i == 0, 0.0, prev_ref[...])
        sh = _conv_shifts(prev8, cur_ref[...], t)
        w = w_ref[...]
        y = b_ref[...] + sh[0] * w[0:1]
        for j in range(1, CONV_WIDTH):
            y = y + sh[j] * w[j:j + 1]
        y_ref[...] = y

    return _pcall(body, name=name, out_shape=jax.ShapeDtypeStruct((s, width), F32), grid=(s // t,),
                  in_specs=[pl.BlockSpec((t, width), lambda i: (i, col_blk)),
                            pl.BlockSpec((SUBLANES, width), lambda i: (jnp.maximum(i * r8 - 1, 0), col_blk)),
                            pl.BlockSpec((SUBLANES, width), lambda i: (0, 0)),
                            pl.BlockSpec((1, width), lambda i: (0, 0))],
                  out_specs=pl.BlockSpec((t, width), lambda i: (i, 0)), sem=("parallel",))(p, p, w8, b)


def _conv_bwd(p, col_blk, width, w8, dy, *, name):
    s = p.shape[0]
    t = _tile(s, 256, SUBLANES)
    r8 = t // SUBLANES
    nt = s // t

    def body(cur_ref, prev_ref, w_ref, dy_ref, dyn_ref, dx_ref, dw_ref, db_ref):
        i = pl.program_id(0)
        prev8 = jnp.where(i == 0, 0.0, prev_ref[...])
        sh = _conv_shifts(prev8, cur_ref[...], t)
        dy = dy_ref[...]
        next8 = jnp.where(i == nt - 1, 0.0, dyn_ref[...])
        dyp = jnp.concatenate([dy, next8], axis=0)
        w = w_ref[...]
        rows = lax.broadcasted_iota(jnp.int32, (SUBLANES, width), 0)
        dx = dy * w[CONV_WIDTH - 1:CONV_WIDTH]
        dw = jnp.zeros((SUBLANES, width), F32)
        for j in range(CONV_WIDTH):
            k = CONV_WIDTH - 1 - j
            if k:
                dx = dx + pltpu.roll(dyp, t + SUBLANES - k, 0)[0:t] * w[j:j + 1]
            dw = dw + jnp.where(rows == j, jnp.sum(dy * sh[j], axis=0, keepdims=True), 0.0)
        dx_ref[...] = dx.astype(BF16)

        @pl.when(i == 0)
        def _():
            dw_ref[...] = jnp.zeros_like(dw_ref)
            db_ref[...] = jnp.zeros_like(db_ref)

        dw_ref[...] += dw
        db_ref[...] += jnp.sum(dy, axis=0, keepdims=True)

    return _pcall(body, name=name,
                  out_shape=(jax.ShapeDtypeStruct((s, width), BF16), jax.ShapeDtypeStruct((SUBLANES, width), F32),
                             jax.ShapeDtypeStruct((1, width), F32)),
                  grid=(nt,),
                  in_specs=[pl.BlockSpec((t, width), lambda i: (i, col_blk)),
                            pl.BlockSpec((SUBLANES, width), lambda i: (jnp.maximum(i * r8 - 1, 0), col_blk)),
                            pl.BlockSpec((SUBLANES, width), lambda i: (0, 0)),
                            pl.BlockSpec((t, width), lambda i: (i, 0)),
                            pl.BlockSpec((SUBLANES, width), lambda i: (jnp.minimum((i + 1) * r8, s // SUBLANES - 1), 0))],
                  out_specs=(pl.BlockSpec((t, width), lambda i: (i, 0)),
                             pl.BlockSpec((SUBLANES, width), lambda i: (0, 0)),
                             pl.BlockSpec((1, width), lambda i: (0, 0))),
                  sem=("arbitrary",))(p, p, w8, dy, dy)


def _dn_chunk(state, c, z, ba, alog_row, dt_row, nw_row, *, heads, bw):
    cs = c.shape[0]
    hd = DN_HEAD_DIM
    qkv = _silu(c)
    gfull = -jnp.exp(alog_row) * _softplus(ba + dt_row)
    beta_full = _sigmoid(ba)
    ri = lax.broadcasted_iota(jnp.int32, (cs, cs), 0)
    ci = lax.broadcasted_iota(jnp.int32, (cs, cs), 1)
    causal = ri >= ci
    strict = ri > ci
    tril = causal.astype(F32)
    eye = (ri == ci).astype(F32)
    gc = _xdot(tril, gfull, 1, 0)
    gct = _xdot(gfull, tril, 0, 1)
    outs, states = [], []
    for h in range(heads):
        q = qkv[:, h * hd:(h + 1) * hd]
        k = qkv[:, bw + h * hd:bw + (h + 1) * hd]
        v = qkv[:, 2 * bw + h * hd:2 * bw + (h + 1) * hd]
        q = q * lax.rsqrt(jnp.sum(q * q, axis=-1, keepdims=True) + NORM_EPS) * (hd ** -0.5)
        k = k * lax.rsqrt(jnp.sum(k * k, axis=-1, keepdims=True) + NORM_EPS)
        beta = _pick_lane(beta_full, h)
        g_col = _pick_lane(gc, heads + h)
        g_row = _pick_row(gct, heads + h)
        decay = jnp.exp(jnp.where(causal, g_col - g_row, -1e30))
        k_beta = k * beta
        v_beta = v * beta
        kk = _bdot(k_beta, k, 1, 1) * decay
        m = -jnp.where(strict, kk, 0.0)
        tinv = eye + m
        pw = m
        for _ in range(int(math.log2(cs)) - 1):
            pw = _xdot(pw, pw, 1, 0)
            tinv = tinv + _xdot(tinv, pw, 1, 0)
        rhs = jnp.concatenate([v_beta, k_beta * jnp.exp(g_col)], axis=-1)
        sol = _xdot(tinv, rhs, 1, 0)
        u, w = sol[:, :hd], sol[:, hd:]
        qk = jnp.where(causal, _bdot(q, k, 1, 1) * decay, 0.0)
        g_last = _pick_row(g_col, cs - 1)
        k_dec = k * jnp.exp(g_last - g_col)
        q_dec = q * jnp.exp(g_col)
        s_h = state[h]
        v_new = u - _bdot(w, s_h, 1, 0)
        o = _bdot(q_dec, s_h, 1, 0) + _bdot(qk, v_new, 1, 0)
        states.append(s_h * jnp.exp(g_last) + _bdot(k_dec, v_new, 0, 0))
        outs.append(_rms(o, nw_row) * _silu(z[:, h * hd:(h + 1) * hd]))
    return jnp.concatenate(outs, axis=-1), tuple(states)


def _dn_rows(a_log, dt_bias, heads):
    z = jnp.zeros((heads,), F32)
    pad = jnp.zeros((BA_PAD - 2 * heads,), F32)
    return (jnp.concatenate([z, a_log, pad]).reshape(1, BA_PAD), jnp.concatenate([z, dt_bias, pad]).reshape(1, BA_PAD))


def _dn_fwd(p, conv, a_log, dt_bias, norm_w, *, bw, ba_blk, name, comm=None):
    s = p.shape[0]
    heads = bw // DN_HEAD_DIM
    cs = min(DN_CHUNK, s)
    n = s // cs
    hd = DN_HEAD_DIM
    alog_row, dt_row = _dn_rows(a_log, dt_bias, heads)
    fn = functools.partial(_dn_chunk, heads=heads, bw=bw)

    def body(c_ref, z_ref, ba_ref, al_ref, dt_ref, nw_ref, o_ref, save_ref, st_ref):
        @pl.when(pl.program_id(0) == 0)
        def _():
            st_ref[...] = jnp.zeros_like(st_ref)

        save_ref[...] = st_ref[...]
        o, new = fn(tuple(st_ref[h] for h in range(heads)), c_ref[...], z_ref[...], ba_ref[...], al_ref[...],
                    dt_ref[...], nw_ref[...])
        o_ref[...] = o.astype(BF16)
        for h in range(heads):
            st_ref[h] = new[h]

    row = lambda wd: pl.BlockSpec((1, wd), lambda i: (0, 0))
    return _pcall(body, name=name, comm=comm,
                  out_shape=(jax.ShapeDtypeStruct((s, bw), BF16), jax.ShapeDtypeStruct((n, heads, hd, hd), F32)),
                  grid=(n,),
                  in_specs=[pl.BlockSpec((cs, 3 * bw), lambda i: (i, 0)), pl.BlockSpec((cs, bw), lambda i: (i, 3)),
                            pl.BlockSpec((cs, BA_PAD), lambda i: (i, ba_blk)), row(BA_PAD), row(BA_PAD), row(hd)],
                  out_specs=(pl.BlockSpec((cs, bw), lambda i: (i, 0)),
                             pl.BlockSpec((None, heads, hd, hd), lambda i: (i, 0, 0, 0))),
                  scratch=[pltpu.VMEM((heads, hd, hd), F32)], sem=("arbitrary",))(
                      conv, p, p, alog_row, dt_row, norm_w.reshape(1, hd))


def _dn_bwd(p, conv, states, d_o, a_log, dt_bias, norm_w, *, bw, ba_blk, name, comm=None):
    s = p.shape[0]
    heads = bw // DN_HEAD_DIM
    cs = min(DN_CHUNK, s)
    n = s // cs
    hd = DN_HEAD_DIM
    alog_row, dt_row = _dn_rows(a_log, dt_bias, heads)
    fn = functools.partial(_dn_chunk, heads=heads, bw=bw)

    def body(c_ref, z_ref, ba_ref, st_ref, do_ref, al_ref, dt_ref, nw_ref,
             dc_ref, dz_ref, dba_ref, dal_ref, ddt_ref, dnw_ref, dst_ref):
        @pl.when(pl.program_id(0) == 0)
        def _():
            dst_ref[...] = jnp.zeros_like(dst_ref)
            dal_ref[...] = jnp.zeros_like(dal_ref)
            ddt_ref[...] = jnp.zeros_like(ddt_ref)
            dnw_ref[...] = jnp.zeros_like(dnw_ref)

        _, vjp = jax.vjp(fn, tuple(st_ref[h] for h in range(heads)), c_ref[...], z_ref[...], ba_ref[...],
                         al_ref[...], dt_ref[...], nw_ref[...])
        dst, dc, dz, dba, dal, ddt, dnw = vjp((do_ref[...].astype(F32), tuple(dst_ref[h] for h in range(heads))))
        for h in range(heads):
            dst_ref[h] = dst[h]
        dc_ref[...] = dc
        dz_ref[...] = dz.astype(BF16)
        dba_ref[...] = dba.astype(BF16)
        dal_ref[...] += dal
        ddt_ref[...] += ddt
        dnw_ref[...] += dnw

    rev = lambda i: n - 1 - i
    row = lambda wd: pl.BlockSpec((1, wd), lambda i: (0, 0))
    return _pcall(body, name=name, comm=comm,
                  out_shape=(jax.ShapeDtypeStruct((s, 3 * bw), F32), jax.ShapeDtypeStruct((s, bw), BF16),
                             jax.ShapeDtypeStruct((s, BA_PAD), BF16), jax.ShapeDtypeStruct((1, BA_PAD), F32),
                             jax.ShapeDtypeStruct((1, BA_PAD), F32), jax.ShapeDtypeStruct((1, hd), F32)),
                  grid=(n,),
                  in_specs=[pl.BlockSpec((cs, 3 * bw), lambda i: (rev(i), 0)),
                            pl.BlockSpec((cs, bw), lambda i: (rev(i), 3)),
                            pl.BlockSpec((cs, BA_PAD), lambda i: (rev(i), ba_blk)),
                            pl.BlockSpec((None, heads, hd, hd), lambda i: (rev(i), 0, 0, 0)),
                            pl.BlockSpec((cs, bw), lambda i: (rev(i), 0)), row(BA_PAD), row(BA_PAD), row(hd)],
                  out_specs=(pl.BlockSpec((cs, 3 * bw), lambda i: (rev(i), 0)),
                             pl.BlockSpec((cs, bw), lambda i: (rev(i), 0)),
                             pl.BlockSpec((cs, BA_PAD), lambda i: (rev(i), 0)), row(BA_PAD), row(BA_PAD), row(hd)),
                  scratch=[pltpu.VMEM((heads, hd, hd), F32)], sem=("arbitrary",))(
                      conv, p, p, states, d_o, alog_row, dt_row, norm_w.reshape(1, hd))


def _lru_gates(xc, wr, br, wi, bi, lam):
    r = _sigmoid(_bdot(xc, wr, 1, 0) + br)
    i = _sigmoid(_bdot(xc, wi, 1, 0) + bi)
    log_a = -LRU_C * r * _softplus(-lam)
    return jnp.exp(log_a), jnp.sqrt(-_expm1(2.0 * log_a)) * (i * xc)


def _scan_rows(t, step, carry):
    def trip(g, cr):
        base = pl.multiple_of(g * SUBLANES, SUBLANES)
        for r in range(SUBLANES):
            cr = step(base + r, cr)
        return cr
    return lax.fori_loop(0, t // SUBLANES, trip, carry)


def _scan_rows_rev(t, step, carry):
    def trip(g, cr):
        base = pl.multiple_of((t // SUBLANES - 1 - g) * SUBLANES, SUBLANES)
        for r in range(SUBLANES - 1, -1, -1):
            cr = step(base + r, cr)
        return cr
    return lax.fori_loop(0, t // SUBLANES, trip, carry)


def _lru_fwd(p, xc, wr, br, wi, bi, lam, *, bw, z_blk, name):
    s = p.shape[0]
    t = _tile(s, 256, SUBLANES)
    nt = s // t

    def body(xc_ref, z_ref, wr_ref, br_ref, wi_ref, bi_ref, lam_ref, o_ref, save_ref, a_s, b_s, h_s, carry_s):
        @pl.when(pl.program_id(0) == 0)
        def _():
            carry_s[...] = jnp.zeros_like(carry_s)

        a, inp = _lru_gates(xc_ref[...], wr_ref[...], br_ref[...], wi_ref[...], bi_ref[...], lam_ref[...])
        a_s[...] = a
        b_s[...] = inp
        h0 = carry_s[...]
        save_ref[...] = h0

        def step(r, h):
            h = a_s[pl.ds(r, 1), :] * h + b_s[pl.ds(r, 1), :]
            h_s[pl.ds(r, 1), :] = h
            return h

        carry_s[...] = _scan_rows(t, step, h0)
        o_ref[...] = (h_s[...] * _silu(z_ref[...])).astype(BF16)

    tok = pl.BlockSpec((t, bw), lambda i: (i, 0))
    row = pl.BlockSpec((1, bw), lambda i: (0, 0))
    mat = pl.BlockSpec((bw, bw), lambda i: (0, 0))
    return _pcall(body, name=name,
                  out_shape=(jax.ShapeDtypeStruct((s, bw), BF16), jax.ShapeDtypeStruct((nt, 1, bw), F32)), grid=(nt,),
                  in_specs=[tok, pl.BlockSpec((t, bw), lambda i: (i, z_blk)), mat, row, mat, row, row],
                  out_specs=(tok, pl.BlockSpec((None, 1, bw), lambda i: (i, 0, 0))),
                  scratch=[pltpu.VMEM((t, bw), F32)] * 3 + [pltpu.VMEM((1, bw), F32)], sem=("arbitrary",))(
                      xc, p, wr, br, wi, bi, lam)


def _lru_bwd(p, xc, saves, d_o, wr, br, wi, bi, lam, *, bw, z_blk, name):
    s = p.shape[0]
    t = _tile(s, 256, SUBLANES)
    nt = s // t

    def body(xc_ref, z_ref, sv_ref, do_ref, wr_ref, br_ref, wi_ref, bi_ref, lam_ref,
             dxc_ref, dz_ref, dwr_ref, dwi_ref, dbr_ref, dbi_ref, dlam_ref, a_s, b_s, h_s, g_s, carry_s):
        @pl.when(pl.program_id(0) == 0)
        def _():
            carry_s[...] = jnp.zeros_like(carry_s)
            for r in (dwr_ref, dwi_ref, dbr_ref, dbi_ref, dlam_ref):
                r[...] = jnp.zeros_like(r)

        (a, inp), vjp_g = jax.vjp(_lru_gates, xc_ref[...], wr_ref[...], br_ref[...], wi_ref[...], bi_ref[...],
                                  lam_ref[...])
        a_s[...] = a
        b_s[...] = inp
        h0 = sv_ref[...]

        def fstep(r, h):
            h_s[pl.ds(r, 1), :] = h
            return a_s[pl.ds(r, 1), :] * h + b_s[pl.ds(r, 1), :]

        _scan_rows(t, fstep, h0)
        a = a_s[...]
        hs = a * h_s[...] + b_s[...]
        z = z_ref[...]
        d_o = do_ref[...].astype(F32)
        _, vjp_o = jax.vjp(lambda hv, zv: hv * _silu(zv), hs, z)
        dhs, dz = vjp_o(d_o)
        dz_ref[...] = dz.astype(BF16)
        g_s[...] = dhs

        def bstep(r, cr):
            g = g_s[pl.ds(r, 1), :] + cr
            g_s[pl.ds(r, 1), :] = g
            return a_s[pl.ds(r, 1), :] * g

        carry_s[...] = _scan_rows_rev(t, bstep, carry_s[...])
        g = g_s[...]
        dxc, dwr, dbr, dwi, dbi, dlam = vjp_g((g * h_s[...], g))
        dxc_ref[...] = dxc
        dwr_ref[...] += dwr
        dwi_ref[...] += dwi
        dbr_ref[...] += dbr
        dbi_ref[...] += dbi
        dlam_ref[...] += dlam

    rev = lambda i: nt - 1 - i
    tok = pl.BlockSpec((t, bw), lambda i: (rev(i), 0))
    row = pl.BlockSpec((1, bw), lambda i: (0, 0))
    mat = pl.BlockSpec((bw, bw), lambda i: (0, 0))
    sd = jax.ShapeDtypeStruct
    return _pcall(body, name=name,
                  out_shape=(sd((s, bw), F32), sd((s, bw), BF16), sd((bw, bw), F32), sd((bw, bw), F32),
                             sd((1, bw), F32), sd((1, bw), F32), sd((1, bw), F32)),
                  grid=(nt,),
                  in_specs=[tok, pl.BlockSpec((t, bw), lambda i: (rev(i), z_blk)),
                            pl.BlockSpec((None, 1, bw), lambda i: (rev(i), 0, 0)), tok, mat, row, mat, row, row],
                  out_specs=(tok, tok, mat, mat, row, row, row),
                  scratch=[pltpu.VMEM((t, bw), F32)] * 4 + [pltpu.VMEM((1, bw), F32)], sem=("arbitrary",))(
                      xc, p, saves, d_o, wr, br, wi, bi, lam)


def _s5_prep(log_dt, a_re, a_im, b_re, b_im, c_re, c_im, d_skip):
    g, n = a_re.shape
    gs = d_skip.shape[1]
    gpb = LANES // gs
    nb = g // gpb
    dt = jnp.exp(log_dt)[:, None]
    mag = jnp.exp(dt * a_re)
    ab_re = mag * jnp.cos(dt * a_im)
    ab_im = mag * jnp.sin(dt * a_im)
    den = a_re * a_re + a_im * a_im
    f_re = ((ab_re - 1.0) * a_re + ab_im * a_im) / den
    f_im = (ab_im * a_re - (ab_re - 1.0) * a_im) / den
    bb_re = f_re[..., None] * b_re - f_im[..., None] * b_im
    bb_im = f_re[..., None] * b_im + f_im[..., None] * b_re
    eye = jnp.eye(gpb, dtype=F32)

    def b_dense(bb):
        t = bb.reshape(nb, gpb, n, gs)
        return jnp.einsum("bgnc,gh->bgchn", t, eye).reshape(nb, gpb * gs, gpb * n)

    def c_dense(cc):
        t = cc.reshape(nb, gpb, gs, n)
        return jnp.einsum("bgcn,gh->bgnhc", t, eye).reshape(nb, gpb * n, gpb * gs)

    lanes = gpb * n
    sub = lanes // LANES
    return (ab_re.reshape(nb, sub, LANES), ab_im.reshape(nb, sub, LANES), b_dense(bb_re), b_dense(bb_im),
            c_dense(c_re), c_dense(c_im), d_skip.reshape(1, g * gs))


def _s5_out(xre, xim, cre, cim, d, u):
    return _gelu(_bdot(xre, cre, 1, 0) - _bdot(xim, cim, 1, 0) + d * u)


S5_LOOKAHEAD = 4


def _cmul(ar, ai, xr, xi):
    return ar * xr - ai * xi, ar * xi + ai * xr


def _s5_scan(ar, ai, b_re, b_im, t_re, t_im, t, reverse):
    k = S5_LOOKAHEAD
    a2 = _cmul(ar, ai, ar, ai)
    a4 = _cmul(*a2, *a2)
    lo = lambda n, off=0: pl.ds(off, n)
    for (src_re, src_im, dst_re, dst_im, pw, d) in ((b_re, b_im, t_re, t_im, (ar, ai), 1),
                                                     (t_re, t_im, b_re, b_im, a2, 2)):
        keep = lo(d, t - d) if reverse else lo(d)
        cur = lo(t - d) if reverse else lo(t - d, d)
        nbr = lo(t - d, d) if reverse else lo(t - d)
        dst_re[keep] = src_re[keep]
        dst_im[keep] = src_im[keep]
        mr, mi = _cmul(*pw, src_re[nbr], src_im[nbr])
        dst_re[cur] = src_re[cur] + mr
        dst_im[cur] = src_im[cur] + mi

    def step(g, cr):
        row = pl.multiple_of(((t // k - 1 - g) if reverse else g) * k, k)
        mr, mi = _cmul(*a4, *cr)
        nr = mr + b_re[pl.ds(row, k)]
        ni = mi + b_im[pl.ds(row, k)]
        b_re[pl.ds(row, k)] = nr
        b_im[pl.ds(row, k)] = ni
        return nr, ni

    zero = jnp.zeros((k,) + ar.shape, F32)
    xr, xi = lax.fori_loop(0, t // k, step, (zero, zero), unroll=2)
    return (xr[0], xi[0]) if reverse else (xr[k - 1], xi[k - 1])


def _s5_fwd(p, prep, *, bw, u_blk0, name):
    s = p.shape[0]
    are, aim, bre, bim, cre, cim, d = prep
    nb, sub, _ = are.shape
    lanes = sub * LANES
    t = _tile(s, 256, SUBLANES)
    nt = s // t

    def body(u_ref, are_ref, aim_ref, bre_ref, bim_ref, cre_ref, cim_ref, d_ref, y_ref, save_ref,
             bre_s, bim_s, xre_s, xim_s, carry_s):
        @pl.when(pl.program_id(1) == 0)
        def _():
            carry_s[...] = jnp.zeros_like(carry_s)

        u = u_ref[...]
        xre_s[...] = _bdot(u, bre_ref[...], 1, 0).reshape(t, sub, LANES)
        xim_s[...] = _bdot(u, bim_ref[...], 1, 0).reshape(t, sub, LANES)
        ar, ai = are_ref[...], aim_ref[...]
        save_ref[...] = carry_s[...]
        er, ei = _cmul(ar, ai, carry_s[0], carry_s[1])
        xre_s[0] += er
        xim_s[0] += ei
        xr, xi = _s5_scan(ar, ai, xre_s, xim_s, bre_s, bim_s, t, False)
        carry_s[0] = xr
        carry_s[1] = xi
        y_ref[...] = _s5_out(xre_s[...].reshape(t, lanes), xim_s[...].reshape(t, lanes), cre_ref[...], cim_ref[...],
                             d_ref[...], u).astype(BF16)

    vec = pl.BlockSpec((None, sub, LANES), lambda j, i: (j, 0, 0))
    bmat = pl.BlockSpec((None, LANES, lanes), lambda j, i: (j, 0, 0))
    cmat = pl.BlockSpec((None, lanes, LANES), lambda j, i: (j, 0, 0))
    return _pcall(body, name=name,
                  out_shape=(jax.ShapeDtypeStruct((s, bw), BF16), jax.ShapeDtypeStruct((nb, nt, 2, sub, LANES), F32)),
                  grid=(nb, nt),
                  in_specs=[pl.BlockSpec((t, LANES), lambda j, i: (i, u_blk0 + j)), vec, vec, bmat, bmat, cmat, cmat,
                            pl.BlockSpec((1, LANES), lambda j, i: (0, j))],
                  out_specs=(pl.BlockSpec((t, LANES), lambda j, i: (i, j)),
                             pl.BlockSpec((None, None, 2, sub, LANES), lambda j, i: (j, i, 0, 0, 0))),
                  scratch=[pltpu.VMEM((t, sub, LANES), F32)] * 4 + [pltpu.VMEM((2, sub, LANES), F32)],
                  sem=("parallel", "arbitrary"))(p, are, aim, bre, bim, cre, cim, d)


def _s5_bwd(p, prep, saves, dyg, *, bw, u_blk0, name):
    s = p.shape[0]
    are, aim, bre, bim, cre, cim, d = prep
    nb, sub, _ = are.shape
    lanes = sub * LANES
    t = _tile(s, 256, SUBLANES)
    nt = s // t

    def body(u_ref, dy_ref, sv_ref, are_ref, aim_ref, bre_ref, bim_ref, cre_ref, cim_ref, d_ref,
             du_ref, dar_ref, dai_ref, dbre_ref, dbim_ref, dcre_ref, dcim_ref, dd_ref,
             bre_s, bim_s, xre_s, xim_s, tre_s, tim_s, carry_s):
        @pl.when(pl.program_id(1) == 0)
        def _():
            carry_s[...] = jnp.zeros_like(carry_s)
            for r in (dar_ref, dai_ref, dbre_ref, dbim_ref, dcre_ref, dcim_ref, dd_ref):
                r[...] = jnp.zeros_like(r)

        u = u_ref[...]
        xre_s[...] = _bdot(u, bre_ref[...], 1, 0).reshape(t, sub, LANES)
        xim_s[...] = _bdot(u, bim_ref[...], 1, 0).reshape(t, sub, LANES)
        ar, ai = are_ref[...], aim_ref[...]
        er, ei = _cmul(ar, ai, sv_ref[0], sv_ref[1])
        xre_s[0] += er
        xim_s[0] += ei
        _s5_scan(ar, ai, xre_s, xim_s, tre_s, tim_s, t, False)
        _, vjp_o = jax.vjp(_s5_out, xre_s[...].reshape(t, lanes), xim_s[...].reshape(t, lanes), cre_ref[...],
                           cim_ref[...], d_ref[...], u)
        dxre, dxim, dcre, dcim, dd, du = vjp_o(dy_ref[...].astype(F32))
        dcre_ref[...] += dcre.astype(F32)
        dcim_ref[...] += dcim.astype(F32)
        dd_ref[...] += dd
        bre_s[...] = dxre.reshape(t, sub, LANES)
        bim_s[...] = dxim.reshape(t, sub, LANES)
        bre_s[t - 1] += carry_s[0]
        bim_s[t - 1] += carry_s[1]
        g0r, g0i = _s5_scan(ar, -ai, bre_s, bim_s, tre_s, tim_s, t, True)
        carry_s[0], carry_s[1] = _cmul(ar, -ai, g0r, g0i)
        gr, gi = bre_s[pl.ds(1, t - 1)], bim_s[pl.ds(1, t - 1)]
        pr, pi = xre_s[pl.ds(0, t - 1)], xim_s[pl.ds(0, t - 1)]
        dar_ref[...] += jnp.sum(gr * pr + gi * pi, axis=0) + bre_s[0] * sv_ref[0] + bim_s[0] * sv_ref[1]
        dai_ref[...] += jnp.sum(gi * pr - gr * pi, axis=0) + bim_s[0] * sv_ref[0] - bre_s[0] * sv_ref[1]
        dbu_re = bre_s[...].reshape(t, lanes)
        dbu_im = bim_s[...].reshape(t, lanes)
        du_ref[...] = (du + _bdot(dbu_re, bre_ref[...], 1, 1) + _bdot(dbu_im, bim_ref[...], 1, 1)).astype(BF16)
        dbre_ref[...] += _bdot(u, dbu_re, 0, 0)
        dbim_ref[...] += _bdot(u, dbu_im, 0, 0)

    rev = lambda i: nt - 1 - i
    vec = pl.BlockSpec((None, sub, LANES), lambda j, i: (j, 0, 0))
    bmat = pl.BlockSpec((None, LANES, lanes), lambda j, i: (j, 0, 0))
    cmat = pl.BlockSpec((None, lanes, LANES), lambda j, i: (j, 0, 0))
    drow = pl.BlockSpec((1, LANES), lambda j, i: (0, j))
    sd = jax.ShapeDtypeStruct
    return _pcall(body, name=name,
                  out_shape=(sd((s, bw), BF16), sd(are.shape, F32), sd(aim.shape, F32), sd(bre.shape, F32),
                             sd(bim.shape, F32), sd(cre.shape, F32), sd(cim.shape, F32), sd((1, bw), F32)),
                  grid=(nb, nt),
                  in_specs=[pl.BlockSpec((t, LANES), lambda j, i: (rev(i), u_blk0 + j)),
                            pl.BlockSpec((t, LANES), lambda j, i: (rev(i), j)),
                            pl.BlockSpec((None, None, 2, sub, LANES), lambda j, i: (j, rev(i), 0, 0, 0)),
                            vec, vec, bmat, bmat, cmat, cmat, drow],
                  out_specs=(pl.BlockSpec((t, LANES), lambda j, i: (rev(i), j)), vec, vec, bmat, bmat, cmat, cmat, drow),
                  scratch=[pltpu.VMEM((t, sub, LANES), F32)] * 6 + [pltpu.VMEM((2, sub, LANES), F32)],
                  sem=("parallel", "arbitrary"))(p, dyg, saves, are, aim, bre, bim, cre, cim, d)


def _glu_gate(pre, z, bw):
    return pre[:, :bw] * _sigmoid(pre[:, bw:]) * _silu(z)


def _glu_fwd(pre, p, *, bw, z_blk, name):
    s = pre.shape[0]
    t = _tile(s, 256, SUBLANES)

    def body(pre_ref, z_ref, o_ref):
        o_ref[...] = _glu_gate(pre_ref[...], z_ref[...], bw).astype(BF16)

    return _pcall(body, name=name, out_shape=jax.ShapeDtypeStruct((s, bw), BF16), grid=(s // t,),
                  in_specs=[pl.BlockSpec((t, 2 * bw), lambda i: (i, 0)), pl.BlockSpec((t, bw), lambda i: (i, z_blk))],
                  out_specs=pl.BlockSpec((t, bw), lambda i: (i, 0)), sem=("parallel",))(pre, p)


def _glu_bwd(pre, p, d_o, *, bw, z_blk, name):
    s = pre.shape[0]
    t = _tile(s, 256, SUBLANES)

    def body(pre_ref, z_ref, do_ref, dpre_ref, dz_ref, db_ref):
        _, vjp = jax.vjp(functools.partial(_glu_gate, bw=bw), pre_ref[...], z_ref[...])
        dpre, dz = vjp(do_ref[...].astype(F32))
        dpre_ref[...] = dpre.astype(BF16)
        dz_ref[...] = dz.astype(BF16)

        @pl.when(pl.program_id(0) == 0)
        def _():
            db_ref[...] = jnp.zeros_like(db_ref)

        db_ref[...] += jnp.sum(dpre, axis=0, keepdims=True)

    sd = jax.ShapeDtypeStruct
    return _pcall(body, name=name, out_shape=(sd((s, 2 * bw), BF16), sd((s, bw), BF16), sd((1, 2 * bw), F32)),
                  grid=(s // t,),
                  in_specs=[pl.BlockSpec((t, 2 * bw), lambda i: (i, 0)), pl.BlockSpec((t, bw), lambda i: (i, z_blk)),
                            pl.BlockSpec((t, bw), lambda i: (i, 0))],
                  out_specs=(pl.BlockSpec((t, 2 * bw), lambda i: (i, 0)), pl.BlockSpec((t, bw), lambda i: (i, 0)),
                             pl.BlockSpec((1, 2 * bw), lambda i: (0, 0))), sem=("arbitrary",))(pre, p, d_o)


def _attn_tile(q, z, kv, *, bw):
    hd = bw // MEM_HEADS
    outs = []
    for h in range(MEM_HEADS):
        k = kv[:, h * hd:(h + 1) * hd]
        v = kv[:, bw + h * hd:bw + (h + 1) * hd]
        sc = _bdot(q[:, h * hd:(h + 1) * hd], k, 1, 1) * (hd ** -0.5)
        e = jnp.exp(sc - lax.stop_gradient(jnp.max(sc, axis=-1, keepdims=True)))
        prob = e / jnp.sum(e, axis=-1, keepdims=True)
        outs.append(_bdot(prob, v, 1, 0))
    return jnp.concatenate(outs, axis=-1) * _silu(z)


def _attn_fwd(p, kv, *, bw, q_blk, z_blk, name):
    s = p.shape[0]
    m = kv.shape[0]
    t = _tile(s, 256, SUBLANES)

    def body(q_ref, z_ref, kv_ref, o_ref):
        o_ref[...] = _attn_tile(q_ref[...], z_ref[...], kv_ref[...], bw=bw).astype(BF16)

    return _pcall(body, name=name, out_shape=jax.ShapeDtypeStruct((s, bw), BF16), grid=(s // t,),
                  in_specs=[pl.BlockSpec((t, bw), lambda i: (i, q_blk)), pl.BlockSpec((t, bw), lambda i: (i, z_blk)),
                            pl.BlockSpec((m, 2 * bw), lambda i: (0, 0))],
                  out_specs=pl.BlockSpec((t, bw), lambda i: (i, 0)), sem=("parallel",))(p, p, kv)


def _attn_bwd(p, kv, d_o, *, bw, q_blk, z_blk, name):
    s = p.shape[0]
    m = kv.shape[0]
    t = _tile(s, 256, SUBLANES)

    def body(q_ref, z_ref, kv_ref, do_ref, dq_ref, dz_ref, dkv_ref):
        _, vjp = jax.vjp(functools.partial(_attn_tile, bw=bw), q_ref[...], z_ref[...], kv_ref[...])
        dq, dz, dkv = vjp(do_ref[...].astype(F32))
        dq_ref[...] = dq.astype(BF16)
        dz_ref[...] = dz.astype(BF16)

        @pl.when(pl.program_id(0) == 0)
        def _():
            dkv_ref[...] = jnp.zeros_like(dkv_ref)

        dkv_ref[...] += dkv

    sd = jax.ShapeDtypeStruct
    tok = pl.BlockSpec((t, bw), lambda i: (i, 0))
    return _pcall(body, name=name, out_shape=(sd((s, bw), BF16), sd((s, bw), BF16), sd((m, 2 * bw), F32)),
                  grid=(s // t,),
                  in_specs=[pl.BlockSpec((t, bw), lambda i: (i, q_blk)), pl.BlockSpec((t, bw), lambda i: (i, z_blk)),
                            pl.BlockSpec((m, 2 * bw), lambda i: (0, 0)), tok],
                  out_specs=(tok, tok, pl.BlockSpec((m, 2 * bw), lambda i: (0, 0))), sem=("arbitrary",))(p, p, kv, d_o)


def _merge_fwd(p, o4, wg, bg, wb, *, rank, g_blk, name):
    s = p.shape[0]
    _, bw, d = wb.shape
    tm, tn = _tile(s, 512), _tile(d, 512)

    def body(g_ref, o_ref, wg_ref, bg_ref, wb_ref, out_ref):
        g = g_ref[...]
        acc = jnp.zeros((tm, tn), F32)
        for n in range(N_BRANCH):
            gate = _sigmoid(_bdot(g, wg_ref[n], 1, 0) + bg_ref[n])
            acc = acc + gate * _bdot(o_ref[n], wb_ref[n], 1, 0)
        out_ref[...] = acc.astype(BF16)

    return _pcall(body, name=name, out_shape=jax.ShapeDtypeStruct((s, d), BF16), grid=(s // tm, d // tn),
                  in_specs=[pl.BlockSpec((tm, rank), lambda i, j: (i, g_blk)),
                            pl.BlockSpec((N_BRANCH, tm, bw), lambda i, j: (0, i, 0)),
                            pl.BlockSpec((N_BRANCH, rank, tn), lambda i, j: (0, 0, j)),
                            pl.BlockSpec((N_BRANCH, 1, tn), lambda i, j: (0, 0, j)),
                            pl.BlockSpec((N_BRANCH, bw, tn), lambda i, j: (0, 0, j))],
                  out_specs=pl.BlockSpec((tm, tn), lambda i, j: (i, j)), sem=("parallel", "parallel"))(
                      p, o4, wg, bg, wb)


def _merge_bwd(p, o4, wg, bg, wb, dmerged, *, rank, g_blk, name):
    s = p.shape[0]
    _, bw, d = wb.shape
    tm, tn = _tile(s, 512), _tile(d, 512)

    def body(g_ref, o_ref, wg_ref, bg_ref, wb_ref, dm_ref, dpre_ref, dbr_ref, dbg_ref):
        g = g_ref[...]
        dm = dm_ref[...].astype(F32)

        @pl.when(pl.program_id(1) == 0)
        def _():
            dbg_ref[...] = jnp.zeros_like(dbg_ref)

        for n in range(N_BRANCH):
            gate = _sigmoid(_bdot(g, wg_ref[n], 1, 0) + bg_ref[n])
            br = _bdot(o_ref[n], wb_ref[n], 1, 0)
            dpre = dm * br * gate * (1.0 - gate)
            dpre_ref[n] = dpre.astype(BF16)
            dbr_ref[n] = (dm * gate).astype(BF16)
            dbg_ref[n] += jnp.sum(dpre, axis=0, keepdims=True)

    sd = jax.ShapeDtypeStruct
    big = pl.BlockSpec((N_BRANCH, tm, tn), lambda j, i: (0, i, j))
    return _pcall(body, name=name,
                  out_shape=(sd((N_BRANCH, s, d), BF16), sd((N_BRANCH, s, d), BF16), sd((N_BRANCH, 1, d), F32)),
                  grid=(d // tn, s // tm),
                  in_specs=[pl.BlockSpec((tm, rank), lambda j, i: (i, g_blk)),
                            pl.BlockSpec((N_BRANCH, tm, bw), lambda j, i: (0, i, 0)),
                            pl.BlockSpec((N_BRANCH, rank, tn), lambda j, i: (0, 0, j)),
                            pl.BlockSpec((N_BRANCH, 1, tn), lambda j, i: (0, 0, j)),
                            pl.BlockSpec((N_BRANCH, bw, tn), lambda j, i: (0, 0, j)),
                            pl.BlockSpec((tm, tn), lambda j, i: (i, j))],
                  out_specs=(big, big, pl.BlockSpec((N_BRANCH, 1, tn), lambda j, i: (0, 0, j))),
                  sem=("parallel", "arbitrary"))(p, o4, wg, bg, wb, dmerged)


def _adamw(w, g, m, v, *, name):
    r, cdim = w.shape
    tr = _tile(r, 128, SUBLANES)
    bc1 = 1.0 - ADAM_B1 ** ADAM_STEP
    bc2 = 1.0 - ADAM_B2 ** ADAM_STEP

    def body(w_ref, g_ref, m_ref, v_ref, d_ref, nm_ref, nv_ref):
        gv = g_ref[...]
        nm = ADAM_B1 * m_ref[...] + (1.0 - ADAM_B1) * gv
        nv = ADAM_B2 * v_ref[...] + (1.0 - ADAM_B2) * (gv * gv)
        d_ref[...] = -ADAM_LR * ((nm / bc1) / (jnp.sqrt(nv / bc2) + ADAM_EPS) + ADAM_WD * w_ref[...])
        nm_ref[...] = nm
        nv_ref[...] = nv

    blk = pl.BlockSpec((tr, cdim), lambda i: (i, 0))
    sd = jax.ShapeDtypeStruct((r, cdim), F32)
    return _pcall(body, name=name, out_shape=(sd, sd, sd), grid=(r // tr,), in_specs=[blk] * 4, out_specs=(blk,) * 3,
                  sem=("parallel",))(w, g, m, v)


HBM_SPEC = pl.BlockSpec(memory_space=pl.ANY)


def _place():
    x, y, c = lax.axis_index("x"), lax.axis_index("y"), lax.axis_index("c")
    return x, y, c, [(1 - x, y), (x, 1 - y), (1 - x, 1 - y)]


def _rcopy(src, dst, send_sem, recv_sem, device):
    return pltpu.make_async_remote_copy(src_ref=src, dst_ref=dst, send_sem=send_sem, recv_sem=recv_sem,
                                        device_id=device, device_id_type=MESH_ID)


def _comm_call(body, *, name, out_shape, n_in, n_sems, n_local=0):
    scratch = [pltpu.SemaphoreType.DMA((n_sems,)), pltpu.SemaphoreType.DMA((n_sems,))]
    if n_local:
        scratch.append(pltpu.SemaphoreType.DMA((n_local,)))
    multi = isinstance(out_shape, (tuple, list))
    return pl.pallas_call(body, name=name, out_shape=out_shape, in_specs=[HBM_SPEC] * n_in,
                          out_specs=tuple(HBM_SPEC for _ in out_shape) if multi else HBM_SPEC,
                          scratch_shapes=scratch, interpret=False)


def _ag4(items):
    n = len(items)

    def copies(ins, outs, send_sems, recv_sems, _, second_stage):
        x, y, c, chips = _place()
        sibling = (x, y, 1 - c)

        def part(i, px, py, h):
            half = ins[i].shape[0] // 2
            return outs[i].at[2 * px + py, pl.ds(h * half, half)]

        local = []
        first, landed, passed, arrived = [], [], [], []
        for i in range(n):
            half = ins[i].shape[0] // 2
            for j, chip in enumerate(chips):
                sems = (send_sems.at[6 * i + j], recv_sems.at[6 * i + j])
                sems2 = (send_sems.at[6 * i + 3 + j], recv_sems.at[6 * i + 3 + j])
                first.append(_rcopy(ins[i].at[pl.ds(c * half, half)], part(i, x, y, c), *sems, (*chip, c)))
                if second_stage:
                    landed.append(_rcopy(part(i, *chip, c), part(i, *chip, c), *sems, (*chip, c)))
                    passed.append(_rcopy(part(i, *chip, c), part(i, *chip, c), *sems2, sibling))
                    arrived.append(_rcopy(part(i, *chip, 1 - c), part(i, *chip, 1 - c), *sems2, sibling))
        return local, first, landed, passed, arrived

    def start(*refs):
        local, first, _, _, _ = copies(*refs, False)
        for cp in local + first:
            cp.start()

    def finish(*refs):
        local, first, landed, passed, arrived = copies(*refs, True)
        for k in range(3 * n):
            landed[k].wait_recv()
            passed[k].start()
        for cp in arrived:
            cp.wait_recv()
        for cp in first + passed:
            cp.wait_send()
        for cp in local:
            cp.wait()

    return _Comm(items, [jax.ShapeDtypeStruct((4,) + t.shape, t.dtype) for t in items], 6 * n, 0, start, finish)


def _own_slot(g, local, chip):
    return [jnp.where(chip == k, local, g[k]) for k in range(4)]


def _sib_halves(items, *, name):
    n = len(items)

    def body(*refs):
        ins, outs = refs[:n], refs[n:2 * n]
        send_sems, recv_sems = refs[2 * n:]
        x, y, c, _ = _place()
        cps = []
        for i in range(n):
            half = ins[i].shape[1] // 2
            cps.append(_rcopy(ins[i].at[:, pl.ds((1 - c) * half, half)], outs[i], send_sems.at[i], recv_sems.at[i],
                              (x, y, 1 - c)))
        for cp in cps:
            cp.start()
        for cp in cps:
            cp.wait()

    shapes = tuple(jax.ShapeDtypeStruct((4, t.shape[1] // 2) + t.shape[2:], t.dtype) for t in items)
    return _comm_call(body, name=name, out_shape=shapes, n_in=n, n_sems=n)(*items)


def _to_chips(items):
    n = len(items)

    def copies(ins, outs, send_sems, recv_sems, _):
        x, y, c, chips = _place()
        return [_rcopy(ins[i].at[2 * chip[0] + chip[1]], outs[i].at[j], send_sems.at[3 * i + j],
                       recv_sems.at[3 * i + j], (*chip, c)) for i in range(n) for j, chip in enumerate(chips)]

    def start(*refs):
        for cp in copies(*refs):
            cp.start()

    def finish(*refs):
        for cp in copies(*refs):
            cp.wait()

    return _Comm(items, [jax.ShapeDtypeStruct((3,) + t.shape[1:], t.dtype) for t in items], 3 * n, 0, start, finish)


def _join_halves(items, c, *, name):
    n = len(items)

    def body(*refs):
        ins, outs = refs[:n], refs[n:2 * n]
        send_sems, recv_sems = refs[2 * n:]
        x, y, cc, _ = _place()
        sibling = (x, y, 1 - cc)
        cps = []
        for i in range(n):
            a = ins[i].shape[0]
            cps.append(_rcopy(ins[i], outs[i].at[pl.ds(cc * a, a)], send_sems.at[i], recv_sems.at[i], sibling))
        for cp in cps:
            cp.start()
        for i in range(n):
            a = ins[i].shape[0]
            cps[i].wait_send()
            _rcopy(ins[i], outs[i].at[pl.ds((1 - cc) * a, a)], send_sems.at[i], recv_sems.at[i], sibling).wait_recv()

    shapes = tuple(jax.ShapeDtypeStruct((2 * t.shape[0],) + t.shape[1:], t.dtype) for t in items)
    got = _comm_call(body, name=name, out_shape=shapes, n_in=n, n_sems=n)(*items)
    out = []
    for q, g in zip(items, got):
        a = q.shape[0]
        out.append(jnp.concatenate([jnp.where(c == h, q, g[h * a:(h + 1) * a]) for h in range(2)], axis=0))
    return out


def _swap_sibling(buf, *, name):
    def body(b_ref, recv_ref, send_sems, recv_sems):
        x, y, c, _ = _place()
        cp = _rcopy(b_ref, recv_ref, send_sems.at[0], recv_sems.at[0], (x, y, 1 - c))
        cp.start()
        cp.wait()

    return _comm_call(body, name=name, out_shape=jax.ShapeDtypeStruct(buf.shape, buf.dtype), n_in=1, n_sems=1)(buf)


def _gather_chips(buf, *, name):
    n, cdim = buf.shape

    def body(b_ref, out_ref, send_sems, recv_sems):
        x, y, c, chips = _place()
        cps = [_rcopy(b_ref, out_ref.at[2 * x + y], send_sems.at[j], recv_sems.at[j], (*chip, c))
               for j, chip in enumerate(chips)]
        for cp in cps:
            cp.start()
        for j, chip in enumerate(chips):
            slot = out_ref.at[2 * chip[0] + chip[1]]
            _rcopy(slot, slot, send_sems.at[j], recv_sems.at[j], (*chip, c)).wait_recv()
        for cp in cps:
            cp.wait_send()

    return _comm_call(body, name=name, out_shape=jax.ShapeDtypeStruct((4, n, cdim), buf.dtype), n_in=1, n_sems=3)(buf)


def _rows_per_block(rows, cdim, itemsize, align):
    return _tile(rows, max(align, ELEMENTWISE_BLOCK_BYTES // (cdim * itemsize) // align * align), align)


def _sum_sib(g4, recv, c_idx, *, name):
    _, rows, cdim = g4.shape
    half = rows // 2
    tr = _rows_per_block(half, cdim, 2, 16)
    nh = half // tr

    def body(c_ref, g_ref, r_ref, o_ref):
        o_ref[...] = (g_ref[...].astype(F32) + r_ref[...].astype(F32)).astype(o_ref.dtype)

    return _pcall(body, name=name, out_shape=jax.ShapeDtypeStruct((4, half, cdim), g4.dtype), grid=(4, nh),
                  num_prefetch=1,
                  in_specs=[pl.BlockSpec((None, tr, cdim), lambda k, i, c_ref: (k, c_ref[0] * nh + i, 0)),
                            pl.BlockSpec((None, tr, cdim), lambda k, i, c_ref: (k, i, 0))],
                  out_specs=pl.BlockSpec((None, tr, cdim), lambda k, i, c_ref: (k, i, 0)),
                  sem=("parallel", "parallel"))(c_idx, g4, recv)


def _sum_chips(h4, recv3, chip_idx, *, name):
    _, n, cdim = h4.shape
    tr = _rows_per_block(n, cdim, 4, 16)

    def body(k_ref, h_ref, r_ref, o_ref):
        acc = h_ref[...].astype(F32)
        for j in range(3):
            acc = acc + r_ref[j].astype(F32)
        o_ref[...] = acc

    return _pcall(body, name=name, out_shape=jax.ShapeDtypeStruct((n, cdim), F32), grid=(n // tr,), num_prefetch=1,
                  in_specs=[pl.BlockSpec((None, tr, cdim), lambda i, k_ref: (k_ref[0], i, 0)),
                            pl.BlockSpec((3, tr, cdim), lambda i, k_ref: (0, i, 0))],
                  out_specs=pl.BlockSpec((tr, cdim), lambda i, k_ref: (i, 0)), sem=("parallel",))(chip_idx, h4, recv3)


def _add2(a, b, *, name):
    n, cdim = a.shape
    tr = _tile(n, 256, SUBLANES)

    def body(a_ref, b_ref, o_ref):
        o_ref[...] = a_ref[...] + b_ref[...]

    blk = pl.BlockSpec((tr, cdim), lambda i: (i, 0))
    return _pcall(body, name=name, out_shape=jax.ShapeDtypeStruct((n, cdim), F32), grid=(n // tr,), in_specs=[blk, blk],
                  out_specs=blk, sem=("parallel",))(a, b)


def _sum4(x4, *, name):
    _, n, cdim = x4.shape
    tr = _tile(n, 256, SUBLANES)

    def body(x_ref, o_ref):
        o_ref[...] = ((x_ref[0] + x_ref[1]) + x_ref[2]) + x_ref[3]

    return _pcall(body, name=name, out_shape=jax.ShapeDtypeStruct((n, cdim), F32), grid=(n // tr,),
                  in_specs=[pl.BlockSpec((4, tr, cdim), lambda i: (0, i, 0))],
                  out_specs=pl.BlockSpec((tr, cdim), lambda i: (i, 0)), sem=("parallel",))(x4)


def _rs_pair(items, c_idx, *, name):
    recv = _sib_halves(items, name=name + "_sib")
    return [_sum_sib(items[i], recv[i], c_idx, name=f"{name}_sum1_{i}") for i in range(len(items))]


def _rs_close(h4, recv3, c_idx, chip_idx, *, name):
    q = [_sum_chips(h4[i], recv3[i], chip_idx, name=f"{name}_sum2_{i}") for i in range(len(h4))]
    return _join_halves(q, c_idx[0], name=name + "_join")


def _all_reduce(buf, chip, *, name):
    pair = _add2(buf, _swap_sibling(buf, name=name + "_sib"), name=name + "_add")
    slots = jnp.stack(_own_slot(_gather_chips(pair, name=name + "_ici"), pair, chip), axis=0)
    return _sum4(slots, name=name + "_sum")


def _pack_rows(flat, lead, align):
    n = flat.shape[-1]
    unit = PACK_COLS * align
    total = -(-n // unit) * unit
    flat = jnp.pad(flat, [(0, 0)] * len(lead) + [(0, total - n)])
    return flat.reshape(*lead, total // PACK_COLS, PACK_COLS)


BIG = (("w_in", 1), ("ssm_w_glu", 1), ("w_kv", 0), ("w_gate", 2), ("w_branch", 2), ("w_out", 0))
SMALL_SHARDED = (("dn_conv_w", 1), ("lru_conv_w", 1), ("b_gate", 1))
SMALL = ("norm_w", "dn_a_log", "dn_dt_bias", "dn_norm_w", "lru_conv_b", "lru_w_r", "lru_b_r", "lru_w_i", "lru_b_i",
         "lru_lambda", "ssm_log_dt", "ssm_a_re", "ssm_a_im", "ssm_b_re", "ssm_b_im", "ssm_c_re", "ssm_c_im", "ssm_d",
         "ssm_b_glu", "mem_norm_w")
WEIGHTS = ("norm_w", "w_in", "dn_conv_w", "dn_a_log", "dn_dt_bias", "dn_norm_w", "lru_conv_w", "lru_conv_b",
           "lru_w_r", "lru_b_r", "lru_w_i", "lru_b_i", "lru_lambda", "ssm_log_dt", "ssm_a_re", "ssm_a_im", "ssm_b_re",
           "ssm_b_im", "ssm_c_re", "ssm_c_im", "ssm_d", "ssm_w_glu", "ssm_b_glu", "mem_norm_w", "w_kv", "w_gate",
           "b_gate", "w_branch", "w_out", "final_norm_w")


REST = BIG[1:]


def _gather_rest(wts, l):
    small = _pack_rows(jnp.concatenate([wts[n][l].reshape(-1) for n, _ in SMALL_SHARDED]), (), 2 * SUBLANES)
    return _ag4([wts[n][l].astype(BF16) for n, _ in REST] + [small])


def _full_rest(g, local, wts, chip):
    out = {n: jnp.concatenate(_own_slot(g[i], local[i], chip), axis=ax) for i, (n, ax) in enumerate(REST)}
    flat, off = jnp.stack(_own_slot(g[-1], local[-1], chip), axis=0).reshape(4, -1), 0
    for n, ax in SMALL_SHARDED:
        shp = wts[n].shape[1:]
        sz = math.prod(shp)
        out[n] = jnp.concatenate(list(flat[:, off:off + sz].reshape(4, *shp)), axis=ax)
        off += sz
    return out


def _chip_rows(t):
    return t.reshape(4, t.shape[0] // 4, t.shape[1])


def _w_in_layout(w, bw, heads, rank):
    d = w.shape[0]
    ba = 4 * bw
    rest = ba + 2 * heads
    return jnp.concatenate([w[:, :ba], w[:, rest:], w[:, ba:rest], jnp.zeros((d, BA_PAD - 2 * heads), w.dtype)], axis=1)


def _w_in_unlayout(dw, bw, heads, rank):
    ba = 4 * bw
    tail = 10 * bw + rank
    return jnp.concatenate([dw[:, :ba], dw[:, tail:tail + 2 * heads], dw[:, ba:tail]], axis=1)


def _lru_dense(w):
    nb, blk, _ = w.shape
    return jnp.einsum("nij,nm->nimj", w, jnp.eye(nb, dtype=w.dtype)).reshape(nb * blk, nb * blk)


def _w8(w):
    return jnp.concatenate([w, jnp.zeros((SUBLANES - CONV_WIDTH, w.shape[1]), w.dtype)], axis=0)


def _layer_fwd(x, mem, w_in_slots, wts, prm, l, chip, dn_comm):
    s, d = x.shape
    bw = d // N_BRANCH
    heads = bw // DN_HEAD_DIM
    rank = wts["w_gate"].shape[2]
    tag = f"l{l}_"
    sv = {"x": x}
    w_in = _w_in_layout(jnp.concatenate(w_in_slots, axis=1), bw, heads, rank)
    sv["w_in"] = w_in
    h = _rms_fwd(x, prm["norm_w"], name=tag + "norm")
    gather = _gather_rest(wts, l)
    p, rest = _mm(h, w_in, comm=gather, name=tag + "in_proj")
    full = _full_rest(rest, gather.inputs, wts, chip)
    sv["h"], sv["p"] = h, p
    conv_a = _conv_fwd(p, 0, 3 * bw, _w8(full["dn_conv_w"]), jnp.zeros((1, 3 * bw), F32), name=tag + "dn_conv")
    ba_blk = (10 * bw + rank) // BA_PAD
    dn = _dn_fwd(p, conv_a, prm["dn_a_log"], prm["dn_dt_bias"], prm["dn_norm_w"], bw=bw, ba_blk=ba_blk,
                 comm=dn_comm, name=tag + "dn")
    (o_a, dn_states), dn_res = dn if dn_comm is not None else (dn, None)
    sv["conv_a"], sv["dn_states"] = conv_a, dn_states
    xc = _conv_fwd(p, 4, bw, _w8(full["lru_conv_w"]), prm["lru_conv_b"].reshape(1, bw), name=tag + "lru_conv")
    (wr, wi), lru_vjp = jax.vjp(lambda a, b: (_lru_dense(a), _lru_dense(b)), prm["lru_w_r"], prm["lru_w_i"])
    row = lambda v: v.reshape(1, bw)
    lru_args = (wr, row(prm["lru_b_r"]), wi, row(prm["lru_b_i"]), row(prm["lru_lambda"]))
    o_b, lru_saves = _lru_fwd(p, xc, *lru_args, bw=bw, z_blk=5, name=tag + "lru")
    sv["xc"], sv["lru_saves"], sv["lru_args"], sv["lru_vjp"] = xc, lru_saves, lru_args, lru_vjp
    prep, s5_vjp = jax.vjp(_s5_prep, prm["ssm_log_dt"], prm["ssm_a_re"], prm["ssm_a_im"], prm["ssm_b_re"],
                           prm["ssm_b_im"], prm["ssm_c_re"], prm["ssm_c_im"], prm["ssm_d"])
    u_blk0 = 6 * bw // LANES
    yg, s5_saves = _s5_fwd(p, prep, bw=bw, u_blk0=u_blk0, name=tag + "s5")
    pre = _mm(yg, full["ssm_w_glu"], bias=prm["ssm_b_glu"].reshape(1, 2 * bw), name=tag + "glu_proj")
    o_c = _glu_fwd(pre, p, bw=bw, z_blk=7, name=tag + "glu")
    sv["prep"], sv["s5_vjp"], sv["s5_saves"], sv["yg"], sv["pre"] = prep, s5_vjp, s5_saves, yg, pre
    mem_n = _rms_fwd(mem, prm["mem_norm_w"], name=tag + "mem_norm")
    kv = _mm(mem_n, full["w_kv"], name=tag + "kv_proj")
    o_d = _attn_fwd(p, kv, bw=bw, q_blk=8, z_blk=9, name=tag + "attn")
    sv["mem_n"], sv["kv"] = mem_n, kv
    o4 = jnp.stack([o_a, o_b, o_c, o_d], axis=0)
    bg = full["b_gate"].reshape(N_BRANCH, 1, d)
    g_blk = 10 * bw // rank
    merged = _merge_fwd(p, o4, full["w_gate"], bg, full["w_branch"], rank=rank, g_blk=g_blk, name=tag + "merge")
    sv["o4"], sv["bg"], sv["merged"] = o4, bg, merged
    return _mm(merged, full["w_out"], add=x, name=tag + "out_proj"), sv, full, dn_res


def _layer_bwd(dx_out, mem, sv, full, prm, l, dn_comm_of):
    x, p, h = sv["x"], sv["p"], sv["h"]
    s, d = x.shape
    bw = d // N_BRANCH
    heads = bw // DN_HEAD_DIM
    rank = full["w_gate"].shape[1]
    tag = f"l{l}b_"
    big, small = {}, {}
    dmerged = _mm(dx_out, full["w_out"], tb=True, out_dtype=BF16, name=tag + "out_dx")
    big["w_out"] = _mm(sv["merged"], dx_out, ta=True, out_dtype=BF16, name=tag + "out_dw")
    g_blk = 10 * bw // rank
    dpre, dbr, dbg = _merge_bwd(p, sv["o4"], full["w_gate"], sv["bg"], full["w_branch"], dmerged, rank=rank,
                                g_blk=g_blk, name=tag + "merge")
    small["b_gate"] = dbg.reshape(N_BRANCH, d)
    glow = p[:, 10 * bw:10 * bw + rank].astype(BF16)
    dglow = None
    dwg, dwb, d_o = [], [], []
    for n in range(N_BRANCH):
        dglow = _mm(dpre, full["w_gate"], la=n, lb=n, tb=True, add=dglow, name=tag + f"gate_dx{n}",
                    out_dtype=BF16 if n == N_BRANCH - 1 else F32)
        dwg.append(_mm(glow, dpre, ta=True, lb=n, out_dtype=BF16, nsplit=4, name=tag + f"gate_dw{n}"))
        d_o.append(_mm(dbr, full["w_branch"], la=n, lb=n, tb=True, name=tag + f"branch_dx{n}"))
        dwb.append(_mm(sv["o4"], dbr, ta=True, la=n, lb=n, out_dtype=BF16, nsplit=4, name=tag + f"branch_dw{n}"))
    dpre_glu, dz_c, dbglu = _glu_bwd(sv["pre"], p, d_o[2], bw=bw, z_blk=7, name=tag + "glu")
    small["ssm_b_glu"] = dbglu[0]
    dyg = _mm(dpre_glu, full["ssm_w_glu"], tb=True, name=tag + "glu_dx")
    big["ssm_w_glu"] = _mm(sv["yg"], dpre_glu, ta=True, out_dtype=BF16, nsplit=4, name=tag + "glu_dw")
    s5 = _s5_bwd(p, sv["prep"], sv["s5_saves"], dyg, bw=bw, u_blk0=6 * bw // LANES, name=tag + "s5")
    du = s5[0]
    (small["ssm_log_dt"], small["ssm_a_re"], small["ssm_a_im"], small["ssm_b_re"], small["ssm_b_im"],
     small["ssm_c_re"], small["ssm_c_im"], small["ssm_d"]) = sv["s5_vjp"](tuple(s5[1:]))
    dq, dz_d, dkv = _attn_bwd(p, sv["kv"], d_o[3], bw=bw, q_blk=8, z_blk=9, name=tag + "attn")
    big["w_kv"] = _mm(sv["mem_n"], dkv, ta=True, out_dtype=BF16, name=tag + "kv_dw")
    dmem_n = _mm(dkv, full["w_kv"], tb=True, name=tag + "kv_dx")
    _, dmnw = _rms_bwd(mem, prm["mem_norm_w"], dmem_n, None, name=tag + "mem_norm")
    small["mem_norm_w"] = dmnw[0]
    dn_comm = dn_comm_of([_chip_rows(big["w_out"]), *dwg, *dwb, big["ssm_w_glu"], _chip_rows(big["w_kv"])])
    ba_blk = (10 * bw + rank) // BA_PAD
    dn = _dn_bwd(p, sv["conv_a"], sv["dn_states"], d_o[0], prm["dn_a_log"], prm["dn_dt_bias"], prm["dn_norm_w"],
                 bw=bw, ba_blk=ba_blk, comm=dn_comm, name=tag + "dn")
    (dconv, dz_a, dba, dal, ddt, dnw), dn_res = dn if dn_comm is not None else (dn, None)
    small["dn_a_log"] = dal[0, heads:2 * heads]
    small["dn_dt_bias"] = ddt[0, heads:2 * heads]
    small["dn_norm_w"] = dnw[0]
    dqkv, dw8_a, _ = _conv_bwd(p, 0, 3 * bw, _w8(full["dn_conv_w"]), dconv, name=tag + "dn_conv")
    small["dn_conv_w"] = dw8_a[:CONV_WIDTH]
    dxc, dz_b, dwr, dwi, dbr_, dbi_, dlam = _lru_bwd(p, sv["xc"], sv["lru_saves"], d_o[1], *sv["lru_args"], bw=bw,
                                                     z_blk=5, name=tag + "lru")
    small["lru_w_r"], small["lru_w_i"] = sv["lru_vjp"]((dwr, dwi))
    small["lru_b_r"], small["lru_b_i"], small["lru_lambda"] = dbr_[0], dbi_[0], dlam[0]
    dlx, dw8_b, dcb = _conv_bwd(p, 4, bw, _w8(full["lru_conv_w"]), dxc, name=tag + "lru_conv")
    small["lru_conv_w"] = dw8_b[:CONV_WIDTH]
    small["lru_conv_b"] = dcb[0]
    dp = jnp.concatenate([dqkv, dz_a, dlx, dz_b, du, dz_c, dq, dz_d, dglow, dba], axis=1).astype(BF16)
    dh = _mm(dp, sv["w_in"], tb=True, name=tag + "in_dx")
    dw_in = _w_in_unlayout(_mm(h, dp, ta=True, out_dtype=BF16, name=tag + "in_dw"), bw, heads, rank)
    dx, dnw_in = _rms_bwd(x, prm["norm_w"], dh, dx_out, name=tag + "norm")
    small["norm_w"] = dnw_in[0]
    late = [dw_in.reshape(d, 4, dw_in.shape[1] // 4).transpose(1, 0, 2)]
    return dx, late, small, dn_res


def _step(wts, mom, vel, x, mem, target):
    depth = wts["norm_w"].shape[0]
    xi, yi, ci = lax.axis_index("x"), lax.axis_index("y"), lax.axis_index("c")
    c_idx = ci.astype(jnp.int32).reshape(1)
    chip = (2 * xi + yi).astype(jnp.int32)
    chip_idx = chip.reshape(1)
    x, mem, target = x[0], mem[0], target[0]

    prms = [{n: wts[n][l] for n in SMALL} for l in range(depth)]
    w_in_local = [wts["w_in"][l].astype(BF16) for l in range(depth)]
    w_in_g = _run_comm(_ag4([w_in_local[0]]), name="gather_w_in0")[0]
    saves, fulls = [], []
    act = x
    for l in range(depth):
        nxt = _ag4([w_in_local[l + 1]]) if l + 1 < depth else None
        act, sv, full, res = _layer_fwd(act, mem, _own_slot(w_in_g, w_in_local[l], chip), wts, prms[l], l, chip, nxt)
        saves.append(sv)
        fulls.append(full)
        w_in_g = res[0] if nxt is not None else None
    loss_part, dx, dfw = _loss_head(act, wts["final_norm_w"], target, name="loss_head")
    loss = lax.psum(loss_part[0, 0], ("x", "y", "c"))

    big_g = [None] * depth
    small_g = [None] * depth
    pending = []
    done = {}
    for l in reversed(range(depth)):
        def dn_comm_of(early, l=l):
            pending.append((l, "early", _rs_pair(early, c_idx, name=f"scatter_g{l}a")))
            return _to_chips([t for _, _, h4 in pending for t in h4])

        dx, late, small_g[l], res = _layer_bwd(dx, mem, saves[l], fulls[l], prms[l], l, dn_comm_of)
        off = 0
        for ll, grp, h4 in pending:
            done[ll, grp] = _rs_close(h4, res[off:off + len(h4)], c_idx, chip_idx, name=f"scatter_g{ll}{grp[0]}c")
            off += len(h4)
        pending = [(l, "late", _rs_pair(late, c_idx, name=f"scatter_g{l}b"))]
    (l, grp, h4), = pending
    done[l, grp] = _rs_close(h4, _run_comm(_to_chips(h4), name=f"scatter_g{l}b_ici"), c_idx, chip_idx,
                             name=f"scatter_g{l}lc")
    for l in range(depth):
        e, t = done[l, "early"], done[l, "late"]
        big_g[l] = {"w_in": t[0], "ssm_w_glu": e[1 + 2 * N_BRANCH], "w_kv": e[2 + 2 * N_BRANCH], "w_out": e[0],
                    "w_gate": jnp.stack(e[1:1 + N_BRANCH], axis=0),
                    "w_branch": jnp.stack(e[1 + N_BRANCH:1 + 2 * N_BRANCH], axis=0)}

    names = SMALL + tuple(n for n, _ in SMALL_SHARDED)
    flat = jnp.concatenate([small_g[l][n].reshape(-1) for l in range(depth) for n in names] + [dfw.reshape(-1)])
    red = _all_reduce(_pack_rows(flat, (), 256), chip, name="reduce_small").reshape(-1)
    grads, off = {n: [] for n in names}, 0
    for l in range(depth):
        for n in names:
            shp = small_g[l][n].shape
            sz = math.prod(shp)
            grads[n].append(red[off:off + sz].reshape(shp))
            off += sz
    grads = {n: jnp.stack(v, axis=0) for n, v in grads.items()}
    grads["final_norm_w"] = red[off:off + dfw.size].reshape(wts["final_norm_w"].shape)
    for n, ax in SMALL_SHARDED:
        width = wts[n].shape[-1]
        grads[n] = lax.dynamic_slice_in_dim(grads[n], chip * width, width, axis=ax + 1)
    for n, _ in BIG:
        grads[n] = jnp.stack([big_g[l][n] for l in range(depth)], axis=0)

    delta, new_m, new_v = {}, {}, {}
    for n, _ in BIG:
        shp = wts[n].shape
        two = lambda t: t.reshape(-1, shp[-1])
        dlt, nm, nv = _adamw(two(wts[n]), two(grads[n]), two(mom[n]), two(vel[n]), name="adamw_" + n)
        delta[n], new_m[n], new_v[n] = dlt.reshape(shp), nm.reshape(shp), nv.reshape(shp)
    rest = [n for n in WEIGHTS if n not in dict(BIG)]
    cat = lambda src: _pack_rows(jnp.concatenate([src[n].reshape(-1) for n in rest]), (), SUBLANES)
    dlt, nm, nv = _adamw(cat(wts), cat(grads), cat(mom), cat(vel), name="adamw_small")
    off = 0
    for n in rest:
        shp = wts[n].shape
        sz = math.prod(shp)
        for dst, src in ((delta, dlt), (new_m, nm), (new_v, nv)):
            dst[n] = src.reshape(-1)[off:off + sz].reshape(shp)
        off += sz
    return (loss, dx[None], *[grads[n] for n in WEIGHTS], *[delta[n] for n in WEIGHTS], *[new_m[n] for n in WEIGHTS],
            *[new_v[n] for n in WEIGHTS])


def kernel(x, mem, norm_w, w_in, dn_conv_w, dn_a_log, dn_dt_bias, dn_norm_w, lru_conv_w, lru_conv_b, lru_w_r, lru_b_r, lru_w_i, lru_b_i, lru_lambda, ssm_log_dt, ssm_a_re, ssm_a_im, ssm_b_re, ssm_b_im, ssm_c_re, ssm_c_im, ssm_d, ssm_w_glu, ssm_b_glu, mem_norm_w, w_kv, w_gate, b_gate, w_branch, w_out, final_norm_w, loss_target, m_norm_w, m_w_in, m_dn_conv_w, m_dn_a_log, m_dn_dt_bias, m_dn_norm_w, m_lru_conv_w, m_lru_conv_b, m_lru_w_r, m_lru_b_r, m_lru_w_i, m_lru_b_i, m_lru_lambda, m_ssm_log_dt, m_ssm_a_re, m_ssm_a_im, m_ssm_b_re, m_ssm_b_im, m_ssm_c_re, m_ssm_c_im, m_ssm_d, m_ssm_w_glu, m_ssm_b_glu, m_mem_norm_w, m_w_kv, m_w_gate, m_b_gate, m_w_branch, m_w_out, m_final_norm_w, v_norm_w, v_w_in, v_dn_conv_w, v_dn_a_log, v_dn_dt_bias, v_dn_norm_w, v_lru_conv_w, v_lru_conv_b, v_lru_w_r, v_lru_b_r, v_lru_w_i, v_lru_b_i, v_lru_lambda, v_ssm_log_dt, v_ssm_a_re, v_ssm_a_im, v_ssm_b_re, v_ssm_b_im, v_ssm_c_re, v_ssm_c_im, v_ssm_d, v_ssm_w_glu, v_ssm_b_glu, v_mem_norm_w, v_w_kv, v_w_gate, v_b_gate, v_w_branch, v_w_out, v_final_norm_w):
    given = dict(locals())
    wts = {n: given[n] for n in WEIGHTS}
    mom = {n: given["m_" + n] for n in WEIGHTS}
    vel = {n: given["v_" + n] for n in WEIGHTS}
    return _step(wts, mom, vel, x, mem, loss_target)
```
